```python
import jax
import jax.numpy as jnp
from jax import lax
import numpy as np

D_MODEL = 1024
BATCH = 2
SEQ = 16384
DEPTH = 2

GRID_W = 64
CTX_LEN = 256
EPS = 1e-6

MLA_HEADS = D_MODEL // 128
MLA_NOPE = 64
MLA_ROPE = 32
MLA_V = 64
MLA_Q_RANK = D_MODEL // 4
MLA_KV_RANK = D_MODEL // 8
MLA_COLS = MLA_Q_RANK + MLA_KV_RANK + MLA_ROPE
MLA_SCALE = (MLA_NOPE + MLA_ROPE) ** -0.5
ROPE_FREQS = MLA_ROPE // 4
ROPE_BASE = 10000.0
Q_BLOCK = 128

RWKV_HEAD = 64
RWKV_DIM = D_MODEL // 2
RWKV_HEADS = RWKV_DIM // RWKV_HEAD
DECAY_LORA = 32
AAA_LORA = 32
GATE_LORA = 96
RWKV_COLS = 3 * RWKV_DIM + 2 * DECAY_LORA + 2 * AAA_LORA + GATE_LORA
RWKV_LN_EPS = 64e-5

IN_COLS = MLA_COLS + RWKV_COLS
MIX_WIDTH = MLA_HEADS * MLA_V + RWKV_DIM

CHUNK = 128
GM_WIDTH = D_MODEL
GM_GROUPS = 8
GM_GROUP_DIM = GM_WIDTH // GM_GROUPS
LN_EPS = 1e-5

N_EXPERTS = 32
TOP_K = 4
EXPERT_FF = D_MODEL
SWIGLU_LIMIT = 7.0
SWIGLU_ALPHA = 1.702
MOE_BLOCK = 128

N_EVEN = (DEPTH + 1) // 2
N_ODD = DEPTH // 2

kernel_name = 'hybrid_mla_rwkv7_gmlp_moe_diffusion_trunk'


def rms_norm(x, g):
    x32 = x.astype(jnp.float32)
    y = x32 * lax.rsqrt(jnp.mean(x32 * x32, axis=-1, keepdims=True) + EPS)
    return y.astype(x.dtype) * g


def layer_norm(x, g, b, eps):
    x32 = x.astype(jnp.float32)
    mu = jnp.mean(x32, axis=-1, keepdims=True)
    var = jnp.mean(jnp.square(x32 - mu), axis=-1, keepdims=True)
    return ((x32 - mu) * lax.rsqrt(var + eps)).astype(x.dtype) * g + b


def modulate(h, shift, scale):
    return h * (1 + scale) + shift


def axial_rope_tables(n_tokens):
    rows = n_tokens // GRID_W
    row = jnp.repeat(jnp.arange(rows), GRID_W)
    col = jnp.arange(rows * GRID_W) % GRID_W
    inv = ROPE_BASE ** (-jnp.arange(ROPE_FREQS, dtype=jnp.float32) / ROPE_FREQS)
    ang = jnp.stack([row[:, None] * inv, col[:, None] * inv], axis=1)
    return jnp.cos(ang), jnp.sin(ang)


def apply_axial_rope(x, cos, sin):
    xr = x.reshape(x.shape[:-1] + (2, 2, ROPE_FREQS))
    x1, x2 = xr[..., 0, :], xr[..., 1, :]
    cos = cos.astype(x.dtype)
    sin = sin.astype(x.dtype)
    out = jnp.stack([x1 * cos - x2 * sin, x2 * cos + x1 * sin], axis=-2)
    return out.reshape(x.shape)


def mla_q(p, q_norm, w_uq):
    q = rms_norm(p[..., :MLA_Q_RANK], q_norm) @ w_uq
    q = q.reshape(q.shape[:-1] + (MLA_HEADS, MLA_NOPE + MLA_ROPE))
    return q[..., :MLA_NOPE], q[..., MLA_NOPE:]


def mla_kv(p, kv_norm, w_ukv):
    c_kv = rms_norm(p[..., MLA_Q_RANK:MLA_Q_RANK + MLA_KV_RANK], kv_norm)
    kv = c_kv @ w_ukv
    kv = kv.reshape(kv.shape[:-1] + (MLA_HEADS, MLA_NOPE + MLA_V))
    k_rope = p[..., MLA_Q_RANK + MLA_KV_RANK:MLA_COLS]
    return kv[..., :MLA_NOPE], k_rope, kv[..., MLA_NOPE:]


def mla_attend(q_nope, q_rope, k_nope, k_rope, v):
    s = (jnp.einsum('bqhd,bkhd->bhqk', q_nope, k_nope, preferred_element_type=jnp.float32)
         + jnp.einsum('bqhd,bkd->bhqk', q_rope, k_rope, preferred_element_type=jnp.float32))
    p = jax.nn.softmax(s * MLA_SCALE, axis=-1).astype(v.dtype)
    return jnp.einsum('bhqk,bkhd->bqhd', p, v)


def mla_attend_blocked(q_nope, q_rope, k_nope, k_rope, v):
    b, s = q_nope.shape[:2]
    nb = s // Q_BLOCK
    def blocks(t):
        return t.reshape((b, nb, Q_BLOCK) + t.shape[2:]).swapaxes(0, 1)
    o = lax.map(lambda qs: mla_attend(qs[0], qs[1], k_nope, k_rope, v), (blocks(q_nope), blocks(q_rope)))
    return o.swapaxes(0, 1).reshape(b, s, MLA_HEADS * MLA_V)


def token_shift(p, mu_prev, mu_next):
    zero = jnp.zeros_like(p[:, :1])
    prev = jnp.concatenate([zero, p[:, :-1]], axis=1)
    nxt = jnp.concatenate([p[:, 1:], zero], axis=1)
    return p + mu_prev * (prev - p) + mu_next * (nxt - p)


def rwkv_prepare(p, mu_prev, mu_next, w0, w2, a0, a2, k_k, k_a):
    p = token_shift(p, mu_prev, mu_next)
    b, t, _ = p.shape
    o1, o2, o3 = RWKV_DIM, 2 * RWKV_DIM, 3 * RWKV_DIM
    o4 = o3 + 2 * DECAY_LORA
    o5 = o4 + 2 * AAA_LORA
    r, k, v = p[..., :o1], p[..., o1:o2], p[..., o2:o3]
    wd = jnp.tanh(p[..., o3:o4].reshape(b, t, 2, DECAY_LORA))
    ad = p[..., o4:o5].reshape(b, t, 2, AAA_LORA)
    gd = p[..., o5:]
    w = (w0 + jnp.einsum('btdl,dlc->btdc', wd, w2)).astype(jnp.float32)
    decay = jnp.exp(-jnp.exp(-jax.nn.softplus(-w) - 0.5)).astype(p.dtype)
    a = jax.nn.sigmoid(a0 + jnp.einsum('btdl,dlc->btdc', ad, a2))
    kk32 = (k * k_k).reshape(b, t, RWKV_HEADS, RWKV_HEAD).astype(jnp.float32)
    kk = kk32 * lax.rsqrt(jnp.maximum(jnp.sum(kk32 * kk32, axis=-1, keepdims=True), 1e-24))
    kk = kk.astype(p.dtype).reshape(b, t, RWKV_DIM)
    k_dir = k[:, :, None] * (1 + (a - 1) * k_a)
    b_dir = kk[:, :, None] * a
    return r, k, v, gd, decay, k_dir, -kk, b_dir


def wkv_scan(s0, decay, k, v, a_vec, b_vec, r, reverse):
    b, t, _ = k.shape
    def tm(z):
        return z.reshape(b, t, RWKV_HEADS, RWKV_HEAD).swapaxes(0, 1)
    xs = (tm(decay), tm(k), tm(v), tm(a_vec), tm(b_vec)) + (() if r is None else (tm(r),))
    def step(s, inp):
        w_t, k_t, v_t, a_t, b_t = inp[:5]
        sa = jnp.einsum('bhij,bhj->bhi', s, a_t)
        s = s * w_t[:, :, None, :] + sa[..., None] * b_t[:, :, None, :] + v_t[..., None] * k_t[:, :, None, :]
        y = None if r is None else jnp.einsum('bhij,bhj->bhi', s, inp[5])
        return s, y
    s, ys = lax.scan(step, s0, xs, reverse=reverse)
    y = None if r is None else ys.swapaxes(0, 1).reshape(b, t, RWKV_DIM)
    return s, y


def rwkv_readout(y, r, k, v, gd, g2, ln_w, ln_b, r_k):
    b, t, _ = y.shape
    yh = y.reshape(b, t, RWKV_HEADS, RWKV_HEAD).astype(jnp.float32)
    mu = jnp.mean(yh, axis=-1, keepdims=True)
    var = jnp.mean(jnp.square(yh - mu), axis=-1, keepdims=True)
    yn = ((yh - mu) * lax.rsqrt(var + RWKV_LN_EPS)).astype(y.dtype).reshape(b, t, RWKV_DIM) * ln_w + ln_b
    rkh = (r * k).reshape(b, t, RWKV_HEADS, RWKV_HEAD)
    bonus = jnp.sum(rkh * r_k, axis=-1, keepdims=True) * v.reshape(b, t, RWKV_HEADS, RWKV_HEAD)
    g = jax.nn.sigmoid(gd) @ g2
    return (yn + bonus.reshape(b, t, RWKV_DIM)) * g


def hybrid_mixer(h_lat, h_ctx, rope_cos, rope_sin, w_in, q_norm, w_uq, kv_norm, w_ukv,
                 mu_prev, mu_next, w0, w2, a0, a2, g2, k_k, k_a, r_k, ln_w, ln_b, w_out, ctx_out):
    b, s, _ = h_lat.shape
    p_lat = h_lat @ w_in
    p_ctx = h_ctx @ w_in
    qn_l, qr_l = mla_q(p_lat[..., :MLA_COLS], q_norm, w_uq)
    kn_l, kr_l, v_l = mla_kv(p_lat[..., :MLA_COLS], kv_norm, w_ukv)
    kn_c, kr_c, v_c = mla_kv(p_ctx[..., :MLA_COLS], kv_norm, w_ukv)
    qr_l = apply_axial_rope(qr_l, rope_cos[:, None], rope_sin[:, None])
    kr_l = apply_axial_rope(kr_l, rope_cos, rope_sin)
    kn_all = jnp.concatenate([kn_c, kn_l], axis=1)
    kr_all = jnp.concatenate([kr_c, kr_l], axis=1)
    v_all = jnp.concatenate([v_c, v_l], axis=1)
    o_lat = mla_attend_blocked(qn_l, qr_l, kn_all, kr_all, v_all)
    r_l, k_l, vv_l, gd_l, dec_l, kd_l, av_l, bd_l = rwkv_prepare(
        p_lat[..., MLA_COLS:], mu_prev, mu_next, w0, w2, a0, a2, k_k, k_a)
    r_c, k_c, vv_c, gd_c, dec_c, kd_c, av_c, bd_c = rwkv_prepare(
        p_ctx[..., MLA_COLS:], mu_prev, mu_next, w0, w2, a0, a2, k_k, k_a)
    s0 = jnp.zeros((b, RWKV_HEADS, RWKV_HEAD, RWKV_HEAD), h_lat.dtype)
    r_c_out = r_c if ctx_out else None
    s_cf, y_cf = wkv_scan(s0, dec_c[:, :, 0], kd_c[:, :, 0], vv_c, av_c, bd_c[:, :, 0], r_c_out, False)
    s_cb, y_cb = wkv_scan(s0, dec_c[:, :, 1], kd_c[:, :, 1], vv_c, av_c, bd_c[:, :, 1], r_c_out, True)
    _, y_lf = wkv_scan(s_cf, dec_l[:, :, 0], kd_l[:, :, 0], vv_l, av_l, bd_l[:, :, 0], r_l, False)
    _, y_lb = wkv_scan(s_cb, dec_l[:, :, 1], kd_l[:, :, 1], vv_l, av_l, bd_l[:, :, 1], r_l, True)
    rw_lat = rwkv_readout(y_lf + y_lb, r_l, k_l, vv_l, gd_l, g2, ln_w, ln_b, r_k)
    y_lat = jnp.concatenate([o_lat, rw_lat], axis=-1) @ w_out
    y_ctx = None
    if ctx_out:
        qn_c, qr_c = mla_q(p_ctx[..., :MLA_COLS], q_norm, w_uq)
        o_ctx = mla_attend(qn_c, qr_c, kn_c, kr_c, v_c).reshape(b, h_ctx.shape[1], MLA_HEADS * MLA_V)
        rw_ctx = rwkv_readout(y_cf + y_cb, r_c, k_c, vv_c, gd_c, g2, ln_w, ln_b, r_k)
        y_ctx = jnp.concatenate([o_ctx, rw_ctx], axis=-1) @ w_out
    return y_lat, y_ctx


def chunk_gmlp(h, w_in, vn_w, vn_b, w_s, b_s, w_out):
    b, t, _ = h.shape
    z = jax.nn.gelu(h @ w_in, approximate=False)
    u, v = z[..., :GM_WIDTH], z[..., GM_WIDTH:]
    v = layer_norm(v, vn_w, vn_b, LN_EPS).reshape(b, t // CHUNK, CHUNK, GM_GROUPS, GM_GROUP_DIM)
    s = jnp.einsum('gij,bnjgc->bnigc', w_s, v) + b_s.T[:, :, None]
    return (u * s.reshape(b, t, GM_WIDTH)) @ w_out


def moe_ffn(h, router_w, router_b, w_gu, b_gu, w_down, b_down):
    n_tok, d = h.shape
    logits = jnp.matmul(h, router_w, preferred_element_type=jnp.float32) + router_b.astype(jnp.float32)
    top_val, top_idx = lax.top_k(logits, TOP_K)
    gates = jax.nn.softmax(top_val, axis=-1).astype(h.dtype)
    n_assign = n_tok * TOP_K
    e_flat = top_idx.reshape(-1)
    order = jnp.argsort(e_flat)
    e_sorted = e_flat[order]
    tok_sorted = (order // TOP_K).astype(jnp.int32)
    gate_sorted = gates.reshape(-1)[order]
    counts = jnp.zeros((N_EXPERTS,), jnp.int32).at[e_flat].add(1)
    padded = (counts + MOE_BLOCK - 1) // MOE_BLOCK * MOE_BLOCK
    start = jnp.cumsum(counts) - counts
    pend = jnp.cumsum(padded)
    pstart = pend - padded
    dest = pstart[e_sorted] + jnp.arange(n_assign, dtype=jnp.int32) - start[e_sorted]
    n_rows = -(-n_assign // MOE_BLOCK) * MOE_BLOCK + N_EXPERTS * MOE_BLOCK
    n_blocks = n_rows // MOE_BLOCK
    row_tok = jnp.full((n_rows,), n_tok, jnp.int32).at[dest].set(tok_sorted)
    row_gate = jnp.zeros((n_rows,), h.dtype).at[dest].set(gate_sorted)
    blk_start = jnp.arange(n_blocks, dtype=jnp.int32) * MOE_BLOCK
    blk_expert = jnp.minimum(jnp.searchsorted(pend, blk_start, side='right'), N_EXPERTS - 1)
    h_ext = jnp.concatenate([h, jnp.zeros((1, d), h.dtype)], axis=0)
    def expert_block(args):
        tok, e = args
        gu = h_ext[tok] @ w_gu[e] + b_gu[e]
        x_glu = jnp.minimum(gu[:, :EXPERT_FF], SWIGLU_LIMIT)
        x_lin = jnp.clip(gu[:, EXPERT_FF:], -SWIGLU_LIMIT, SWIGLU_LIMIT)
        act = x_glu * jax.nn.sigmoid(SWIGLU_ALPHA * x_glu) * (x_lin + 1)
        return act @ w_down[e] + b_down[e]
    y = lax.map(expert_block, (row_tok.reshape(n_blocks, MOE_BLOCK), blk_expert))
    y = y.reshape(n_rows, d) * row_gate[:, None]
    return jnp.zeros((n_tok + 1, d), h.dtype).at[row_tok].add(y)[:n_tok]


def setup_inputs(seed: int = 0) -> dict:
    key = jax.random.key(seed)
    ks = iter(jax.random.split(key, 48))
    def nrm(shape, scale):
        return jax.random.normal(next(ks), shape, jnp.float32) * scale
    def gain(shape):
        return 1.0 + nrm(shape, 0.02)
    def unif(shape, lo, hi):
        return jax.random.uniform(next(ks), shape, jnp.float32, lo, hi)
    d = D_MODEL
    return {
        'x': nrm((BATCH, SEQ, d), 1.0),
        'c': nrm((BATCH, d), 1.0),
        'ctx': nrm((BATCH, CTX_LEN, d), 1.0),
        'c_ctx': nrm((d,), 1.0),
        'ada_w': nrm((DEPTH, d, 6 * d), 0.5 * d ** -0.5),
        'ada_b': nrm((DEPTH, 6 * d), 0.02),
        'norm_mix_pre': gain((DEPTH, d)),
        'norm_mix_post': gain((DEPTH, d)),
        'norm_ffn_pre': gain((DEPTH, d)),
        'norm_ffn_post': gain((DEPTH, d)),
        'router_w': nrm((DEPTH, d, N_EXPERTS), d ** -0.5),
        'router_b': nrm((DEPTH, N_EXPERTS), 0.01),
        'moe_w_gu': nrm((DEPTH, N_EXPERTS, d, 2 * EXPERT_FF), d ** -0.5),
        'moe_b_gu': nrm((DEPTH, N_EXPERTS, 2 * EXPERT_FF), 0.02),
        'moe_w_down': nrm((DEPTH, N_EXPERTS, EXPERT_FF, d), EXPERT_FF ** -0.5),
        'moe_b_down': nrm((DEPTH, N_EXPERTS, d), 0.02),
        'hy_w_in': nrm((N_EVEN, d, IN_COLS), d ** -0.5),
        'mla_q_norm': gain((N_EVEN, MLA_Q_RANK)),
        'mla_w_uq': nrm((N_EVEN, MLA_Q_RANK, MLA_HEADS * (MLA_NOPE + MLA_ROPE)), MLA_Q_RANK ** -0.5),
        'mla_kv_norm': gain((N_EVEN, MLA_KV_RANK)),
        'mla_w_ukv': nrm((N_EVEN, MLA_KV_RANK, MLA_HEADS * (MLA_NOPE + MLA_V)), MLA_KV_RANK ** -0.5),
        'rwkv_mu_prev': unif((N_EVEN, RWKV_COLS), 0.1, 0.45),
        'rwkv_mu_next': unif((N_EVEN, RWKV_COLS), 0.1, 0.45),
        'rwkv_w0': unif((N_EVEN, 2, RWKV_DIM), -6.0, 0.0),
        'rwkv_w2': nrm((N_EVEN, 2, DECAY_LORA, RWKV_DIM), 0.1),
        'rwkv_a0': nrm((N_EVEN, 2, RWKV_DIM), 0.5),
        'rwkv_a2': nrm((N_EVEN, 2, AAA_LORA, RWKV_DIM), 0.1),
        'rwkv_g2': nrm((N_EVEN, GATE_LORA, RWKV_DIM), GATE_LORA ** -0.5),
        'rwkv_k_k': 0.85 + nrm((N_EVEN, RWKV_DIM), 0.05),
        'rwkv_k_a': gain((N_EVEN, RWKV_DIM)),
        'rwkv_r_k': nrm((N_EVEN, RWKV_HEADS, RWKV_HEAD), 0.1),
        'rwkv_ln_w': gain((N_EVEN, RWKV_DIM)),
        'rwkv_ln_b': nrm((N_EVEN, RWKV_DIM), 0.02),
        'hy_w_out': nrm((N_EVEN, MIX_WIDTH, d), MIX_WIDTH ** -0.5),
        'gm_w_in': nrm((N_ODD, d, 2 * GM_WIDTH), d ** -0.5),
        'gm_v_norm_w': gain((N_ODD, GM_WIDTH)),
        'gm_v_norm_b': nrm((N_ODD, GM_WIDTH), 0.02),
        'gm_w_s': nrm((N_ODD, GM_GROUPS, CHUNK, CHUNK), CHUNK ** -0.5),
        'gm_b_s': nrm((N_ODD, GM_GROUPS, CHUNK), 0.02),
        'gm_w_out': nrm((N_ODD, GM_WIDTH, d), GM_WIDTH ** -0.5),
    }


def reference(x, c, ctx, c_ctx, ada_w, ada_b, norm_mix_pre, norm_mix_post, norm_ffn_pre, norm_ffn_post,
              router_w, router_b, moe_w_gu, moe_b_gu, moe_w_down, moe_b_down,
              hy_w_in, mla_q_norm, mla_w_uq, mla_kv_norm, mla_w_ukv,
              rwkv_mu_prev, rwkv_mu_next, rwkv_w0, rwkv_w2, rwkv_a0, rwkv_a2, rwkv_g2,
              rwkv_k_k, rwkv_k_a, rwkv_r_k, rwkv_ln_w, rwkv_ln_b, hy_w_out,
              gm_w_in, gm_v_norm_w, gm_v_norm_b, gm_w_s, gm_b_s, gm_w_out):
    b, s, d = x.shape
    rope_cos, rope_sin = axial_rope_tables(s)
    silu_c = jax.nn.silu(c)
    silu_cc = jax.nn.silu(c_ctx)
    for l in range(DEPTH):
        ctx_out = any(j % 2 == 0 for j in range(l + 1, DEPTH))
        need_ctx = (l % 2 == 0) or ctx_out
        mod = (silu_c @ ada_w[l] + ada_b[l])[:, None, :]
        sh1, sc1, g1, sh2, sc2, g2 = jnp.split(mod, 6, axis=-1)
        h = modulate(rms_norm(x, norm_mix_pre[l]), sh1, sc1)
        if need_ctx:
            modc = silu_cc @ ada_w[l] + ada_b[l]
            sh1c, sc1c, g1c, sh2c, sc2c, g2c = jnp.split(modc, 6, axis=-1)
            hc = modulate(rms_norm(ctx, norm_mix_pre[l]), sh1c, sc1c)
        if l % 2 == 0:
            e = l // 2
            y, yc = hybrid_mixer(h, hc, rope_cos, rope_sin, hy_w_in[e], mla_q_norm[e], mla_w_uq[e],
                                 mla_kv_norm[e], mla_w_ukv[e], rwkv_mu_prev[e], rwkv_mu_next[e],
                                 rwkv_w0[e], rwkv_w2[e], rwkv_a0[e], rwkv_a2[e], rwkv_g2[e],
                                 rwkv_k_k[e], rwkv_k_a[e], rwkv_r_k[e], rwkv_ln_w[e], rwkv_ln_b[e],
                                 hy_w_out[e], ctx_out)
        else:
            o = l // 2
            y = chunk_gmlp(h, gm_w_in[o], gm_v_norm_w[o], gm_v_norm_b[o], gm_w_s[o], gm_b_s[o], gm_w_out[o])
            yc = None
            if ctx_out:
                yc = chunk_gmlp(hc, gm_w_in[o], gm_v_norm_w[o], gm_v_norm_b[o], gm_w_s[o], gm_b_s[o], gm_w_out[o])
        x = x + g1 * rms_norm(y, norm_mix_post[l])
        hf = modulate(rms_norm(x, norm_ffn_pre[l]), sh2, sc2).reshape(b * s, d)
        f = moe_ffn(hf, router_w[l], router_b[l], moe_w_gu[l], moe_b_gu[l], moe_w_down[l], moe_b_down[l])
        x = x + g2 * rms_norm(f.reshape(b, s, d), norm_ffn_post[l])
        if ctx_out:
            ctx = ctx + g1c * rms_norm(yc, norm_mix_post[l])
            lc = ctx.shape[1]
            hfc = modulate(rms_norm(ctx, norm_ffn_pre[l]), sh2c, sc2c).reshape(b * lc, d)
            fc = moe_ffn(hfc, router_w[l], router_b[l], moe_w_gu[l], moe_b_gu[l], moe_w_down[l], moe_b_down[l])
            ctx = ctx + g2c * rms_norm(fc.reshape(b, lc, d), norm_ffn_post[l])
    return x
```

```python
import functools

import jax
import jax.numpy as jnp
import numpy as np
from jax import lax
from jax.experimental import pallas as pl
from jax.experimental.pallas import tpu as pltpu

F32 = jnp.float32
BF16 = jnp.bfloat16
HIGHEST = lax.Precision.HIGHEST

D_MODEL = 1024
GRID_W = 64
EPS = 1e-6

MLA_HEADS = 8
MLA_NOPE = 64
MLA_ROPE = 32
MLA_V = 64
MLA_Q_RANK = 256
MLA_KV_RANK = 128
MLA_SCALE = (MLA_NOPE + MLA_ROPE) ** -0.5
ROPE_FREQS = MLA_ROPE // 4
ROPE_BASE = 10000.0
HEAD_PAD = 128

RWKV_HEAD = 64
RWKV_DIM = 512
RWKV_HEADS = 8
DECAY_LORA = 32
AAA_LORA = 32
GATE_LORA = 96
RWKV_PAD_COLS = 3 * RWKV_DIM + 128 + 128
RWKV_LN_EPS = 64e-5
WKV_CHUNK = 64

CHUNK = 128
GM_WIDTH = 1024
GM_GROUPS = 8
LN_EPS = 1e-5

N_EXPERTS = 32
TOP_K = 4
EXPERT_FF = 1024
SWIGLU_LIMIT = 7.0
SWIGLU_ALPHA = 1.702
LOGIT_PAD = 128
NEG_BIG = -1e30

ROW_TILE = 256
VMEM_LIMIT = 48 * 1024 * 1024


def _cparams(sem):
    return pltpu.CompilerParams(dimension_semantics=sem, vmem_limit_bytes=VMEM_LIMIT)


def _rms(x):
    return x * lax.rsqrt(jnp.mean(x * x, axis=-1, keepdims=True) + EPS)


def _dot(a, b, precision=None):
    return jnp.dot(a, b, preferred_element_type=F32, precision=precision)


def _dot_nt(a, b, precision=None):
    return lax.dot_general(a, b, (((1,), (1,)), ((), ())), preferred_element_type=F32, precision=precision)


def _dot_tn(a, b, precision=None):
    return lax.dot_general(a, b, (((0,), (0,)), ((), ())), preferred_element_type=F32, precision=precision)


def _mod_kernel(s_ref, w_ref, b_ref, o_ref):
    s = s_ref[...]
    s = s * jax.nn.sigmoid(s)
    o_ref[...] = _dot(s, w_ref[...], HIGHEST) + b_ref[...]


def _modulation(cond_rows, ada_w, ada_b):
    n_l, d, n6 = ada_w.shape
    tn = 1536
    return pl.pallas_call(
        _mod_kernel,
        grid=(n_l, n6 // tn),
        in_specs=[pl.BlockSpec((8, d), lambda l, j: (0, 0)),
                  pl.BlockSpec((None, d, tn), lambda l, j: (l, 0, j)),
                  pl.BlockSpec((None, 1, tn), lambda l, j: (l, 0, j))],
        out_specs=pl.BlockSpec((None, 8, tn), lambda l, j: (l, 0, j)),
        out_shape=jax.ShapeDtypeStruct((n_l, 8, n6), F32),
        compiler_params=_cparams(("arbitrary", "arbitrary")),
        name="adaln_mod",
    )(cond_rows, ada_w, ada_b.reshape(n_l, 1, n6))


def _inproj_kernel(x_ref, g_ref, sh_ref, sc_ref, cos_ref, sin_ref, w1_ref, qn_ref, kvn_ref, wq_ref, wk_ref,
                   q_ref, k_ref, v_ref, prw_ref):
    h = _rms(x_ref[...]) * g_ref[...]
    h = h * (1.0 + sc_ref[...]) + sh_ref[...]
    p = _dot(h.astype(BF16), w1_ref[...])
    cos = jnp.concatenate([cos_ref[...]] * MLA_HEADS, axis=1)
    sin = jnp.concatenate([sin_ref[...]] * MLA_HEADS, axis=1)
    nq = MLA_HEADS * HEAD_PAD
    qn = _rms(p[:, :MLA_Q_RANK]) * qn_ref[...]
    qq = _dot(qn.astype(BF16), wq_ref[...])
    q_ref[...] = (qq[:, :nq] * cos + qq[:, nq:] * sin).astype(BF16)
    kvn = _rms(p[:, MLA_Q_RANK:MLA_Q_RANK + MLA_KV_RANK]) * kvn_ref[...]
    x2 = jnp.concatenate([kvn, p[:, MLA_Q_RANK + MLA_KV_RANK:512]], axis=1).astype(BF16)
    kk = _dot(x2, wk_ref[...])
    k_ref[...] = (kk[:, :nq] * cos + kk[:, nq:2 * nq] * sin).astype(BF16)
    v_ref[...] = kk[:, 2 * nq:].astype(BF16)
    prw_ref[...] = p[:, 512:]


def _rotate_half_cols(w):
    wr = w.reshape(w.shape[:-1] + (2, 2, ROPE_FREQS))
    return jnp.stack([-wr[..., 1, :], wr[..., 0, :]], axis=-2).reshape(w.shape)


def _inproj_weights(hy_w_in, w_uq, w_ukv):
    d = hy_w_in.shape[0]
    mla_cols = MLA_Q_RANK + MLA_KV_RANK + MLA_ROPE
    w_mla = hy_w_in[:, :mla_cols]
    w_rw = hy_w_in[:, mla_cols:]
    w1 = jnp.concatenate([w_mla, jnp.zeros((d, 512 - mla_cols), F32), _pad_rwkv_cols(w_rw)], axis=1).astype(BF16)
    wq = (w_uq * MLA_SCALE).reshape(MLA_Q_RANK, MLA_HEADS, MLA_NOPE + MLA_ROPE)
    z_n = jnp.zeros((MLA_Q_RANK, MLA_HEADS, MLA_NOPE), F32)
    z_p = jnp.zeros((MLA_Q_RANK, MLA_HEADS, HEAD_PAD - MLA_NOPE - MLA_ROPE), F32)
    wq_plain = jnp.concatenate([wq[..., :MLA_NOPE], wq[..., MLA_NOPE:], z_p], axis=-1)
    wq_rot = jnp.concatenate([z_n, _rotate_half_cols(wq[..., MLA_NOPE:]), z_p], axis=-1)
    wq2 = jnp.concatenate([wq_plain.reshape(MLA_Q_RANK, -1), wq_rot.reshape(MLA_Q_RANK, -1)], axis=1).astype(BF16)
    wkv = w_ukv.reshape(MLA_KV_RANK, MLA_HEADS, MLA_NOPE + MLA_V)
    eye = jnp.broadcast_to(jnp.eye(MLA_ROPE, dtype=F32)[:, None, :], (MLA_ROPE, MLA_HEADS, MLA_ROPE))
    pad_k = HEAD_PAD - MLA_NOPE - MLA_ROPE
    top_plain = jnp.concatenate([wkv[..., :MLA_NOPE], jnp.zeros((MLA_KV_RANK, MLA_HEADS, MLA_ROPE + pad_k), F32)], -1)
    mid_plain = jnp.concatenate([jnp.zeros((MLA_ROPE, MLA_HEADS, MLA_NOPE), F32), eye,
                                 jnp.zeros((MLA_ROPE, MLA_HEADS, pad_k), F32)], -1)
    mid_rot = jnp.concatenate([jnp.zeros((MLA_ROPE, MLA_HEADS, MLA_NOPE), F32), _rotate_half_cols(eye),
                               jnp.zeros((MLA_ROPE, MLA_HEADS, pad_k), F32)], -1)
    nq = MLA_HEADS * HEAD_PAD
    rows_c = jnp.concatenate([top_plain.reshape(MLA_KV_RANK, nq), jnp.zeros((MLA_KV_RANK, nq), F32),
                              wkv[..., MLA_NOPE:].reshape(MLA_KV_RANK, MLA_HEADS * MLA_V)], axis=1)
    rows_r = jnp.concatenate([mid_plain.reshape(MLA_ROPE, nq), mid_rot.reshape(MLA_ROPE, nq),
                              jnp.zeros((MLA_ROPE, MLA_HEADS * MLA_V), F32)], axis=1)
    rows_z = jnp.zeros((256 - MLA_KV_RANK - MLA_ROPE, rows_c.shape[1]), F32)
    wk2 = jnp.concatenate([rows_c, rows_r, rows_z], axis=0).astype(BF16)
    return w1, wq2, wk2


def _pad_rwkv_cols(w):
    pad = jnp.zeros(w.shape[:-1] + (RWKV_PAD_COLS - w.shape[-1],), w.dtype)
    return jnp.concatenate([w, pad], axis=-1)


def _rope_tables(n_lat, n_ctx):
    t = jnp.arange(n_lat)
    row = (t // GRID_W).astype(F32)
    col = (t % GRID_W).astype(F32)
    inv = ROPE_BASE ** (-jnp.arange(ROPE_FREQS, dtype=F32) / ROPE_FREQS)
    ang = jnp.stack([row[:, None] * inv, col[:, None] * inv], axis=1)
    ang = jnp.broadcast_to(ang[:, :, None, :], (n_lat, 2, 2, ROPE_FREQS)).reshape(n_lat, MLA_ROPE)
    cos = jnp.concatenate([jnp.ones((n_lat, MLA_NOPE), F32), jnp.cos(ang),
                           jnp.ones((n_lat, HEAD_PAD - MLA_NOPE - MLA_ROPE), F32)], axis=1)
    sin = jnp.concatenate([jnp.zeros((n_lat, MLA_NOPE), F32), jnp.sin(ang),
                           jnp.zeros((n_lat, HEAD_PAD - MLA_NOPE - MLA_ROPE), F32)], axis=1)
    cos = jnp.concatenate([cos, jnp.ones((n_ctx, HEAD_PAD), F32)], axis=0)
    sin = jnp.concatenate([sin, jnp.zeros((n_ctx, HEAD_PAD), F32)], axis=0)
    return cos, sin


def _inproj(x_all, n_lat, gain, shift, scale, cos, sin, w1, q_norm, kv_norm, wq2, wk2):
    b, t_all, d = x_all.shape
    tm = ROW_TILE
    n_lat_tiles = n_lat // tm
    nq = MLA_HEADS * HEAD_PAD

    def mod_map(bi, i):
        return (2 * bi + jnp.where(i >= n_lat_tiles, 1, 0), 0, 0)

    const = lambda bi, i: (0, 0)
    row = lambda bi, i: (bi, i, 0)
    return pl.pallas_call(
        _inproj_kernel,
        grid=(b, t_all // tm),
        in_specs=[pl.BlockSpec((None, tm, d), row),
                  pl.BlockSpec((1, d), const),
                  pl.BlockSpec((None, 1, d), mod_map),
                  pl.BlockSpec((None, 1, d), mod_map),
                  pl.BlockSpec((tm, HEAD_PAD), lambda bi, i: (i, 0)),
                  pl.BlockSpec((tm, HEAD_PAD), lambda bi, i: (i, 0)),
                  pl.BlockSpec(w1.shape, const),
                  pl.BlockSpec((1, MLA_Q_RANK), const),
                  pl.BlockSpec((1, MLA_KV_RANK), const),
                  pl.BlockSpec(wq2.shape, const),
                  pl.BlockSpec(wk2.shape, const)],
        out_specs=[pl.BlockSpec((None, tm, nq), row),
                   pl.BlockSpec((None, tm, nq), row),
                   pl.BlockSpec((None, tm, MLA_HEADS * MLA_V), row),
                   pl.BlockSpec((None, tm, RWKV_PAD_COLS), row)],
        out_shape=[jax.ShapeDtypeStruct((b, t_all, nq), BF16),
                   jax.ShapeDtypeStruct((b, t_all, nq), BF16),
                   jax.ShapeDtypeStruct((b, t_all, MLA_HEADS * MLA_V), BF16),
                   jax.ShapeDtypeStruct((b, t_all, RWKV_PAD_COLS), F32)],
        compiler_params=_cparams(("parallel", "parallel")),
        name="hybrid_inproj",
    )(x_all, gain.reshape(1, d), shift, scale, cos, sin, w1, q_norm.reshape(1, -1), kv_norm.reshape(1, -1), wq2, wk2)


def _attn_kernel(q_ref, k_ref, v_ref, o_ref, m_sc, l_sc, acc_sc):
    ki = pl.program_id(3)

    @pl.when(ki == 0)
    def _():
        m_sc[...] = jnp.full(m_sc.shape, -jnp.inf, F32)
        l_sc[...] = jnp.zeros(l_sc.shape, F32)
        acc_sc[...] = jnp.zeros(acc_sc.shape, F32)

    v = v_ref[...]
    for h in range(2):
        q = q_ref[:, h * HEAD_PAD:(h + 1) * HEAD_PAD]
        k = k_ref[:, h * HEAD_PAD:(h + 1) * HEAD_PAD]
        s = _dot_nt(q, k)
        m_prev = m_sc[h]
        m_new = jnp.maximum(m_prev, jnp.max(s, axis=1, keepdims=True))
        alpha = jnp.exp(m_prev - m_new)
        p = jnp.exp(s - m_new[:, :1])
        l_sc[h] = alpha * l_sc[h] + jnp.sum(p, axis=1, keepdims=True)
        acc_sc[h] = acc_sc[h] * alpha + _dot(p.astype(BF16), v)
        m_sc[h] = m_new

    @pl.when(ki == pl.num_programs(3) - 1)
    def _():
        o0 = acc_sc[0] / l_sc[0]
        o1 = acc_sc[1] / l_sc[1]
        lane = lax.broadcasted_iota(jnp.int32, o0.shape, 1)
        o_ref[...] = jnp.where(lane < MLA_V, o0, o1).astype(BF16)


def _attention(q, k, v, n_lat, tq, tk):
    b, t_all, _ = k.shape
    return pl.pallas_call(
        _attn_kernel,
        grid=(b, MLA_HEADS // 2, n_lat // tq, t_all // tk),
        in_specs=[pl.BlockSpec((None, tq, 2 * HEAD_PAD), lambda bi, hp, qi, ki: (bi, qi, hp)),
                  pl.BlockSpec((None, tk, 2 * HEAD_PAD), lambda bi, hp, qi, ki: (bi, ki, hp)),
                  pl.BlockSpec((None, tk, 2 * MLA_V), lambda bi, hp, qi, ki: (bi, ki, hp))],
        out_specs=pl.BlockSpec((None, tq, 2 * MLA_V), lambda bi, hp, qi, ki: (bi, qi, hp)),
        out_shape=jax.ShapeDtypeStruct((b, n_lat, MLA_HEADS * MLA_V), BF16),
        scratch_shapes=[pltpu.VMEM((2, tq, 128), F32), pltpu.VMEM((2, tq, 128), F32), pltpu.VMEM((2, tq, 128), F32)],
        compiler_params=_cparams(("parallel", "parallel", "parallel", "arbitrary")),
        name="mla_attention",
    )(q, k, v)


def _seg_sum(x, bd_ref):
    hi = x.astype(BF16)
    lo = (x - hi.astype(F32)).astype(BF16)
    bd = bd_ref[...]
    return _dot(hi, bd) + _dot(lo, bd)


def _rwkv_prep_kernel(p_ref, prev_ref, next_ref, mup_ref, mun_ref, wl_ref, w0a0_ref, g2_ref, kk_ref, ka_ref, bd_ref,
                      r_ref, k_ref, v_ref, g_ref, kkn_ref, lw_ref, kd_ref, bdir_ref, *, n_lat_tiles, n_tiles):
    i = pl.program_id(1)
    p = p_ref[...]
    tm = p.shape[0]
    first = jnp.logical_or(i == 0, i == n_lat_tiles)
    last = jnp.logical_or(i == n_lat_tiles - 1, i == n_tiles - 1)
    prev_row = jnp.where(first, 0.0, prev_ref[7:8, :])
    next_row = jnp.where(last, 0.0, next_ref[0:1, :])
    ridx = lax.broadcasted_iota(jnp.int32, p.shape, 0)
    prev = jnp.where(ridx == 0, prev_row, pltpu.roll(p, 1, 0))
    nxt = jnp.where(ridx == tm - 1, next_row, pltpu.roll(p, tm - 1, 0))
    p = p + mup_ref[...] * (prev - p) + mun_ref[...] * (nxt - p)
    c = RWKV_DIM
    r, k, v = p[:, :c], p[:, c:2 * c], p[:, 2 * c:3 * c]
    lo = p[:, 3 * c:3 * c + 128]
    lane = lax.broadcasted_iota(jnp.int32, lo.shape, 1)
    lo = jnp.where(lane < 2 * DECAY_LORA, jnp.tanh(lo), lo)
    wa = _dot(lo.astype(BF16), wl_ref[...]) + w0a0_ref[...]
    gd = p[:, 3 * c + 128:]
    g_ref[...] = _dot(jax.nn.sigmoid(gd).astype(BF16), g2_ref[...])
    kk = k * kk_ref[...]
    kk = kk * lax.rsqrt(jnp.maximum(_seg_sum(kk * kk, bd_ref), 1e-24))
    r_ref[...] = r
    k_ref[...] = k
    v_ref[...] = v
    kkn_ref[...] = kk
    ka = ka_ref[...]
    for d in range(2):
        w = wa[:, d * c:(d + 1) * c]
        a = jax.nn.sigmoid(wa[:, (2 + d) * c:(3 + d) * c])
        lw_ref[:, d * c:(d + 1) * c] = -float(np.exp(-0.5)) * jax.nn.sigmoid(w)
        kd_ref[:, d * c:(d + 1) * c] = k * (1.0 + (a - 1.0) * ka)
        bdir_ref[:, d * c:(d + 1) * c] = kk * a


def _rwkv_prepare(prw, n_lat, mu_prev, mu_next, w0, w2, a0, a2, g2, k_k, k_a, bd_ones):
    b, t_all, pc = prw.shape
    tm = ROW_TILE
    n_tiles = t_all // tm
    n_lat_tiles = n_lat // tm
    c = RWKV_DIM
    z = jnp.zeros((DECAY_LORA, c), F32)
    wl = jnp.concatenate([
        jnp.concatenate([w2[0], z, z, z], axis=1), jnp.concatenate([z, w2[1], z, z], axis=1),
        jnp.concatenate([z, z, a2[0], z], axis=1), jnp.concatenate([z, z, z, a2[1]], axis=1)], axis=0).astype(BF16)
    w0a0 = jnp.concatenate([w0[0], w0[1], a0[0], a0[1]]).reshape(1, 4 * c)
    g2p = jnp.concatenate([g2, jnp.zeros((128 - GATE_LORA, c), F32)], axis=0).astype(BF16)
    row = lambda bi, i: (bi, i, 0)
    const = lambda bi, i: (0, 0)
    hb = tm // 8
    n_hb = t_all // 8
    kern = functools.partial(_rwkv_prep_kernel, n_lat_tiles=n_lat_tiles, n_tiles=n_tiles)
    o_c = jax.ShapeDtypeStruct((b, t_all, c), F32)
    o_2c = jax.ShapeDtypeStruct((b, t_all, 2 * c), F32)
    return pl.pallas_call(
        kern,
        grid=(b, n_tiles),
        in_specs=[pl.BlockSpec((None, tm, pc), row),
                  pl.BlockSpec((None, 8, pc), lambda bi, i: (bi, jnp.maximum(i * hb - 1, 0), 0)),
                  pl.BlockSpec((None, 8, pc), lambda bi, i: (bi, jnp.minimum((i + 1) * hb, n_hb - 1), 0)),
                  pl.BlockSpec((1, pc), const), pl.BlockSpec((1, pc), const),
                  pl.BlockSpec(wl.shape, const), pl.BlockSpec((1, 4 * c), const),
                  pl.BlockSpec(g2p.shape, const), pl.BlockSpec((1, c), const), pl.BlockSpec((1, c), const),
                  pl.BlockSpec((c, c), const)],
        out_specs=[pl.BlockSpec((None, tm, c), row)] * 5 + [pl.BlockSpec((None, tm, 2 * c), row)] * 3,
        out_shape=[o_c] * 5 + [o_2c] * 3,
        compiler_params=_cparams(("parallel", "parallel")),
        name="rwkv_prepare",
    )(prw, prw, prw, _pad_rwkv_cols(mu_prev).reshape(1, pc), _pad_rwkv_cols(mu_next).reshape(1, pc),
      wl, w0a0, g2p, k_k.reshape(1, c), k_a.reshape(1, c), bd_ones)


def _wkv_chunk_kernel(r_ref, v_ref, kk_ref, lw_ref, kd_ref, bd_ref, m_ref, n_ref, rr_ref, yv_ref):
    d = pl.program_id(1)
    cs = WKV_CHUNK
    ti = lax.broadcasted_iota(jnp.int32, (cs, cs), 0)
    si = lax.broadcasted_iota(jnp.int32, (cs, cs), 1)
    rel = (si - ti) * (1 - 2 * d)
    incl = rel <= 0
    strict = rel < 0
    eye = si == ti
    lw = lw_ref[...]
    g = _dot(incl.astype(F32), lw, HIGHEST)
    total = jnp.sum(lw, axis=0, keepdims=True)
    e_in = jnp.exp(g)
    e_ex = jnp.exp(g - lw)
    e_inv = jnp.exp(-g)
    e_end = jnp.exp(total - g)
    gam = jnp.exp(total)
    kd = kd_ref[...]
    bd = bd_ref[...]
    at = -kk_ref[...] * e_ex
    rt = r_ref[...] * e_in
    kt = kd * e_inv
    bt = bd * e_inv
    ke = kd * e_end
    be = bd * e_end
    v = v_ref[...]
    outs = [[], [], [], []]
    for h in range(RWKV_HEADS):
        sl = slice(h * RWKV_HEAD, (h + 1) * RWKV_HEAD)
        a_h, r_h, k_h, b_h, v_h = at[:, sl], rt[:, sl], kt[:, sl], bt[:, sl], v[:, sl]
        aa = _dot_nt(jnp.concatenate([a_h, r_h], axis=0), jnp.concatenate([b_h, k_h], axis=0))
        a_ab = jnp.where(strict, aa[:cs, :cs], 0.0)
        a_ak = jnp.where(strict, aa[:cs, cs:], 0.0)
        a_rb = jnp.where(incl, aa[cs:, :cs], 0.0)
        a_rk = jnp.where(incl, aa[cs:, cs:], 0.0)
        z = jnp.concatenate([a_h, _dot(a_ak, v_h)], axis=1)
        pw = a_ab
        n_sq = int(np.log2(cs))
        for it in range(n_sq):
            z = z + _dot(pw, z, HIGHEST)
            if it + 1 < n_sq:
                pw = _dot(pw, pw, HIGHEST)
        w2 = _dot(a_rb, z)
        w3 = _dot_tn(be[:, sl], z)
        outs[0].append(jnp.where(eye, gam[:, sl], 0.0) + w3[:, :cs])
        outs[1].append(_dot_tn(ke[:, sl], v_h) + w3[:, cs:])
        outs[2].append(r_h + w2[:, :cs])
        outs[3].append(_dot(a_rk, v_h) + w2[:, cs:])
    m_ref[...] = jnp.concatenate(outs[0], axis=1)
    n_ref[...] = jnp.concatenate(outs[1], axis=1)
    rr_ref[...] = jnp.concatenate(outs[2], axis=1)
    yv_ref[...] = jnp.concatenate(outs[3], axis=1)


def _wkv_chunks(r, v, kkn, lw, kd, bdir):
    b, t_all, c = r.shape
    cs = WKV_CHUNK
    nch = t_all // cs
    shared = pl.BlockSpec((None, cs, c), lambda bi, d, ci: (bi, ci, 0))
    per_dir = pl.BlockSpec((None, cs, c), lambda bi, d, ci: (bi, ci, d))
    out = pl.BlockSpec((None, None, None, cs, c), lambda bi, d, ci: (bi, d, ci, 0, 0))
    o_s = jax.ShapeDtypeStruct((b, 2, nch, cs, c), F32)
    return pl.pallas_call(
        _wkv_chunk_kernel,
        grid=(b, 2, nch),
        in_specs=[shared, shared, shared, per_dir, per_dir, per_dir],
        out_specs=[out] * 4,
        out_shape=[o_s] * 4,
        compiler_params=_cparams(("parallel", "parallel", "parallel")),
        name="wkv_chunk_summaries",
    )(r, v, kkn, lw, kd, bdir)


def _wkv_state_kernel(order_ref, mf_ref, nf_ref, rf_ref, yf_ref, mb_ref, nb_ref, rb_ref, yb_ref,
                      of_ref, ob_ref, st_sc):
    del order_ref
    s = pl.program_id(0)

    @pl.when(s == 0)
    def _():
        st_sc[...] = jnp.zeros(st_sc.shape, F32)

    n_b = st_sc.shape[0]
    ins = ((mf_ref, nf_ref, rf_ref, yf_ref, of_ref), (mb_ref, nb_ref, rb_ref, yb_ref, ob_ref))
    for bi in range(n_b):
        for d in range(2):
            m_r, n_r, r_r, y_r, o_r = ins[d]
            st = st_sc[bi, d]
            m, n, rr, yv = m_r[bi], n_r[bi], r_r[bi], y_r[bi]
            ys, sts = [], []
            for h in range(RWKV_HEADS):
                sl = slice(h * RWKV_HEAD, (h + 1) * RWKV_HEAD)
                st_h = st[:, sl]
                ys.append(_dot(rr[:, sl], st_h) + yv[:, sl])
                sts.append(_dot(m[:, sl], st_h, HIGHEST) + n[:, sl])
            o_r[bi] = jnp.concatenate(ys, axis=1)
            st_sc[bi, d] = jnp.concatenate(sts, axis=1)


def _wkv_states(order, m, n, rr, yv, t_all):
    b, _, nch, cs, c = m.shape
    fwd = pl.BlockSpec((b, None, None, cs, c), lambda s, o: (0, 0, o[0, s], 0, 0))
    bwd = pl.BlockSpec((b, None, None, cs, c), lambda s, o: (0, 1, o[1, s], 0, 0))
    out_f = pl.BlockSpec((b, cs, c), lambda s, o: (0, o[0, s], 0))
    out_b = pl.BlockSpec((b, cs, c), lambda s, o: (0, o[1, s], 0))
    o_s = jax.ShapeDtypeStruct((b, t_all, c), F32)
    return pl.pallas_call(
        _wkv_state_kernel,
        grid_spec=pltpu.PrefetchScalarGridSpec(
            num_scalar_prefetch=1,
            grid=(nch,),
            in_specs=[fwd] * 4 + [bwd] * 4,
            out_specs=[out_f, out_b],
            scratch_shapes=[pltpu.VMEM((b, 2, cs, c), F32)]),
        out_shape=[o_s, o_s],
        compiler_params=_cparams(("arbitrary",)),
        name="wkv_state_pass",
    )(order, m, n, rr, yv, m, n, rr, yv)


def _ffn_pre(x1, gain_ref, sh_ref, sc_ref, rw_ref, rb_ref, hf_ref, idx_ref, gate_ref):
    hf = _rms(x1) * gain_ref[...]
    hf = hf * (1.0 + sc_ref[...]) + sh_ref[...]
    hf_ref[...] = hf.astype(BF16)
    logits = _dot(hf, rw_ref[...], HIGHEST) + rb_ref[...]
    lane = lax.broadcasted_iota(jnp.int32, logits.shape, 1)
    lane_f = lane.astype(F32)
    work = logits
    idx_out = jnp.zeros(logits.shape, F32)
    val_out = jnp.full(logits.shape, NEG_BIG, F32)
    for kth in range(TOP_K):
        m = jnp.max(work, axis=1, keepdims=True)
        idx = jnp.min(jnp.where(work == m, lane_f, float(LOGIT_PAD)), axis=1, keepdims=True)
        work = jnp.where(lane_f == idx, -jnp.inf, work)
        idx_out = jnp.where(lane == kth, idx, idx_out)
        val_out = jnp.where(lane == kth, m, val_out)
    e = jnp.exp(val_out - jnp.max(val_out, axis=1, keepdims=True))
    e = jnp.where(lane < TOP_K, e, 0.0)
    idx_ref[...] = idx_out.astype(jnp.int32)
    gate_ref[...] = e / jnp.sum(e, axis=1, keepdims=True)


def _router_pads(router_w, router_b):
    d = router_w.shape[0]
    rw = jnp.concatenate([router_w, jnp.zeros((d, LOGIT_PAD - N_EXPERTS), F32)], axis=1)
    rb = jnp.concatenate([router_b, jnp.full((LOGIT_PAD - N_EXPERTS,), NEG_BIG, F32)]).reshape(1, LOGIT_PAD)
    return rw, rb


def _mix_out_kernel(yf_ref, yb_ref, r_ref, k_ref, v_ref, g_ref, o_ref, x_ref,
                    rk_ref, lnw_ref, lnb_ref, bdm_ref, bds_ref, wout_ref, npost_ref, g1_ref,
                    gpre_ref, sh_ref, sc_ref, rw_ref, rb_ref,
                    x1_ref, hf_ref, idx_ref, gate_ref):
    y = yf_ref[...] + yb_ref[...]
    mu = _seg_sum(y, bdm_ref)
    dlt = y - mu
    var = _seg_sum(dlt * dlt, bdm_ref)
    yn = dlt * lax.rsqrt(var + RWKV_LN_EPS) * lnw_ref[...] + lnb_ref[...]
    bonus = _seg_sum(r_ref[...] * k_ref[...] * rk_ref[...], bds_ref) * v_ref[...]
    rw = (yn + bonus) * g_ref[...]
    mix_in = jnp.concatenate([o_ref[...], rw.astype(BF16)], axis=1)
    mix = _dot(mix_in, wout_ref[...])
    x1 = x_ref[...] + g1_ref[...] * (_rms(mix) * npost_ref[...])
    x1_ref[...] = x1
    _ffn_pre(x1, gpre_ref, sh_ref, sc_ref, rw_ref, rb_ref, hf_ref, idx_ref, gate_ref)


def _token_out_specs(tm, d):
    row = lambda bi, i: (bi, i, 0)
    specs = [pl.BlockSpec((None, tm, d), row), pl.BlockSpec((None, tm, d), row),
             pl.BlockSpec((None, tm, LOGIT_PAD), row), pl.BlockSpec((None, tm, LOGIT_PAD), row)]
    return specs


def _token_out_shapes(b, s, d):
    return [jax.ShapeDtypeStruct((b, s, d), F32), jax.ShapeDtypeStruct((b, s, d), BF16),
            jax.ShapeDtypeStruct((b, s, LOGIT_PAD), jnp.int32), jax.ShapeDtypeStruct((b, s, LOGIT_PAD), F32)]


def _mix_out(yf, yb, r, k, v, g, o_attn, x, r_k, ln_w, ln_b, bd_mean, bd_ones, w_out, n_post, g1,
             n_pre, sh2, sc2, rw_pad, rb_pad):
    b, s, d = x.shape
    c = RWKV_DIM
    tm = ROW_TILE
    row = lambda bi, i: (bi, i, 0)
    const = lambda bi, i: (0, 0)
    per_b = lambda bi, i: (bi, 0, 0)
    rc = pl.BlockSpec((None, tm, c), row)
    vec_c = pl.BlockSpec((1, c), const)
    vec_d = pl.BlockSpec((1, d), const)
    mod_d = pl.BlockSpec((None, 1, d), per_b)
    return pl.pallas_call(
        _mix_out_kernel,
        grid=(b, s // tm),
        in_specs=[rc] * 7 + [pl.BlockSpec((None, tm, d), row),
                             vec_c, vec_c, vec_c, pl.BlockSpec((c, c), const), pl.BlockSpec((c, c), const),
                             pl.BlockSpec((d, d), const), vec_d, mod_d,
                             vec_d, mod_d, mod_d, pl.BlockSpec((d, LOGIT_PAD), const),
                             pl.BlockSpec((1, LOGIT_PAD), const)],
        out_specs=_token_out_specs(tm, d),
        out_shape=_token_out_shapes(b, s, d),
        compiler_params=_cparams(("parallel", "parallel")),
        name="mixer_out_router",
    )(yf, yb, r, k, v, g, o_attn, x, r_k.reshape(1, c), ln_w.reshape(1, c), ln_b.reshape(1, c), bd_mean, bd_ones,
      w_out, n_post.reshape(1, d), g1, n_pre.reshape(1, d), sh2, sc2, rw_pad, rb_pad)


def _gmlp_kernel(x_ref, gpre1_ref, sh1_ref, sc1_ref, win_ref, vnw_ref, vnb_ref, ws_ref, bs_ref, wout_ref,
                 npost_ref, g1_ref, gpre_ref, sh_ref, sc_ref, rw_ref, rb_ref,
                 x1_ref, hf_ref, idx_ref, gate_ref):
    x = x_ref[...]
    h = _rms(x) * gpre1_ref[...]
    h = h * (1.0 + sc1_ref[...]) + sh1_ref[...]
    z = _dot(h.astype(BF16), win_ref[...])
    z = 0.5 * z * (1.0 + lax.erf(z * float(2.0 ** -0.5)))
    u, v = z[:, :GM_WIDTH], z[:, GM_WIDTH:]
    mu = jnp.mean(v, axis=-1, keepdims=True)
    dv = v - mu
    var = jnp.mean(dv * dv, axis=-1, keepdims=True)
    v = (dv * lax.rsqrt(var + LN_EPS) * vnw_ref[...] + vnb_ref[...]).astype(BF16)
    gw = GM_WIDTH // GM_GROUPS
    rows = []
    for ci in range(x.shape[0] // CHUNK):
        cols = []
        for gi in range(GM_GROUPS):
            cols.append(_dot(ws_ref[gi], v[ci * CHUNK:(ci + 1) * CHUNK, gi * gw:(gi + 1) * gw]))
        rows.append(jnp.concatenate(cols, axis=1) + bs_ref[...])
    sp = jnp.concatenate(rows, axis=0)
    y = _dot((u * sp).astype(BF16), wout_ref[...])
    x1 = x + g1_ref[...] * (_rms(y) * npost_ref[...])
    x1_ref[...] = x1
    _ffn_pre(x1, gpre_ref, sh_ref, sc_ref, rw_ref, rb_ref, hf_ref, idx_ref, gate_ref)


def _gmlp(x, n_pre1, sh1, sc1, w_in, vn_w, vn_b, w_s, b_s, w_out, n_post, g1, n_pre, sh2, sc2, rw_pad, rb_pad):
    b, s, d = x.shape
    tm = ROW_TILE
    gw = GM_WIDTH // GM_GROUPS
    bs_full = jnp.repeat(b_s.T, gw, axis=1)
    row = lambda bi, i: (bi, i, 0)
    const = lambda bi, i: (0, 0)
    per_b = lambda bi, i: (bi, 0, 0)
    vec_d = pl.BlockSpec((1, d), const)
    vec_g = pl.BlockSpec((1, GM_WIDTH), const)
    mod_d = pl.BlockSpec((None, 1, d), per_b)
    return pl.pallas_call(
        _gmlp_kernel,
        grid=(b, s // tm),
        in_specs=[pl.BlockSpec((None, tm, d), row), vec_d, mod_d, mod_d,
                  pl.BlockSpec((d, 2 * GM_WIDTH), const), vec_g, vec_g,
                  pl.BlockSpec((GM_GROUPS, CHUNK, CHUNK), lambda bi, i: (0, 0, 0)),
                  pl.BlockSpec((CHUNK, GM_WIDTH), const), pl.BlockSpec((GM_WIDTH, d), const),
                  vec_d, mod_d, vec_d, mod_d, mod_d,
                  pl.BlockSpec((d, LOGIT_PAD), const), pl.BlockSpec((1, LOGIT_PAD), const)],
        out_specs=_token_out_specs(tm, d),
        out_shape=_token_out_shapes(b, s, d),
        compiler_params=_cparams(("parallel", "parallel")),
        name="gmlp_router",
    )(x, n_pre1.reshape(1, d), sh1, sc1, w_in.astype(BF16), vn_w.reshape(1, -1), vn_b.reshape(1, -1),
      w_s.astype(BF16), bs_full, w_out.astype(BF16), n_post.reshape(1, d), g1, n_pre.reshape(1, d), sh2, sc2,
      rw_pad, rb_pad)


MOE_ROWS = 256


def _expert_kernel(be_ref, na_ref, x_ref, wgu_ref, bgu_ref, wd_ref, bd_ref, y_ref, wgu_bf, wd_bf):
    i = pl.program_id(0)
    prev = be_ref[jnp.maximum(i - 1, 0)]
    changed = jnp.logical_or(i == 0, be_ref[i] != prev)

    @pl.when(changed)
    def _():
        wgu_bf[...] = wgu_ref[...].astype(BF16)
        wd_bf[...] = wd_ref[...].astype(BF16)

    @pl.when(i < na_ref[0])
    def _():
        gu = _dot(x_ref[...], wgu_bf[...]) + bgu_ref[...]
        x_glu = jnp.minimum(gu[:, :EXPERT_FF], SWIGLU_LIMIT)
        x_lin = jnp.clip(gu[:, EXPERT_FF:], -SWIGLU_LIMIT, SWIGLU_LIMIT)
        act = x_glu * jax.nn.sigmoid(SWIGLU_ALPHA * x_glu) * (x_lin + 1.0)
        y_ref[...] = (_dot(act.astype(BF16), wd_bf[...]) + bd_ref[...]).astype(BF16)

    @pl.when(i >= na_ref[0])
    def _():
        y_ref[...] = jnp.zeros(y_ref.shape, BF16)


def _experts(blk_expert, n_active, xg, w_gu, b_gu, w_down, b_down):
    n_rows, d = xg.shape
    n_e, _, ff2 = w_gu.shape
    tb = MOE_ROWS
    n_blocks = n_rows // tb
    return pl.pallas_call(
        _expert_kernel,
        grid_spec=pltpu.PrefetchScalarGridSpec(
            num_scalar_prefetch=2,
            grid=(n_blocks,),
            in_specs=[pl.BlockSpec((tb, d), lambda i, be, na: (i, 0)),
                      pl.BlockSpec((None, d, ff2), lambda i, be, na: (be[i], 0, 0)),
                      pl.BlockSpec((None, 1, ff2), lambda i, be, na: (be[i], 0, 0)),
                      pl.BlockSpec((None, ff2 // 2, d), lambda i, be, na: (be[i], 0, 0)),
                      pl.BlockSpec((None, 1, d), lambda i, be, na: (be[i], 0, 0))],
            out_specs=pl.BlockSpec((tb, d), lambda i, be, na: (i, 0)),
            scratch_shapes=[pltpu.VMEM((d, ff2), BF16), pltpu.VMEM((ff2 // 2, d), BF16)]),
        out_shape=jax.ShapeDtypeStruct((n_rows, d), BF16),
        compiler_params=_cparams(("arbitrary",)),
        name="moe_experts",
    )(blk_expert, n_active, xg, w_gu, b_gu.reshape(n_e, 1, ff2), w_down, b_down.reshape(n_e, 1, d))


def _combine_kernel(x_ref, y_ref, gate_ref, npost_ref, g2_ref, o_ref):
    gate = gate_ref[...]
    f = jnp.zeros(x_ref.shape, F32)
    for kth in range(TOP_K):
        f = f + gate[:, kth:kth + 1] * y_ref[kth].astype(F32)
    o_ref[...] = x_ref[...] + g2_ref[...] * (_rms(f) * npost_ref[...])


def _combine(x1, yk, gates, n_post, g2):
    b, s, d = x1.shape
    tm = ROW_TILE
    row = lambda bi, i: (bi, i, 0)
    return pl.pallas_call(
        _combine_kernel,
        grid=(b, s // tm),
        in_specs=[pl.BlockSpec((None, tm, d), row),
                  pl.BlockSpec((TOP_K, None, tm, d), lambda bi, i: (0, bi, i, 0)),
                  pl.BlockSpec((None, tm, LOGIT_PAD), row),
                  pl.BlockSpec((1, d), lambda bi, i: (0, 0)),
                  pl.BlockSpec((None, 1, d), lambda bi, i: (bi, 0, 0))],
        out_specs=pl.BlockSpec((None, tm, d), row),
        out_shape=jax.ShapeDtypeStruct((b, s, d), F32),
        compiler_params=_cparams(("parallel", "parallel")),
        name="moe_combine_residual",
    )(x1, yk, gates, n_post.reshape(1, d), g2)


def _moe(x1, hf, top_idx, gates, w_gu, b_gu, w_down, b_down, n_post, g2):
    b, s, d = x1.shape
    n_tok = b * s
    tb = MOE_ROWS
    n_assign = n_tok * TOP_K
    e_flat = top_idx[..., :TOP_K].reshape(-1)
    order = jnp.argsort(e_flat)
    e_sorted = e_flat[order]
    counts = jnp.zeros((N_EXPERTS,), jnp.int32).at[e_flat].add(1)
    padded = (counts + tb - 1) // tb * tb
    start = jnp.cumsum(counts) - counts
    pend = jnp.cumsum(padded)
    pstart = pend - padded
    dest = pstart[e_sorted] + jnp.arange(n_assign, dtype=jnp.int32) - start[e_sorted]
    n_rows = -(-n_assign // tb) * tb + N_EXPERTS * tb
    n_blocks = n_rows // tb
    row_tok = jnp.zeros((n_rows,), jnp.int32).at[dest].set((order // TOP_K).astype(jnp.int32))
    pos = jnp.zeros((n_assign,), jnp.int32).at[order].set(dest)
    blk_start = jnp.arange(n_blocks, dtype=jnp.int32) * tb
    blk_expert = jnp.minimum(jnp.searchsorted(pend, blk_start, side='right'), N_EXPERTS - 1).astype(jnp.int32)
    n_active = (pend[-1] // tb).astype(jnp.int32).reshape(1)
    xg = jnp.take(hf.reshape(n_tok, d), row_tok, axis=0)
    y = _experts(blk_expert, n_active, xg, w_gu, b_gu, w_down, b_down)
    yk = jnp.take(y, pos.reshape(n_tok, TOP_K).T, axis=0).reshape(TOP_K, b, s, d)
    return _combine(x1, yk, gates, n_post, g2)


def _block_diag(n, blk, val):
    return (jnp.kron(jnp.eye(n // blk, dtype=F32), jnp.ones((blk, blk), F32)) * val).astype(BF16)


def kernel(x, c, ctx, c_ctx, ada_w, ada_b, norm_mix_pre, norm_mix_post, norm_ffn_pre, norm_ffn_post, router_w, router_b, moe_w_gu, moe_b_gu, moe_w_down, moe_b_down, hy_w_in, mla_q_norm, mla_w_uq, mla_kv_norm, mla_w_ukv, rwkv_mu_prev, rwkv_mu_next, rwkv_w0, rwkv_w2, rwkv_a0, rwkv_a2, rwkv_g2, rwkv_k_k, rwkv_k_a, rwkv_r_k, rwkv_ln_w, rwkv_ln_b, hy_w_out, gm_w_in, gm_v_norm_w, gm_v_norm_b, gm_w_s, gm_b_s, gm_w_out):
    b, s, d = x.shape
    n_ctx = ctx.shape[1]
    t_all = s + n_ctx
    assert b + 1 <= 8 and s % ROW_TILE == 0 and n_ctx % ROW_TILE == 0

    cond_rows = jnp.concatenate([c, c_ctx[None], jnp.zeros((8 - b - 1, d), F32)], axis=0)
    mod = _modulation(cond_rows, ada_w, ada_b)

    def lat_mod(l, j):
        return mod[l, :b, j * d:(j + 1) * d].reshape(b, 1, d)

    sh_all = jnp.stack([mod[0, :b, 0:d], jnp.broadcast_to(mod[0, b, 0:d], (b, d))], axis=1).reshape(2 * b, 1, d)
    sc_all = jnp.stack([mod[0, :b, d:2 * d], jnp.broadcast_to(mod[0, b, d:2 * d], (b, d))], axis=1).reshape(2 * b, 1, d)
    x_all = jnp.concatenate([x, ctx], axis=1)
    cos, sin = _rope_tables(s, n_ctx)
    w1, wq2, wk2 = _inproj_weights(hy_w_in[0], mla_w_uq[0], mla_w_ukv[0])
    q, k, v, prw = _inproj(x_all, s, norm_mix_pre[0], sh_all, sc_all, cos, sin, w1, mla_q_norm[0], mla_kv_norm[0],
                           wq2, wk2)
    tq = 512 if s % 512 == 0 else ROW_TILE
    tk = 1280 if t_all % 1280 == 0 else ROW_TILE
    o_attn = _attention(q, k, v, s, tq, tk)

    bd_ones = _block_diag(RWKV_DIM, RWKV_HEAD, 1.0)
    bd_mean = _block_diag(RWKV_DIM, RWKV_HEAD, 1.0 / RWKV_HEAD)
    r, kx, vx, g, kkn, lw, kd, bdir = _rwkv_prepare(prw, s, rwkv_mu_prev[0], rwkv_mu_next[0], rwkv_w0[0], rwkv_w2[0],
                                                    rwkv_a0[0], rwkv_a2[0], rwkv_g2[0], rwkv_k_k[0], rwkv_k_a[0],
                                                    bd_ones)
    m_c, n_c, r_c, y_c = _wkv_chunks(r, vx, kkn, lw, kd, bdir)
    n_lat_ch = s // WKV_CHUNK
    n_ctx_ch = n_ctx // WKV_CHUNK
    lat_ch = np.arange(n_lat_ch)
    ctx_ch = n_lat_ch + np.arange(n_ctx_ch)
    order = jnp.asarray(np.stack([np.concatenate([ctx_ch, lat_ch]),
                                  np.concatenate([ctx_ch[::-1], lat_ch[::-1]])]).astype(np.int32))
    yf, yb = _wkv_states(order, m_c, n_c, r_c, y_c, t_all)

    rw_pad, rb_pad = _router_pads(router_w[0], router_b[0])
    x1, hf, top_idx, gates = _mix_out(yf, yb, r, kx, vx, g, o_attn, x, rwkv_r_k[0].reshape(-1), rwkv_ln_w[0],
                                      rwkv_ln_b[0], bd_mean, bd_ones, hy_w_out[0].astype(BF16), norm_mix_post[0],
                                      lat_mod(0, 2), norm_ffn_pre[0], lat_mod(0, 3), lat_mod(0, 4), rw_pad, rb_pad)
    x2 = _moe(x1, hf, top_idx, gates, moe_w_gu[0], moe_b_gu[0], moe_w_down[0], moe_b_down[0], norm_ffn_post[0],
              lat_mod(0, 5))

    rw_pad, rb_pad = _router_pads(router_w[1], router_b[1])
    x3, hf, top_idx, gates = _gmlp(x2, norm_mix_pre[1], lat_mod(1, 0), lat_mod(1, 1), gm_w_in[0], gm_v_norm_w[0],
                                   gm_v_norm_b[0], gm_w_s[0], gm_b_s[0], gm_w_out[0], norm_mix_post[1], lat_mod(1, 2),
                                   norm_ffn_pre[1], lat_mod(1, 3), lat_mod(1, 4), rw_pad, rb_pad)
    return _moe(x3, hf, top_idx, gates, moe_w_gu[1], moe_b_gu[1], moe_w_down[1], moe_b_down[1], norm_ffn_post[1],
                lat_mod(1, 5))
```

```python
import functools

import jax
import jax.numpy as jnp
import numpy as np
from jax import lax
from jax.experimental import pallas as pl
from jax.experimental.pallas import tpu as pltpu

F32 = jnp.float32
BF16 = jnp.bfloat16
HIGHEST = lax.Precision.HIGHEST

D_MODEL = 1024
GRID_W = 64
EPS = 1e-6

MLA_HEADS = 8
MLA_NOPE = 64
MLA_ROPE = 32
MLA_V = 64
MLA_Q_RANK = 256
MLA_KV_RANK = 128
MLA_SCALE = (MLA_NOPE + MLA_ROPE) ** -0.5
ROPE_FREQS = MLA_ROPE // 4
ROPE_BASE = 10000.0
HEAD_PAD = 128

RWKV_HEAD = 64
RWKV_DIM = 512
RWKV_HEADS = 8
DECAY_LORA = 32
AAA_LORA = 32
GATE_LORA = 96
RWKV_PAD_COLS = 3 * RWKV_DIM + 128 + 128
RWKV_LN_EPS = 64e-5
WKV_CHUNK = 64
WKV_CHUNKS_PER_STEP = 2

CHUNK = 128
GM_WIDTH = 1024
GM_GROUPS = 8
LN_EPS = 1e-5

N_EXPERTS = 32
TOP_K = 4
EXPERT_FF = 1024
SWIGLU_LIMIT = 7.0
SWIGLU_ALPHA = 1.702
LOGIT_PAD = 128
NEG_BIG = -1e30

ROW_TILE = 256
VMEM_LIMIT = 48 * 1024 * 1024


def _cparams(sem):
    return pltpu.CompilerParams(dimension_semantics=sem, vmem_limit_bytes=VMEM_LIMIT)


def _rms(x):
    return x * lax.rsqrt(jnp.mean(x * x, axis=-1, keepdims=True) + EPS)


def _dot(a, b, precision=None):
    return jnp.dot(a, b, preferred_element_type=F32, precision=precision)


def _dot_nt(a, b, precision=None):
    return lax.dot_general(a, b, (((1,), (1,)), ((), ())), preferred_element_type=F32, precision=precision)


def _dot_tn(a, b, precision=None):
    return lax.dot_general(a, b, (((0,), (0,)), ((), ())), preferred_element_type=F32, precision=precision)


def _split(x):
    hi = x.astype(BF16)
    return hi, (x - hi.astype(F32)).astype(BF16)


def _dot_split(a, b):
    a_hi, a_lo = _split(a)
    b_hi, b_lo = _split(b)
    return _dot(jnp.concatenate([a_hi, a_hi, a_lo], axis=1), jnp.concatenate([b_hi, b_lo, b_hi], axis=0))


def _mod_kernel(s_ref, w_ref, b_ref, o_ref):
    s = s_ref[...]
    s = s * jax.nn.sigmoid(s)
    o_ref[...] = _dot(s, w_ref[...], HIGHEST) + b_ref[...]


def _modulation(cond_rows, ada_w, ada_b):
    n_l, d, n6 = ada_w.shape
    tn = 1536
    return pl.pallas_call(
        _mod_kernel,
        grid=(n_l, n6 // tn),
        in_specs=[pl.BlockSpec((8, d), lambda l, j: (0, 0)),
                  pl.BlockSpec((None, d, tn), lambda l, j: (l, 0, j)),
                  pl.BlockSpec((None, 1, tn), lambda l, j: (l, 0, j))],
        out_specs=pl.BlockSpec((None, 8, tn), lambda l, j: (l, 0, j)),
        out_shape=jax.ShapeDtypeStruct((n_l, 8, n6), F32),
        compiler_params=_cparams(("arbitrary", "arbitrary")),
        name="adaln_mod",
    )(cond_rows, ada_w, ada_b.reshape(n_l, 1, n6))


def _inproj_kernel(x_ref, g_ref, sh_ref, sc_ref, cos_ref, sin_ref, w1_ref, qn_ref, kvn_ref, wq_ref, wk_ref,
                   q_ref, k_ref, v_ref, prw_ref):
    h = _rms(x_ref[...]) * g_ref[...]
    h = h * (1.0 + sc_ref[...]) + sh_ref[...]
    p = _dot(h.astype(BF16), w1_ref[...])
    cos = jnp.concatenate([cos_ref[...]] * MLA_HEADS, axis=1)
    sin = jnp.concatenate([sin_ref[...]] * MLA_HEADS, axis=1)
    nq = MLA_HEADS * HEAD_PAD
    qn = _rms(p[:, :MLA_Q_RANK]) * qn_ref[...]
    qq = _dot(qn.astype(BF16), wq_ref[...])
    q_ref[...] = (qq[:, :nq] * cos + qq[:, nq:] * sin).astype(BF16)
    kvn = _rms(p[:, MLA_Q_RANK:MLA_Q_RANK + MLA_KV_RANK]) * kvn_ref[...]
    x2 = jnp.concatenate([kvn, p[:, MLA_Q_RANK + MLA_KV_RANK:512]], axis=1).astype(BF16)
    kk = _dot(x2, wk_ref[...])
    k_ref[...] = (kk[:, :nq] * cos + kk[:, nq:2 * nq] * sin).astype(BF16)
    v_ref[...] = kk[:, 2 * nq:].astype(BF16)
    prw_ref[...] = p[:, 512:]


def _rotate_half_cols(w):
    wr = w.reshape(w.shape[:-1] + (2, 2, ROPE_FREQS))
    return jnp.stack([-wr[..., 1, :], wr[..., 0, :]], axis=-2).reshape(w.shape)


def _inproj_weights(hy_w_in, w_uq, w_ukv):
    d = hy_w_in.shape[0]
    mla_cols = MLA_Q_RANK + MLA_KV_RANK + MLA_ROPE
    w_mla = hy_w_in[:, :mla_cols]
    w_rw = hy_w_in[:, mla_cols:]
    w1 = jnp.concatenate([w_mla, jnp.zeros((d, 512 - mla_cols), F32), _pad_rwkv_cols(w_rw)], axis=1).astype(BF16)
    wq = (w_uq * MLA_SCALE).reshape(MLA_Q_RANK, MLA_HEADS, MLA_NOPE + MLA_ROPE)
    z_n = jnp.zeros((MLA_Q_RANK, MLA_HEADS, MLA_NOPE), F32)
    z_p = jnp.zeros((MLA_Q_RANK, MLA_HEADS, HEAD_PAD - MLA_NOPE - MLA_ROPE), F32)
    wq_plain = jnp.concatenate([wq[..., :MLA_NOPE], wq[..., MLA_NOPE:], z_p], axis=-1)
    wq_rot = jnp.concatenate([z_n, _rotate_half_cols(wq[..., MLA_NOPE:]), z_p], axis=-1)
    wq2 = jnp.concatenate([wq_plain.reshape(MLA_Q_RANK, -1), wq_rot.reshape(MLA_Q_RANK, -1)], axis=1).astype(BF16)
    wkv = w_ukv.reshape(MLA_KV_RANK, MLA_HEADS, MLA_NOPE + MLA_V)
    eye = jnp.broadcast_to(jnp.eye(MLA_ROPE, dtype=F32)[:, None, :], (MLA_ROPE, MLA_HEADS, MLA_ROPE))
    pad_k = HEAD_PAD - MLA_NOPE - MLA_ROPE
    top_plain = jnp.concatenate([wkv[..., :MLA_NOPE], jnp.zeros((MLA_KV_RANK, MLA_HEADS, MLA_ROPE + pad_k), F32)], -1)
    mid_plain = jnp.concatenate([jnp.zeros((MLA_ROPE, MLA_HEADS, MLA_NOPE), F32), eye,
                                 jnp.zeros((MLA_ROPE, MLA_HEADS, pad_k), F32)], -1)
    mid_rot = jnp.concatenate([jnp.zeros((MLA_ROPE, MLA_HEADS, MLA_NOPE), F32), _rotate_half_cols(eye),
                               jnp.zeros((MLA_ROPE, MLA_HEADS, pad_k), F32)], -1)
    nq = MLA_HEADS * HEAD_PAD
    rows_c = jnp.concatenate([top_plain.reshape(MLA_KV_RANK, nq), jnp.zeros((MLA_KV_RANK, nq), F32),
                              wkv[..., MLA_NOPE:].reshape(MLA_KV_RANK, MLA_HEADS * MLA_V)], axis=1)
    rows_r = jnp.concatenate([mid_plain.reshape(MLA_ROPE, nq), mid_rot.reshape(MLA_ROPE, nq),
                              jnp.zeros((MLA_ROPE, MLA_HEADS * MLA_V), F32)], axis=1)
    rows_z = jnp.zeros((256 - MLA_KV_RANK - MLA_ROPE, rows_c.shape[1]), F32)
    wk2 = jnp.concatenate([rows_c, rows_r, rows_z], axis=0).astype(BF16)
    return w1, wq2, wk2


def _pad_rwkv_cols(w):
    pad = jnp.zeros(w.shape[:-1] + (RWKV_PAD_COLS - w.shape[-1],), w.dtype)
    return jnp.concatenate([w, pad], axis=-1)


def _rope_tables(n_lat, n_ctx):
    t = jnp.arange(n_lat)
    row = (t // GRID_W).astype(F32)
    col = (t % GRID_W).astype(F32)
    inv = ROPE_BASE ** (-jnp.arange(ROPE_FREQS, dtype=F32) / ROPE_FREQS)
    ang = jnp.stack([row[:, None] * inv, col[:, None] * inv], axis=1)
    ang = jnp.broadcast_to(ang[:, :, None, :], (n_lat, 2, 2, ROPE_FREQS)).reshape(n_lat, MLA_ROPE)
    cos = jnp.concatenate([jnp.ones((n_lat, MLA_NOPE), F32), jnp.cos(ang),
                           jnp.ones((n_lat, HEAD_PAD - MLA_NOPE - MLA_ROPE), F32)], axis=1)
    sin = jnp.concatenate([jnp.zeros((n_lat, MLA_NOPE), F32), jnp.sin(ang),
                           jnp.zeros((n_lat, HEAD_PAD - MLA_NOPE - MLA_ROPE), F32)], axis=1)
    cos = jnp.concatenate([cos, jnp.ones((n_ctx, HEAD_PAD), F32)], axis=0)
    sin = jnp.concatenate([sin, jnp.zeros((n_ctx, HEAD_PAD), F32)], axis=0)
    return cos, sin


def _inproj(x_all, n_lat, gain, shift, scale, cos, sin, w1, q_norm, kv_norm, wq2, wk2):
    b, t_all, d = x_all.shape
    tm = ROW_TILE
    n_lat_tiles = n_lat // tm
    nq = MLA_HEADS * HEAD_PAD

    def mod_map(bi, i):
        return (2 * bi + jnp.where(i >= n_lat_tiles, 1, 0), 0, 0)

    const = lambda bi, i: (0, 0)
    row = lambda bi, i: (bi, i, 0)
    return pl.pallas_call(
        _inproj_kernel,
        grid=(b, t_all // tm),
        in_specs=[pl.BlockSpec((None, tm, d), row),
                  pl.BlockSpec((1, d), const),
                  pl.BlockSpec((None, 1, d), mod_map),
                  pl.BlockSpec((None, 1, d), mod_map),
                  pl.BlockSpec((tm, HEAD_PAD), lambda bi, i: (i, 0)),
                  pl.BlockSpec((tm, HEAD_PAD), lambda bi, i: (i, 0)),
                  pl.BlockSpec(w1.shape, const),
                  pl.BlockSpec((1, MLA_Q_RANK), const),
                  pl.BlockSpec((1, MLA_KV_RANK), const),
                  pl.BlockSpec(wq2.shape, const),
                  pl.BlockSpec(wk2.shape, const)],
        out_specs=[pl.BlockSpec((None, tm, nq), row),
                   pl.BlockSpec((None, tm, nq), row),
                   pl.BlockSpec((None, tm, MLA_HEADS * MLA_V), row),
                   pl.BlockSpec((None, tm, RWKV_PAD_COLS), row)],
        out_shape=[jax.ShapeDtypeStruct((b, t_all, nq), BF16),
                   jax.ShapeDtypeStruct((b, t_all, nq), BF16),
                   jax.ShapeDtypeStruct((b, t_all, MLA_HEADS * MLA_V), BF16),
                   jax.ShapeDtypeStruct((b, t_all, RWKV_PAD_COLS), F32)],
        compiler_params=_cparams(("parallel", "parallel")),
        name="hybrid_inproj",
    )(x_all, gain.reshape(1, d), shift, scale, cos, sin, w1, q_norm.reshape(1, -1), kv_norm.reshape(1, -1), wq2, wk2)


def _attn_kernel(q_ref, k_ref, v_ref, o_ref, m_sc, l_sc, acc_sc):
    ki = pl.program_id(3)

    @pl.when(ki == 0)
    def _():
        m_sc[...] = jnp.full(m_sc.shape, -jnp.inf, F32)
        l_sc[...] = jnp.zeros(l_sc.shape, F32)
        acc_sc[...] = jnp.zeros(acc_sc.shape, F32)

    v = v_ref[...]
    for h in range(2):
        q = q_ref[:, h * HEAD_PAD:(h + 1) * HEAD_PAD]
        k = k_ref[:, h * HEAD_PAD:(h + 1) * HEAD_PAD]
        s = _dot_nt(q, k)
        m_prev = m_sc[h]
        m_new = jnp.maximum(m_prev, jnp.max(s, axis=1, keepdims=True))
        alpha = jnp.exp(m_prev - m_new)
        p = jnp.exp(s - m_new[:, :1])
        l_sc[h] = alpha * l_sc[h] + jnp.sum(p, axis=1, keepdims=True)
        acc_sc[h] = acc_sc[h] * alpha + _dot(p.astype(BF16), v)
        m_sc[h] = m_new

    @pl.when(ki == pl.num_programs(3) - 1)
    def _():
        o0 = acc_sc[0] / l_sc[0]
        o1 = acc_sc[1] / l_sc[1]
        lane = lax.broadcasted_iota(jnp.int32, o0.shape, 1)
        o_ref[...] = jnp.where(lane < MLA_V, o0, o1).astype(BF16)


def _attention(q, k, v, n_lat, tq, tk):
    b, t_all, _ = k.shape
    return pl.pallas_call(
        _attn_kernel,
        grid=(b, MLA_HEADS // 2, n_lat // tq, t_all // tk),
        in_specs=[pl.BlockSpec((None, tq, 2 * HEAD_PAD), lambda bi, hp, qi, ki: (bi, qi, hp)),
                  pl.BlockSpec((None, tk, 2 * HEAD_PAD), lambda bi, hp, qi, ki: (bi, ki, hp)),
                  pl.BlockSpec((None, tk, 2 * MLA_V), lambda bi, hp, qi, ki: (bi, ki, hp))],
        out_specs=pl.BlockSpec((None, tq, 2 * MLA_V), lambda bi, hp, qi, ki: (bi, qi, hp)),
        out_shape=jax.ShapeDtypeStruct((b, n_lat, MLA_HEADS * MLA_V), BF16),
        scratch_shapes=[pltpu.VMEM((2, tq, 128), F32), pltpu.VMEM((2, tq, 128), F32), pltpu.VMEM((2, tq, 128), F32)],
        compiler_params=_cparams(("parallel", "parallel", "parallel", "arbitrary")),
        name="mla_attention",
    )(q, k, v)


def _seg_sum(x, bd_ref):
    hi = x.astype(BF16)
    lo = (x - hi.astype(F32)).astype(BF16)
    bd = bd_ref[...]
    return _dot(hi, bd) + _dot(lo, bd)


def _rwkv_prep_kernel(p_ref, prev_ref, next_ref, mup_ref, mun_ref, wl_ref, w0a0_ref, g2_ref, kk_ref, ka_ref, bd_ref,
                      r_ref, k_ref, v_ref, g_ref, kkn_ref, lw_ref, kd_ref, bdir_ref, *, n_lat_tiles, n_tiles):
    i = pl.program_id(1)
    p = p_ref[...]
    tm = p.shape[0]
    first = jnp.logical_or(i == 0, i == n_lat_tiles)
    last = jnp.logical_or(i == n_lat_tiles - 1, i == n_tiles - 1)
    prev_row = jnp.where(first, 0.0, prev_ref[7:8, :])
    next_row = jnp.where(last, 0.0, next_ref[0:1, :])
    ridx = lax.broadcasted_iota(jnp.int32, p.shape, 0)
    prev = jnp.where(ridx == 0, prev_row, pltpu.roll(p, 1, 0))
    nxt = jnp.where(ridx == tm - 1, next_row, pltpu.roll(p, tm - 1, 0))
    p = p + mup_ref[...] * (prev - p) + mun_ref[...] * (nxt - p)
    c = RWKV_DIM
    r, k, v = p[:, :c], p[:, c:2 * c], p[:, 2 * c:3 * c]
    lo = p[:, 3 * c:3 * c + 128]
    lane = lax.broadcasted_iota(jnp.int32, lo.shape, 1)
    lo = jnp.where(lane < 2 * DECAY_LORA, jnp.tanh(lo), lo)
    wa = _dot(lo.astype(BF16), wl_ref[...]) + w0a0_ref[...]
    gd = p[:, 3 * c + 128:]
    g_ref[...] = _dot(jax.nn.sigmoid(gd).astype(BF16), g2_ref[...])
    kk = k * kk_ref[...]
    kk = kk * lax.rsqrt(jnp.maximum(_seg_sum(kk * kk, bd_ref), 1e-24))
    r_ref[...] = r
    k_ref[...] = k
    v_ref[...] = v
    kkn_ref[...] = kk
    ka = ka_ref[...]
    for d in range(2):
        w = wa[:, d * c:(d + 1) * c]
        a = jax.nn.sigmoid(wa[:, (2 + d) * c:(3 + d) * c])
        lw_ref[:, d * c:(d + 1) * c] = -float(np.exp(-0.5)) * jax.nn.sigmoid(w)
        kd_ref[:, d * c:(d + 1) * c] = k * (1.0 + (a - 1.0) * ka)
        bdir_ref[:, d * c:(d + 1) * c] = kk * a


def _rwkv_prepare(prw, n_lat, mu_prev, mu_next, w0, w2, a0, a2, g2, k_k, k_a, bd_ones):
    b, t_all, pc = prw.shape
    tm = ROW_TILE
    n_tiles = t_all // tm
    n_lat_tiles = n_lat // tm
    c = RWKV_DIM
    z = jnp.zeros((DECAY_LORA, c), F32)
    wl = jnp.concatenate([
        jnp.concatenate([w2[0], z, z, z], axis=1), jnp.concatenate([z, w2[1], z, z], axis=1),
        jnp.concatenate([z, z, a2[0], z], axis=1), jnp.concatenate([z, z, z, a2[1]], axis=1)], axis=0).astype(BF16)
    w0a0 = jnp.concatenate([w0[0], w0[1], a0[0], a0[1]]).reshape(1, 4 * c)
    g2p = jnp.concatenate([g2, jnp.zeros((128 - GATE_LORA, c), F32)], axis=0).astype(BF16)
    row = lambda bi, i: (bi, i, 0)
    const = lambda bi, i: (0, 0)
    hb = tm // 8
    n_hb = t_all // 8
    kern = functools.partial(_rwkv_prep_kernel, n_lat_tiles=n_lat_tiles, n_tiles=n_tiles)
    o_c = jax.ShapeDtypeStruct((b, t_all, c), F32)
    o_2c = jax.ShapeDtypeStruct((b, t_all, 2 * c), F32)
    return pl.pallas_call(
        kern,
        grid=(b, n_tiles),
        in_specs=[pl.BlockSpec((None, tm, pc), row),
                  pl.BlockSpec((None, 8, pc), lambda bi, i: (bi, jnp.maximum(i * hb - 1, 0), 0)),
                  pl.BlockSpec((None, 8, pc), lambda bi, i: (bi, jnp.minimum((i + 1) * hb, n_hb - 1), 0)),
                  pl.BlockSpec((1, pc), const), pl.BlockSpec((1, pc), const),
                  pl.BlockSpec(wl.shape, const), pl.BlockSpec((1, 4 * c), const),
                  pl.BlockSpec(g2p.shape, const), pl.BlockSpec((1, c), const), pl.BlockSpec((1, c), const),
                  pl.BlockSpec((c, c), const)],
        out_specs=[pl.BlockSpec((None, tm, c), row)] * 5 + [pl.BlockSpec((None, tm, 2 * c), row)] * 3,
        out_shape=[o_c] * 5 + [o_2c] * 3,
        compiler_params=_cparams(("parallel", "parallel")),
        name="rwkv_prepare",
    )(prw, prw, prw, _pad_rwkv_cols(mu_prev).reshape(1, pc), _pad_rwkv_cols(mu_next).reshape(1, pc),
      wl, w0a0, g2p, k_k.reshape(1, c), k_a.reshape(1, c), bd_ones)


def _wkv_chunk_kernel(r_ref, v_ref, kk_ref, lw_ref, kd_ref, bd_ref, m_ref, n_ref, rr_ref, yv_ref):
    d = pl.program_id(1)
    cs = WKV_CHUNK
    nc = r_ref.shape[0] // cs
    ti = lax.broadcasted_iota(jnp.int32, (cs, cs), 0)
    si = lax.broadcasted_iota(jnp.int32, (cs, cs), 1)
    rel = (si - ti) * (1 - 2 * d)
    incl = rel <= 0
    strict = rel < 0
    eye = si == ti
    incl_f = incl.astype(F32)
    ops = []
    for ci in range(nc):
        rows = slice(ci * cs, (ci + 1) * cs)
        lw = lw_ref[rows, :]
        g = _dot(incl_f, lw, HIGHEST)
        total = jnp.sum(lw, axis=0, keepdims=True)
        e_inv = jnp.exp(-g)
        e_end = jnp.exp(total - g)
        gam = jnp.exp(total)
        kd = kd_ref[rows, :]
        bd = bd_ref[rows, :]
        at = -kk_ref[rows, :] * jnp.exp(g - lw)
        rt = r_ref[rows, :] * jnp.exp(g)
        kt = (kd * e_inv).astype(BF16)
        bt = (bd * e_inv).astype(BF16)
        ke = (kd * e_end).astype(BF16)
        be = (bd * e_end).astype(BF16)
        v = v_ref[rows, :].astype(BF16)
        for h in range(RWKV_HEADS):
            sl = slice(h * RWKV_HEAD, (h + 1) * RWKV_HEAD)
            ops.append((at[:, sl], rt[:, sl], kt[:, sl], bt[:, sl], ke[:, sl], be[:, sl], v[:, sl], gam[:, sl]))
    n_it = len(ops)
    aa = [_dot_nt(jnp.concatenate([o[0], o[1]], axis=0).astype(BF16), jnp.concatenate([o[3], o[2]], axis=0))
          for o in ops]
    a_ab = [jnp.where(strict, x[:cs, :cs], 0.0) for x in aa]
    a_ak = [jnp.where(strict, x[:cs, cs:], 0.0).astype(BF16) for x in aa]
    a_rb = [jnp.where(incl, x[cs:, :cs], 0.0).astype(BF16) for x in aa]
    a_rk = [jnp.where(incl, x[cs:, cs:], 0.0).astype(BF16) for x in aa]
    z = [jnp.concatenate([ops[i][0], _dot(a_ak[i], ops[i][6])], axis=1) for i in range(n_it)]
    pw = a_ab
    n_sq = int(np.log2(cs))
    for it in range(n_sq):
        more = it + 1 < n_sq
        res = [_dot_split(pw[i], jnp.concatenate([z[i], pw[i]], axis=1) if more else z[i]) for i in range(n_it)]
        z = [z[i] + res[i][:, :2 * cs] for i in range(n_it)]
        if more:
            pw = [x[:, 2 * cs:] for x in res]
    zb = [x.astype(BF16) for x in z]
    w2 = [_dot(a_rb[i], zb[i]) for i in range(n_it)]
    w3 = [_dot_tn(ops[i][5], zb[i]) for i in range(n_it)]
    n0 = [_dot_tn(ops[i][4], ops[i][6]) for i in range(n_it)]
    y0 = [_dot(a_rk[i], ops[i][6]) for i in range(n_it)]
    nh = RWKV_HEADS
    for ci in range(nc):
        rows = slice(ci * cs, (ci + 1) * cs)
        ids = range(ci * nh, (ci + 1) * nh)
        m_ref[rows, :] = jnp.concatenate([jnp.where(eye, ops[i][7], 0.0) + w3[i][:, :cs] for i in ids], axis=1)
        n_ref[rows, :] = jnp.concatenate([n0[i] + w3[i][:, cs:] for i in ids], axis=1)
        rr_ref[rows, :] = jnp.concatenate([ops[i][1] + w2[i][:, :cs] for i in ids], axis=1)
        yv_ref[rows, :] = jnp.concatenate([y0[i] + w2[i][:, cs:] for i in ids], axis=1)


def _wkv_chunks(r, v, kkn, lw, kd, bdir):
    b, t_all, c = r.shape
    rows = WKV_CHUNK * WKV_CHUNKS_PER_STEP
    nst = t_all // rows
    shared = pl.BlockSpec((None, rows, c), lambda bi, d, ci: (bi, ci, 0))
    per_dir = pl.BlockSpec((None, rows, c), lambda bi, d, ci: (bi, ci, d))
    out = pl.BlockSpec((None, None, rows, c), lambda bi, d, ci: (bi, d, ci, 0))
    o_s = jax.ShapeDtypeStruct((b, 2, t_all, c), F32)
    return pl.pallas_call(
        _wkv_chunk_kernel,
        grid=(b, 2, nst),
        in_specs=[shared, shared, shared, per_dir, per_dir, per_dir],
        out_specs=[out] * 4,
        out_shape=[o_s] * 4,
        compiler_params=_cparams(("parallel", "parallel", "parallel")),
        name="wkv_chunk_summaries",
    )(r, v, kkn, lw, kd, bdir)


def _wkv_state_kernel(order_ref, mf_ref, nf_ref, rf_ref, yf_ref, mb_ref, nb_ref, rb_ref, yb_ref,
                      of_ref, ob_ref, st_sc):
    del order_ref
    s = pl.program_id(0)

    @pl.when(s == 0)
    def _():
        st_sc[...] = jnp.zeros(st_sc.shape, F32)

    n_b = st_sc.shape[0]
    ins = ((mf_ref, nf_ref, rf_ref, yf_ref, of_ref), (mb_ref, nb_ref, rb_ref, yb_ref, ob_ref))
    for bi in range(n_b):
        for d in range(2):
            m_r, n_r, r_r, y_r, o_r = ins[d]
            st = st_sc[bi, d]
            m, n, rr, yv = m_r[bi], n_r[bi], r_r[bi], y_r[bi]
            ys, sts = [], []
            for h in range(RWKV_HEADS):
                sl = slice(h * RWKV_HEAD, (h + 1) * RWKV_HEAD)
                st_h = st[:, sl]
                ys.append(_dot(rr[:, sl], st_h) + yv[:, sl])
                sts.append(_dot(m[:, sl], st_h, HIGHEST) + n[:, sl])
            o_r[bi] = jnp.concatenate(ys, axis=1)
            st_sc[bi, d] = jnp.concatenate(sts, axis=1)


def _wkv_states(order, m, n, rr, yv, t_all):
    b, _, _, c = m.shape
    cs = WKV_CHUNK
    nch = t_all // cs
    fwd = pl.BlockSpec((b, None, cs, c), lambda s, o: (0, 0, o[0, s], 0))
    bwd = pl.BlockSpec((b, None, cs, c), lambda s, o: (0, 1, o[1, s], 0))
    out_f = pl.BlockSpec((b, cs, c), lambda s, o: (0, o[0, s], 0))
    out_b = pl.BlockSpec((b, cs, c), lambda s, o: (0, o[1, s], 0))
    o_s = jax.ShapeDtypeStruct((b, t_all, c), F32)
    return pl.pallas_call(
        _wkv_state_kernel,
        grid_spec=pltpu.PrefetchScalarGridSpec(
            num_scalar_prefetch=1,
            grid=(nch,),
            in_specs=[fwd] * 4 + [bwd] * 4,
            out_specs=[out_f, out_b],
            scratch_shapes=[pltpu.VMEM((b, 2, cs, c), F32)]),
        out_shape=[o_s, o_s],
        compiler_params=_cparams(("arbitrary",)),
        name="wkv_state_pass",
    )(order, m, n, rr, yv, m, n, rr, yv)


def _ffn_pre(x1, gain_ref, sh_ref, sc_ref, rw_ref, rb_ref, hf_ref, idx_ref, gate_ref):
    hf = _rms(x1) * gain_ref[...]
    hf = hf * (1.0 + sc_ref[...]) + sh_ref[...]
    hf_ref[...] = hf.astype(BF16)
    logits = _dot(hf, rw_ref[...], HIGHEST) + rb_ref[...]
    lane = lax.broadcasted_iota(jnp.int32, logits.shape, 1)
    lane_f = lane.astype(F32)
    work = logits
    idx_out = jnp.zeros(logits.shape, F32)
    val_out = jnp.full(logits.shape, NEG_BIG, F32)
    for kth in range(TOP_K):
        m = jnp.max(work, axis=1, keepdims=True)
        idx = jnp.min(jnp.where(work == m, lane_f, float(LOGIT_PAD)), axis=1, keepdims=True)
        work = jnp.where(lane_f == idx, -jnp.inf, work)
        idx_out = jnp.where(lane == kth, idx, idx_out)
        val_out = jnp.where(lane == kth, m, val_out)
    e = jnp.exp(val_out - jnp.max(val_out, axis=1, keepdims=True))
    e = jnp.where(lane < TOP_K, e, 0.0)
    idx_ref[...] = idx_out.astype(jnp.int32)
    gate_ref[...] = e / jnp.sum(e, axis=1, keepdims=True)


def _router_pads(router_w, router_b):
    d = router_w.shape[0]
    rw = jnp.concatenate([router_w, jnp.zeros((d, LOGIT_PAD - N_EXPERTS), F32)], axis=1)
    rb = jnp.concatenate([router_b, jnp.full((LOGIT_PAD - N_EXPERTS,), NEG_BIG, F32)]).reshape(1, LOGIT_PAD)
    return rw, rb


def _mix_out_kernel(yf_ref, yb_ref, r_ref, k_ref, v_ref, g_ref, o_ref, x_ref,
                    rk_ref, lnw_ref, lnb_ref, bdm_ref, bds_ref, wout_ref, npost_ref, g1_ref,
                    gpre_ref, sh_ref, sc_ref, rw_ref, rb_ref,
                    x1_ref, hf_ref, idx_ref, gate_ref):
    y = yf_ref[...] + yb_ref[...]
    mu = _seg_sum(y, bdm_ref)
    dlt = y - mu
    var = _seg_sum(dlt * dlt, bdm_ref)
    yn = dlt * lax.rsqrt(var + RWKV_LN_EPS) * lnw_ref[...] + lnb_ref[...]
    bonus = _seg_sum(r_ref[...] * k_ref[...] * rk_ref[...], bds_ref) * v_ref[...]
    rw = (yn + bonus) * g_ref[...]
    mix_in = jnp.concatenate([o_ref[...], rw.astype(BF16)], axis=1)
    mix = _dot(mix_in, wout_ref[...])
    x1 = x_ref[...] + g1_ref[...] * (_rms(mix) * npost_ref[...])
    x1_ref[...] = x1
    _ffn_pre(x1, gpre_ref, sh_ref, sc_ref, rw_ref, rb_ref, hf_ref, idx_ref, gate_ref)


def _token_out_specs(tm, d):
    row = lambda bi, i: (bi, i, 0)
    specs = [pl.BlockSpec((None, tm, d), row), pl.BlockSpec((None, tm, d), row),
             pl.BlockSpec((None, tm, LOGIT_PAD), row), pl.BlockSpec((None, tm, LOGIT_PAD), row)]
    return specs


def _token_out_shapes(b, s, d):
    return [jax.ShapeDtypeStruct((b, s, d), F32), jax.ShapeDtypeStruct((b, s, d), BF16),
            jax.ShapeDtypeStruct((b, s, LOGIT_PAD), jnp.int32), jax.ShapeDtypeStruct((b, s, LOGIT_PAD), F32)]


def _mix_out(yf, yb, r, k, v, g, o_attn, x, r_k, ln_w, ln_b, bd_mean, bd_ones, w_out, n_post, g1,
             n_pre, sh2, sc2, rw_pad, rb_pad):
    b, s, d = x.shape
    c = RWKV_DIM
    tm = ROW_TILE
    row = lambda bi, i: (bi, i, 0)
    const = lambda bi, i: (0, 0)
    per_b = lambda bi, i: (bi, 0, 0)
    rc = pl.BlockSpec((None, tm, c), row)
    vec_c = pl.BlockSpec((1, c), const)
    vec_d = pl.BlockSpec((1, d), const)
    mod_d = pl.BlockSpec((None, 1, d), per_b)
    return pl.pallas_call(
        _mix_out_kernel,
        grid=(b, s // tm),
        in_specs=[rc] * 7 + [pl.BlockSpec((None, tm, d), row),
                             vec_c, vec_c, vec_c, pl.BlockSpec((c, c), const), pl.BlockSpec((c, c), const),
                             pl.BlockSpec((d, d), const), vec_d, mod_d,
                             vec_d, mod_d, mod_d, pl.BlockSpec((d, LOGIT_PAD), const),
                             pl.BlockSpec((1, LOGIT_PAD), const)],
        out_specs=_token_out_specs(tm, d),
        out_shape=_token_out_shapes(b, s, d),
        compiler_params=_cparams(("parallel", "parallel")),
        name="mixer_out_router",
    )(yf, yb, r, k, v, g, o_attn, x, r_k.reshape(1, c), ln_w.reshape(1, c), ln_b.reshape(1, c), bd_mean, bd_ones,
      w_out, n_post.reshape(1, d), g1, n_pre.reshape(1, d), sh2, sc2, rw_pad, rb_pad)


def _gmlp_kernel(x_ref, gpre1_ref, sh1_ref, sc1_ref, win_ref, vnw_ref, vnb_ref, ws_ref, bs_ref, wout_ref,
                 npost_ref, g1_ref, gpre_ref, sh_ref, sc_ref, rw_ref, rb_ref,
                 x1_ref, hf_ref, idx_ref, gate_ref):
    x = x_ref[...]
    h = _rms(x) * gpre1_ref[...]
    h = h * (1.0 + sc1_ref[...]) + sh1_ref[...]
    z = _dot(h.astype(BF16), win_ref[...])
    z = 0.5 * z * (1.0 + lax.erf(z * float(2.0 ** -0.5)))
    u, v = z[:, :GM_WIDTH], z[:, GM_WIDTH:]
    mu = jnp.mean(v, axis=-1, keepdims=True)
    dv = v - mu
    var = jnp.mean(dv * dv, axis=-1, keepdims=True)
    v = (dv * lax.rsqrt(var + LN_EPS) * vnw_ref[...] + vnb_ref[...]).astype(BF16)
    gw = GM_WIDTH // GM_GROUPS
    rows = []
    for ci in range(x.shape[0] // CHUNK):
        cols = []
        for gi in range(GM_GROUPS):
            cols.append(_dot(ws_ref[gi], v[ci * CHUNK:(ci + 1) * CHUNK, gi * gw:(gi + 1) * gw]))
        rows.append(jnp.concatenate(cols, axis=1) + bs_ref[...])
    sp = jnp.concatenate(rows, axis=0)
    y = _dot((u * sp).astype(BF16), wout_ref[...])
    x1 = x + g1_ref[...] * (_rms(y) * npost_ref[...])
    x1_ref[...] = x1
    _ffn_pre(x1, gpre_ref, sh_ref, sc_ref, rw_ref, rb_ref, hf_ref, idx_ref, gate_ref)


def _gmlp(x, n_pre1, sh1, sc1, w_in, vn_w, vn_b, w_s, b_s, w_out, n_post, g1, n_pre, sh2, sc2, rw_pad, rb_pad):
    b, s, d = x.shape
    tm = ROW_TILE
    gw = GM_WIDTH // GM_GROUPS
    bs_full = jnp.repeat(b_s.T, gw, axis=1)
    row = lambda bi, i: (bi, i, 0)
    const = lambda bi, i: (0, 0)
    per_b = lambda bi, i: (bi, 0, 0)
    vec_d = pl.BlockSpec((1, d), const)
    vec_g = pl.BlockSpec((1, GM_WIDTH), const)
    mod_d = pl.BlockSpec((None, 1, d), per_b)
    return pl.pallas_call(
        _gmlp_kernel,
        grid=(b, s // tm),
        in_specs=[pl.BlockSpec((None, tm, d), row), vec_d, mod_d, mod_d,
                  pl.BlockSpec((d, 2 * GM_WIDTH), const), vec_g, vec_g,
                  pl.BlockSpec((GM_GROUPS, CHUNK, CHUNK), lambda bi, i: (0, 0, 0)),
                  pl.BlockSpec((CHUNK, GM_WIDTH), const), pl.BlockSpec((GM_WIDTH, d), const),
                  vec_d, mod_d, vec_d, mod_d, mod_d,
                  pl.BlockSpec((d, LOGIT_PAD), const), pl.BlockSpec((1, LOGIT_PAD), const)],
        out_specs=_token_out_specs(tm, d),
        out_shape=_token_out_shapes(b, s, d),
        compiler_params=_cparams(("parallel", "parallel")),
        name="gmlp_router",
    )(x, n_pre1.reshape(1, d), sh1, sc1, w_in.astype(BF16), vn_w.reshape(1, -1), vn_b.reshape(1, -1),
      w_s.astype(BF16), bs_full, w_out.astype(BF16), n_post.reshape(1, d), g1, n_pre.reshape(1, d), sh2, sc2,
      rw_pad, rb_pad)


MOE_ROWS = 256


def _expert_kernel(be_ref, na_ref, x_ref, wgu_ref, bgu_ref, wd_ref, bd_ref, y_ref, wgu_bf, wd_bf):
    i = pl.program_id(0)
    prev = be_ref[jnp.maximum(i - 1, 0)]
    changed = jnp.logical_or(i == 0, be_ref[i] != prev)

    @pl.when(changed)
    def _():
        wgu_bf[...] = wgu_ref[...].astype(BF16)
        wd_bf[...] = wd_ref[...].astype(BF16)

    @pl.when(i < na_ref[0])
    def _():
        gu = _dot(x_ref[...], wgu_bf[...]) + bgu_ref[...]
        x_glu = jnp.minimum(gu[:, :EXPERT_FF], SWIGLU_LIMIT)
        x_lin = jnp.clip(gu[:, EXPERT_FF:], -SWIGLU_LIMIT, SWIGLU_LIMIT)
        act = x_glu * jax.nn.sigmoid(SWIGLU_ALPHA * x_glu) * (x_lin + 1.0)
        y_ref[...] = (_dot(act.astype(BF16), wd_bf[...]) + bd_ref[...]).astype(BF16)

    @pl.when(i >= na_ref[0])
    def _():
        y_ref[...] = jnp.zeros(y_ref.shape, BF16)


def _experts(blk_expert, n_active, xg, w_gu, b_gu, w_down, b_down):
    n_rows, d = xg.shape
    n_e, _, ff2 = w_gu.shape
    tb = MOE_ROWS
    n_blocks = n_rows // tb
    return pl.pallas_call(
        _expert_kernel,
        grid_spec=pltpu.PrefetchScalarGridSpec(
            num_scalar_prefetch=2,
            grid=(n_blocks,),
            in_specs=[pl.BlockSpec((tb, d), lambda i, be, na: (i, 0)),
                      pl.BlockSpec((None, d, ff2), lambda i, be, na: (be[i], 0, 0)),
                      pl.BlockSpec((None, 1, ff2), lambda i, be, na: (be[i], 0, 0)),
                      pl.BlockSpec((None, ff2 // 2, d), lambda i, be, na: (be[i], 0, 0)),
                      pl.BlockSpec((None, 1, d), lambda i, be, na: (be[i], 0, 0))],
            out_specs=pl.BlockSpec((tb, d), lambda i, be, na: (i, 0)),
            scratch_shapes=[pltpu.VMEM((d, ff2), BF16), pltpu.VMEM((ff2 // 2, d), BF16)]),
        out_shape=jax.ShapeDtypeStruct((n_rows, d), BF16),
        compiler_params=_cparams(("arbitrary",)),
        name="moe_experts",
    )(blk_expert, n_active, xg, w_gu, b_gu.reshape(n_e, 1, ff2), w_down, b_down.reshape(n_e, 1, d))


def _combine_kernel(x_ref, y_ref, gate_ref, npost_ref, g2_ref, o_ref):
    gate = gate_ref[...]
    f = jnp.zeros(x_ref.shape, F32)
    for kth in range(TOP_K):
        f = f + gate[:, kth:kth + 1] * y_ref[kth].astype(F32)
    o_ref[...] = x_ref[...] + g2_ref[...] * (_rms(f) * npost_ref[...])


def _combine(x1, yk, gates, n_post, g2):
    b, s, d = x1.shape
    tm = ROW_TILE
    row = lambda bi, i: (bi, i, 0)
    return pl.pallas_call(
        _combine_kernel,
        grid=(b, s // tm),
        in_specs=[pl.BlockSpec((None, tm, d), row),
                  pl.BlockSpec((TOP_K, None, tm, d), lambda bi, i: (0, bi, i, 0)),
                  pl.BlockSpec((None, tm, LOGIT_PAD), row),
                  pl.BlockSpec((1, d), lambda bi, i: (0, 0)),
                  pl.BlockSpec((None, 1, d), lambda bi, i: (bi, 0, 0))],
        out_specs=pl.BlockSpec((None, tm, d), row),
        out_shape=jax.ShapeDtypeStruct((b, s, d), F32),
        compiler_params=_cparams(("parallel", "parallel")),
        name="moe_combine_residual",
    )(x1, yk, gates, n_post.reshape(1, d), g2)


def _moe(x1, hf, top_idx, gates, w_gu, b_gu, w_down, b_down, n_post, g2):
    b, s, d = x1.shape
    n_tok = b * s
    tb = MOE_ROWS
    n_assign = n_tok * TOP_K
    e_flat = top_idx[..., :TOP_K].reshape(-1)
    order = jnp.argsort(e_flat)
    e_sorted = e_flat[order]
    counts = jnp.zeros((N_EXPERTS,), jnp.int32).at[e_flat].add(1)
    padded = (counts + tb - 1) // tb * tb
    start = jnp.cumsum(counts) - counts
    pend = jnp.cumsum(padded)
    pstart = pend - padded
    dest = pstart[e_sorted] + jnp.arange(n_assign, dtype=jnp.int32) - start[e_sorted]
    n_rows = -(-n_assign // tb) * tb + N_EXPERTS * tb
    n_blocks = n_rows // tb
    row_tok = jnp.zeros((n_rows,), jnp.int32).at[dest].set((order // TOP_K).astype(jnp.int32))
    pos = jnp.zeros((n_assign,), jnp.int32).at[order].set(dest)
    blk_start = jnp.arange(n_blocks, dtype=jnp.int32) * tb
    blk_expert = jnp.minimum(jnp.searchsorted(pend, blk_start, side='right'), N_EXPERTS - 1).astype(jnp.int32)
    n_active = (pend[-1] // tb).astype(jnp.int32).reshape(1)
    xg = jnp.take(hf.reshape(n_tok, d), row_tok, axis=0)
    y = _experts(blk_expert, n_active, xg, w_gu, b_gu, w_down, b_down)
    yk = jnp.take(y, pos.reshape(n_tok, TOP_K).T, axis=0).reshape(TOP_K, b, s, d)
    return _combine(x1, yk, gates, n_post, g2)


def _block_diag(n, blk, val):
    return (jnp.kron(jnp.eye(n // blk, dtype=F32), jnp.ones((blk, blk), F32)) * val).astype(BF16)


def kernel(x, c, ctx, c_ctx, ada_w, ada_b, norm_mix_pre, norm_mix_post, norm_ffn_pre, norm_ffn_post, router_w, router_b, moe_w_gu, moe_b_gu, moe_w_down, moe_b_down, hy_w_in, mla_q_norm, mla_w_uq, mla_kv_norm, mla_w_ukv, rwkv_mu_prev, rwkv_mu_next, rwkv_w0, rwkv_w2, rwkv_a0, rwkv_a2, rwkv_g2, rwkv_k_k, rwkv_k_a, rwkv_r_k, rwkv_ln_w, rwkv_ln_b, hy_w_out, gm_w_in, gm_v_norm_w, gm_v_norm_b, gm_w_s, gm_b_s, gm_w_out):
    b, s, d = x.shape
    n_ctx = ctx.shape[1]
    t_all = s + n_ctx
    assert b + 1 <= 8 and s % ROW_TILE == 0 and n_ctx % ROW_TILE == 0

    cond_rows = jnp.concatenate([c, c_ctx[None], jnp.zeros((8 - b - 1, d), F32)], axis=0)
    mod = _modulation(cond_rows, ada_w, ada_b)

    def lat_mod(l, j):
        return mod[l, :b, j * d:(j + 1) * d].reshape(b, 1, d)

    sh_all = jnp.stack([mod[0, :b, 0:d], jnp.broadcast_to(mod[0, b, 0:d], (b, d))], axis=1).reshape(2 * b, 1, d)
    sc_all = jnp.stack([mod[0, :b, d:2 * d], jnp.broadcast_to(mod[0, b, d:2 * d], (b, d))], axis=1).reshape(2 * b, 1, d)
    x_all = jnp.concatenate([x, ctx], axis=1)
    cos, sin = _rope_tables(s, n_ctx)
    w1, wq2, wk2 = _inproj_weights(hy_w_in[0], mla_w_uq[0], mla_w_ukv[0])
    q, k, v, prw = _inproj(x_all, s, norm_mix_pre[0], sh_all, sc_all, cos, sin, w1, mla_q_norm[0], mla_kv_norm[0],
                           wq2, wk2)
    tq = 512 if s % 512 == 0 else ROW_TILE
    tk = 1280 if t_all % 1280 == 0 else ROW_TILE
    o_attn = _attention(q, k, v, s, tq, tk)

    bd_ones = _block_diag(RWKV_DIM, RWKV_HEAD, 1.0)
    bd_mean = _block_diag(RWKV_DIM, RWKV_HEAD, 1.0 / RWKV_HEAD)
    r, kx, vx, g, kkn, lw, kd, bdir = _rwkv_prepare(prw, s, rwkv_mu_prev[0], rwkv_mu_next[0], rwkv_w0[0], rwkv_w2[0],
                                                    rwkv_a0[0], rwkv_a2[0], rwkv_g2[0], rwkv_k_k[0], rwkv_k_a[0],
                                                    bd_ones)
    m_c, n_c, r_c, y_c = _wkv_chunks(r, vx, kkn, lw, kd, bdir)
    n_lat_ch = s // WKV_CHUNK
    n_ctx_ch = n_ctx // WKV_CHUNK
    lat_ch = np.arange(n_lat_ch)
    ctx_ch = n_lat_ch + np.arange(n_ctx_ch)
    order = jnp.asarray(np.stack([np.concatenate([ctx_ch, lat_ch]),
                                  np.concatenate([ctx_ch[::-1], lat_ch[::-1]])]).astype(np.int32))
    yf, yb = _wkv_states(order, m_c, n_c, r_c, y_c, t_all)

    rw_pad, rb_pad = _router_pads(router_w[0], router_b[0])
    x1, hf, top_idx, gates = _mix_out(yf, yb, r, kx, vx, g, o_attn, x, rwkv_r_k[0].reshape(-1), rwkv_ln_w[0],
                                      rwkv_ln_b[0], bd_mean, bd_ones, hy_w_out[0].astype(BF16), norm_mix_post[0],
                                      lat_mod(0, 2), norm_ffn_pre[0], lat_mod(0, 3), lat_mod(0, 4), rw_pad, rb_pad)
    x2 = _moe(x1, hf, top_idx, gates, moe_w_gu[0], moe_b_gu[0], moe_w_down[0], moe_b_down[0], norm_ffn_post[0],
              lat_mod(0, 5))

    rw_pad, rb_pad = _router_pads(router_w[1], router_b[1])
    x3, hf, top_idx, gates = _gmlp(x2, norm_mix_pre[1], lat_mod(1, 0), lat_mod(1, 1), gm_w_in[0], gm_v_norm_w[0],
                                   gm_v_norm_b[0], gm_w_s[0], gm_b_s[0], gm_w_out[0], norm_mix_post[1], lat_mod(1, 2),
                                   norm_ffn_pre[1], lat_mod(1, 3), lat_mod(1, 4), rw_pad, rb_pad)
    return _moe(x3, hf, top_idx, gates, moe_w_gu[1], moe_b_gu[1], moe_w_down[1], moe_b_down[1], norm_ffn_post[1],
                lat_mod(1, 5))
```

```python
import functools

import jax
import jax.numpy as jnp
import numpy as np
from jax import lax
from jax.experimental import pallas as pl
from jax.experimental.pallas import tpu as pltpu

F32 = jnp.float32
BF16 = jnp.bfloat16
HIGHEST = lax.Precision.HIGHEST

D_MODEL = 1024
GRID_W = 64
EPS = 1e-6

MLA_HEADS = 8
MLA_NOPE = 64
MLA_ROPE = 32
MLA_V = 64
MLA_Q_RANK = 256
MLA_KV_RANK = 128
MLA_SCALE = (MLA_NOPE + MLA_ROPE) ** -0.5
ROPE_FREQS = MLA_ROPE // 4
ROPE_BASE = 10000.0
HEAD_PAD = 128
ATT_QSUB = 128
ATT_KSUB = 640
LOG2E = 1.4426950408889634
ATT_EXP_DTYPE = jnp.float32

RWKV_HEAD = 64
RWKV_DIM = 512
RWKV_HEADS = 8
DECAY_LORA = 32
AAA_LORA = 32
GATE_LORA = 96
RWKV_PAD_COLS = 3 * RWKV_DIM + 128 + 128
RWKV_LN_EPS = 64e-5
WKV_CHUNK = 64
WKV_CHUNKS_PER_STEP = 2

CHUNK = 128
GM_WIDTH = 1024
GM_GROUPS = 8
LN_EPS = 1e-5

N_EXPERTS = 32
TOP_K = 4
EXPERT_FF = 1024
SWIGLU_LIMIT = 7.0
SWIGLU_ALPHA = 1.702
LOGIT_PAD = 128
NEG_BIG = -1e30

ROW_TILE = 256
VMEM_LIMIT = 48 * 1024 * 1024


def _cparams(sem):
    return pltpu.CompilerParams(dimension_semantics=sem, vmem_limit_bytes=VMEM_LIMIT)


def _rms(x):
    return x * lax.rsqrt(jnp.mean(x * x, axis=-1, keepdims=True) + EPS)


def _dot(a, b, precision=None):
    return jnp.dot(a, b, preferred_element_type=F32, precision=precision)


def _dot_nt(a, b, precision=None):
    return lax.dot_general(a, b, (((1,), (1,)), ((), ())), preferred_element_type=F32, precision=precision)


def _dot_tn(a, b, precision=None):
    return lax.dot_general(a, b, (((0,), (0,)), ((), ())), preferred_element_type=F32, precision=precision)


def _split(x):
    hi = x.astype(BF16)
    return hi, (x - hi.astype(F32)).astype(BF16)


def _dot_split(a, b):
    a_hi, a_lo = _split(a)
    b_hi, b_lo = _split(b)
    return _dot(jnp.concatenate([a_hi, a_hi, a_lo], axis=1), jnp.concatenate([b_hi, b_lo, b_hi], axis=0))


def _mod_kernel(s_ref, w_ref, b_ref, o_ref):
    s = s_ref[...]
    s = s * jax.nn.sigmoid(s)
    o_ref[...] = _dot(s, w_ref[...], HIGHEST) + b_ref[...]


def _modulation(cond_rows, ada_w, ada_b):
    n_l, d, n6 = ada_w.shape
    tn = 1536
    return pl.pallas_call(
        _mod_kernel,
        grid=(n_l, n6 // tn),
        in_specs=[pl.BlockSpec((8, d), lambda l, j: (0, 0)),
                  pl.BlockSpec((None, d, tn), lambda l, j: (l, 0, j)),
                  pl.BlockSpec((None, 1, tn), lambda l, j: (l, 0, j))],
        out_specs=pl.BlockSpec((None, 8, tn), lambda l, j: (l, 0, j)),
        out_shape=jax.ShapeDtypeStruct((n_l, 8, n6), F32),
        compiler_params=_cparams(("arbitrary", "arbitrary")),
        name="adaln_mod",
    )(cond_rows, ada_w, ada_b.reshape(n_l, 1, n6))


def _inproj_kernel(x_ref, g_ref, sh_ref, sc_ref, cos_ref, sin_ref, w1_ref, qn_ref, kvn_ref, wq_ref, wk_ref,
                   q_ref, k_ref, vt_ref, prw_ref):
    h = _rms(x_ref[...]) * g_ref[...]
    h = h * (1.0 + sc_ref[...]) + sh_ref[...]
    p = _dot(h.astype(BF16), w1_ref[...])
    cos = jnp.concatenate([cos_ref[...]] * MLA_HEADS, axis=1)
    sin = jnp.concatenate([sin_ref[...]] * MLA_HEADS, axis=1)
    nq = MLA_HEADS * HEAD_PAD
    qn = _rms(p[:, :MLA_Q_RANK]) * qn_ref[...]
    qq = _dot(qn.astype(BF16), wq_ref[...])
    q_ref[...] = (qq[:, :nq] * cos + qq[:, nq:] * sin).astype(BF16)
    kvn = _rms(p[:, MLA_Q_RANK:MLA_Q_RANK + MLA_KV_RANK]) * kvn_ref[...]
    x2 = jnp.concatenate([kvn, p[:, MLA_Q_RANK + MLA_KV_RANK:512]], axis=1).astype(BF16)
    kk = _dot(x2, wk_ref[...])
    k_ref[...] = (kk[:, :nq] * cos + kk[:, nq:2 * nq] * sin).astype(BF16)
    vt_ref[...] = kk[:, 2 * nq:].T.astype(BF16)
    prw_ref[...] = p[:, 512:]


def _rotate_half_cols(w):
    wr = w.reshape(w.shape[:-1] + (2, 2, ROPE_FREQS))
    return jnp.stack([-wr[..., 1, :], wr[..., 0, :]], axis=-2).reshape(w.shape)


def _inproj_weights(hy_w_in, w_uq, w_ukv):
    d = hy_w_in.shape[0]
    mla_cols = MLA_Q_RANK + MLA_KV_RANK + MLA_ROPE
    w_mla = hy_w_in[:, :mla_cols]
    w_rw = hy_w_in[:, mla_cols:]
    w1 = jnp.concatenate([w_mla, jnp.zeros((d, 512 - mla_cols), F32), _pad_rwkv_cols(w_rw)], axis=1).astype(BF16)
    wq = (w_uq * (MLA_SCALE * LOG2E)).reshape(MLA_Q_RANK, MLA_HEADS, MLA_NOPE + MLA_ROPE)
    z_n = jnp.zeros((MLA_Q_RANK, MLA_HEADS, MLA_NOPE), F32)
    z_p = jnp.zeros((MLA_Q_RANK, MLA_HEADS, HEAD_PAD - MLA_NOPE - MLA_ROPE), F32)
    wq_plain = jnp.concatenate([wq[..., :MLA_NOPE], wq[..., MLA_NOPE:], z_p], axis=-1)
    wq_rot = jnp.concatenate([z_n, _rotate_half_cols(wq[..., MLA_NOPE:]), z_p], axis=-1)
    wq2 = jnp.concatenate([wq_plain.reshape(MLA_Q_RANK, -1), wq_rot.reshape(MLA_Q_RANK, -1)], axis=1).astype(BF16)
    wkv = w_ukv.reshape(MLA_KV_RANK, MLA_HEADS, MLA_NOPE + MLA_V)
    eye = jnp.broadcast_to(jnp.eye(MLA_ROPE, dtype=F32)[:, None, :], (MLA_ROPE, MLA_HEADS, MLA_ROPE))
    pad_k = HEAD_PAD - MLA_NOPE - MLA_ROPE
    top_plain = jnp.concatenate([wkv[..., :MLA_NOPE], jnp.zeros((MLA_KV_RANK, MLA_HEADS, MLA_ROPE + pad_k), F32)], -1)
    mid_plain = jnp.concatenate([jnp.zeros((MLA_ROPE, MLA_HEADS, MLA_NOPE), F32), eye,
                                 jnp.zeros((MLA_ROPE, MLA_HEADS, pad_k), F32)], -1)
    mid_rot = jnp.concatenate([jnp.zeros((MLA_ROPE, MLA_HEADS, MLA_NOPE), F32), _rotate_half_cols(eye),
                               jnp.zeros((MLA_ROPE, MLA_HEADS, pad_k), F32)], -1)
    nq = MLA_HEADS * HEAD_PAD
    rows_c = jnp.concatenate([top_plain.reshape(MLA_KV_RANK, nq), jnp.zeros((MLA_KV_RANK, nq), F32),
                              wkv[..., MLA_NOPE:].reshape(MLA_KV_RANK, MLA_HEADS * MLA_V)], axis=1)
    rows_r = jnp.concatenate([mid_plain.reshape(MLA_ROPE, nq), mid_rot.reshape(MLA_ROPE, nq),
                              jnp.zeros((MLA_ROPE, MLA_HEADS * MLA_V), F32)], axis=1)
    rows_z = jnp.zeros((256 - MLA_KV_RANK - MLA_ROPE, rows_c.shape[1]), F32)
    wk2 = jnp.concatenate([rows_c, rows_r, rows_z], axis=0).astype(BF16)
    return w1, wq2, wk2


def _pad_rwkv_cols(w):
    pad = jnp.zeros(w.shape[:-1] + (RWKV_PAD_COLS - w.shape[-1],), w.dtype)
    return jnp.concatenate([w, pad], axis=-1)


def _rope_tables(n_lat, n_ctx):
    t = jnp.arange(n_lat)
    row = (t // GRID_W).astype(F32)
    col = (t % GRID_W).astype(F32)
    inv = ROPE_BASE ** (-jnp.arange(ROPE_FREQS, dtype=F32) / ROPE_FREQS)
    ang = jnp.stack([row[:, None] * inv, col[:, None] * inv], axis=1)
    ang = jnp.broadcast_to(ang[:, :, None, :], (n_lat, 2, 2, ROPE_FREQS)).reshape(n_lat, MLA_ROPE)
    cos = jnp.concatenate([jnp.ones((n_lat, MLA_NOPE), F32), jnp.cos(ang),
                           jnp.ones((n_lat, HEAD_PAD - MLA_NOPE - MLA_ROPE), F32)], axis=1)
    sin = jnp.concatenate([jnp.zeros((n_lat, MLA_NOPE), F32), jnp.sin(ang),
                           jnp.zeros((n_lat, HEAD_PAD - MLA_NOPE - MLA_ROPE), F32)], axis=1)
    cos = jnp.concatenate([cos, jnp.ones((n_ctx, HEAD_PAD), F32)], axis=0)
    sin = jnp.concatenate([sin, jnp.zeros((n_ctx, HEAD_PAD), F32)], axis=0)
    return cos, sin


def _inproj(x_all, n_lat, gain, shift, scale, cos, sin, w1, q_norm, kv_norm, wq2, wk2):
    b, t_all, d = x_all.shape
    tm = ROW_TILE
    n_lat_tiles = n_lat // tm
    nq = MLA_HEADS * HEAD_PAD

    def mod_map(bi, i):
        return (2 * bi + jnp.where(i >= n_lat_tiles, 1, 0), 0, 0)

    const = lambda bi, i: (0, 0)
    row = lambda bi, i: (bi, i, 0)
    return pl.pallas_call(
        _inproj_kernel,
        grid=(b, t_all // tm),
        in_specs=[pl.BlockSpec((None, tm, d), row),
                  pl.BlockSpec((1, d), const),
                  pl.BlockSpec((None, 1, d), mod_map),
                  pl.BlockSpec((None, 1, d), mod_map),
                  pl.BlockSpec((tm, HEAD_PAD), lambda bi, i: (i, 0)),
                  pl.BlockSpec((tm, HEAD_PAD), lambda bi, i: (i, 0)),
                  pl.BlockSpec(w1.shape, const),
                  pl.BlockSpec((1, MLA_Q_RANK), const),
                  pl.BlockSpec((1, MLA_KV_RANK), const),
                  pl.BlockSpec(wq2.shape, const),
                  pl.BlockSpec(wk2.shape, const)],
        out_specs=[pl.BlockSpec((None, tm, nq), row),
                   pl.BlockSpec((None, tm, nq), row),
                   pl.BlockSpec((None, MLA_HEADS * MLA_V, tm), lambda bi, i: (bi, 0, i)),
                   pl.BlockSpec((None, tm, RWKV_PAD_COLS), row)],
        out_shape=[jax.ShapeDtypeStruct((b, t_all, nq), BF16),
                   jax.ShapeDtypeStruct((b, t_all, nq), BF16),
                   jax.ShapeDtypeStruct((b, MLA_HEADS * MLA_V, t_all), BF16),
                   jax.ShapeDtypeStruct((b, t_all, RWKV_PAD_COLS), F32)],
        compiler_params=_cparams(("parallel", "parallel")),
        name="hybrid_inproj",
    )(x_all, gain.reshape(1, d), shift, scale, cos, sin, w1, q_norm.reshape(1, -1), kv_norm.reshape(1, -1), wq2, wk2)


def _attn_kernel(q_ref, k_ref, vt_ref, o_ref, m_sc, l_sc, acc_sc):
    ki = pl.program_id(3)

    @pl.when(ki == 0)
    def _():
        m_sc[...] = jnp.full(m_sc.shape, -jnp.inf, F32)
        l_sc[...] = jnp.zeros(l_sc.shape, F32)
        acc_sc[...] = jnp.zeros(acc_sc.shape, F32)

    tq, tk = q_ref.shape[0], k_ref.shape[0]
    chains = [(h, qs) for h in range(2) for qs in range(tq // ATT_QSUB)]
    hsl = [slice(h * HEAD_PAD, (h + 1) * HEAD_PAD) for h, _ in chains]
    vsl = [slice(h * MLA_V, (h + 1) * MLA_V) for h, _ in chains]
    qsl = [slice(qs * ATT_QSUB, (qs + 1) * ATT_QSUB) for _, qs in chains]
    nch = len(chains)
    q = [q_ref[qsl[c], hsl[c]] for c in range(nch)]
    m = [m_sc[chains[c][0], 0:1, qsl[c]] for c in range(nch)]
    l = [l_sc[chains[c][0], 0:1, qsl[c]] for c in range(nch)]
    acc = [acc_sc[chains[c][0], :, qsl[c]] for c in range(nch)]
    ksub = ATT_KSUB if tk % ATT_KSUB == 0 else tk
    for kb in range(tk // ksub):
        ksl = slice(kb * ksub, (kb + 1) * ksub)
        s = [_dot_nt(k_ref[ksl, hsl[c]], q[c]).astype(ATT_EXP_DTYPE) for c in range(nch)]
        m_new = [jnp.maximum(m[c], jnp.max(s[c], axis=0, keepdims=True).astype(F32)) for c in range(nch)]
        alpha = [jnp.exp2(m[c] - m_new[c]) for c in range(nch)]
        p = [jnp.exp2(s[c] - m_new[c].astype(ATT_EXP_DTYPE)) for c in range(nch)]
        l = [alpha[c] * l[c] + jnp.sum(p[c].astype(F32), axis=0, keepdims=True) for c in range(nch)]
        acc = [acc[c] * alpha[c] + _dot(vt_ref[vsl[c], ksl], p[c].astype(BF16)) for c in range(nch)]
        m = m_new
    for c in range(nch):
        h = chains[c][0]
        m_sc[h, :, qsl[c]] = jnp.broadcast_to(m[c], (8, ATT_QSUB))
        l_sc[h, :, qsl[c]] = jnp.broadcast_to(l[c], (8, ATT_QSUB))
        acc_sc[h, :, qsl[c]] = acc[c]

    @pl.when(ki == pl.num_programs(3) - 1)
    def _():
        o0 = (acc_sc[0] / l_sc[0, 0:1, :]).T
        o1 = (acc_sc[1] / l_sc[1, 0:1, :]).T
        o_ref[...] = jnp.concatenate([o0, o1], axis=1).astype(BF16)


def _attention(q, k, vt, n_lat, tq, tk):
    b, t_all, _ = k.shape
    return pl.pallas_call(
        _attn_kernel,
        grid=(b, MLA_HEADS // 2, n_lat // tq, t_all // tk),
        in_specs=[pl.BlockSpec((None, tq, 2 * HEAD_PAD), lambda bi, hp, qi, ki: (bi, qi, hp)),
                  pl.BlockSpec((None, tk, 2 * HEAD_PAD), lambda bi, hp, qi, ki: (bi, ki, hp)),
                  pl.BlockSpec((None, 2 * MLA_V, tk), lambda bi, hp, qi, ki: (bi, hp, ki))],
        out_specs=pl.BlockSpec((None, tq, 2 * MLA_V), lambda bi, hp, qi, ki: (bi, qi, hp)),
        out_shape=jax.ShapeDtypeStruct((b, n_lat, MLA_HEADS * MLA_V), BF16),
        scratch_shapes=[pltpu.VMEM((2, 8, tq), F32), pltpu.VMEM((2, 8, tq), F32), pltpu.VMEM((2, MLA_V, tq), F32)],
        compiler_params=_cparams(("parallel", "parallel", "parallel", "arbitrary")),
        name="mla_attention",
    )(q, k, vt)


def _seg_sum(x, bd_ref):
    hi = x.astype(BF16)
    lo = (x - hi.astype(F32)).astype(BF16)
    bd = bd_ref[...]
    return _dot(hi, bd) + _dot(lo, bd)


def _rwkv_prep_kernel(p_ref, prev_ref, next_ref, mup_ref, mun_ref, wl_ref, w0a0_ref, g2_ref, kk_ref, ka_ref, bd_ref,
                      r_ref, k_ref, v_ref, g_ref, kkn_ref, lw_ref, kd_ref, bdir_ref, *, n_lat_tiles, n_tiles):
    i = pl.program_id(1)
    p = p_ref[...]
    tm = p.shape[0]
    first = jnp.logical_or(i == 0, i == n_lat_tiles)
    last = jnp.logical_or(i == n_lat_tiles - 1, i == n_tiles - 1)
    prev_row = jnp.where(first, 0.0, prev_ref[7:8, :])
    next_row = jnp.where(last, 0.0, next_ref[0:1, :])
    ridx = lax.broadcasted_iota(jnp.int32, p.shape, 0)
    prev = jnp.where(ridx == 0, prev_row, pltpu.roll(p, 1, 0))
    nxt = jnp.where(ridx == tm - 1, next_row, pltpu.roll(p, tm - 1, 0))
    p = p + mup_ref[...] * (prev - p) + mun_ref[...] * (nxt - p)
    c = RWKV_DIM
    r, k, v = p[:, :c], p[:, c:2 * c], p[:, 2 * c:3 * c]
    lo = p[:, 3 * c:3 * c + 128]
    lane = lax.broadcasted_iota(jnp.int32, lo.shape, 1)
    lo = jnp.where(lane < 2 * DECAY_LORA, jnp.tanh(lo), lo)
    wa = _dot(lo.astype(BF16), wl_ref[...]) + w0a0_ref[...]
    gd = p[:, 3 * c + 128:]
    g_ref[...] = _dot(jax.nn.sigmoid(gd).astype(BF16), g2_ref[...])
    kk = k * kk_ref[...]
    kk = kk * lax.rsqrt(jnp.maximum(_seg_sum(kk * kk, bd_ref), 1e-24))
    r_ref[...] = r
    k_ref[...] = k
    v_ref[...] = v
    kkn_ref[...] = kk
    ka = ka_ref[...]
    for d in range(2):
        w = wa[:, d * c:(d + 1) * c]
        a = jax.nn.sigmoid(wa[:, (2 + d) * c:(3 + d) * c])
        lw_ref[:, d * c:(d + 1) * c] = -float(np.exp(-0.5)) * jax.nn.sigmoid(w)
        kd_ref[:, d * c:(d + 1) * c] = k * (1.0 + (a - 1.0) * ka)
        bdir_ref[:, d * c:(d + 1) * c] = kk * a


def _rwkv_prepare(prw, n_lat, mu_prev, mu_next, w0, w2, a0, a2, g2, k_k, k_a, bd_ones):
    b, t_all, pc = prw.shape
    tm = ROW_TILE
    n_tiles = t_all // tm
    n_lat_tiles = n_lat // tm
    c = RWKV_DIM
    z = jnp.zeros((DECAY_LORA, c), F32)
    wl = jnp.concatenate([
        jnp.concatenate([w2[0], z, z, z], axis=1), jnp.concatenate([z, w2[1], z, z], axis=1),
        jnp.concatenate([z, z, a2[0], z], axis=1), jnp.concatenate([z, z, z, a2[1]], axis=1)], axis=0).astype(BF16)
    w0a0 = jnp.concatenate([w0[0], w0[1], a0[0], a0[1]]).reshape(1, 4 * c)
    g2p = jnp.concatenate([g2, jnp.zeros((128 - GATE_LORA, c), F32)], axis=0).astype(BF16)
    row = lambda bi, i: (bi, i, 0)
    const = lambda bi, i: (0, 0)
    hb = tm // 8
    n_hb = t_all // 8
    kern = functools.partial(_rwkv_prep_kernel, n_lat_tiles=n_lat_tiles, n_tiles=n_tiles)
    o_c = jax.ShapeDtypeStruct((b, t_all, c), F32)
    o_2c = jax.ShapeDtypeStruct((b, t_all, 2 * c), F32)
    return pl.pallas_call(
        kern,
        grid=(b, n_tiles),
        in_specs=[pl.BlockSpec((None, tm, pc), row),
                  pl.BlockSpec((None, 8, pc), lambda bi, i: (bi, jnp.maximum(i * hb - 1, 0), 0)),
                  pl.BlockSpec((None, 8, pc), lambda bi, i: (bi, jnp.minimum((i + 1) * hb, n_hb - 1), 0)),
                  pl.BlockSpec((1, pc), const), pl.BlockSpec((1, pc), const),
                  pl.BlockSpec(wl.shape, const), pl.BlockSpec((1, 4 * c), const),
                  pl.BlockSpec(g2p.shape, const), pl.BlockSpec((1, c), const), pl.BlockSpec((1, c), const),
                  pl.BlockSpec((c, c), const)],
        out_specs=[pl.BlockSpec((None, tm, c), row)] * 5 + [pl.BlockSpec((None, tm, 2 * c), row)] * 3,
        out_shape=[o_c] * 5 + [o_2c] * 3,
        compiler_params=_cparams(("parallel", "parallel")),
        name="rwkv_prepare",
    )(prw, prw, prw, _pad_rwkv_cols(mu_prev).reshape(1, pc), _pad_rwkv_cols(mu_next).reshape(1, pc),
      wl, w0a0, g2p, k_k.reshape(1, c), k_a.reshape(1, c), bd_ones)


def _wkv_chunk_kernel(r_ref, v_ref, kk_ref, lw_ref, kd_ref, bd_ref, m_ref, n_ref, rr_ref, yv_ref):
    d = pl.program_id(1)
    cs = WKV_CHUNK
    nc = r_ref.shape[0] // cs
    ti = lax.broadcasted_iota(jnp.int32, (cs, cs), 0)
    si = lax.broadcasted_iota(jnp.int32, (cs, cs), 1)
    rel = (si - ti) * (1 - 2 * d)
    incl = rel <= 0
    strict = rel < 0
    eye = si == ti
    incl_f = incl.astype(F32)
    ops = []
    for ci in range(nc):
        rows = slice(ci * cs, (ci + 1) * cs)
        lw = lw_ref[rows, :]
        g = _dot(incl_f, lw, HIGHEST)
        total = jnp.sum(lw, axis=0, keepdims=True)
        e_inv = jnp.exp(-g)
        e_end = jnp.exp(total - g)
        gam = jnp.exp(total)
        kd = kd_ref[rows, :]
        bd = bd_ref[rows, :]
        at = -kk_ref[rows, :] * jnp.exp(g - lw)
        rt = r_ref[rows, :] * jnp.exp(g)
        kt = (kd * e_inv).astype(BF16)
        bt = (bd * e_inv).astype(BF16)
        ke = (kd * e_end).astype(BF16)
        be = (bd * e_end).astype(BF16)
        v = v_ref[rows, :].astype(BF16)
        for h in range(RWKV_HEADS):
            sl = slice(h * RWKV_HEAD, (h + 1) * RWKV_HEAD)
            ops.append((at[:, sl], rt[:, sl], kt[:, sl], bt[:, sl], ke[:, sl], be[:, sl], v[:, sl], gam[:, sl]))
    n_it = len(ops)
    aa = [_dot_nt(jnp.concatenate([o[0], o[1]], axis=0).astype(BF16), jnp.concatenate([o[3], o[2]], axis=0))
          for o in ops]
    a_ab = [jnp.where(strict, x[:cs, :cs], 0.0) for x in aa]
    a_ak = [jnp.where(strict, x[:cs, cs:], 0.0).astype(BF16) for x in aa]
    a_rb = [jnp.where(incl, x[cs:, :cs], 0.0).astype(BF16) for x in aa]
    a_rk = [jnp.where(incl, x[cs:, cs:], 0.0).astype(BF16) for x in aa]
    z = [jnp.concatenate([ops[i][0], _dot(a_ak[i], ops[i][6])], axis=1) for i in range(n_it)]
    pw = a_ab
    n_sq = int(np.log2(cs))
    for it in range(n_sq):
        more = it + 1 < n_sq
        res = [_dot_split(pw[i], jnp.concatenate([z[i], pw[i]], axis=1) if more else z[i]) for i in range(n_it)]
        z = [z[i] + res[i][:, :2 * cs] for i in range(n_it)]
        if more:
            pw = [x[:, 2 * cs:] for x in res]
    zb = [x.astype(BF16) for x in z]
    w2 = [_dot(a_rb[i], zb[i]) for i in range(n_it)]
    w3 = [_dot_tn(ops[i][5], zb[i]) for i in range(n_it)]
    n0 = [_dot_tn(ops[i][4], ops[i][6]) for i in range(n_it)]
    y0 = [_dot(a_rk[i], ops[i][6]) for i in range(n_it)]
    nh = RWKV_HEADS
    for ci in range(nc):
        rows = slice(ci * cs, (ci + 1) * cs)
        ids = range(ci * nh, (ci + 1) * nh)
        m_ref[rows, :] = jnp.concatenate([jnp.where(eye, ops[i][7], 0.0) + w3[i][:, :cs] for i in ids], axis=1)
        n_ref[rows, :] = jnp.concatenate([n0[i] + w3[i][:, cs:] for i in ids], axis=1)
        rr_ref[rows, :] = jnp.concatenate([ops[i][1] + w2[i][:, :cs] for i in ids], axis=1)
        yv_ref[rows, :] = jnp.concatenate([y0[i] + w2[i][:, cs:] for i in ids], axis=1)


def _wkv_chunks(r, v, kkn, lw, kd, bdir):
    b, t_all, c = r.shape
    rows = WKV_CHUNK * WKV_CHUNKS_PER_STEP
    nst = t_all // rows
    shared = pl.BlockSpec((None, rows, c), lambda bi, d, ci: (bi, ci, 0))
    per_dir = pl.BlockSpec((None, rows, c), lambda bi, d, ci: (bi, ci, d))
    out = pl.BlockSpec((None, None, rows, c), lambda bi, d, ci: (bi, d, ci, 0))
    o_s = jax.ShapeDtypeStruct((b, 2, t_all, c), F32)
    return pl.pallas_call(
        _wkv_chunk_kernel,
        grid=(b, 2, nst),
        in_specs=[shared, shared, shared, per_dir, per_dir, per_dir],
        out_specs=[out] * 4,
        out_shape=[o_s] * 4,
        compiler_params=_cparams(("parallel", "parallel", "parallel")),
        name="wkv_chunk_summaries",
    )(r, v, kkn, lw, kd, bdir)


def _wkv_state_kernel(order_ref, mf_ref, nf_ref, rf_ref, yf_ref, mb_ref, nb_ref, rb_ref, yb_ref,
                      of_ref, ob_ref, st_sc):
    del order_ref
    s = pl.program_id(0)

    @pl.when(s == 0)
    def _():
        st_sc[...] = jnp.zeros(st_sc.shape, F32)

    n_b = st_sc.shape[0]
    ins = ((mf_ref, nf_ref, rf_ref, yf_ref, of_ref), (mb_ref, nb_ref, rb_ref, yb_ref, ob_ref))
    for bi in range(n_b):
        for d in range(2):
            m_r, n_r, r_r, y_r, o_r = ins[d]
            st = st_sc[bi, d]
            m, n, rr, yv = m_r[bi], n_r[bi], r_r[bi], y_r[bi]
            ys, sts = [], []
            for h in range(RWKV_HEADS):
                sl = slice(h * RWKV_HEAD, (h + 1) * RWKV_HEAD)
                st_h = st[:, sl]
                ys.append(_dot(rr[:, sl], st_h) + yv[:, sl])
                sts.append(_dot(m[:, sl], st_h, HIGHEST) + n[:, sl])
            o_r[bi] = jnp.concatenate(ys, axis=1)
            st_sc[bi, d] = jnp.concatenate(sts, axis=1)


def _wkv_states(order, m, n, rr, yv, t_all):
    b, _, _, c = m.shape
    cs = WKV_CHUNK
    nch = t_all // cs
    fwd = pl.BlockSpec((b, None, cs, c), lambda s, o: (0, 0, o[0, s], 0))
    bwd = pl.BlockSpec((b, None, cs, c), lambda s, o: (0, 1, o[1, s], 0))
    out_f = pl.BlockSpec((b, cs, c), lambda s, o: (0, o[0, s], 0))
    out_b = pl.BlockSpec((b, cs, c), lambda s, o: (0, o[1, s], 0))
    o_s = jax.ShapeDtypeStruct((b, t_all, c), F32)
    return pl.pallas_call(
        _wkv_state_kernel,
        grid_spec=pltpu.PrefetchScalarGridSpec(
            num_scalar_prefetch=1,
            grid=(nch,),
            in_specs=[fwd] * 4 + [bwd] * 4,
            out_specs=[out_f, out_b],
            scratch_shapes=[pltpu.VMEM((b, 2, cs, c), F32)]),
        out_shape=[o_s, o_s],
        compiler_params=_cparams(("arbitrary",)),
        name="wkv_state_pass",
    )(order, m, n, rr, yv, m, n, rr, yv)


def _ffn_pre(x1, gain_ref, sh_ref, sc_ref, rw_ref, rb_ref, hf_ref, idx_ref, gate_ref):
    hf = _rms(x1) * gain_ref[...]
    hf = hf * (1.0 + sc_ref[...]) + sh_ref[...]
    hf_ref[...] = hf.astype(BF16)
    logits = _dot(hf, rw_ref[...], HIGHEST) + rb_ref[...]
    lane = lax.broadcasted_iota(jnp.int32, logits.shape, 1)
    lane_f = lane.astype(F32)
    work = logits
    idx_out = jnp.zeros(logits.shape, F32)
    val_out = jnp.full(logits.shape, NEG_BIG, F32)
    for kth in range(TOP_K):
        m = jnp.max(work, axis=1, keepdims=True)
        idx = jnp.min(jnp.where(work == m, lane_f, float(LOGIT_PAD)), axis=1, keepdims=True)
        work = jnp.where(lane_f == idx, -jnp.inf, work)
        idx_out = jnp.where(lane == kth, idx, idx_out)
        val_out = jnp.where(lane == kth, m, val_out)
    e = jnp.exp(val_out - jnp.max(val_out, axis=1, keepdims=True))
    e = jnp.where(lane < TOP_K, e, 0.0)
    idx_ref[...] = idx_out.astype(jnp.int32)
    gate_ref[...] = e / jnp.sum(e, axis=1, keepdims=True)


def _router_pads(router_w, router_b):
    d = router_w.shape[0]
    rw = jnp.concatenate([router_w, jnp.zeros((d, LOGIT_PAD - N_EXPERTS), F32)], axis=1)
    rb = jnp.concatenate([router_b, jnp.full((LOGIT_PAD - N_EXPERTS,), NEG_BIG, F32)]).reshape(1, LOGIT_PAD)
    return rw, rb


def _mix_out_kernel(yf_ref, yb_ref, r_ref, k_ref, v_ref, g_ref, o_ref, x_ref,
                    rk_ref, lnw_ref, lnb_ref, bdm_ref, bds_ref, wout_ref, npost_ref, g1_ref,
                    gpre_ref, sh_ref, sc_ref, rw_ref, rb_ref,
                    x1_ref, hf_ref, idx_ref, gate_ref):
    y = yf_ref[...] + yb_ref[...]
    mu = _seg_sum(y, bdm_ref)
    dlt = y - mu
    var = _seg_sum(dlt * dlt, bdm_ref)
    yn = dlt * lax.rsqrt(var + RWKV_LN_EPS) * lnw_ref[...] + lnb_ref[...]
    bonus = _seg_sum(r_ref[...] * k_ref[...] * rk_ref[...], bds_ref) * v_ref[...]
    rw = (yn + bonus) * g_ref[...]
    mix_in = jnp.concatenate([o_ref[...], rw.astype(BF16)], axis=1)
    mix = _dot(mix_in, wout_ref[...])
    x1 = x_ref[...] + g1_ref[...] * (_rms(mix) * npost_ref[...])
    x1_ref[...] = x1
    _ffn_pre(x1, gpre_ref, sh_ref, sc_ref, rw_ref, rb_ref, hf_ref, idx_ref, gate_ref)


def _token_out_specs(tm, d):
    row = lambda bi, i: (bi, i, 0)
    specs = [pl.BlockSpec((None, tm, d), row), pl.BlockSpec((None, tm, d), row),
             pl.BlockSpec((None, tm, LOGIT_PAD), row), pl.BlockSpec((None, tm, LOGIT_PAD), row)]
    return specs


def _token_out_shapes(b, s, d):
    return [jax.ShapeDtypeStruct((b, s, d), F32), jax.ShapeDtypeStruct((b, s, d), BF16),
            jax.ShapeDtypeStruct((b, s, LOGIT_PAD), jnp.int32), jax.ShapeDtypeStruct((b, s, LOGIT_PAD), F32)]


def _mix_out(yf, yb, r, k, v, g, o_attn, x, r_k, ln_w, ln_b, bd_mean, bd_ones, w_out, n_post, g1,
             n_pre, sh2, sc2, rw_pad, rb_pad):
    b, s, d = x.shape
    c = RWKV_DIM
    tm = ROW_TILE
    row = lambda bi, i: (bi, i, 0)
    const = lambda bi, i: (0, 0)
    per_b = lambda bi, i: (bi, 0, 0)
    rc = pl.BlockSpec((None, tm, c), row)
    vec_c = pl.BlockSpec((1, c), const)
    vec_d = pl.BlockSpec((1, d), const)
    mod_d = pl.BlockSpec((None, 1, d), per_b)
    return pl.pallas_call(
        _mix_out_kernel,
        grid=(b, s // tm),
        in_specs=[rc] * 7 + [pl.BlockSpec((None, tm, d), row),
                             vec_c, vec_c, vec_c, pl.BlockSpec((c, c), const), pl.BlockSpec((c, c), const),
                             pl.BlockSpec((d, d), const), vec_d, mod_d,
                             vec_d, mod_d, mod_d, pl.BlockSpec((d, LOGIT_PAD), const),
                             pl.BlockSpec((1, LOGIT_PAD), const)],
        out_specs=_token_out_specs(tm, d),
        out_shape=_token_out_shapes(b, s, d),
        compiler_params=_cparams(("parallel", "parallel")),
        name="mixer_out_router",
    )(yf, yb, r, k, v, g, o_attn, x, r_k.reshape(1, c), ln_w.reshape(1, c), ln_b.reshape(1, c), bd_mean, bd_ones,
      w_out, n_post.reshape(1, d), g1, n_pre.reshape(1, d), sh2, sc2, rw_pad, rb_pad)


def _gmlp_kernel(x_ref, gpre1_ref, sh1_ref, sc1_ref, win_ref, vnw_ref, vnb_ref, ws_ref, bs_ref, wout_ref,
                 npost_ref, g1_ref, gpre_ref, sh_ref, sc_ref, rw_ref, rb_ref,
                 x1_ref, hf_ref, idx_ref, gate_ref):
    x = x_ref[...]
    h = _rms(x) * gpre1_ref[...]
    h = h * (1.0 + sc1_ref[...]) + sh1_ref[...]
    z = _dot(h.astype(BF16), win_ref[...])
    z = 0.5 * z * (1.0 + lax.erf(z * float(2.0 ** -0.5)))
    u, v = z[:, :GM_WIDTH], z[:, GM_WIDTH:]
    mu = jnp.mean(v, axis=-1, keepdims=True)
    dv = v - mu
    var = jnp.mean(dv * dv, axis=-1, keepdims=True)
    v = (dv * lax.rsqrt(var + LN_EPS) * vnw_ref[...] + vnb_ref[...]).astype(BF16)
    gw = GM_WIDTH // GM_GROUPS
    rows = []
    for ci in range(x.shape[0] // CHUNK):
        cols = []
        for gi in range(GM_GROUPS):
            cols.append(_dot(ws_ref[gi], v[ci * CHUNK:(ci + 1) * CHUNK, gi * gw:(gi + 1) * gw]))
        rows.append(jnp.concatenate(cols, axis=1) + bs_ref[...])
    sp = jnp.concatenate(rows, axis=0)
    y = _dot((u * sp).astype(BF16), wout_ref[...])
    x1 = x + g1_ref[...] * (_rms(y) * npost_ref[...])
    x1_ref[...] = x1
    _ffn_pre(x1, gpre_ref, sh_ref, sc_ref, rw_ref, rb_ref, hf_ref, idx_ref, gate_ref)


def _gmlp(x, n_pre1, sh1, sc1, w_in, vn_w, vn_b, w_s, b_s, w_out, n_post, g1, n_pre, sh2, sc2, rw_pad, rb_pad):
    b, s, d = x.shape
    tm = ROW_TILE
    gw = GM_WIDTH // GM_GROUPS
    bs_full = jnp.repeat(b_s.T, gw, axis=1)
    row = lambda bi, i: (bi, i, 0)
    const = lambda bi, i: (0, 0)
    per_b = lambda bi, i: (bi, 0, 0)
    vec_d = pl.BlockSpec((1, d), const)
    vec_g = pl.BlockSpec((1, GM_WIDTH), const)
    mod_d = pl.BlockSpec((None, 1, d), per_b)
    return pl.pallas_call(
        _gmlp_kernel,
        grid=(b, s // tm),
        in_specs=[pl.BlockSpec((None, tm, d), row), vec_d, mod_d, mod_d,
                  pl.BlockSpec((d, 2 * GM_WIDTH), const), vec_g, vec_g,
                  pl.BlockSpec((GM_GROUPS, CHUNK, CHUNK), lambda bi, i: (0, 0, 0)),
                  pl.BlockSpec((CHUNK, GM_WIDTH), const), pl.BlockSpec((GM_WIDTH, d), const),
                  vec_d, mod_d, vec_d, mod_d, mod_d,
                  pl.BlockSpec((d, LOGIT_PAD), const), pl.BlockSpec((1, LOGIT_PAD), const)],
        out_specs=_token_out_specs(tm, d),
        out_shape=_token_out_shapes(b, s, d),
        compiler_params=_cparams(("parallel", "parallel")),
        name="gmlp_router",
    )(x, n_pre1.reshape(1, d), sh1, sc1, w_in.astype(BF16), vn_w.reshape(1, -1), vn_b.reshape(1, -1),
      w_s.astype(BF16), bs_full, w_out.astype(BF16), n_post.reshape(1, d), g1, n_pre.reshape(1, d), sh2, sc2,
      rw_pad, rb_pad)


MOE_ROWS = 256


def _expert_kernel(be_ref, na_ref, x_ref, wgu_ref, bgu_ref, wd_ref, bd_ref, y_ref, wgu_bf, wd_bf):
    i = pl.program_id(0)
    prev = be_ref[jnp.maximum(i - 1, 0)]
    changed = jnp.logical_or(i == 0, be_ref[i] != prev)

    @pl.when(changed)
    def _():
        wgu_bf[...] = wgu_ref[...].astype(BF16)
        wd_bf[...] = wd_ref[...].astype(BF16)

    @pl.when(i < na_ref[0])
    def _():
        gu = _dot(x_ref[...], wgu_bf[...]) + bgu_ref[...]
        x_glu = jnp.minimum(gu[:, :EXPERT_FF], SWIGLU_LIMIT)
        x_lin = jnp.clip(gu[:, EXPERT_FF:], -SWIGLU_LIMIT, SWIGLU_LIMIT)
        act = x_glu * jax.nn.sigmoid(SWIGLU_ALPHA * x_glu) * (x_lin + 1.0)
        y_ref[...] = (_dot(act.astype(BF16), wd_bf[...]) + bd_ref[...]).astype(BF16)

    @pl.when(i >= na_ref[0])
    def _():
        y_ref[...] = jnp.zeros(y_ref.shape, BF16)


def _experts(blk_expert, n_active, xg, layer, w_gu, b_gu, w_down, b_down):
    n_rows, d = xg.shape
    n_l, n_e, _, ff2 = w_gu.shape
    tb = MOE_ROWS
    n_blocks = n_rows // tb
    return pl.pallas_call(
        _expert_kernel,
        grid_spec=pltpu.PrefetchScalarGridSpec(
            num_scalar_prefetch=2,
            grid=(n_blocks,),
            in_specs=[pl.BlockSpec((tb, d), lambda i, be, na: (i, 0)),
                      pl.BlockSpec((None, None, d, ff2), lambda i, be, na: (layer, be[i], 0, 0)),
                      pl.BlockSpec((None, None, 1, ff2), lambda i, be, na: (layer, be[i], 0, 0)),
                      pl.BlockSpec((None, None, ff2 // 2, d), lambda i, be, na: (layer, be[i], 0, 0)),
                      pl.BlockSpec((None, None, 1, d), lambda i, be, na: (layer, be[i], 0, 0))],
            out_specs=pl.BlockSpec((tb, d), lambda i, be, na: (i, 0)),
            scratch_shapes=[pltpu.VMEM((d, ff2), BF16), pltpu.VMEM((ff2 // 2, d), BF16)]),
        out_shape=jax.ShapeDtypeStruct((n_rows, d), BF16),
        compiler_params=_cparams(("arbitrary",)),
        name="moe_experts",
    )(blk_expert, n_active, xg, w_gu, b_gu.reshape(n_l, n_e, 1, ff2), w_down, b_down.reshape(n_l, n_e, 1, d))


def _combine_kernel(x_ref, y_ref, gate_ref, npost_ref, g2_ref, o_ref):
    gate = gate_ref[...]
    f = jnp.zeros(x_ref.shape, F32)
    for kth in range(TOP_K):
        f = f + gate[:, kth:kth + 1] * y_ref[kth].astype(F32)
    o_ref[...] = x_ref[...] + g2_ref[...] * (_rms(f) * npost_ref[...])


def _combine(x1, yk, gates, n_post, g2):
    b, s, d = x1.shape
    tm = ROW_TILE
    row = lambda bi, i: (bi, i, 0)
    return pl.pallas_call(
        _combine_kernel,
        grid=(b, s // tm),
        in_specs=[pl.BlockSpec((None, tm, d), row),
                  pl.BlockSpec((TOP_K, None, tm, d), lambda bi, i: (0, bi, i, 0)),
                  pl.BlockSpec((None, tm, LOGIT_PAD), row),
                  pl.BlockSpec((1, d), lambda bi, i: (0, 0)),
                  pl.BlockSpec((None, 1, d), lambda bi, i: (bi, 0, 0))],
        out_specs=pl.BlockSpec((None, tm, d), row),
        out_shape=jax.ShapeDtypeStruct((b, s, d), F32),
        compiler_params=_cparams(("parallel", "parallel")),
        name="moe_combine_residual",
    )(x1, yk, gates, n_post.reshape(1, d), g2)


def _lookup(table, idx):
    n = table.shape[0]
    return jnp.sum(jnp.where(idx[:, None] == jnp.arange(n, dtype=jnp.int32)[None, :], table[None, :], 0), axis=1)


def _rows(a, idx):
    return a.at[idx].get(mode="promise_in_bounds")


def _moe(x1, hf, top_idx, gates, layer,w_gu, b_gu, w_down, b_down, n_post, g2):
    b, s, d = x1.shape
    n_tok = b * s
    tb = MOE_ROWS
    n_assign = n_tok * TOP_K
    assert N_EXPERTS * n_assign < 2 ** 31
    i32 = jnp.int32
    e_flat = top_idx[..., :TOP_K].reshape(-1)
    skey = jnp.sort(e_flat * n_assign + jnp.arange(n_assign, dtype=i32))
    order = skey % n_assign
    _, inv = lax.sort((order, jnp.arange(n_assign, dtype=i32)), num_keys=1)
    edges = jnp.arange(N_EXPERTS + 1, dtype=i32) * n_assign
    bounds = jnp.sum((skey[None, :] < edges[:, None]).astype(i32), axis=1)
    start = bounds[:-1]
    counts = bounds[1:] - start
    padded = (counts + tb - 1) // tb * tb
    pend = jnp.cumsum(padded)
    pstart = pend - padded
    n_rows = -(-n_assign // tb) * tb + N_EXPERTS * tb
    n_blocks = n_rows // tb
    blk_start = jnp.arange(n_blocks, dtype=i32) * tb
    blk_expert = jnp.minimum(jnp.sum((pend[None, :] <= blk_start[:, None]).astype(i32), axis=1), N_EXPERTS - 1)
    n_active = (pend[-1] // tb).astype(i32).reshape(1)
    j = jnp.arange(n_rows, dtype=i32) - jnp.repeat(_lookup(pstart, blk_expert), tb)
    src = jnp.repeat(_lookup(start, blk_expert), tb) + j
    valid = j < jnp.repeat(_lookup(counts, blk_expert), tb)
    row_tok = jnp.where(valid, _rows(order, jnp.clip(src, 0, n_assign - 1)) // TOP_K, 0)
    pos = _lookup(pstart - start, e_flat) + inv
    xg = _rows(hf.reshape(n_tok, d), row_tok)
    y = _experts(blk_expert, n_active, xg, layer, w_gu, b_gu, w_down, b_down)
    yk = _rows(y, pos.reshape(n_tok, TOP_K).T).reshape(TOP_K, b, s, d)
    return _combine(x1, yk, gates, n_post, g2)


def _block_diag(n, blk, val):
    return (jnp.kron(jnp.eye(n // blk, dtype=F32), jnp.ones((blk, blk), F32)) * val).astype(BF16)


def kernel(x, c, ctx, c_ctx, ada_w, ada_b, norm_mix_pre, norm_mix_post, norm_ffn_pre, norm_ffn_post, router_w, router_b, moe_w_gu, moe_b_gu, moe_w_down, moe_b_down, hy_w_in, mla_q_norm, mla_w_uq, mla_kv_norm, mla_w_ukv, rwkv_mu_prev, rwkv_mu_next, rwkv_w0, rwkv_w2, rwkv_a0, rwkv_a2, rwkv_g2, rwkv_k_k, rwkv_k_a, rwkv_r_k, rwkv_ln_w, rwkv_ln_b, hy_w_out, gm_w_in, gm_v_norm_w, gm_v_norm_b, gm_w_s, gm_b_s, gm_w_out):
    b, s, d = x.shape
    n_ctx = ctx.shape[1]
    t_all = s + n_ctx
    assert b + 1 <= 8 and s % ROW_TILE == 0 and n_ctx % ROW_TILE == 0

    cond_rows = jnp.concatenate([c, c_ctx[None], jnp.zeros((8 - b - 1, d), F32)], axis=0)
    mod = _modulation(cond_rows, ada_w, ada_b)

    def lat_mod(l, j):
        return mod[l, :b, j * d:(j + 1) * d].reshape(b, 1, d)

    sh_all = jnp.stack([mod[0, :b, 0:d], jnp.broadcast_to(mod[0, b, 0:d], (b, d))], axis=1).reshape(2 * b, 1, d)
    sc_all = jnp.stack([mod[0, :b, d:2 * d], jnp.broadcast_to(mod[0, b, d:2 * d], (b, d))], axis=1).reshape(2 * b, 1, d)
    x_all = jnp.concatenate([x, ctx], axis=1)
    cos, sin = _rope_tables(s, n_ctx)
    w1, wq2, wk2 = _inproj_weights(hy_w_in[0], mla_w_uq[0], mla_w_ukv[0])
    q, k, v, prw = _inproj(x_all, s, norm_mix_pre[0], sh_all, sc_all, cos, sin, w1, mla_q_norm[0], mla_kv_norm[0],
                           wq2, wk2)
    tq = 512 if s % 512 == 0 else ROW_TILE
    tk = 1280 if t_all % 1280 == 0 else ROW_TILE
    o_attn = _attention(q, k, v, s, tq, tk)

    bd_ones = _block_diag(RWKV_DIM, RWKV_HEAD, 1.0)
    bd_mean = _block_diag(RWKV_DIM, RWKV_HEAD, 1.0 / RWKV_HEAD)
    r, kx, vx, g, kkn, lw, kd, bdir = _rwkv_prepare(prw, s, rwkv_mu_prev[0], rwkv_mu_next[0], rwkv_w0[0], rwkv_w2[0],
                                                    rwkv_a0[0], rwkv_a2[0], rwkv_g2[0], rwkv_k_k[0], rwkv_k_a[0],
                                                    bd_ones)
    m_c, n_c, r_c, y_c = _wkv_chunks(r, vx, kkn, lw, kd, bdir)
    n_lat_ch = s // WKV_CHUNK
    n_ctx_ch = n_ctx // WKV_CHUNK
    lat_ch = np.arange(n_lat_ch)
    ctx_ch = n_lat_ch + np.arange(n_ctx_ch)
    order = jnp.asarray(np.stack([np.concatenate([ctx_ch, lat_ch]),
                                  np.concatenate([ctx_ch[::-1], lat_ch[::-1]])]).astype(np.int32))
    yf, yb = _wkv_states(order, m_c, n_c, r_c, y_c, t_all)

    rw_pad, rb_pad = _router_pads(router_w[0], router_b[0])
    x1, hf, top_idx, gates = _mix_out(yf, yb, r, kx, vx, g, o_attn, x, rwkv_r_k[0].reshape(-1), rwkv_ln_w[0],
                                      rwkv_ln_b[0], bd_mean, bd_ones, hy_w_out[0].astype(BF16), norm_mix_post[0],
                                      lat_mod(0, 2), norm_ffn_pre[0], lat_mod(0, 3), lat_mod(0, 4), rw_pad, rb_pad)
    x2 = _moe(x1, hf, top_idx, gates, 0, moe_w_gu, moe_b_gu, moe_w_down, moe_b_down, norm_ffn_post[0], lat_mod(0, 5))

    rw_pad, rb_pad = _router_pads(router_w[1], router_b[1])
    x3, hf, top_idx, gates = _gmlp(x2, norm_mix_pre[1], lat_mod(1, 0), lat_mod(1, 1), gm_w_in[0], gm_v_norm_w[0],
                                   gm_v_norm_b[0], gm_w_s[0], gm_b_s[0], gm_w_out[0], norm_mix_post[1], lat_mod(1, 2),
                                   norm_ffn_pre[1], lat_mod(1, 3), lat_mod(1, 4), rw_pad, rb_pad)
    return _moe(x3, hf, top_idx, gates, 1, moe_w_gu, moe_b_gu, moe_w_down, moe_b_down, norm_ffn_post[1], lat_mod(1, 5))
```

```python
import functools

import jax
import jax.numpy as jnp
import numpy as np
from jax import lax
from jax.experimental import pallas as pl
from jax.experimental.pallas import tpu as pltpu

F32 = jnp.float32
BF16 = jnp.bfloat16
HIGHEST = lax.Precision.HIGHEST

D_MODEL = 1024
GRID_W = 64
EPS = 1e-6

MLA_HEADS = 8
MLA_NOPE = 64
MLA_ROPE = 32
MLA_V = 64
MLA_Q_RANK = 256
MLA_KV_RANK = 128
MLA_SCALE = (MLA_NOPE + MLA_ROPE) ** -0.5
ROPE_FREQS = MLA_ROPE // 4
ROPE_BASE = 10000.0
HEAD_PAD = 128
ATT_QSUB = 128
ATT_KSUB = 640
LOG2E = 1.4426950408889634
ATT_EXP_DTYPE = jnp.float32

RWKV_HEAD = 64
RWKV_DIM = 512
RWKV_HEADS = 8
DECAY_LORA = 32
AAA_LORA = 32
GATE_LORA = 96
RWKV_PAD_COLS = 3 * RWKV_DIM + 128 + 128
RWKV_LN_EPS = 64e-5
WKV_CHUNK = 64
WKV_CHUNKS_PER_STEP = 4

CHUNK = 128
GM_WIDTH = 1024
GM_GROUPS = 8
LN_EPS = 1e-5

N_EXPERTS = 32
TOP_K = 4
EXPERT_FF = 1024
SWIGLU_LIMIT = 7.0
SWIGLU_ALPHA = 1.702
LOGIT_PAD = 128
NEG_BIG = -1e30

ROW_TILE = 256
VMEM_LIMIT = 48 * 1024 * 1024


def _cparams(sem):
    return pltpu.CompilerParams(dimension_semantics=sem, vmem_limit_bytes=VMEM_LIMIT)


def _rms(x):
    return x * lax.rsqrt(jnp.mean(x * x, axis=-1, keepdims=True) + EPS)


def _dot(a, b, precision=None):
    return jnp.dot(a, b, preferred_element_type=F32, precision=precision)


def _dot_nt(a, b, precision=None):
    return lax.dot_general(a, b, (((1,), (1,)), ((), ())), preferred_element_type=F32, precision=precision)


def _dot_tn(a, b, precision=None):
    return lax.dot_general(a, b, (((0,), (0,)), ((), ())), preferred_element_type=F32, precision=precision)


def _split(x):
    hi = x.astype(BF16)
    return hi, (x - hi.astype(F32)).astype(BF16)


def _dot_split(a, b):
    a_hi, a_lo = _split(a)
    b_hi, b_lo = _split(b)
    return _dot(jnp.concatenate([a_hi, a_hi, a_lo], axis=1), jnp.concatenate([b_hi, b_lo, b_hi], axis=0))


def _mod_kernel(s_ref, w_ref, b_ref, o_ref):
    s = s_ref[...]
    s = s * jax.nn.sigmoid(s)
    o_ref[...] = _dot(s, w_ref[...], HIGHEST) + b_ref[...]


def _modulation(cond_rows, ada_w, ada_b):
    n_l, d, n6 = ada_w.shape
    tn = 1536
    return pl.pallas_call(
        _mod_kernel,
        grid=(n_l, n6 // tn),
        in_specs=[pl.BlockSpec((8, d), lambda l, j: (0, 0)),
                  pl.BlockSpec((None, d, tn), lambda l, j: (l, 0, j)),
                  pl.BlockSpec((None, 1, tn), lambda l, j: (l, 0, j))],
        out_specs=pl.BlockSpec((None, 8, tn), lambda l, j: (l, 0, j)),
        out_shape=jax.ShapeDtypeStruct((n_l, 8, n6), F32),
        compiler_params=_cparams(("arbitrary", "arbitrary")),
        name="adaln_mod",
    )(cond_rows, ada_w, ada_b.reshape(n_l, 1, n6))


def _inproj_kernel(x_ref, g_ref, sh_ref, sc_ref, cos_ref, sin_ref, w1_ref, qn_ref, kvn_ref, wq_ref, wk_ref,
                   q_ref, k_ref, vt_ref, prw_ref):
    h = _rms(x_ref[...]) * g_ref[...]
    h = h * (1.0 + sc_ref[...]) + sh_ref[...]
    p = _dot(h.astype(BF16), w1_ref[...])
    cos = jnp.concatenate([cos_ref[...]] * MLA_HEADS, axis=1)
    sin = jnp.concatenate([sin_ref[...]] * MLA_HEADS, axis=1)
    nq = MLA_HEADS * HEAD_PAD
    qn = _rms(p[:, :MLA_Q_RANK]) * qn_ref[...]
    qq = _dot(qn.astype(BF16), wq_ref[...])
    q_ref[...] = (qq[:, :nq] * cos + qq[:, nq:] * sin).astype(BF16)
    kvn = _rms(p[:, MLA_Q_RANK:MLA_Q_RANK + MLA_KV_RANK]) * kvn_ref[...]
    x2 = jnp.concatenate([kvn, p[:, MLA_Q_RANK + MLA_KV_RANK:512]], axis=1).astype(BF16)
    kk = _dot(x2, wk_ref[...])
    k_ref[...] = (kk[:, :nq] * cos + kk[:, nq:2 * nq] * sin).astype(BF16)
    vt_ref[...] = kk[:, 2 * nq:].T.astype(BF16)
    prw_ref[...] = p[:, 512:]


def _rotate_half_cols(w):
    wr = w.reshape(w.shape[:-1] + (2, 2, ROPE_FREQS))
    return jnp.stack([-wr[..., 1, :], wr[..., 0, :]], axis=-2).reshape(w.shape)


def _inproj_weights(hy_w_in, w_uq, w_ukv):
    d = hy_w_in.shape[0]
    mla_cols = MLA_Q_RANK + MLA_KV_RANK + MLA_ROPE
    w_mla = hy_w_in[:, :mla_cols]
    w_rw = hy_w_in[:, mla_cols:]
    w1 = jnp.concatenate([w_mla, jnp.zeros((d, 512 - mla_cols), F32), _pad_rwkv_cols(w_rw)], axis=1).astype(BF16)
    wq = (w_uq * (MLA_SCALE * LOG2E)).reshape(MLA_Q_RANK, MLA_HEADS, MLA_NOPE + MLA_ROPE)
    z_n = jnp.zeros((MLA_Q_RANK, MLA_HEADS, MLA_NOPE), F32)
    z_p = jnp.zeros((MLA_Q_RANK, MLA_HEADS, HEAD_PAD - MLA_NOPE - MLA_ROPE), F32)
    wq_plain = jnp.concatenate([wq[..., :MLA_NOPE], wq[..., MLA_NOPE:], z_p], axis=-1)
    wq_rot = jnp.concatenate([z_n, _rotate_half_cols(wq[..., MLA_NOPE:]), z_p], axis=-1)
    wq2 = jnp.concatenate([wq_plain.reshape(MLA_Q_RANK, -1), wq_rot.reshape(MLA_Q_RANK, -1)], axis=1).astype(BF16)
    wkv = w_ukv.reshape(MLA_KV_RANK, MLA_HEADS, MLA_NOPE + MLA_V)
    eye = jnp.broadcast_to(jnp.eye(MLA_ROPE, dtype=F32)[:, None, :], (MLA_ROPE, MLA_HEADS, MLA_ROPE))
    pad_k = HEAD_PAD - MLA_NOPE - MLA_ROPE
    top_plain = jnp.concatenate([wkv[..., :MLA_NOPE], jnp.zeros((MLA_KV_RANK, MLA_HEADS, MLA_ROPE + pad_k), F32)], -1)
    mid_plain = jnp.concatenate([jnp.zeros((MLA_ROPE, MLA_HEADS, MLA_NOPE), F32), eye,
                                 jnp.zeros((MLA_ROPE, MLA_HEADS, pad_k), F32)], -1)
    mid_rot = jnp.concatenate([jnp.zeros((MLA_ROPE, MLA_HEADS, MLA_NOPE), F32), _rotate_half_cols(eye),
                               jnp.zeros((MLA_ROPE, MLA_HEADS, pad_k), F32)], -1)
    nq = MLA_HEADS * HEAD_PAD
    rows_c = jnp.concatenate([top_plain.reshape(MLA_KV_RANK, nq), jnp.zeros((MLA_KV_RANK, nq), F32),
                              wkv[..., MLA_NOPE:].reshape(MLA_KV_RANK, MLA_HEADS * MLA_V)], axis=1)
    rows_r = jnp.concatenate([mid_plain.reshape(MLA_ROPE, nq), mid_rot.reshape(MLA_ROPE, nq),
                              jnp.zeros((MLA_ROPE, MLA_HEADS * MLA_V), F32)], axis=1)
    rows_z = jnp.zeros((256 - MLA_KV_RANK - MLA_ROPE, rows_c.shape[1]), F32)
    wk2 = jnp.concatenate([rows_c, rows_r, rows_z], axis=0).astype(BF16)
    return w1, wq2, wk2


def _pad_rwkv_cols(w):
    pad = jnp.zeros(w.shape[:-1] + (RWKV_PAD_COLS - w.shape[-1],), w.dtype)
    return jnp.concatenate([w, pad], axis=-1)


def _rope_tables(n_lat, n_ctx):
    t = jnp.arange(n_lat)
    row = (t // GRID_W).astype(F32)
    col = (t % GRID_W).astype(F32)
    inv = ROPE_BASE ** (-jnp.arange(ROPE_FREQS, dtype=F32) / ROPE_FREQS)
    ang = jnp.stack([row[:, None] * inv, col[:, None] * inv], axis=1)
    ang = jnp.broadcast_to(ang[:, :, None, :], (n_lat, 2, 2, ROPE_FREQS)).reshape(n_lat, MLA_ROPE)
    cos = jnp.concatenate([jnp.ones((n_lat, MLA_NOPE), F32), jnp.cos(ang),
                           jnp.ones((n_lat, HEAD_PAD - MLA_NOPE - MLA_ROPE), F32)], axis=1)
    sin = jnp.concatenate([jnp.zeros((n_lat, MLA_NOPE), F32), jnp.sin(ang),
                           jnp.zeros((n_lat, HEAD_PAD - MLA_NOPE - MLA_ROPE), F32)], axis=1)
    cos = jnp.concatenate([cos, jnp.ones((n_ctx, HEAD_PAD), F32)], axis=0)
    sin = jnp.concatenate([sin, jnp.zeros((n_ctx, HEAD_PAD), F32)], axis=0)
    return cos, sin


def _inproj(x_all, n_lat, gain, shift, scale, cos, sin, w1, q_norm, kv_norm, wq2, wk2):
    b, t_all, d = x_all.shape
    tm = ROW_TILE
    n_lat_tiles = n_lat // tm
    nq = MLA_HEADS * HEAD_PAD

    def mod_map(bi, i):
        return (2 * bi + jnp.where(i >= n_lat_tiles, 1, 0), 0, 0)

    const = lambda bi, i: (0, 0)
    row = lambda bi, i: (bi, i, 0)
    return pl.pallas_call(
        _inproj_kernel,
        grid=(b, t_all // tm),
        in_specs=[pl.BlockSpec((None, tm, d), row),
                  pl.BlockSpec((1, d), const),
                  pl.BlockSpec((None, 1, d), mod_map),
                  pl.BlockSpec((None, 1, d), mod_map),
                  pl.BlockSpec((tm, HEAD_PAD), lambda bi, i: (i, 0)),
                  pl.BlockSpec((tm, HEAD_PAD), lambda bi, i: (i, 0)),
                  pl.BlockSpec(w1.shape, const),
                  pl.BlockSpec((1, MLA_Q_RANK), const),
                  pl.BlockSpec((1, MLA_KV_RANK), const),
                  pl.BlockSpec(wq2.shape, const),
                  pl.BlockSpec(wk2.shape, const)],
        out_specs=[pl.BlockSpec((None, tm, nq), row),
                   pl.BlockSpec((None, tm, nq), row),
                   pl.BlockSpec((None, MLA_HEADS * MLA_V, tm), lambda bi, i: (bi, 0, i)),
                   pl.BlockSpec((None, tm, RWKV_PAD_COLS), row)],
        out_shape=[jax.ShapeDtypeStruct((b, t_all, nq), BF16),
                   jax.ShapeDtypeStruct((b, t_all, nq), BF16),
                   jax.ShapeDtypeStruct((b, MLA_HEADS * MLA_V, t_all), BF16),
                   jax.ShapeDtypeStruct((b, t_all, RWKV_PAD_COLS), F32)],
        compiler_params=_cparams(("parallel", "parallel")),
        name="hybrid_inproj",
    )(x_all, gain.reshape(1, d), shift, scale, cos, sin, w1, q_norm.reshape(1, -1), kv_norm.reshape(1, -1), wq2, wk2)


def _attn_kernel(q_ref, k_ref, vt_ref, o_ref, m_sc, l_sc, acc_sc):
    ki = pl.program_id(3)

    @pl.when(ki == 0)
    def _():
        m_sc[...] = jnp.full(m_sc.shape, -jnp.inf, F32)
        l_sc[...] = jnp.zeros(l_sc.shape, F32)
        acc_sc[...] = jnp.zeros(acc_sc.shape, F32)

    tq, tk = q_ref.shape[0], k_ref.shape[0]
    chains = [(h, qs) for h in range(2) for qs in range(tq // ATT_QSUB)]
    hsl = [slice(h * HEAD_PAD, (h + 1) * HEAD_PAD) for h, _ in chains]
    vsl = [slice(h * MLA_V, (h + 1) * MLA_V) for h, _ in chains]
    qsl = [slice(qs * ATT_QSUB, (qs + 1) * ATT_QSUB) for _, qs in chains]
    nch = len(chains)
    q = [q_ref[qsl[c], hsl[c]] for c in range(nch)]
    m = [m_sc[chains[c][0], 0:1, qsl[c]] for c in range(nch)]
    l = [l_sc[chains[c][0], 0:1, qsl[c]] for c in range(nch)]
    acc = [acc_sc[chains[c][0], :, qsl[c]] for c in range(nch)]
    ksub = ATT_KSUB if tk % ATT_KSUB == 0 else tk
    for kb in range(tk // ksub):
        ksl = slice(kb * ksub, (kb + 1) * ksub)
        s = [_dot_nt(k_ref[ksl, hsl[c]], q[c]).astype(ATT_EXP_DTYPE) for c in range(nch)]
        m_new = [jnp.maximum(m[c], jnp.max(s[c], axis=0, keepdims=True).astype(F32)) for c in range(nch)]
        alpha = [jnp.exp2(m[c] - m_new[c]) for c in range(nch)]
        p = [jnp.exp2(s[c] - m_new[c].astype(ATT_EXP_DTYPE)) for c in range(nch)]
        l = [alpha[c] * l[c] + jnp.sum(p[c].astype(F32), axis=0, keepdims=True) for c in range(nch)]
        acc = [acc[c] * alpha[c] + _dot(vt_ref[vsl[c], ksl], p[c].astype(BF16)) for c in range(nch)]
        m = m_new
    for c in range(nch):
        h = chains[c][0]
        m_sc[h, :, qsl[c]] = jnp.broadcast_to(m[c], (8, ATT_QSUB))
        l_sc[h, :, qsl[c]] = jnp.broadcast_to(l[c], (8, ATT_QSUB))
        acc_sc[h, :, qsl[c]] = acc[c]

    @pl.when(ki == pl.num_programs(3) - 1)
    def _():
        o0 = (acc_sc[0] / l_sc[0, 0:1, :]).T
        o1 = (acc_sc[1] / l_sc[1, 0:1, :]).T
        o_ref[...] = jnp.concatenate([o0, o1], axis=1).astype(BF16)


def _attention(q, k, vt, n_lat, tq, tk):
    b, t_all, _ = k.shape
    return pl.pallas_call(
        _attn_kernel,
        grid=(b, MLA_HEADS // 2, n_lat // tq, t_all // tk),
        in_specs=[pl.BlockSpec((None, tq, 2 * HEAD_PAD), lambda bi, hp, qi, ki: (bi, qi, hp)),
                  pl.BlockSpec((None, tk, 2 * HEAD_PAD), lambda bi, hp, qi, ki: (bi, ki, hp)),
                  pl.BlockSpec((None, 2 * MLA_V, tk), lambda bi, hp, qi, ki: (bi, hp, ki))],
        out_specs=pl.BlockSpec((None, tq, 2 * MLA_V), lambda bi, hp, qi, ki: (bi, qi, hp)),
        out_shape=jax.ShapeDtypeStruct((b, n_lat, MLA_HEADS * MLA_V), BF16),
        scratch_shapes=[pltpu.VMEM((2, 8, tq), F32), pltpu.VMEM((2, 8, tq), F32), pltpu.VMEM((2, MLA_V, tq), F32)],
        compiler_params=_cparams(("parallel", "parallel", "parallel", "arbitrary")),
        name="mla_attention",
    )(q, k, vt)


def _seg_sum(x, bd_ref):
    hi = x.astype(BF16)
    lo = (x - hi.astype(F32)).astype(BF16)
    bd = bd_ref[...]
    return _dot(hi, bd) + _dot(lo, bd)


def _rwkv_prep_kernel(p_ref, prev_ref, next_ref, mup_ref, mun_ref, wl_ref, w0a0_ref, g2_ref, kk_ref, ka_ref, bd_ref,
                      r_ref, k_ref, v_ref, g_ref, kkn_ref, lw_ref, kd_ref, bdir_ref, *, n_lat_tiles, n_tiles):
    i = pl.program_id(1)
    p = p_ref[...]
    tm = p.shape[0]
    first = jnp.logical_or(i == 0, i == n_lat_tiles)
    last = jnp.logical_or(i == n_lat_tiles - 1, i == n_tiles - 1)
    prev_row = jnp.where(first, 0.0, prev_ref[7:8, :])
    next_row = jnp.where(last, 0.0, next_ref[0:1, :])
    ridx = lax.broadcasted_iota(jnp.int32, p.shape, 0)
    prev = jnp.where(ridx == 0, prev_row, pltpu.roll(p, 1, 0))
    nxt = jnp.where(ridx == tm - 1, next_row, pltpu.roll(p, tm - 1, 0))
    p = p + mup_ref[...] * (prev - p) + mun_ref[...] * (nxt - p)
    c = RWKV_DIM
    r, k, v = p[:, :c], p[:, c:2 * c], p[:, 2 * c:3 * c]
    lo = p[:, 3 * c:3 * c + 128]
    lane = lax.broadcasted_iota(jnp.int32, lo.shape, 1)
    lo = jnp.where(lane < 2 * DECAY_LORA, jnp.tanh(lo), lo)
    wa = _dot(lo.astype(BF16), wl_ref[...]) + w0a0_ref[...]
    gd = p[:, 3 * c + 128:]
    g_ref[...] = _dot(jax.nn.sigmoid(gd).astype(BF16), g2_ref[...])
    kk = k * kk_ref[...]
    kk = kk * lax.rsqrt(jnp.maximum(_seg_sum(kk * kk, bd_ref), 1e-24))
    r_ref[...] = r
    k_ref[...] = k
    v_ref[...] = v
    kkn_ref[...] = kk
    ka = ka_ref[...]
    for d in range(2):
        w = wa[:, d * c:(d + 1) * c]
        a = jax.nn.sigmoid(wa[:, (2 + d) * c:(3 + d) * c])
        lw_ref[:, d * c:(d + 1) * c] = -float(np.exp(-0.5)) * jax.nn.sigmoid(w)
        kd_ref[:, d * c:(d + 1) * c] = k * (1.0 + (a - 1.0) * ka)
        bdir_ref[:, d * c:(d + 1) * c] = kk * a


def _rwkv_prepare(prw, n_lat, mu_prev, mu_next, w0, w2, a0, a2, g2, k_k, k_a, bd_ones):
    b, t_all, pc = prw.shape
    tm = ROW_TILE
    n_tiles = t_all // tm
    n_lat_tiles = n_lat // tm
    c = RWKV_DIM
    z = jnp.zeros((DECAY_LORA, c), F32)
    wl = jnp.concatenate([
        jnp.concatenate([w2[0], z, z, z], axis=1), jnp.concatenate([z, w2[1], z, z], axis=1),
        jnp.concatenate([z, z, a2[0], z], axis=1), jnp.concatenate([z, z, z, a2[1]], axis=1)], axis=0).astype(BF16)
    w0a0 = jnp.concatenate([w0[0], w0[1], a0[0], a0[1]]).reshape(1, 4 * c)
    g2p = jnp.concatenate([g2, jnp.zeros((128 - GATE_LORA, c), F32)], axis=0).astype(BF16)
    row = lambda bi, i: (bi, i, 0)
    const = lambda bi, i: (0, 0)
    hb = tm // 8
    n_hb = t_all // 8
    kern = functools.partial(_rwkv_prep_kernel, n_lat_tiles=n_lat_tiles, n_tiles=n_tiles)
    o_c = jax.ShapeDtypeStruct((b, t_all, c), F32)
    o_2c = jax.ShapeDtypeStruct((b, t_all, 2 * c), F32)
    return pl.pallas_call(
        kern,
        grid=(b, n_tiles),
        in_specs=[pl.BlockSpec((None, tm, pc), row),
                  pl.BlockSpec((None, 8, pc), lambda bi, i: (bi, jnp.maximum(i * hb - 1, 0), 0)),
                  pl.BlockSpec((None, 8, pc), lambda bi, i: (bi, jnp.minimum((i + 1) * hb, n_hb - 1), 0)),
                  pl.BlockSpec((1, pc), const), pl.BlockSpec((1, pc), const),
                  pl.BlockSpec(wl.shape, const), pl.BlockSpec((1, 4 * c), const),
                  pl.BlockSpec(g2p.shape, const), pl.BlockSpec((1, c), const), pl.BlockSpec((1, c), const),
                  pl.BlockSpec((c, c), const)],
        out_specs=[pl.BlockSpec((None, tm, c), row)] * 5 + [pl.BlockSpec((None, tm, 2 * c), row)] * 3,
        out_shape=[o_c] * 5 + [o_2c] * 3,
        compiler_params=_cparams(("parallel", "parallel")),
        name="rwkv_prepare",
    )(prw, prw, prw, _pad_rwkv_cols(mu_prev).reshape(1, pc), _pad_rwkv_cols(mu_next).reshape(1, pc),
      wl, w0a0, g2p, k_k.reshape(1, c), k_a.reshape(1, c), bd_ones)


def _wkv_chunk_kernel(r_ref, v_ref, kk_ref, lw_ref, kd_ref, bd_ref, m_ref, n_ref, rr_ref, yv_ref):
    d = pl.program_id(1)
    cs = WKV_CHUNK
    nc = r_ref.shape[0] // cs
    ti = lax.broadcasted_iota(jnp.int32, (cs, cs), 0)
    si = lax.broadcasted_iota(jnp.int32, (cs, cs), 1)
    rel = (si - ti) * (1 - 2 * d)
    incl = rel <= 0
    strict = rel < 0
    eye = si == ti
    incl_f = incl.astype(F32)
    ops = []
    for ci in range(nc):
        rows = slice(ci * cs, (ci + 1) * cs)
        lw = lw_ref[rows, :]
        g = _dot(incl_f, lw, HIGHEST)
        total = jnp.sum(lw, axis=0, keepdims=True)
        e_inv = jnp.exp(-g)
        e_end = jnp.exp(total - g)
        gam = jnp.exp(total)
        kd = kd_ref[rows, :]
        bd = bd_ref[rows, :]
        at = -kk_ref[rows, :] * jnp.exp(g - lw)
        rt = r_ref[rows, :] * jnp.exp(g)
        kt = (kd * e_inv).astype(BF16)
        bt = (bd * e_inv).astype(BF16)
        ke = (kd * e_end).astype(BF16)
        be = (bd * e_end).astype(BF16)
        v = v_ref[rows, :].astype(BF16)
        for h in range(RWKV_HEADS):
            sl = slice(h * RWKV_HEAD, (h + 1) * RWKV_HEAD)
            ops.append((at[:, sl], rt[:, sl], kt[:, sl], bt[:, sl], ke[:, sl], be[:, sl], v[:, sl], gam[:, sl]))
    n_it = len(ops)
    aa = [_dot_nt(jnp.concatenate([o[0], o[1]], axis=0).astype(BF16), jnp.concatenate([o[3], o[2]], axis=0))
          for o in ops]
    a_ab = [jnp.where(strict, x[:cs, :cs], 0.0) for x in aa]
    a_ak = [jnp.where(strict, x[:cs, cs:], 0.0).astype(BF16) for x in aa]
    a_rb = [jnp.where(incl, x[cs:, :cs], 0.0).astype(BF16) for x in aa]
    a_rk = [jnp.where(incl, x[cs:, cs:], 0.0).astype(BF16) for x in aa]
    z = [jnp.concatenate([ops[i][0], _dot(a_ak[i], ops[i][6])], axis=1) for i in range(n_it)]
    pw = a_ab
    n_sq = int(np.log2(cs))
    for it in range(n_sq):
        more = it + 1 < n_sq
        res = [_dot_split(pw[i], jnp.concatenate([z[i], pw[i]], axis=1) if more else z[i]) for i in range(n_it)]
        z = [z[i] + res[i][:, :2 * cs] for i in range(n_it)]
        if more:
            pw = [x[:, 2 * cs:] for x in res]
    zb = [x.astype(BF16) for x in z]
    w2 = [_dot(a_rb[i], zb[i]) for i in range(n_it)]
    w3 = [_dot_tn(ops[i][5], zb[i]) for i in range(n_it)]
    n0 = [_dot_tn(ops[i][4], ops[i][6]) for i in range(n_it)]
    y0 = [_dot(a_rk[i], ops[i][6]) for i in range(n_it)]
    nh = RWKV_HEADS
    for ci in range(nc):
        rows = slice(ci * cs, (ci + 1) * cs)
        ids = range(ci * nh, (ci + 1) * nh)
        m_ref[rows, :] = jnp.concatenate([jnp.where(eye, ops[i][7], 0.0) + w3[i][:, :cs] for i in ids], axis=1)
        n_ref[rows, :] = jnp.concatenate([n0[i] + w3[i][:, cs:] for i in ids], axis=1)
        rr_ref[rows, :] = jnp.concatenate([ops[i][1] + w2[i][:, :cs] for i in ids], axis=1)
        yv_ref[rows, :] = jnp.concatenate([y0[i] + w2[i][:, cs:] for i in ids], axis=1)


def _wkv_chunks(r, v, kkn, lw, kd, bdir):
    b, t_all, c = r.shape
    rows = WKV_CHUNK * WKV_CHUNKS_PER_STEP
    nst = t_all // rows
    shared = pl.BlockSpec((None, rows, c), lambda bi, d, ci: (bi, ci, 0))
    per_dir = pl.BlockSpec((None, rows, c), lambda bi, d, ci: (bi, ci, d))
    out = pl.BlockSpec((None, None, rows, c), lambda bi, d, ci: (bi, d, ci, 0))
    o_s = jax.ShapeDtypeStruct((b, 2, t_all, c), F32)
    return pl.pallas_call(
        _wkv_chunk_kernel,
        grid=(b, 2, nst),
        in_specs=[shared, shared, shared, per_dir, per_dir, per_dir],
        out_specs=[out] * 4,
        out_shape=[o_s] * 4,
        compiler_params=_cparams(("parallel", "parallel", "parallel")),
        name="wkv_chunk_summaries",
    )(r, v, kkn, lw, kd, bdir)


def _wkv_state_kernel(order_ref, mf_ref, nf_ref, rf_ref, yf_ref, mb_ref, nb_ref, rb_ref, yb_ref,
                      of_ref, ob_ref, st_sc):
    del order_ref
    s = pl.program_id(0)

    @pl.when(s == 0)
    def _():
        st_sc[...] = jnp.zeros(st_sc.shape, F32)

    n_b = st_sc.shape[0]
    ins = ((mf_ref, nf_ref, rf_ref, yf_ref, of_ref), (mb_ref, nb_ref, rb_ref, yb_ref, ob_ref))
    for bi in range(n_b):
        for d in range(2):
            m_r, n_r, r_r, y_r, o_r = ins[d]
            st = st_sc[bi, d]
            m, n, rr, yv = m_r[bi], n_r[bi], r_r[bi], y_r[bi]
            ys, sts = [], []
            for h in range(RWKV_HEADS):
                sl = slice(h * RWKV_HEAD, (h + 1) * RWKV_HEAD)
                st_h = st[:, sl]
                ys.append(_dot(rr[:, sl], st_h) + yv[:, sl])
                sts.append(_dot(m[:, sl], st_h, HIGHEST) + n[:, sl])
            o_r[bi] = jnp.concatenate(ys, axis=1)
            st_sc[bi, d] = jnp.concatenate(sts, axis=1)


def _wkv_states(order, m, n, rr, yv, t_all):
    b, _, _, c = m.shape
    cs = WKV_CHUNK
    nch = t_all // cs
    fwd = pl.BlockSpec((b, None, cs, c), lambda s, o: (0, 0, o[0, s], 0))
    bwd = pl.BlockSpec((b, None, cs, c), lambda s, o: (0, 1, o[1, s], 0))
    out_f = pl.BlockSpec((b, cs, c), lambda s, o: (0, o[0, s], 0))
    out_b = pl.BlockSpec((b, cs, c), lambda s, o: (0, o[1, s], 0))
    o_s = jax.ShapeDtypeStruct((b, t_all, c), F32)
    return pl.pallas_call(
        _wkv_state_kernel,
        grid_spec=pltpu.PrefetchScalarGridSpec(
            num_scalar_prefetch=1,
            grid=(nch,),
            in_specs=[fwd] * 4 + [bwd] * 4,
            out_specs=[out_f, out_b],
            scratch_shapes=[pltpu.VMEM((b, 2, cs, c), F32)]),
        out_shape=[o_s, o_s],
        compiler_params=_cparams(("arbitrary",)),
        name="wkv_state_pass",
    )(order, m, n, rr, yv, m, n, rr, yv)


def _ffn_pre(x1, gain_ref, sh_ref, sc_ref, rw_ref, rb_ref, hf_ref, idx_ref, gate_ref):
    hf = _rms(x1) * gain_ref[...]
    hf = hf * (1.0 + sc_ref[...]) + sh_ref[...]
    hf_ref[...] = hf.astype(BF16)
    hf_hi, hf_lo = _split(hf)
    t = _dot(hf_hi, rw_ref[...])
    u = _dot(hf_lo, rw_ref[:, :LOGIT_PAD])
    logits = t[:, :LOGIT_PAD] + t[:, LOGIT_PAD:] + u + rb_ref[...]
    lane = lax.broadcasted_iota(jnp.int32, logits.shape, 1)
    lane_f = lane.astype(F32)
    work = logits
    idx_out = jnp.zeros(logits.shape, F32)
    val_out = jnp.full(logits.shape, NEG_BIG, F32)
    for kth in range(TOP_K):
        m = jnp.max(work, axis=1, keepdims=True)
        idx = jnp.min(jnp.where(work == m, lane_f, float(LOGIT_PAD)), axis=1, keepdims=True)
        work = jnp.where(lane_f == idx, -jnp.inf, work)
        idx_out = jnp.where(lane == kth, idx, idx_out)
        val_out = jnp.where(lane == kth, m, val_out)
    e = jnp.exp(val_out - jnp.max(val_out, axis=1, keepdims=True))
    e = jnp.where(lane < TOP_K, e, 0.0)
    idx_ref[...] = idx_out.astype(jnp.int32)
    gate_ref[...] = e / jnp.sum(e, axis=1, keepdims=True)


def _router_pads(router_w, router_b):
    d = router_w.shape[0]
    rw = jnp.concatenate([router_w, jnp.zeros((d, LOGIT_PAD - N_EXPERTS), F32)], axis=1)
    rb = jnp.concatenate([router_b, jnp.full((LOGIT_PAD - N_EXPERTS,), NEG_BIG, F32)]).reshape(1, LOGIT_PAD)
    return jnp.concatenate(_split(rw), axis=1), rb


def _mix_out_kernel(yf_ref, yb_ref, r_ref, k_ref, v_ref, g_ref, o_ref, x_ref,
                    rk_ref, lnw_ref, lnb_ref, bdm_ref, bds_ref, wout_ref, npost_ref, g1_ref,
                    gpre_ref, sh_ref, sc_ref, rw_ref, rb_ref,
                    x1_ref, hf_ref, idx_ref, gate_ref):
    y = yf_ref[...] + yb_ref[...]
    mu = _seg_sum(y, bdm_ref)
    dlt = y - mu
    var = _seg_sum(dlt * dlt, bdm_ref)
    yn = dlt * lax.rsqrt(var + RWKV_LN_EPS) * lnw_ref[...] + lnb_ref[...]
    bonus = _seg_sum(r_ref[...] * k_ref[...] * rk_ref[...], bds_ref) * v_ref[...]
    rw = (yn + bonus) * g_ref[...]
    mix_in = jnp.concatenate([o_ref[...], rw.astype(BF16)], axis=1)
    mix = _dot(mix_in, wout_ref[...])
    x1 = x_ref[...] + g1_ref[...] * (_rms(mix) * npost_ref[...])
    x1_ref[...] = x1
    _ffn_pre(x1, gpre_ref, sh_ref, sc_ref, rw_ref, rb_ref, hf_ref, idx_ref, gate_ref)


def _token_out_specs(tm, d):
    row = lambda bi, i: (bi, i, 0)
    specs = [pl.BlockSpec((None, tm, d), row), pl.BlockSpec((None, tm, d), row),
             pl.BlockSpec((None, tm, LOGIT_PAD), row), pl.BlockSpec((None, tm, LOGIT_PAD), row)]
    return specs


def _token_out_shapes(b, s, d):
    return [jax.ShapeDtypeStruct((b, s, d), F32), jax.ShapeDtypeStruct((b, s, d), BF16),
            jax.ShapeDtypeStruct((b, s, LOGIT_PAD), jnp.int32), jax.ShapeDtypeStruct((b, s, LOGIT_PAD), F32)]


def _mix_out(yf, yb, r, k, v, g, o_attn, x, r_k, ln_w, ln_b, bd_mean, bd_ones, w_out, n_post, g1,
             n_pre, sh2, sc2, rw_pad, rb_pad):
    b, s, d = x.shape
    c = RWKV_DIM
    tm = ROW_TILE
    row = lambda bi, i: (bi, i, 0)
    const = lambda bi, i: (0, 0)
    per_b = lambda bi, i: (bi, 0, 0)
    rc = pl.BlockSpec((None, tm, c), row)
    vec_c = pl.BlockSpec((1, c), const)
    vec_d = pl.BlockSpec((1, d), const)
    mod_d = pl.BlockSpec((None, 1, d), per_b)
    return pl.pallas_call(
        _mix_out_kernel,
        grid=(b, s // tm),
        in_specs=[rc] * 7 + [pl.BlockSpec((None, tm, d), row),
                             vec_c, vec_c, vec_c, pl.BlockSpec((c, c), const), pl.BlockSpec((c, c), const),
                             pl.BlockSpec((d, d), const), vec_d, mod_d,
                             vec_d, mod_d, mod_d, pl.BlockSpec((d, 2 * LOGIT_PAD), const),
                             pl.BlockSpec((1, LOGIT_PAD), const)],
        out_specs=_token_out_specs(tm, d),
        out_shape=_token_out_shapes(b, s, d),
        compiler_params=_cparams(("parallel", "parallel")),
        name="mixer_out_router",
    )(yf, yb, r, k, v, g, o_attn, x, r_k.reshape(1, c), ln_w.reshape(1, c), ln_b.reshape(1, c), bd_mean, bd_ones,
      w_out, n_post.reshape(1, d), g1, n_pre.reshape(1, d), sh2, sc2, rw_pad, rb_pad)


def _gmlp_kernel(x_ref, gpre1_ref, sh1_ref, sc1_ref, win_ref, vnw_ref, vnb_ref, ws_ref, bs_ref, wout_ref,
                 npost_ref, g1_ref, gpre_ref, sh_ref, sc_ref, rw_ref, rb_ref,
                 x1_ref, hf_ref, idx_ref, gate_ref):
    x = x_ref[...]
    h = _rms(x) * gpre1_ref[...]
    h = h * (1.0 + sc1_ref[...]) + sh1_ref[...]
    z = _dot(h.astype(BF16), win_ref[...])
    z = 0.5 * z * (1.0 + lax.erf(z * float(2.0 ** -0.5)))
    u, v = z[:, :GM_WIDTH], z[:, GM_WIDTH:]
    mu = jnp.mean(v, axis=-1, keepdims=True)
    dv = v - mu
    var = jnp.mean(dv * dv, axis=-1, keepdims=True)
    v = (dv * lax.rsqrt(var + LN_EPS) * vnw_ref[...] + vnb_ref[...]).astype(BF16)
    gw = GM_WIDTH // GM_GROUPS
    rows = []
    for ci in range(x.shape[0] // CHUNK):
        cols = []
        for gi in range(GM_GROUPS):
            cols.append(_dot(ws_ref[gi], v[ci * CHUNK:(ci + 1) * CHUNK, gi * gw:(gi + 1) * gw]))
        rows.append(jnp.concatenate(cols, axis=1) + bs_ref[...])
    sp = jnp.concatenate(rows, axis=0)
    y = _dot((u * sp).astype(BF16), wout_ref[...])
    x1 = x + g1_ref[...] * (_rms(y) * npost_ref[...])
    x1_ref[...] = x1
    _ffn_pre(x1, gpre_ref, sh_ref, sc_ref, rw_ref, rb_ref, hf_ref, idx_ref, gate_ref)


def _gmlp(x, n_pre1, sh1, sc1, w_in, vn_w, vn_b, w_s, b_s, w_out, n_post, g1, n_pre, sh2, sc2, rw_pad, rb_pad):
    b, s, d = x.shape
    tm = ROW_TILE
    gw = GM_WIDTH // GM_GROUPS
    bs_full = jnp.repeat(b_s.T, gw, axis=1)
    row = lambda bi, i: (bi, i, 0)
    const = lambda bi, i: (0, 0)
    per_b = lambda bi, i: (bi, 0, 0)
    vec_d = pl.BlockSpec((1, d), const)
    vec_g = pl.BlockSpec((1, GM_WIDTH), const)
    mod_d = pl.BlockSpec((None, 1, d), per_b)
    return pl.pallas_call(
        _gmlp_kernel,
        grid=(b, s // tm),
        in_specs=[pl.BlockSpec((None, tm, d), row), vec_d, mod_d, mod_d,
                  pl.BlockSpec((d, 2 * GM_WIDTH), const), vec_g, vec_g,
                  pl.BlockSpec((GM_GROUPS, CHUNK, CHUNK), lambda bi, i: (0, 0, 0)),
                  pl.BlockSpec((CHUNK, GM_WIDTH), const), pl.BlockSpec((GM_WIDTH, d), const),
                  vec_d, mod_d, vec_d, mod_d, mod_d,
                  pl.BlockSpec((d, 2 * LOGIT_PAD), const), pl.BlockSpec((1, LOGIT_PAD), const)],
        out_specs=_token_out_specs(tm, d),
        out_shape=_token_out_shapes(b, s, d),
        compiler_params=_cparams(("parallel", "parallel")),
        name="gmlp_router",
    )(x, n_pre1.reshape(1, d), sh1, sc1, w_in.astype(BF16), vn_w.reshape(1, -1), vn_b.reshape(1, -1),
      w_s.astype(BF16), bs_full, w_out.astype(BF16), n_post.reshape(1, d), g1, n_pre.reshape(1, d), sh2, sc2,
      rw_pad, rb_pad)


MOE_ROWS = 256


def _expert_kernel(be_ref, na_ref, x_ref, wgu_ref, bgu_ref, wd_ref, bd_ref, y_ref, wgu_bf, wd_bf):
    i = pl.program_id(0)
    prev = be_ref[jnp.maximum(i - 1, 0)]
    changed = jnp.logical_or(i == 0, be_ref[i] != prev)

    @pl.when(changed)
    def _():
        wgu_bf[...] = wgu_ref[...].astype(BF16)
        wd_bf[...] = wd_ref[...].astype(BF16)

    @pl.when(i < na_ref[0])
    def _():
        gu = _dot(x_ref[...], wgu_bf[...]) + bgu_ref[...]
        x_glu = jnp.minimum(gu[:, :EXPERT_FF], SWIGLU_LIMIT)
        x_lin = jnp.clip(gu[:, EXPERT_FF:], -SWIGLU_LIMIT, SWIGLU_LIMIT)
        act = x_glu * jax.nn.sigmoid(SWIGLU_ALPHA * x_glu) * (x_lin + 1.0)
        y_ref[...] = (_dot(act.astype(BF16), wd_bf[...]) + bd_ref[...]).astype(BF16)

    @pl.when(i >= na_ref[0])
    def _():
        y_ref[...] = jnp.zeros(y_ref.shape, BF16)


def _experts(blk_expert, n_active, xg, layer, w_gu, b_gu, w_down, b_down):
    n_rows, d = xg.shape
    n_l, n_e, _, ff2 = w_gu.shape
    tb = MOE_ROWS
    n_blocks = n_rows // tb
    return pl.pallas_call(
        _expert_kernel,
        grid_spec=pltpu.PrefetchScalarGridSpec(
            num_scalar_prefetch=2,
            grid=(n_blocks,),
            in_specs=[pl.BlockSpec((tb, d), lambda i, be, na: (i, 0)),
                      pl.BlockSpec((None, None, d, ff2), lambda i, be, na: (layer, be[i], 0, 0)),
                      pl.BlockSpec((None, None, 1, ff2), lambda i, be, na: (layer, be[i], 0, 0)),
                      pl.BlockSpec((None, None, ff2 // 2, d), lambda i, be, na: (layer, be[i], 0, 0)),
                      pl.BlockSpec((None, None, 1, d), lambda i, be, na: (layer, be[i], 0, 0))],
            out_specs=pl.BlockSpec((tb, d), lambda i, be, na: (i, 0)),
            scratch_shapes=[pltpu.VMEM((d, ff2), BF16), pltpu.VMEM((ff2 // 2, d), BF16)]),
        out_shape=jax.ShapeDtypeStruct((n_rows, d), BF16),
        compiler_params=_cparams(("arbitrary",)),
        name="moe_experts",
    )(blk_expert, n_active, xg, w_gu, b_gu.reshape(n_l, n_e, 1, ff2), w_down, b_down.reshape(n_l, n_e, 1, d))


def _combine_kernel(x_ref, y_ref, gate_ref, npost_ref, g2_ref, o_ref):
    gate = gate_ref[...]
    f = jnp.zeros(x_ref.shape, F32)
    for kth in range(TOP_K):
        f = f + gate[:, kth:kth + 1] * y_ref[kth].astype(F32)
    o_ref[...] = x_ref[...] + g2_ref[...] * (_rms(f) * npost_ref[...])


def _combine(x1, yk, gates, n_post, g2):
    b, s, d = x1.shape
    tm = ROW_TILE
    row = lambda bi, i: (bi, i, 0)
    return pl.pallas_call(
        _combine_kernel,
        grid=(b, s // tm),
        in_specs=[pl.BlockSpec((None, tm, d), row),
                  pl.BlockSpec((TOP_K, None, tm, d), lambda bi, i: (0, bi, i, 0)),
                  pl.BlockSpec((None, tm, LOGIT_PAD), row),
                  pl.BlockSpec((1, d), lambda bi, i: (0, 0)),
                  pl.BlockSpec((None, 1, d), lambda bi, i: (bi, 0, 0))],
        out_specs=pl.BlockSpec((None, tm, d), row),
        out_shape=jax.ShapeDtypeStruct((b, s, d), F32),
        compiler_params=_cparams(("parallel", "parallel")),
        name="moe_combine_residual",
    )(x1, yk, gates, n_post.reshape(1, d), g2)


def _lookup(table, idx):
    n = table.shape[0]
    return jnp.sum(jnp.where(idx[:, None] == jnp.arange(n, dtype=jnp.int32)[None, :], table[None, :], 0), axis=1)


def _rows(a, idx):
    return a.at[idx].get(mode="promise_in_bounds")


def _moe(x1, hf, top_idx, gates, layer,w_gu, b_gu, w_down, b_down, n_post, g2):
    b, s, d = x1.shape
    n_tok = b * s
    tb = MOE_ROWS
    n_assign = n_tok * TOP_K
    assert N_EXPERTS * n_assign < 2 ** 31
    i32 = jnp.int32
    e_flat = top_idx[..., :TOP_K].reshape(-1)
    skey = jnp.sort(e_flat * n_assign + jnp.arange(n_assign, dtype=i32))
    order = skey % n_assign
    _, inv = lax.sort((order, jnp.arange(n_assign, dtype=i32)), num_keys=1)
    edges = jnp.arange(N_EXPERTS + 1, dtype=i32) * n_assign
    bounds = jnp.sum((skey[None, :] < edges[:, None]).astype(i32), axis=1)
    start = bounds[:-1]
    counts = bounds[1:] - start
    padded = (counts + tb - 1) // tb * tb
    pend = jnp.cumsum(padded)
    pstart = pend - padded
    n_rows = -(-n_assign // tb) * tb + N_EXPERTS * tb
    n_blocks = n_rows // tb
    blk_start = jnp.arange(n_blocks, dtype=i32) * tb
    blk_expert = jnp.minimum(jnp.sum((pend[None, :] <= blk_start[:, None]).astype(i32), axis=1), N_EXPERTS - 1)
    n_active = (pend[-1] // tb).astype(i32).reshape(1)
    j = jnp.arange(n_rows, dtype=i32) - jnp.repeat(_lookup(pstart, blk_expert), tb)
    src = jnp.repeat(_lookup(start, blk_expert), tb) + j
    valid = j < jnp.repeat(_lookup(counts, blk_expert), tb)
    row_tok = jnp.where(valid, _rows(order, jnp.clip(src, 0, n_assign - 1)) // TOP_K, 0)
    pos = _lookup(pstart - start, e_flat) + inv
    xg = _rows(hf.reshape(n_tok, d), row_tok)
    y = _experts(blk_expert, n_active, xg, layer, w_gu, b_gu, w_down, b_down)
    yk = _rows(y, pos.reshape(n_tok, TOP_K).T).reshape(TOP_K, b, s, d)
    return _combine(x1, yk, gates, n_post, g2)


def _block_diag(n, blk, val):
    return (jnp.kron(jnp.eye(n // blk, dtype=F32), jnp.ones((blk, blk), F32)) * val).astype(BF16)


def kernel(x, c, ctx, c_ctx, ada_w, ada_b, norm_mix_pre, norm_mix_post, norm_ffn_pre, norm_ffn_post, router_w, router_b, moe_w_gu, moe_b_gu, moe_w_down, moe_b_down, hy_w_in, mla_q_norm, mla_w_uq, mla_kv_norm, mla_w_ukv, rwkv_mu_prev, rwkv_mu_next, rwkv_w0, rwkv_w2, rwkv_a0, rwkv_a2, rwkv_g2, rwkv_k_k, rwkv_k_a, rwkv_r_k, rwkv_ln_w, rwkv_ln_b, hy_w_out, gm_w_in, gm_v_norm_w, gm_v_norm_b, gm_w_s, gm_b_s, gm_w_out):
    b, s, d = x.shape
    n_ctx = ctx.shape[1]
    t_all = s + n_ctx
    assert b + 1 <= 8 and s % ROW_TILE == 0 and n_ctx % ROW_TILE == 0

    cond_rows = jnp.concatenate([c, c_ctx[None], jnp.zeros((8 - b - 1, d), F32)], axis=0)
    mod = _modulation(cond_rows, ada_w, ada_b)

    def lat_mod(l, j):
        return mod[l, :b, j * d:(j + 1) * d].reshape(b, 1, d)

    sh_all = jnp.stack([mod[0, :b, 0:d], jnp.broadcast_to(mod[0, b, 0:d], (b, d))], axis=1).reshape(2 * b, 1, d)
    sc_all = jnp.stack([mod[0, :b, d:2 * d], jnp.broadcast_to(mod[0, b, d:2 * d], (b, d))], axis=1).reshape(2 * b, 1, d)
    x_all = jnp.concatenate([x, ctx], axis=1)
    cos, sin = _rope_tables(s, n_ctx)
    w1, wq2, wk2 = _inproj_weights(hy_w_in[0], mla_w_uq[0], mla_w_ukv[0])
    q, k, v, prw = _inproj(x_all, s, norm_mix_pre[0], sh_all, sc_all, cos, sin, w1, mla_q_norm[0], mla_kv_norm[0],
                           wq2, wk2)
    tq = 512 if s % 512 == 0 else ROW_TILE
    tk = next((t for t in (8320, 1280) if t_all % t == 0), ROW_TILE)
    o_attn = _attention(q, k, v, s, tq, tk)

    bd_ones = _block_diag(RWKV_DIM, RWKV_HEAD, 1.0)
    bd_mean = _block_diag(RWKV_DIM, RWKV_HEAD, 1.0 / RWKV_HEAD)
    r, kx, vx, g, kkn, lw, kd, bdir = _rwkv_prepare(prw, s, rwkv_mu_prev[0], rwkv_mu_next[0], rwkv_w0[0], rwkv_w2[0],
                                                    rwkv_a0[0], rwkv_a2[0], rwkv_g2[0], rwkv_k_k[0], rwkv_k_a[0],
                                                    bd_ones)
    m_c, n_c, r_c, y_c = _wkv_chunks(r, vx, kkn, lw, kd, bdir)
    n_lat_ch = s // WKV_CHUNK
    n_ctx_ch = n_ctx // WKV_CHUNK
    lat_ch = np.arange(n_lat_ch)
    ctx_ch = n_lat_ch + np.arange(n_ctx_ch)
    order = jnp.asarray(np.stack([np.concatenate([ctx_ch, lat_ch]),
                                  np.concatenate([ctx_ch[::-1], lat_ch[::-1]])]).astype(np.int32))
    yf, yb = _wkv_states(order, m_c, n_c, r_c, y_c, t_all)

    rw_pad, rb_pad = _router_pads(router_w[0], router_b[0])
    x1, hf, top_idx, gates = _mix_out(yf, yb, r, kx, vx, g, o_attn, x, rwkv_r_k[0].reshape(-1), rwkv_ln_w[0],
                                      rwkv_ln_b[0], bd_mean, bd_ones, hy_w_out[0].astype(BF16), norm_mix_post[0],
                                      lat_mod(0, 2), norm_ffn_pre[0], lat_mod(0, 3), lat_mod(0, 4), rw_pad, rb_pad)
    x2 = _moe(x1, hf, top_idx, gates, 0, moe_w_gu, moe_b_gu, moe_w_down, moe_b_down, norm_ffn_post[0], lat_mod(0, 5))

    rw_pad, rb_pad = _router_pads(router_w[1], router_b[1])
    x3, hf, top_idx, gates = _gmlp(x2, norm_mix_pre[1], lat_mod(1, 0), lat_mod(1, 1), gm_w_in[0], gm_v_norm_w[0],
                                   gm_v_norm_b[0], gm_w_s[0], gm_b_s[0], gm_w_out[0], norm_mix_post[1], lat_mod(1, 2),
                                   norm_ffn_pre[1], lat_mod(1, 3), lat_mod(1, 4), rw_pad, rb_pad)
    return _moe(x3, hf, top_idx, gates, 1, moe_w_gu, moe_b_gu, moe_w_down, moe_b_down, norm_ffn_post[1], lat_mod(1, 5))
```

```python
import functools

import jax
import jax.numpy as jnp
import numpy as np
from jax import lax
from jax.experimental import pallas as pl
from jax.experimental.pallas import tpu as pltpu

F32 = jnp.float32
BF16 = jnp.bfloat16
HIGHEST = lax.Precision.HIGHEST

D_MODEL = 1024
GRID_W = 64
EPS = 1e-6

MLA_HEADS = 8
MLA_NOPE = 64
MLA_ROPE = 32
MLA_V = 64
MLA_Q_RANK = 256
MLA_KV_RANK = 128
MLA_SCALE = (MLA_NOPE + MLA_ROPE) ** -0.5
ROPE_FREQS = MLA_ROPE // 4
ROPE_BASE = 10000.0
HEAD_PAD = 128
ATT_QSUB = 128
ATT_KSUB = 640
LOG2E = 1.4426950408889634
ATT_EXP_DTYPE = jnp.float32

RWKV_HEAD = 64
RWKV_DIM = 512
RWKV_HEADS = 8
DECAY_LORA = 32
AAA_LORA = 32
GATE_LORA = 96
RWKV_PAD_COLS = 3 * RWKV_DIM + 128 + 128
RWKV_LN_EPS = 64e-5
WKV_CHUNK = 64
WKV_CHUNKS_PER_STEP = 4

CHUNK = 128
GM_WIDTH = 1024
GM_GROUPS = 8
LN_EPS = 1e-5

N_EXPERTS = 32
TOP_K = 4
EXPERT_FF = 1024
SWIGLU_LIMIT = 7.0
SWIGLU_ALPHA = 1.702
LOGIT_PAD = 128
NEG_BIG = -1e30

ROW_TILE = 256
VMEM_LIMIT = 48 * 1024 * 1024


def _cparams(sem):
    return pltpu.CompilerParams(dimension_semantics=sem, vmem_limit_bytes=VMEM_LIMIT)


def _rms(x):
    return x * lax.rsqrt(jnp.mean(x * x, axis=-1, keepdims=True) + EPS)


def _dot(a, b, precision=None):
    return jnp.dot(a, b, preferred_element_type=F32, precision=precision)


def _dot_nt(a, b, precision=None):
    return lax.dot_general(a, b, (((1,), (1,)), ((), ())), preferred_element_type=F32, precision=precision)


def _dot_tn(a, b, precision=None):
    return lax.dot_general(a, b, (((0,), (0,)), ((), ())), preferred_element_type=F32, precision=precision)


def _split(x):
    hi = x.astype(BF16)
    return hi, (x - hi.astype(F32)).astype(BF16)


def _dot_split(a, b):
    a_hi, a_lo = _split(a)
    b_hi, b_lo = _split(b)
    return _dot(jnp.concatenate([a_hi, a_hi, a_lo], axis=1), jnp.concatenate([b_hi, b_lo, b_hi], axis=0))


def _mod_kernel(s_ref, w_ref, b_ref, o_ref):
    s = s_ref[...]
    s = s * jax.nn.sigmoid(s)
    o_ref[...] = _dot(s, w_ref[...], HIGHEST) + b_ref[...]


def _modulation(cond_rows, ada_w, ada_b):
    n_l, d, n6 = ada_w.shape
    tn = 1536
    return pl.pallas_call(
        _mod_kernel,
        grid=(n_l, n6 // tn),
        in_specs=[pl.BlockSpec((8, d), lambda l, j: (0, 0)),
                  pl.BlockSpec((None, d, tn), lambda l, j: (l, 0, j)),
                  pl.BlockSpec((None, 1, tn), lambda l, j: (l, 0, j))],
        out_specs=pl.BlockSpec((None, 8, tn), lambda l, j: (l, 0, j)),
        out_shape=jax.ShapeDtypeStruct((n_l, 8, n6), F32),
        compiler_params=_cparams(("arbitrary", "arbitrary")),
        name="adaln_mod",
    )(cond_rows, ada_w, ada_b.reshape(n_l, 1, n6))


def _inproj_kernel(x_ref, g_ref, sh_ref, sc_ref, cos_ref, sin_ref, w1_ref, qn_ref, kvn_ref, wq_ref, wk_ref,
                   q_ref, k_ref, vt_ref, prw_ref):
    h = _rms(x_ref[...]) * g_ref[...]
    h = h * (1.0 + sc_ref[...]) + sh_ref[...]
    p = _dot(h.astype(BF16), w1_ref[...])
    cos = jnp.concatenate([cos_ref[...]] * MLA_HEADS, axis=1)
    sin = jnp.concatenate([sin_ref[...]] * MLA_HEADS, axis=1)
    nq = MLA_HEADS * HEAD_PAD
    qn = _rms(p[:, :MLA_Q_RANK]) * qn_ref[...]
    qq = _dot(qn.astype(BF16), wq_ref[...])
    q_ref[...] = (qq[:, :nq] * cos + qq[:, nq:] * sin).astype(BF16)
    kvn = _rms(p[:, MLA_Q_RANK:MLA_Q_RANK + MLA_KV_RANK]) * kvn_ref[...]
    x2 = jnp.concatenate([kvn, p[:, MLA_Q_RANK + MLA_KV_RANK:512]], axis=1).astype(BF16)
    kk = _dot(x2, wk_ref[...])
    k_ref[...] = (kk[:, :nq] * cos + kk[:, nq:2 * nq] * sin).astype(BF16)
    vt_ref[...] = kk[:, 2 * nq:].T.astype(BF16)
    prw_ref[...] = p[:, 512:]


def _rotate_half_cols(w):
    wr = w.reshape(w.shape[:-1] + (2, 2, ROPE_FREQS))
    return jnp.stack([-wr[..., 1, :], wr[..., 0, :]], axis=-2).reshape(w.shape)


def _inproj_weights(hy_w_in, w_uq, w_ukv):
    d = hy_w_in.shape[0]
    mla_cols = MLA_Q_RANK + MLA_KV_RANK + MLA_ROPE
    w_mla = hy_w_in[:, :mla_cols]
    w_rw = hy_w_in[:, mla_cols:]
    w1 = jnp.concatenate([w_mla, jnp.zeros((d, 512 - mla_cols), F32), _pad_rwkv_cols(w_rw)], axis=1).astype(BF16)
    wq = (w_uq * (MLA_SCALE * LOG2E)).reshape(MLA_Q_RANK, MLA_HEADS, MLA_NOPE + MLA_ROPE)
    z_n = jnp.zeros((MLA_Q_RANK, MLA_HEADS, MLA_NOPE), F32)
    z_p = jnp.zeros((MLA_Q_RANK, MLA_HEADS, HEAD_PAD - MLA_NOPE - MLA_ROPE), F32)
    wq_plain = jnp.concatenate([wq[..., :MLA_NOPE], wq[..., MLA_NOPE:], z_p], axis=-1)
    wq_rot = jnp.concatenate([z_n, _rotate_half_cols(wq[..., MLA_NOPE:]), z_p], axis=-1)
    wq2 = jnp.concatenate([wq_plain.reshape(MLA_Q_RANK, -1), wq_rot.reshape(MLA_Q_RANK, -1)], axis=1).astype(BF16)
    wkv = w_ukv.reshape(MLA_KV_RANK, MLA_HEADS, MLA_NOPE + MLA_V)
    eye = jnp.broadcast_to(jnp.eye(MLA_ROPE, dtype=F32)[:, None, :], (MLA_ROPE, MLA_HEADS, MLA_ROPE))
    pad_k = HEAD_PAD - MLA_NOPE - MLA_ROPE
    top_plain = jnp.concatenate([wkv[..., :MLA_NOPE], jnp.zeros((MLA_KV_RANK, MLA_HEADS, MLA_ROPE + pad_k), F32)], -1)
    mid_plain = jnp.concatenate([jnp.zeros((MLA_ROPE, MLA_HEADS, MLA_NOPE), F32), eye,
                                 jnp.zeros((MLA_ROPE, MLA_HEADS, pad_k), F32)], -1)
    mid_rot = jnp.concatenate([jnp.zeros((MLA_ROPE, MLA_HEADS, MLA_NOPE), F32), _rotate_half_cols(eye),
                               jnp.zeros((MLA_ROPE, MLA_HEADS, pad_k), F32)], -1)
    nq = MLA_HEADS * HEAD_PAD
    rows_c = jnp.concatenate([top_plain.reshape(MLA_KV_RANK, nq), jnp.zeros((MLA_KV_RANK, nq), F32),
                              wkv[..., MLA_NOPE:].reshape(MLA_KV_RANK, MLA_HEADS * MLA_V)], axis=1)
    rows_r = jnp.concatenate([mid_plain.reshape(MLA_ROPE, nq), mid_rot.reshape(MLA_ROPE, nq),
                              jnp.zeros((MLA_ROPE, MLA_HEADS * MLA_V), F32)], axis=1)
    rows_z = jnp.zeros((256 - MLA_KV_RANK - MLA_ROPE, rows_c.shape[1]), F32)
    wk2 = jnp.concatenate([rows_c, rows_r, rows_z], axis=0).astype(BF16)
    return w1, wq2, wk2


def _pad_rwkv_cols(w):
    pad = jnp.zeros(w.shape[:-1] + (RWKV_PAD_COLS - w.shape[-1],), w.dtype)
    return jnp.concatenate([w, pad], axis=-1)


def _rope_tables(n_lat, n_ctx):
    t = jnp.arange(n_lat)
    row = (t // GRID_W).astype(F32)
    col = (t % GRID_W).astype(F32)
    inv = ROPE_BASE ** (-jnp.arange(ROPE_FREQS, dtype=F32) / ROPE_FREQS)
    ang = jnp.stack([row[:, None] * inv, col[:, None] * inv], axis=1)
    ang = jnp.broadcast_to(ang[:, :, None, :], (n_lat, 2, 2, ROPE_FREQS)).reshape(n_lat, MLA_ROPE)
    cos = jnp.concatenate([jnp.ones((n_lat, MLA_NOPE), F32), jnp.cos(ang),
                           jnp.ones((n_lat, HEAD_PAD - MLA_NOPE - MLA_ROPE), F32)], axis=1)
    sin = jnp.concatenate([jnp.zeros((n_lat, MLA_NOPE), F32), jnp.sin(ang),
                           jnp.zeros((n_lat, HEAD_PAD - MLA_NOPE - MLA_ROPE), F32)], axis=1)
    cos = jnp.concatenate([cos, jnp.ones((n_ctx, HEAD_PAD), F32)], axis=0)
    sin = jnp.concatenate([sin, jnp.zeros((n_ctx, HEAD_PAD), F32)], axis=0)
    return cos, sin


def _inproj(x_all, n_lat, gain, shift, scale, cos, sin, w1, q_norm, kv_norm, wq2, wk2):
    b, t_all, d = x_all.shape
    tm = ROW_TILE
    n_lat_tiles = n_lat // tm
    nq = MLA_HEADS * HEAD_PAD

    def mod_map(bi, i):
        return (2 * bi + jnp.where(i >= n_lat_tiles, 1, 0), 0, 0)

    const = lambda bi, i: (0, 0)
    row = lambda bi, i: (bi, i, 0)
    return pl.pallas_call(
        _inproj_kernel,
        grid=(b, t_all // tm),
        in_specs=[pl.BlockSpec((None, tm, d), row),
                  pl.BlockSpec((1, d), const),
                  pl.BlockSpec((None, 1, d), mod_map),
                  pl.BlockSpec((None, 1, d), mod_map),
                  pl.BlockSpec((tm, HEAD_PAD), lambda bi, i: (i, 0)),
                  pl.BlockSpec((tm, HEAD_PAD), lambda bi, i: (i, 0)),
                  pl.BlockSpec(w1.shape, const),
                  pl.BlockSpec((1, MLA_Q_RANK), const),
                  pl.BlockSpec((1, MLA_KV_RANK), const),
                  pl.BlockSpec(wq2.shape, const),
                  pl.BlockSpec(wk2.shape, const)],
        out_specs=[pl.BlockSpec((None, tm, nq), row),
                   pl.BlockSpec((None, tm, nq), row),
                   pl.BlockSpec((None, MLA_HEADS * MLA_V, tm), lambda bi, i: (bi, 0, i)),
                   pl.BlockSpec((None, tm, RWKV_PAD_COLS), row)],
        out_shape=[jax.ShapeDtypeStruct((b, t_all, nq), BF16),
                   jax.ShapeDtypeStruct((b, t_all, nq), BF16),
                   jax.ShapeDtypeStruct((b, MLA_HEADS * MLA_V, t_all), BF16),
                   jax.ShapeDtypeStruct((b, t_all, RWKV_PAD_COLS), F32)],
        compiler_params=_cparams(("parallel", "parallel")),
        name="hybrid_inproj",
    )(x_all, gain.reshape(1, d), shift, scale, cos, sin, w1, q_norm.reshape(1, -1), kv_norm.reshape(1, -1), wq2, wk2)


def _attn_kernel(q_ref, k_ref, vt_ref, o_ref, m_sc, l_sc, acc_sc):
    ki = pl.program_id(3)

    @pl.when(ki == 0)
    def _():
        m_sc[...] = jnp.full(m_sc.shape, -jnp.inf, F32)
        l_sc[...] = jnp.zeros(l_sc.shape, F32)
        acc_sc[...] = jnp.zeros(acc_sc.shape, F32)

    tq, tk = q_ref.shape[0], k_ref.shape[0]
    chains = [(h, qs) for h in range(2) for qs in range(tq // ATT_QSUB)]
    hsl = [slice(h * HEAD_PAD, (h + 1) * HEAD_PAD) for h, _ in chains]
    vsl = [slice(h * MLA_V, (h + 1) * MLA_V) for h, _ in chains]
    qsl = [slice(qs * ATT_QSUB, (qs + 1) * ATT_QSUB) for _, qs in chains]
    nch = len(chains)
    q = [q_ref[qsl[c], hsl[c]] for c in range(nch)]
    m = [m_sc[chains[c][0], 0:1, qsl[c]] for c in range(nch)]
    l = [l_sc[chains[c][0], 0:1, qsl[c]] for c in range(nch)]
    acc = [acc_sc[chains[c][0], :, qsl[c]] for c in range(nch)]
    ksub = ATT_KSUB if tk % ATT_KSUB == 0 else tk
    for kb in range(tk // ksub):
        ksl = slice(kb * ksub, (kb + 1) * ksub)
        s = [_dot_nt(k_ref[ksl, hsl[c]], q[c]).astype(ATT_EXP_DTYPE) for c in range(nch)]
        m_new = [jnp.maximum(m[c], jnp.max(s[c], axis=0, keepdims=True).astype(F32)) for c in range(nch)]
        alpha = [jnp.exp2(m[c] - m_new[c]) for c in range(nch)]
        p = [jnp.exp2(s[c] - m_new[c].astype(ATT_EXP_DTYPE)) for c in range(nch)]
        l = [alpha[c] * l[c] + jnp.sum(p[c].astype(F32), axis=0, keepdims=True) for c in range(nch)]
        acc = [acc[c] * alpha[c] + _dot(vt_ref[vsl[c], ksl], p[c].astype(BF16)) for c in range(nch)]
        m = m_new
    for c in range(nch):
        h = chains[c][0]
        m_sc[h, :, qsl[c]] = jnp.broadcast_to(m[c], (8, ATT_QSUB))
        l_sc[h, :, qsl[c]] = jnp.broadcast_to(l[c], (8, ATT_QSUB))
        acc_sc[h, :, qsl[c]] = acc[c]

    @pl.when(ki == pl.num_programs(3) - 1)
    def _():
        o0 = (acc_sc[0] / l_sc[0, 0:1, :]).T
        o1 = (acc_sc[1] / l_sc[1, 0:1, :]).T
        o_ref[...] = jnp.concatenate([o0, o1], axis=1).astype(BF16)


def _attention(q, k, vt, n_lat, tq, tk):
    b, t_all, _ = k.shape
    return pl.pallas_call(
        _attn_kernel,
        grid=(b, MLA_HEADS // 2, n_lat // tq, t_all // tk),
        in_specs=[pl.BlockSpec((None, tq, 2 * HEAD_PAD), lambda bi, hp, qi, ki: (bi, qi, hp)),
                  pl.BlockSpec((None, tk, 2 * HEAD_PAD), lambda bi, hp, qi, ki: (bi, ki, hp)),
                  pl.BlockSpec((None, 2 * MLA_V, tk), lambda bi, hp, qi, ki: (bi, hp, ki))],
        out_specs=pl.BlockSpec((None, tq, 2 * MLA_V), lambda bi, hp, qi, ki: (bi, qi, hp)),
        out_shape=jax.ShapeDtypeStruct((b, n_lat, MLA_HEADS * MLA_V), BF16),
        scratch_shapes=[pltpu.VMEM((2, 8, tq), F32), pltpu.VMEM((2, 8, tq), F32), pltpu.VMEM((2, MLA_V, tq), F32)],
        compiler_params=_cparams(("parallel", "parallel", "parallel", "arbitrary")),
        name="mla_attention",
    )(q, k, vt)


def _seg_sum(x, bd_ref):
    hi = x.astype(BF16)
    lo = (x - hi.astype(F32)).astype(BF16)
    bd = bd_ref[...]
    return _dot(hi, bd) + _dot(lo, bd)


def _rwkv_prep_kernel(p_ref, prev_ref, next_ref, mup_ref, mun_ref, wl_ref, w0a0_ref, g2_ref, kk_ref, ka_ref, bd_ref,
                      r_ref, k_ref, v_ref, g_ref, kkn_ref, lw_ref, kd_ref, bdir_ref, *, n_lat_tiles, n_tiles):
    i = pl.program_id(1)
    p = p_ref[...]
    tm = p.shape[0]
    first = jnp.logical_or(i == 0, i == n_lat_tiles)
    last = jnp.logical_or(i == n_lat_tiles - 1, i == n_tiles - 1)
    prev_row = jnp.where(first, 0.0, prev_ref[7:8, :])
    next_row = jnp.where(last, 0.0, next_ref[0:1, :])
    ridx = lax.broadcasted_iota(jnp.int32, p.shape, 0)
    prev = jnp.where(ridx == 0, prev_row, pltpu.roll(p, 1, 0))
    nxt = jnp.where(ridx == tm - 1, next_row, pltpu.roll(p, tm - 1, 0))
    p = p + mup_ref[...] * (prev - p) + mun_ref[...] * (nxt - p)
    c = RWKV_DIM
    r, k, v = p[:, :c], p[:, c:2 * c], p[:, 2 * c:3 * c]
    lo = p[:, 3 * c:3 * c + 128]
    lane = lax.broadcasted_iota(jnp.int32, lo.shape, 1)
    lo = jnp.where(lane < 2 * DECAY_LORA, jnp.tanh(lo), lo)
    wa = _dot(lo.astype(BF16), wl_ref[...]) + w0a0_ref[...]
    gd = p[:, 3 * c + 128:]
    g_ref[...] = _dot(jax.nn.sigmoid(gd).astype(BF16), g2_ref[...])
    kk = k * kk_ref[...]
    kk = kk * lax.rsqrt(jnp.maximum(_seg_sum(kk * kk, bd_ref), 1e-24))
    r_ref[...] = r
    k_ref[...] = k
    v_ref[...] = v
    kkn_ref[...] = kk
    ka = ka_ref[...]
    for d in range(2):
        w = wa[:, d * c:(d + 1) * c]
        a = jax.nn.sigmoid(wa[:, (2 + d) * c:(3 + d) * c])
        lw_ref[:, d * c:(d + 1) * c] = -float(np.exp(-0.5)) * jax.nn.sigmoid(w)
        kd_ref[:, d * c:(d + 1) * c] = k * (1.0 + (a - 1.0) * ka)
        bdir_ref[:, d * c:(d + 1) * c] = kk * a


def _rwkv_prepare(prw, n_lat, mu_prev, mu_next, w0, w2, a0, a2, g2, k_k, k_a, bd_ones):
    b, t_all, pc = prw.shape
    tm = ROW_TILE
    n_tiles = t_all // tm
    n_lat_tiles = n_lat // tm
    c = RWKV_DIM
    z = jnp.zeros((DECAY_LORA, c), F32)
    wl = jnp.concatenate([
        jnp.concatenate([w2[0], z, z, z], axis=1), jnp.concatenate([z, w2[1], z, z], axis=1),
        jnp.concatenate([z, z, a2[0], z], axis=1), jnp.concatenate([z, z, z, a2[1]], axis=1)], axis=0).astype(BF16)
    w0a0 = jnp.concatenate([w0[0], w0[1], a0[0], a0[1]]).reshape(1, 4 * c)
    g2p = jnp.concatenate([g2, jnp.zeros((128 - GATE_LORA, c), F32)], axis=0).astype(BF16)
    row = lambda bi, i: (bi, i, 0)
    const = lambda bi, i: (0, 0)
    hb = tm // 8
    n_hb = t_all // 8
    kern = functools.partial(_rwkv_prep_kernel, n_lat_tiles=n_lat_tiles, n_tiles=n_tiles)
    o_c = jax.ShapeDtypeStruct((b, t_all, c), F32)
    o_2c = jax.ShapeDtypeStruct((b, t_all, 2 * c), F32)
    return pl.pallas_call(
        kern,
        grid=(b, n_tiles),
        in_specs=[pl.BlockSpec((None, tm, pc), row),
                  pl.BlockSpec((None, 8, pc), lambda bi, i: (bi, jnp.maximum(i * hb - 1, 0), 0)),
                  pl.BlockSpec((None, 8, pc), lambda bi, i: (bi, jnp.minimum((i + 1) * hb, n_hb - 1), 0)),
                  pl.BlockSpec((1, pc), const), pl.BlockSpec((1, pc), const),
                  pl.BlockSpec(wl.shape, const), pl.BlockSpec((1, 4 * c), const),
                  pl.BlockSpec(g2p.shape, const), pl.BlockSpec((1, c), const), pl.BlockSpec((1, c), const),
                  pl.BlockSpec((c, c), const)],
        out_specs=[pl.BlockSpec((None, tm, c), row)] * 5 + [pl.BlockSpec((None, tm, 2 * c), row)] * 3,
        out_shape=[o_c] * 5 + [o_2c] * 3,
        compiler_params=_cparams(("parallel", "parallel")),
        name="rwkv_prepare",
    )(prw, prw, prw, _pad_rwkv_cols(mu_prev).reshape(1, pc), _pad_rwkv_cols(mu_next).reshape(1, pc),
      wl, w0a0, g2p, k_k.reshape(1, c), k_a.reshape(1, c), bd_ones)


def _wkv_chunk_kernel(r_ref, v_ref, kk_ref, lw_ref, kd_ref, bd_ref, m_ref, n_ref, rr_ref, yv_ref):
    d = pl.program_id(1)
    cs = WKV_CHUNK
    nc = r_ref.shape[0] // cs
    ti = lax.broadcasted_iota(jnp.int32, (cs, cs), 0)
    si = lax.broadcasted_iota(jnp.int32, (cs, cs), 1)
    rel = (si - ti) * (1 - 2 * d)
    incl = rel <= 0
    strict = rel < 0
    eye = si == ti
    incl_f = incl.astype(F32)
    ops = []
    for ci in range(nc):
        rows = slice(ci * cs, (ci + 1) * cs)
        lw = lw_ref[rows, :]
        g = _dot(incl_f, lw, HIGHEST)
        total = jnp.sum(lw, axis=0, keepdims=True)
        e_inv = jnp.exp(-g)
        e_end = jnp.exp(total - g)
        gam = jnp.exp(total)
        kd = kd_ref[rows, :]
        bd = bd_ref[rows, :]
        at = -kk_ref[rows, :] * jnp.exp(g - lw)
        rt = r_ref[rows, :] * jnp.exp(g)
        kt = (kd * e_inv).astype(BF16)
        bt = (bd * e_inv).astype(BF16)
        ke = (kd * e_end).astype(BF16)
        be = (bd * e_end).astype(BF16)
        v = v_ref[rows, :].astype(BF16)
        for h in range(RWKV_HEADS):
            sl = slice(h * RWKV_HEAD, (h + 1) * RWKV_HEAD)
            ops.append((at[:, sl], rt[:, sl], kt[:, sl], bt[:, sl], ke[:, sl], be[:, sl], v[:, sl], gam[:, sl]))
    n_it = len(ops)
    aa = [_dot_nt(jnp.concatenate([o[0], o[1]], axis=0).astype(BF16), jnp.concatenate([o[3], o[2]], axis=0))
          for o in ops]
    a_ab = [jnp.where(strict, x[:cs, :cs], 0.0) for x in aa]
    a_ak = [jnp.where(strict, x[:cs, cs:], 0.0).astype(BF16) for x in aa]
    a_rb = [jnp.where(incl, x[cs:, :cs], 0.0).astype(BF16) for x in aa]
    a_rk = [jnp.where(incl, x[cs:, cs:], 0.0).astype(BF16) for x in aa]
    z = [jnp.concatenate([ops[i][0], _dot(a_ak[i], ops[i][6])], axis=1) for i in range(n_it)]
    pw = a_ab
    n_sq = int(np.log2(cs))
    for it in range(n_sq):
        more = it + 1 < n_sq
        res = [_dot_split(pw[i], jnp.concatenate([z[i], pw[i]], axis=1) if more else z[i]) for i in range(n_it)]
        z = [z[i] + res[i][:, :2 * cs] for i in range(n_it)]
        if more:
            pw = [x[:, 2 * cs:] for x in res]
    zb = [x.astype(BF16) for x in z]
    w2 = [_dot(a_rb[i], zb[i]) for i in range(n_it)]
    w3 = [_dot_tn(ops[i][5], zb[i]) for i in range(n_it)]
    n0 = [_dot_tn(ops[i][4], ops[i][6]) for i in range(n_it)]
    y0 = [_dot(a_rk[i], ops[i][6]) for i in range(n_it)]
    nh = RWKV_HEADS
    for ci in range(nc):
        rows = slice(ci * cs, (ci + 1) * cs)
        ids = range(ci * nh, (ci + 1) * nh)
        m_ref[rows, :] = jnp.concatenate([jnp.where(eye, ops[i][7], 0.0) + w3[i][:, :cs] for i in ids], axis=1)
        n_ref[rows, :] = jnp.concatenate([n0[i] + w3[i][:, cs:] for i in ids], axis=1)
        rr_ref[rows, :] = jnp.concatenate([ops[i][1] + w2[i][:, :cs] for i in ids], axis=1)
        yv_ref[rows, :] = jnp.concatenate([y0[i] + w2[i][:, cs:] for i in ids], axis=1)


def _wkv_chunks(r, v, kkn, lw, kd, bdir):
    b, t_all, c = r.shape
    rows = WKV_CHUNK * WKV_CHUNKS_PER_STEP
    nst = t_all // rows
    shared = pl.BlockSpec((None, rows, c), lambda bi, d, ci: (bi, ci, 0))
    per_dir = pl.BlockSpec((None, rows, c), lambda bi, d, ci: (bi, ci, d))
    out = pl.BlockSpec((None, None, rows, c), lambda bi, d, ci: (bi, d, ci, 0))
    o_s = jax.ShapeDtypeStruct((b, 2, t_all, c), F32)
    return pl.pallas_call(
        _wkv_chunk_kernel,
        grid=(b, 2, nst),
        in_specs=[shared, shared, shared, per_dir, per_dir, per_dir],
        out_specs=[out] * 4,
        out_shape=[o_s] * 4,
        compiler_params=_cparams(("parallel", "parallel", "parallel")),
        name="wkv_chunk_summaries",
    )(r, v, kkn, lw, kd, bdir)


def _wkv_state_kernel(order_ref, mf_ref, nf_ref, rf_ref, yf_ref, mb_ref, nb_ref, rb_ref, yb_ref,
                      of_ref, ob_ref, st_sc):
    del order_ref
    s = pl.program_id(0)

    @pl.when(s == 0)
    def _():
        st_sc[...] = jnp.zeros(st_sc.shape, F32)

    n_b = st_sc.shape[0]
    ins = ((mf_ref, nf_ref, rf_ref, yf_ref, of_ref), (mb_ref, nb_ref, rb_ref, yb_ref, ob_ref))
    for bi in range(n_b):
        for d in range(2):
            m_r, n_r, r_r, y_r, o_r = ins[d]
            st = st_sc[bi, d]
            m, n, rr, yv = m_r[bi], n_r[bi], r_r[bi], y_r[bi]
            ys, sts = [], []
            for h in range(RWKV_HEADS):
                sl = slice(h * RWKV_HEAD, (h + 1) * RWKV_HEAD)
                st_h = st[:, sl]
                ys.append(_dot(rr[:, sl], st_h) + yv[:, sl])
                sts.append(_dot(m[:, sl], st_h, HIGHEST) + n[:, sl])
            o_r[bi] = jnp.concatenate(ys, axis=1)
            st_sc[bi, d] = jnp.concatenate(sts, axis=1)


def _wkv_states(order, m, n, rr, yv, t_all):
    b, _, _, c = m.shape
    cs = WKV_CHUNK
    nch = t_all // cs
    fwd = pl.BlockSpec((b, None, cs, c), lambda s, o: (0, 0, o[0, s], 0))
    bwd = pl.BlockSpec((b, None, cs, c), lambda s, o: (0, 1, o[1, s], 0))
    out_f = pl.BlockSpec((b, cs, c), lambda s, o: (0, o[0, s], 0))
    out_b = pl.BlockSpec((b, cs, c), lambda s, o: (0, o[1, s], 0))
    o_s = jax.ShapeDtypeStruct((b, t_all, c), F32)
    return pl.pallas_call(
        _wkv_state_kernel,
        grid_spec=pltpu.PrefetchScalarGridSpec(
            num_scalar_prefetch=1,
            grid=(nch,),
            in_specs=[fwd] * 4 + [bwd] * 4,
            out_specs=[out_f, out_b],
            scratch_shapes=[pltpu.VMEM((b, 2, cs, c), F32)]),
        out_shape=[o_s, o_s],
        compiler_params=_cparams(("arbitrary",)),
        name="wkv_state_pass",
    )(order, m, n, rr, yv, m, n, rr, yv)


def _ffn_pre(x1, gain_ref, sh_ref, sc_ref, rw_ref, rb_ref, hf_ref, idx_ref, gate_ref):
    hf = _rms(x1) * gain_ref[...]
    hf = hf * (1.0 + sc_ref[...]) + sh_ref[...]
    hf_ref[...] = hf.astype(BF16)
    hf_hi, hf_lo = _split(hf)
    t = _dot(hf_hi, rw_ref[...])
    u = _dot(hf_lo, rw_ref[:, :LOGIT_PAD])
    logits = t[:, :LOGIT_PAD] + t[:, LOGIT_PAD:] + u + rb_ref[...]
    lane = lax.broadcasted_iota(jnp.int32, logits.shape, 1)
    lane_f = lane.astype(F32)
    work = logits
    idx_out = jnp.zeros(logits.shape, F32)
    val_out = jnp.full(logits.shape, NEG_BIG, F32)
    for kth in range(TOP_K):
        m = jnp.max(work, axis=1, keepdims=True)
        idx = jnp.min(jnp.where(work == m, lane_f, float(LOGIT_PAD)), axis=1, keepdims=True)
        work = jnp.where(lane_f == idx, -jnp.inf, work)
        idx_out = jnp.where(lane == kth, idx, idx_out)
        val_out = jnp.where(lane == kth, m, val_out)
    e = jnp.exp(val_out - jnp.max(val_out, axis=1, keepdims=True))
    e = jnp.where(lane < TOP_K, e, 0.0)
    idx_ref[...] = idx_out.astype(jnp.int32)
    gate_ref[...] = e / jnp.sum(e, axis=1, keepdims=True)


def _router_pads(router_w, router_b):
    d = router_w.shape[0]
    rw = jnp.concatenate([router_w, jnp.zeros((d, LOGIT_PAD - N_EXPERTS), F32)], axis=1)
    rb = jnp.concatenate([router_b, jnp.full((LOGIT_PAD - N_EXPERTS,), NEG_BIG, F32)]).reshape(1, LOGIT_PAD)
    return jnp.concatenate(_split(rw), axis=1), rb


def _mix_out_kernel(yf_ref, yb_ref, r_ref, k_ref, v_ref, g_ref, o_ref, x_ref,
                    rk_ref, lnw_ref, lnb_ref, bdm_ref, bds_ref, wout_ref, npost_ref, g1_ref,
                    gpre_ref, sh_ref, sc_ref, rw_ref, rb_ref,
                    x1_ref, hf_ref, idx_ref, gate_ref):
    y = yf_ref[...] + yb_ref[...]
    mu = _seg_sum(y, bdm_ref)
    dlt = y - mu
    var = _seg_sum(dlt * dlt, bdm_ref)
    yn = dlt * lax.rsqrt(var + RWKV_LN_EPS) * lnw_ref[...] + lnb_ref[...]
    bonus = _seg_sum(r_ref[...] * k_ref[...] * rk_ref[...], bds_ref) * v_ref[...]
    rw = (yn + bonus) * g_ref[...]
    mix_in = jnp.concatenate([o_ref[...], rw.astype(BF16)], axis=1)
    mix = _dot(mix_in, wout_ref[...])
    x1 = x_ref[...] + g1_ref[...] * (_rms(mix) * npost_ref[...])
    x1_ref[...] = x1
    _ffn_pre(x1, gpre_ref, sh_ref, sc_ref, rw_ref, rb_ref, hf_ref, idx_ref, gate_ref)


def _token_out_specs(tm, d):
    row = lambda bi, i: (bi, i, 0)
    specs = [pl.BlockSpec((None, tm, d), row), pl.BlockSpec((None, tm, d), row),
             pl.BlockSpec((None, tm, LOGIT_PAD), row), pl.BlockSpec((None, tm, LOGIT_PAD), row)]
    return specs


def _token_out_shapes(b, s, d):
    return [jax.ShapeDtypeStruct((b, s, d), F32), jax.ShapeDtypeStruct((b, s, d), BF16),
            jax.ShapeDtypeStruct((b, s, LOGIT_PAD), jnp.int32), jax.ShapeDtypeStruct((b, s, LOGIT_PAD), F32)]


def _mix_out(bsel, yf, yb, r, k, v, g, o_attn, x, r_k, ln_w, ln_b, bd_mean, bd_ones, w_out, n_post, g1,
             n_pre, sh2, sc2, rw_pad, rb_pad):
    _, s, d = x.shape
    c = RWKV_DIM
    tm = ROW_TILE
    row = lambda bi, i: (bsel, i, 0)
    const = lambda bi, i: (0, 0)
    per_b = lambda bi, i: (bi, 0, 0)
    rc = pl.BlockSpec((None, tm, c), row)
    vec_c = pl.BlockSpec((1, c), const)
    vec_d = pl.BlockSpec((1, d), const)
    mod_d = pl.BlockSpec((None, 1, d), per_b)
    return pl.pallas_call(
        _mix_out_kernel,
        grid=(1, s // tm),
        in_specs=[rc] * 7 + [pl.BlockSpec((None, tm, d), row),
                             vec_c, vec_c, vec_c, pl.BlockSpec((c, c), const), pl.BlockSpec((c, c), const),
                             pl.BlockSpec((d, d), const), vec_d, mod_d,
                             vec_d, mod_d, mod_d, pl.BlockSpec((d, 2 * LOGIT_PAD), const),
                             pl.BlockSpec((1, LOGIT_PAD), const)],
        out_specs=_token_out_specs(tm, d),
        out_shape=_token_out_shapes(1, s, d),
        compiler_params=_cparams(("parallel", "parallel")),
        name="mixer_out_router",
    )(yf, yb, r, k, v, g, o_attn, x, r_k.reshape(1, c), ln_w.reshape(1, c), ln_b.reshape(1, c), bd_mean, bd_ones,
      w_out, n_post.reshape(1, d), g1, n_pre.reshape(1, d), sh2, sc2, rw_pad, rb_pad)


def _gmlp_kernel(x_ref, gpre1_ref, sh1_ref, sc1_ref, win_ref, vnw_ref, vnb_ref, ws_ref, bs_ref, wout_ref,
                 npost_ref, g1_ref, gpre_ref, sh_ref, sc_ref, rw_ref, rb_ref,
                 x1_ref, hf_ref, idx_ref, gate_ref):
    x = x_ref[...]
    h = _rms(x) * gpre1_ref[...]
    h = h * (1.0 + sc1_ref[...]) + sh1_ref[...]
    z = _dot(h.astype(BF16), win_ref[...])
    z = 0.5 * z * (1.0 + lax.erf(z * float(2.0 ** -0.5)))
    u, v = z[:, :GM_WIDTH], z[:, GM_WIDTH:]
    mu = jnp.mean(v, axis=-1, keepdims=True)
    dv = v - mu
    var = jnp.mean(dv * dv, axis=-1, keepdims=True)
    v = (dv * lax.rsqrt(var + LN_EPS) * vnw_ref[...] + vnb_ref[...]).astype(BF16)
    gw = GM_WIDTH // GM_GROUPS
    rows = []
    for ci in range(x.shape[0] // CHUNK):
        cols = []
        for gi in range(GM_GROUPS):
            cols.append(_dot(ws_ref[gi], v[ci * CHUNK:(ci + 1) * CHUNK, gi * gw:(gi + 1) * gw]))
        rows.append(jnp.concatenate(cols, axis=1) + bs_ref[...])
    sp = jnp.concatenate(rows, axis=0)
    y = _dot((u * sp).astype(BF16), wout_ref[...])
    x1 = x + g1_ref[...] * (_rms(y) * npost_ref[...])
    x1_ref[...] = x1
    _ffn_pre(x1, gpre_ref, sh_ref, sc_ref, rw_ref, rb_ref, hf_ref, idx_ref, gate_ref)


def _gmlp(x, n_pre1, sh1, sc1, w_in, vn_w, vn_b, w_s, b_s, w_out, n_post, g1, n_pre, sh2, sc2, rw_pad, rb_pad):
    b, s, d = x.shape
    tm = ROW_TILE
    gw = GM_WIDTH // GM_GROUPS
    bs_full = jnp.repeat(b_s.T, gw, axis=1)
    row = lambda bi, i: (bi, i, 0)
    const = lambda bi, i: (0, 0)
    per_b = lambda bi, i: (bi, 0, 0)
    vec_d = pl.BlockSpec((1, d), const)
    vec_g = pl.BlockSpec((1, GM_WIDTH), const)
    mod_d = pl.BlockSpec((None, 1, d), per_b)
    return pl.pallas_call(
        _gmlp_kernel,
        grid=(b, s // tm),
        in_specs=[pl.BlockSpec((None, tm, d), row), vec_d, mod_d, mod_d,
                  pl.BlockSpec((d, 2 * GM_WIDTH), const), vec_g, vec_g,
                  pl.BlockSpec((GM_GROUPS, CHUNK, CHUNK), lambda bi, i: (0, 0, 0)),
                  pl.BlockSpec((CHUNK, GM_WIDTH), const), pl.BlockSpec((GM_WIDTH, d), const),
                  vec_d, mod_d, vec_d, mod_d, mod_d,
                  pl.BlockSpec((d, 2 * LOGIT_PAD), const), pl.BlockSpec((1, LOGIT_PAD), const)],
        out_specs=_token_out_specs(tm, d),
        out_shape=_token_out_shapes(b, s, d),
        compiler_params=_cparams(("parallel", "parallel")),
        name="gmlp_router",
    )(x, n_pre1.reshape(1, d), sh1, sc1, w_in.astype(BF16), vn_w.reshape(1, -1), vn_b.reshape(1, -1),
      w_s.astype(BF16), bs_full, w_out.astype(BF16), n_post.reshape(1, d), g1, n_pre.reshape(1, d), sh2, sc2,
      rw_pad, rb_pad)


MOE_ROWS = 256


def _expert_kernel(be_ref, na_ref, x_ref, wgu_ref, bgu_ref, wd_ref, bd_ref, y_ref, wgu_bf, wd_bf):
    i = pl.program_id(0)
    prev = be_ref[jnp.maximum(i - 1, 0)]
    changed = jnp.logical_or(i == 0, be_ref[i] != prev)

    @pl.when(changed)
    def _():
        wgu_bf[...] = wgu_ref[...].astype(BF16)
        wd_bf[...] = wd_ref[...].astype(BF16)

    @pl.when(i < na_ref[0])
    def _():
        gu = _dot(x_ref[...], wgu_bf[...]) + bgu_ref[...]
        x_glu = jnp.minimum(gu[:, :EXPERT_FF], SWIGLU_LIMIT)
        x_lin = jnp.clip(gu[:, EXPERT_FF:], -SWIGLU_LIMIT, SWIGLU_LIMIT)
        act = x_glu * jax.nn.sigmoid(SWIGLU_ALPHA * x_glu) * (x_lin + 1.0)
        y_ref[...] = (_dot(act.astype(BF16), wd_bf[...]) + bd_ref[...]).astype(BF16)

    @pl.when(i >= na_ref[0])
    def _():
        y_ref[...] = jnp.zeros(y_ref.shape, BF16)


def _experts(blk_expert, n_active, xg, layer, w_gu, b_gu, w_down, b_down):
    n_rows, d = xg.shape
    n_l, n_e, _, ff2 = w_gu.shape
    tb = MOE_ROWS
    n_blocks = n_rows // tb
    return pl.pallas_call(
        _expert_kernel,
        grid_spec=pltpu.PrefetchScalarGridSpec(
            num_scalar_prefetch=2,
            grid=(n_blocks,),
            in_specs=[pl.BlockSpec((tb, d), lambda i, be, na: (i, 0)),
                      pl.BlockSpec((None, None, d, ff2), lambda i, be, na: (layer, be[i], 0, 0)),
                      pl.BlockSpec((None, None, 1, ff2), lambda i, be, na: (layer, be[i], 0, 0)),
                      pl.BlockSpec((None, None, ff2 // 2, d), lambda i, be, na: (layer, be[i], 0, 0)),
                      pl.BlockSpec((None, None, 1, d), lambda i, be, na: (layer, be[i], 0, 0))],
            out_specs=pl.BlockSpec((tb, d), lambda i, be, na: (i, 0)),
            scratch_shapes=[pltpu.VMEM((d, ff2), BF16), pltpu.VMEM((ff2 // 2, d), BF16)]),
        out_shape=jax.ShapeDtypeStruct((n_rows, d), BF16),
        compiler_params=_cparams(("arbitrary",)),
        name="moe_experts",
    )(blk_expert, n_active, xg, w_gu, b_gu.reshape(n_l, n_e, 1, ff2), w_down, b_down.reshape(n_l, n_e, 1, d))


def _combine_kernel(x_ref, y_ref, gate_ref, npost_ref, g2_ref, o_ref):
    gate = gate_ref[...]
    f = jnp.zeros(x_ref.shape, F32)
    for kth in range(TOP_K):
        f = f + gate[:, kth:kth + 1] * y_ref[kth].astype(F32)
    o_ref[...] = x_ref[...] + g2_ref[...] * (_rms(f) * npost_ref[...])


def _combine(x1, yk, gates, n_post, g2):
    b, s, d = x1.shape
    tm = ROW_TILE
    row = lambda bi, i: (bi, i, 0)
    return pl.pallas_call(
        _combine_kernel,
        grid=(b, s // tm),
        in_specs=[pl.BlockSpec((None, tm, d), row),
                  pl.BlockSpec((TOP_K, None, tm, d), lambda bi, i: (0, bi, i, 0)),
                  pl.BlockSpec((None, tm, LOGIT_PAD), row),
                  pl.BlockSpec((1, d), lambda bi, i: (0, 0)),
                  pl.BlockSpec((None, 1, d), lambda bi, i: (bi, 0, 0))],
        out_specs=pl.BlockSpec((None, tm, d), row),
        out_shape=jax.ShapeDtypeStruct((b, s, d), F32),
        compiler_params=_cparams(("parallel", "parallel")),
        name="moe_combine_residual",
    )(x1, yk, gates, n_post.reshape(1, d), g2)


def _lookup(table, idx):
    n = table.shape[0]
    return jnp.sum(jnp.where(idx[:, None] == jnp.arange(n, dtype=jnp.int32)[None, :], table[None, :], 0), axis=1)


def _rows(a, idx):
    return a.at[idx].get(mode="promise_in_bounds")


def _moe(x1, hf, top_idx, gates, layer,w_gu, b_gu, w_down, b_down, n_post, g2):
    b, s, d = x1.shape
    n_tok = b * s
    tb = MOE_ROWS
    n_assign = n_tok * TOP_K
    assert N_EXPERTS * n_assign < 2 ** 31
    i32 = jnp.int32
    e_flat = top_idx[..., :TOP_K].reshape(-1)
    skey = jnp.sort(e_flat * n_assign + jnp.arange(n_assign, dtype=i32))
    order = skey % n_assign
    _, inv = lax.sort((order, jnp.arange(n_assign, dtype=i32)), num_keys=1)
    edges = jnp.arange(N_EXPERTS + 1, dtype=i32) * n_assign
    bounds = jnp.sum((skey[None, :] < edges[:, None]).astype(i32), axis=1)
    start = bounds[:-1]
    counts = bounds[1:] - start
    padded = (counts + tb - 1) // tb * tb
    pend = jnp.cumsum(padded)
    pstart = pend - padded
    n_rows = -(-n_assign // tb) * tb + N_EXPERTS * tb
    n_blocks = n_rows // tb
    blk_start = jnp.arange(n_blocks, dtype=i32) * tb
    blk_expert = jnp.minimum(jnp.sum((pend[None, :] <= blk_start[:, None]).astype(i32), axis=1), N_EXPERTS - 1)
    n_active = (pend[-1] // tb).astype(i32).reshape(1)
    j = jnp.arange(n_rows, dtype=i32) - jnp.repeat(_lookup(pstart, blk_expert), tb)
    src = jnp.repeat(_lookup(start, blk_expert), tb) + j
    valid = j < jnp.repeat(_lookup(counts, blk_expert), tb)
    row_tok = jnp.where(valid, _rows(order, jnp.clip(src, 0, n_assign - 1)) // TOP_K, 0)
    pos = _lookup(pstart - start, e_flat) + inv
    xg = _rows(hf.reshape(n_tok, d), row_tok)
    y = _experts(blk_expert, n_active, xg, layer, w_gu, b_gu, w_down, b_down)
    yk = _rows(y, pos.reshape(n_tok, TOP_K).T).reshape(TOP_K, b, s, d)
    return _combine(x1, yk, gates, n_post, g2)


def _block_diag(n, blk, val):
    return (jnp.kron(jnp.eye(n // blk, dtype=F32), jnp.ones((blk, blk), F32)) * val).astype(BF16)


def kernel(x, c, ctx, c_ctx, ada_w, ada_b, norm_mix_pre, norm_mix_post, norm_ffn_pre, norm_ffn_post, router_w, router_b, moe_w_gu, moe_b_gu, moe_w_down, moe_b_down, hy_w_in, mla_q_norm, mla_w_uq, mla_kv_norm, mla_w_ukv, rwkv_mu_prev, rwkv_mu_next, rwkv_w0, rwkv_w2, rwkv_a0, rwkv_a2, rwkv_g2, rwkv_k_k, rwkv_k_a, rwkv_r_k, rwkv_ln_w, rwkv_ln_b, hy_w_out, gm_w_in, gm_v_norm_w, gm_v_norm_b, gm_w_s, gm_b_s, gm_w_out):
    b, s, d = x.shape
    n_ctx = ctx.shape[1]
    t_all = s + n_ctx
    assert b + 1 <= 8 and s % ROW_TILE == 0 and n_ctx % ROW_TILE == 0

    cond_rows = jnp.concatenate([c, c_ctx[None], jnp.zeros((8 - b - 1, d), F32)], axis=0)
    mod = _modulation(cond_rows, ada_w, ada_b)

    def lat_mod(l, j):
        return mod[l, :b, j * d:(j + 1) * d].reshape(b, 1, d)

    sh_all = jnp.stack([mod[0, :b, 0:d], jnp.broadcast_to(mod[0, b, 0:d], (b, d))], axis=1).reshape(2 * b, 1, d)
    sc_all = jnp.stack([mod[0, :b, d:2 * d], jnp.broadcast_to(mod[0, b, d:2 * d], (b, d))], axis=1).reshape(2 * b, 1, d)
    x_all = jnp.concatenate([x, ctx], axis=1)
    cos, sin = _rope_tables(s, n_ctx)
    w1, wq2, wk2 = _inproj_weights(hy_w_in[0], mla_w_uq[0], mla_w_ukv[0])
    q, k, v, prw = _inproj(x_all, s, norm_mix_pre[0], sh_all, sc_all, cos, sin, w1, mla_q_norm[0], mla_kv_norm[0],
                           wq2, wk2)
    tq = 512 if s % 512 == 0 else ROW_TILE
    tk = next((t for t in (8320, 1280) if t_all % t == 0), ROW_TILE)
    o_attn = _attention(q, k, v, s, tq, tk)

    bd_ones = _block_diag(RWKV_DIM, RWKV_HEAD, 1.0)
    bd_mean = _block_diag(RWKV_DIM, RWKV_HEAD, 1.0 / RWKV_HEAD)
    r, kx, vx, g, kkn, lw, kd, bdir = _rwkv_prepare(prw, s, rwkv_mu_prev[0], rwkv_mu_next[0], rwkv_w0[0], rwkv_w2[0],
                                                    rwkv_a0[0], rwkv_a2[0], rwkv_g2[0], rwkv_k_k[0], rwkv_k_a[0],
                                                    bd_ones)
    m_c, n_c, r_c, y_c = _wkv_chunks(r, vx, kkn, lw, kd, bdir)
    n_lat_ch = s // WKV_CHUNK
    n_ctx_ch = n_ctx // WKV_CHUNK
    lat_ch = np.arange(n_lat_ch)
    ctx_ch = n_lat_ch + np.arange(n_ctx_ch)
    order = jnp.asarray(np.stack([np.concatenate([ctx_ch, lat_ch]),
                                  np.concatenate([ctx_ch[::-1], lat_ch[::-1]])]).astype(np.int32))
    yf, yb = _wkv_states(order, m_c, n_c, r_c, y_c, t_all)

    rw0, rb0 = _router_pads(router_w[0], router_b[0])
    rw1, rb1 = _router_pads(router_w[1], router_b[1])
    w_out0 = hy_w_out[0].astype(BF16)
    outs = []
    for bi in range(b):
        def bmod(l, j):
            return lat_mod(l, j)[bi:bi + 1]

        x1, hf, top_idx, gates = _mix_out(bi, yf, yb, r, kx, vx, g, o_attn, x, rwkv_r_k[0].reshape(-1), rwkv_ln_w[0],
                                          rwkv_ln_b[0], bd_mean, bd_ones, w_out0, norm_mix_post[0],
                                          bmod(0, 2), norm_ffn_pre[0], bmod(0, 3), bmod(0, 4), rw0, rb0)
        x2 = _moe(x1, hf, top_idx, gates, 0, moe_w_gu, moe_b_gu, moe_w_down, moe_b_down, norm_ffn_post[0], bmod(0, 5))
        x3, hf, top_idx, gates = _gmlp(x2, norm_mix_pre[1], bmod(1, 0), bmod(1, 1), gm_w_in[0], gm_v_norm_w[0],
                                       gm_v_norm_b[0], gm_w_s[0], gm_b_s[0], gm_w_out[0], norm_mix_post[1], bmod(1, 2),
                                       norm_ffn_pre[1], bmod(1, 3), bmod(1, 4), rw1, rb1)
        outs.append(_moe(x3, hf, top_idx, gates, 1, moe_w_gu, moe_b_gu, moe_w_down, moe_b_down, norm_ffn_post[1],
                         bmod(1, 5)))
    return jnp.concatenate(outs, axis=0)
```

```python
import functools

import jax
import jax.numpy as jnp
import numpy as np
from jax import lax
from jax.experimental import pallas as pl
from jax.experimental.pallas import tpu as pltpu

F32 = jnp.float32
BF16 = jnp.bfloat16
HIGHEST = lax.Precision.HIGHEST

D_MODEL = 1024
GRID_W = 64
EPS = 1e-6

MLA_HEADS = 8
MLA_NOPE = 64
MLA_ROPE = 32
MLA_V = 64
MLA_Q_RANK = 256
MLA_KV_RANK = 128
MLA_SCALE = (MLA_NOPE + MLA_ROPE) ** -0.5
ROPE_FREQS = MLA_ROPE // 4
ROPE_BASE = 10000.0
HEAD_PAD = 128
ATT_QSUB = 128
ATT_KSUB = 640
LOG2E = 1.4426950408889634
ATT_EXP_DTYPE = jnp.bfloat16
V_EXT = 80

RWKV_HEAD = 64
RWKV_DIM = 512
RWKV_HEADS = 8
DECAY_LORA = 32
AAA_LORA = 32
GATE_LORA = 96
RWKV_PAD_COLS = 3 * RWKV_DIM + 128 + 128
RWKV_LN_EPS = 64e-5
WKV_CHUNK = 64
WKV_CHUNKS_PER_STEP = 4

CHUNK = 128
GM_WIDTH = 1024
GM_GROUPS = 8
LN_EPS = 1e-5

N_EXPERTS = 32
TOP_K = 4
EXPERT_FF = 1024
SWIGLU_LIMIT = 7.0
SWIGLU_ALPHA = 1.702
LOGIT_PAD = 128
NEG_BIG = -1e30

ROW_TILE = 256
VMEM_LIMIT = 48 * 1024 * 1024


def _cparams(sem):
    return pltpu.CompilerParams(dimension_semantics=sem, vmem_limit_bytes=VMEM_LIMIT)


def _rms(x):
    return x * lax.rsqrt(jnp.mean(x * x, axis=-1, keepdims=True) + EPS)


def _dot(a, b, precision=None):
    return jnp.dot(a, b, preferred_element_type=F32, precision=precision)


def _dot_nt(a, b, precision=None):
    return lax.dot_general(a, b, (((1,), (1,)), ((), ())), preferred_element_type=F32, precision=precision)


def _dot_tn(a, b, precision=None):
    return lax.dot_general(a, b, (((0,), (0,)), ((), ())), preferred_element_type=F32, precision=precision)


def _split(x):
    hi = x.astype(BF16)
    return hi, (x - hi.astype(F32)).astype(BF16)


def _dot_split(a, b):
    a_hi, a_lo = _split(a)
    b_hi, b_lo = _split(b)
    return _dot(jnp.concatenate([a_hi, a_hi, a_lo], axis=1), jnp.concatenate([b_hi, b_lo, b_hi], axis=0))


def _mod_kernel(s_ref, w_ref, b_ref, o_ref):
    s = s_ref[...]
    s = s * jax.nn.sigmoid(s)
    o_ref[...] = _dot(s, w_ref[...], HIGHEST) + b_ref[...]


def _modulation(cond_rows, ada_w, ada_b):
    n_l, d, n6 = ada_w.shape
    tn = 1536
    return pl.pallas_call(
        _mod_kernel,
        grid=(n_l, n6 // tn),
        in_specs=[pl.BlockSpec((8, d), lambda l, j: (0, 0)),
                  pl.BlockSpec((None, d, tn), lambda l, j: (l, 0, j)),
                  pl.BlockSpec((None, 1, tn), lambda l, j: (l, 0, j))],
        out_specs=pl.BlockSpec((None, 8, tn), lambda l, j: (l, 0, j)),
        out_shape=jax.ShapeDtypeStruct((n_l, 8, n6), F32),
        compiler_params=_cparams(("arbitrary", "arbitrary")),
        name="adaln_mod",
    )(cond_rows, ada_w, ada_b.reshape(n_l, 1, n6))


def _inproj_kernel(x_ref, g_ref, sh_ref, sc_ref, cos_ref, sin_ref, w1_ref, qn_ref, kvn_ref, wq_ref, wk_ref,
                   q_ref, k_ref, vt_ref, prw_ref):
    h = _rms(x_ref[...]) * g_ref[...]
    h = h * (1.0 + sc_ref[...]) + sh_ref[...]
    p = _dot(h.astype(BF16), w1_ref[...])
    cos = jnp.concatenate([cos_ref[...]] * MLA_HEADS, axis=1)
    sin = jnp.concatenate([sin_ref[...]] * MLA_HEADS, axis=1)
    nq = MLA_HEADS * HEAD_PAD
    qn = _rms(p[:, :MLA_Q_RANK]) * qn_ref[...]
    qq = _dot(qn.astype(BF16), wq_ref[...])
    q_ref[...] = (qq[:, :nq] * cos + qq[:, nq:] * sin).astype(BF16)
    kvn = _rms(p[:, MLA_Q_RANK:MLA_Q_RANK + MLA_KV_RANK]) * kvn_ref[...]
    x2 = jnp.concatenate([kvn, p[:, MLA_Q_RANK + MLA_KV_RANK:512]], axis=1).astype(BF16)
    kk = _dot(x2, wk_ref[...])
    k_ref[...] = (kk[:, :nq] * cos + kk[:, nq:2 * nq] * sin).astype(BF16)
    vt = kk[:, 2 * nq:].T.astype(BF16)
    ones = jnp.ones((V_EXT - MLA_V, vt.shape[1]), BF16)
    for hd in range(MLA_HEADS):
        vt_ref[hd * V_EXT:hd * V_EXT + MLA_V, :] = vt[hd * MLA_V:(hd + 1) * MLA_V, :]
        vt_ref[hd * V_EXT + MLA_V:(hd + 1) * V_EXT, :] = ones
    prw_ref[...] = p[:, 512:]


def _rotate_half_cols(w):
    wr = w.reshape(w.shape[:-1] + (2, 2, ROPE_FREQS))
    return jnp.stack([-wr[..., 1, :], wr[..., 0, :]], axis=-2).reshape(w.shape)


def _inproj_weights(hy_w_in, w_uq, w_ukv):
    d = hy_w_in.shape[0]
    mla_cols = MLA_Q_RANK + MLA_KV_RANK + MLA_ROPE
    w_mla = hy_w_in[:, :mla_cols]
    w_rw = hy_w_in[:, mla_cols:]
    w1 = jnp.concatenate([w_mla, jnp.zeros((d, 512 - mla_cols), F32), _pad_rwkv_cols(w_rw)], axis=1).astype(BF16)
    wq = (w_uq * (MLA_SCALE * LOG2E)).reshape(MLA_Q_RANK, MLA_HEADS, MLA_NOPE + MLA_ROPE)
    z_n = jnp.zeros((MLA_Q_RANK, MLA_HEADS, MLA_NOPE), F32)
    z_p = jnp.zeros((MLA_Q_RANK, MLA_HEADS, HEAD_PAD - MLA_NOPE - MLA_ROPE), F32)
    wq_plain = jnp.concatenate([wq[..., :MLA_NOPE], wq[..., MLA_NOPE:], z_p], axis=-1)
    wq_rot = jnp.concatenate([z_n, _rotate_half_cols(wq[..., MLA_NOPE:]), z_p], axis=-1)
    wq2 = jnp.concatenate([wq_plain.reshape(MLA_Q_RANK, -1), wq_rot.reshape(MLA_Q_RANK, -1)], axis=1).astype(BF16)
    wkv = w_ukv.reshape(MLA_KV_RANK, MLA_HEADS, MLA_NOPE + MLA_V)
    eye = jnp.broadcast_to(jnp.eye(MLA_ROPE, dtype=F32)[:, None, :], (MLA_ROPE, MLA_HEADS, MLA_ROPE))
    pad_k = HEAD_PAD - MLA_NOPE - MLA_ROPE
    top_plain = jnp.concatenate([wkv[..., :MLA_NOPE], jnp.zeros((MLA_KV_RANK, MLA_HEADS, MLA_ROPE + pad_k), F32)], -1)
    mid_plain = jnp.concatenate([jnp.zeros((MLA_ROPE, MLA_HEADS, MLA_NOPE), F32), eye,
                                 jnp.zeros((MLA_ROPE, MLA_HEADS, pad_k), F32)], -1)
    mid_rot = jnp.concatenate([jnp.zeros((MLA_ROPE, MLA_HEADS, MLA_NOPE), F32), _rotate_half_cols(eye),
                               jnp.zeros((MLA_ROPE, MLA_HEADS, pad_k), F32)], -1)
    nq = MLA_HEADS * HEAD_PAD
    rows_c = jnp.concatenate([top_plain.reshape(MLA_KV_RANK, nq), jnp.zeros((MLA_KV_RANK, nq), F32),
                              wkv[..., MLA_NOPE:].reshape(MLA_KV_RANK, MLA_HEADS * MLA_V)], axis=1)
    rows_r = jnp.concatenate([mid_plain.reshape(MLA_ROPE, nq), mid_rot.reshape(MLA_ROPE, nq),
                              jnp.zeros((MLA_ROPE, MLA_HEADS * MLA_V), F32)], axis=1)
    rows_z = jnp.zeros((256 - MLA_KV_RANK - MLA_ROPE, rows_c.shape[1]), F32)
    wk2 = jnp.concatenate([rows_c, rows_r, rows_z], axis=0).astype(BF16)
    return w1, wq2, wk2


def _pad_rwkv_cols(w):
    pad = jnp.zeros(w.shape[:-1] + (RWKV_PAD_COLS - w.shape[-1],), w.dtype)
    return jnp.concatenate([w, pad], axis=-1)


def _rope_tables(n_lat, n_ctx):
    t = jnp.arange(n_lat)
    row = (t // GRID_W).astype(F32)
    col = (t % GRID_W).astype(F32)
    inv = ROPE_BASE ** (-jnp.arange(ROPE_FREQS, dtype=F32) / ROPE_FREQS)
    ang = jnp.stack([row[:, None] * inv, col[:, None] * inv], axis=1)
    ang = jnp.broadcast_to(ang[:, :, None, :], (n_lat, 2, 2, ROPE_FREQS)).reshape(n_lat, MLA_ROPE)
    cos = jnp.concatenate([jnp.ones((n_lat, MLA_NOPE), F32), jnp.cos(ang),
                           jnp.ones((n_lat, HEAD_PAD - MLA_NOPE - MLA_ROPE), F32)], axis=1)
    sin = jnp.concatenate([jnp.zeros((n_lat, MLA_NOPE), F32), jnp.sin(ang),
                           jnp.zeros((n_lat, HEAD_PAD - MLA_NOPE - MLA_ROPE), F32)], axis=1)
    cos = jnp.concatenate([cos, jnp.ones((n_ctx, HEAD_PAD), F32)], axis=0)
    sin = jnp.concatenate([sin, jnp.zeros((n_ctx, HEAD_PAD), F32)], axis=0)
    return cos, sin


def _inproj(x_all, n_lat, gain, shift, scale, cos, sin, w1, q_norm, kv_norm, wq2, wk2):
    b, t_all, d = x_all.shape
    tm = ROW_TILE
    n_lat_tiles = n_lat // tm
    nq = MLA_HEADS * HEAD_PAD

    def mod_map(bi, i):
        return (2 * bi + jnp.where(i >= n_lat_tiles, 1, 0), 0, 0)

    const = lambda bi, i: (0, 0)
    row = lambda bi, i: (bi, i, 0)
    return pl.pallas_call(
        _inproj_kernel,
        grid=(b, t_all // tm),
        in_specs=[pl.BlockSpec((None, tm, d), row),
                  pl.BlockSpec((1, d), const),
                  pl.BlockSpec((None, 1, d), mod_map),
                  pl.BlockSpec((None, 1, d), mod_map),
                  pl.BlockSpec((tm, HEAD_PAD), lambda bi, i: (i, 0)),
                  pl.BlockSpec((tm, HEAD_PAD), lambda bi, i: (i, 0)),
                  pl.BlockSpec(w1.shape, const),
                  pl.BlockSpec((1, MLA_Q_RANK), const),
                  pl.BlockSpec((1, MLA_KV_RANK), const),
                  pl.BlockSpec(wq2.shape, const),
                  pl.BlockSpec(wk2.shape, const)],
        out_specs=[pl.BlockSpec((None, tm, nq), row),
                   pl.BlockSpec((None, tm, nq), row),
                   pl.BlockSpec((None, MLA_HEADS * V_EXT, tm), lambda bi, i: (bi, 0, i)),
                   pl.BlockSpec((None, tm, RWKV_PAD_COLS), row)],
        out_shape=[jax.ShapeDtypeStruct((b, t_all, nq), BF16),
                   jax.ShapeDtypeStruct((b, t_all, nq), BF16),
                   jax.ShapeDtypeStruct((b, MLA_HEADS * V_EXT, t_all), BF16),
                   jax.ShapeDtypeStruct((b, t_all, RWKV_PAD_COLS), F32)],
        compiler_params=_cparams(("parallel", "parallel")),
        name="hybrid_inproj",
    )(x_all, gain.reshape(1, d), shift, scale, cos, sin, w1, q_norm.reshape(1, -1), kv_norm.reshape(1, -1), wq2, wk2)


def _attn_kernel(q_ref, k_ref, vt_ref, o_ref, m_sc, acc_sc):
    ki = pl.program_id(3)

    @pl.when(ki == 0)
    def _():
        m_sc[...] = jnp.full(m_sc.shape, -jnp.inf, F32)
        acc_sc[...] = jnp.zeros(acc_sc.shape, F32)

    tq, tk = q_ref.shape[0], k_ref.shape[0]
    chains = [(h, qs) for h in range(2) for qs in range(tq // ATT_QSUB)]
    hsl = [slice(h * HEAD_PAD, (h + 1) * HEAD_PAD) for h, _ in chains]
    vsl = [slice(h * V_EXT, (h + 1) * V_EXT) for h, _ in chains]
    qsl = [slice(qs * ATT_QSUB, (qs + 1) * ATT_QSUB) for _, qs in chains]
    nch = len(chains)
    q = [q_ref[qsl[c], hsl[c]] for c in range(nch)]
    m = [m_sc[chains[c][0], 0:1, qsl[c]] for c in range(nch)]
    acc = [acc_sc[chains[c][0], :, qsl[c]] for c in range(nch)]
    ksub = ATT_KSUB if tk % ATT_KSUB == 0 else tk
    for kb in range(tk // ksub):
        ksl = slice(kb * ksub, (kb + 1) * ksub)
        s = [_dot_nt(k_ref[ksl, hsl[c]], q[c]).astype(ATT_EXP_DTYPE) for c in range(nch)]
        m_new = [jnp.maximum(m[c], jnp.max(s[c], axis=0, keepdims=True).astype(F32)) for c in range(nch)]
        alpha = [jnp.exp2(m[c] - m_new[c]) for c in range(nch)]
        p = [jnp.exp2(s[c] - m_new[c].astype(ATT_EXP_DTYPE)).astype(BF16) for c in range(nch)]
        acc = [acc[c] * alpha[c] + _dot(vt_ref[vsl[c], ksl], p[c]) for c in range(nch)]
        m = m_new
    for c in range(nch):
        h = chains[c][0]
        m_sc[h, :, qsl[c]] = jnp.broadcast_to(m[c], (8, ATT_QSUB))
        acc_sc[h, :, qsl[c]] = acc[c]

    @pl.when(ki == pl.num_programs(3) - 1)
    def _():
        o0 = (acc_sc[0, :MLA_V, :] / acc_sc[0, MLA_V:MLA_V + 1, :]).T
        o1 = (acc_sc[1, :MLA_V, :] / acc_sc[1, MLA_V:MLA_V + 1, :]).T
        o_ref[...] = jnp.concatenate([o0, o1], axis=1).astype(BF16)


def _attention(q, k, vt, n_lat, tq, tk):
    b, t_all, _ = k.shape
    return pl.pallas_call(
        _attn_kernel,
        grid=(b, MLA_HEADS // 2, n_lat // tq, t_all // tk),
        in_specs=[pl.BlockSpec((None, tq, 2 * HEAD_PAD), lambda bi, hp, qi, ki: (bi, qi, hp)),
                  pl.BlockSpec((None, tk, 2 * HEAD_PAD), lambda bi, hp, qi, ki: (bi, ki, hp)),
                  pl.BlockSpec((None, 2 * V_EXT, tk), lambda bi, hp, qi, ki: (bi, hp, ki))],
        out_specs=pl.BlockSpec((None, tq, 2 * MLA_V), lambda bi, hp, qi, ki: (bi, qi, hp)),
        out_shape=jax.ShapeDtypeStruct((b, n_lat, MLA_HEADS * MLA_V), BF16),
        scratch_shapes=[pltpu.VMEM((2, 8, tq), F32), pltpu.VMEM((2, V_EXT, tq), F32)],
        compiler_params=_cparams(("parallel", "parallel", "parallel", "arbitrary")),
        name="mla_attention",
    )(q, k, vt)


def _seg_sum(x, bd_ref):
    hi = x.astype(BF16)
    lo = (x - hi.astype(F32)).astype(BF16)
    bd = bd_ref[...]
    return _dot(hi, bd) + _dot(lo, bd)


def _rwkv_prep_kernel(p_ref, prev_ref, next_ref, mup_ref, mun_ref, wl_ref, w0a0_ref, g2_ref, kk_ref, ka_ref, bd_ref,
                      r_ref, k_ref, v_ref, g_ref, kkn_ref, lw_ref, kd_ref, bdir_ref, *, n_lat_tiles, n_tiles):
    i = pl.program_id(1)
    p = p_ref[...]
    tm = p.shape[0]
    first = jnp.logical_or(i == 0, i == n_lat_tiles)
    last = jnp.logical_or(i == n_lat_tiles - 1, i == n_tiles - 1)
    prev_row = jnp.where(first, 0.0, prev_ref[7:8, :])
    next_row = jnp.where(last, 0.0, next_ref[0:1, :])
    ridx = lax.broadcasted_iota(jnp.int32, p.shape, 0)
    prev = jnp.where(ridx == 0, prev_row, pltpu.roll(p, 1, 0))
    nxt = jnp.where(ridx == tm - 1, next_row, pltpu.roll(p, tm - 1, 0))
    p = p + mup_ref[...] * (prev - p) + mun_ref[...] * (nxt - p)
    c = RWKV_DIM
    r, k, v = p[:, :c], p[:, c:2 * c], p[:, 2 * c:3 * c]
    lo = p[:, 3 * c:3 * c + 128]
    lane = lax.broadcasted_iota(jnp.int32, lo.shape, 1)
    lo = jnp.where(lane < 2 * DECAY_LORA, jnp.tanh(lo), lo)
    wa = _dot(lo.astype(BF16), wl_ref[...]) + w0a0_ref[...]
    gd = p[:, 3 * c + 128:]
    g_ref[...] = _dot(jax.nn.sigmoid(gd).astype(BF16), g2_ref[...])
    kk = k * kk_ref[...]
    kk = kk * lax.rsqrt(jnp.maximum(_seg_sum(kk * kk, bd_ref), 1e-24))
    r_ref[...] = r
    k_ref[...] = k
    v_ref[...] = v
    kkn_ref[...] = kk
    ka = ka_ref[...]
    for d in range(2):
        w = wa[:, d * c:(d + 1) * c]
        a = jax.nn.sigmoid(wa[:, (2 + d) * c:(3 + d) * c])
        lw_ref[:, d * c:(d + 1) * c] = -float(np.exp(-0.5)) * jax.nn.sigmoid(w)
        kd_ref[:, d * c:(d + 1) * c] = k * (1.0 + (a - 1.0) * ka)
        bdir_ref[:, d * c:(d + 1) * c] = kk * a


def _rwkv_prepare(prw, n_lat, mu_prev, mu_next, w0, w2, a0, a2, g2, k_k, k_a, bd_ones):
    b, t_all, pc = prw.shape
    tm = ROW_TILE
    n_tiles = t_all // tm
    n_lat_tiles = n_lat // tm
    c = RWKV_DIM
    z = jnp.zeros((DECAY_LORA, c), F32)
    wl = jnp.concatenate([
        jnp.concatenate([w2[0], z, z, z], axis=1), jnp.concatenate([z, w2[1], z, z], axis=1),
        jnp.concatenate([z, z, a2[0], z], axis=1), jnp.concatenate([z, z, z, a2[1]], axis=1)], axis=0).astype(BF16)
    w0a0 = jnp.concatenate([w0[0], w0[1], a0[0], a0[1]]).reshape(1, 4 * c)
    g2p = jnp.concatenate([g2, jnp.zeros((128 - GATE_LORA, c), F32)], axis=0).astype(BF16)
    row = lambda bi, i: (bi, i, 0)
    const = lambda bi, i: (0, 0)
    hb = tm // 8
    n_hb = t_all // 8
    kern = functools.partial(_rwkv_prep_kernel, n_lat_tiles=n_lat_tiles, n_tiles=n_tiles)
    o_c = jax.ShapeDtypeStruct((b, t_all, c), F32)
    o_2c = jax.ShapeDtypeStruct((b, t_all, 2 * c), F32)
    return pl.pallas_call(
        kern,
        grid=(b, n_tiles),
        in_specs=[pl.BlockSpec((None, tm, pc), row),
                  pl.BlockSpec((None, 8, pc), lambda bi, i: (bi, jnp.maximum(i * hb - 1, 0), 0)),
                  pl.BlockSpec((None, 8, pc), lambda bi, i: (bi, jnp.minimum((i + 1) * hb, n_hb - 1), 0)),
                  pl.BlockSpec((1, pc), const), pl.BlockSpec((1, pc), const),
                  pl.BlockSpec(wl.shape, const), pl.BlockSpec((1, 4 * c), const),
                  pl.BlockSpec(g2p.shape, const), pl.BlockSpec((1, c), const), pl.BlockSpec((1, c), const),
                  pl.BlockSpec((c, c), const)],
        out_specs=[pl.BlockSpec((None, tm, c), row)] * 5 + [pl.BlockSpec((None, tm, 2 * c), row)] * 3,
        out_shape=[o_c] * 5 + [o_2c] * 3,
        compiler_params=_cparams(("parallel", "parallel")),
        name="rwkv_prepare",
    )(prw, prw, prw, _pad_rwkv_cols(mu_prev).reshape(1, pc), _pad_rwkv_cols(mu_next).reshape(1, pc),
      wl, w0a0, g2p, k_k.reshape(1, c), k_a.reshape(1, c), bd_ones)


def _wkv_chunk_kernel(r_ref, v_ref, kk_ref, lw_ref, kd_ref, bd_ref, m_ref, n_ref, rr_ref, yv_ref):
    d = pl.program_id(1)
    cs = WKV_CHUNK
    nc = r_ref.shape[0] // cs
    ti = lax.broadcasted_iota(jnp.int32, (cs, cs), 0)
    si = lax.broadcasted_iota(jnp.int32, (cs, cs), 1)
    rel = (si - ti) * (1 - 2 * d)
    incl = rel <= 0
    strict = rel < 0
    eye = si == ti
    incl_f = incl.astype(F32)
    ops = []
    for ci in range(nc):
        rows = slice(ci * cs, (ci + 1) * cs)
        lw = lw_ref[rows, :]
        g = _dot(incl_f, lw, HIGHEST)
        total = jnp.sum(lw, axis=0, keepdims=True)
        e_inv = jnp.exp(-g)
        e_end = jnp.exp(total - g)
        gam = jnp.exp(total)
        kd = kd_ref[rows, :]
        bd = bd_ref[rows, :]
        at = -kk_ref[rows, :] * jnp.exp(g - lw)
        rt = r_ref[rows, :] * jnp.exp(g)
        kt = (kd * e_inv).astype(BF16)
        bt = (bd * e_inv).astype(BF16)
        ke = (kd * e_end).astype(BF16)
        be = (bd * e_end).astype(BF16)
        v32 = v_ref[rows, :]
        v = v32.astype(BF16)
        for h in range(RWKV_HEADS):
            sl = slice(h * RWKV_HEAD, (h + 1) * RWKV_HEAD)
            ops.append((at[:, sl], rt[:, sl], kt[:, sl], bt[:, sl], ke[:, sl], be[:, sl], v[:, sl], gam[:, sl], v32[:, sl]))
    n_it = len(ops)
    aa = [_dot_nt(jnp.concatenate([o[0], o[1]], axis=0).astype(BF16), jnp.concatenate([o[3], o[2]], axis=0))
          for o in ops]
    a_ab = [jnp.where(strict, x[:cs, :cs], 0.0) for x in aa]
    a_ak = [jnp.where(strict, x[:cs, cs:], 0.0).astype(BF16) for x in aa]
    ti2 = lax.broadcasted_iota(jnp.int32, (cs, 2 * cs), 0)
    si2 = lax.broadcasted_iota(jnp.int32, (cs, 2 * cs), 1)
    incl2 = (jnp.where(si2 >= cs, si2 - cs, si2) - ti2) * (1 - 2 * d) <= 0
    a_rbk = [jnp.where(incl2, x[cs:, :], 0.0).astype(BF16) for x in aa]
    z = [jnp.concatenate([ops[i][0], _dot(a_ak[i], ops[i][6])], axis=1) for i in range(n_it)]
    pw = a_ab
    n_sq = int(np.log2(cs))
    for it in range(n_sq):
        more = it + 1 < n_sq
        res = [_dot_split(pw[i], jnp.concatenate([z[i], pw[i]], axis=1) if more else z[i]) for i in range(n_it)]
        z = [z[i] + res[i][:, :2 * cs] for i in range(n_it)]
        if more:
            pw = [x[:, 2 * cs:] for x in res]
    zero = jnp.zeros((cs, cs), F32)
    wmat = [jnp.concatenate([z[i], jnp.concatenate([zero, ops[i][8]], axis=1)], axis=0).astype(BF16)
            for i in range(n_it)]
    w2 = [_dot(a_rbk[i], wmat[i]) for i in range(n_it)]
    w3 = [_dot_tn(jnp.concatenate([ops[i][5], ops[i][4]], axis=0), wmat[i]) for i in range(n_it)]
    nh = RWKV_HEADS
    for ci in range(nc):
        rows = slice(ci * cs, (ci + 1) * cs)
        ids = range(ci * nh, (ci + 1) * nh)
        m_ref[rows, :] = jnp.concatenate([jnp.where(eye, ops[i][7], 0.0) + w3[i][:, :cs] for i in ids], axis=1)
        n_ref[rows, :] = jnp.concatenate([w3[i][:, cs:] for i in ids], axis=1)
        rr_ref[rows, :] = jnp.concatenate([ops[i][1] + w2[i][:, :cs] for i in ids], axis=1)
        yv_ref[rows, :] = jnp.concatenate([w2[i][:, cs:] for i in ids], axis=1)


def _wkv_chunks(r, v, kkn, lw, kd, bdir):
    b, t_all, c = r.shape
    rows = WKV_CHUNK * WKV_CHUNKS_PER_STEP
    nst = t_all // rows
    shared = pl.BlockSpec((None, rows, c), lambda bi, d, ci: (bi, ci, 0))
    per_dir = pl.BlockSpec((None, rows, c), lambda bi, d, ci: (bi, ci, d))
    out = pl.BlockSpec((None, None, rows, c), lambda bi, d, ci: (bi, d, ci, 0))
    o_s = jax.ShapeDtypeStruct((b, 2, t_all, c), F32)
    return pl.pallas_call(
        _wkv_chunk_kernel,
        grid=(b, 2, nst),
        in_specs=[shared, shared, shared, per_dir, per_dir, per_dir],
        out_specs=[out] * 4,
        out_shape=[o_s] * 4,
        compiler_params=_cparams(("parallel", "parallel", "parallel")),
        name="wkv_chunk_summaries",
    )(r, v, kkn, lw, kd, bdir)


def _wkv_state_kernel(order_ref, mf_ref, nf_ref, rf_ref, yf_ref, mb_ref, nb_ref, rb_ref, yb_ref,
                      of_ref, ob_ref, st_sc):
    del order_ref
    s = pl.program_id(0)

    @pl.when(s == 0)
    def _():
        st_sc[...] = jnp.zeros(st_sc.shape, F32)

    n_b = st_sc.shape[0]
    ins = ((mf_ref, nf_ref, rf_ref, yf_ref, of_ref), (mb_ref, nb_ref, rb_ref, yb_ref, ob_ref))
    for bi in range(n_b):
        for d in range(2):
            m_r, n_r, r_r, y_r, o_r = ins[d]
            st = st_sc[bi, d]
            m, n, rr, yv = m_r[bi], n_r[bi], r_r[bi], y_r[bi]
            ys, sts = [], []
            for h in range(RWKV_HEADS):
                sl = slice(h * RWKV_HEAD, (h + 1) * RWKV_HEAD)
                st_h = st[:, sl]
                ys.append(_dot(rr[:, sl], st_h) + yv[:, sl])
                sts.append(_dot(m[:, sl], st_h, HIGHEST) + n[:, sl])
            o_r[bi] = jnp.concatenate(ys, axis=1)
            st_sc[bi, d] = jnp.concatenate(sts, axis=1)


def _wkv_states(order, m, n, rr, yv, t_all):
    b, _, _, c = m.shape
    cs = WKV_CHUNK
    nch = t_all // cs
    fwd = pl.BlockSpec((b, None, cs, c), lambda s, o: (0, 0, o[0, s], 0))
    bwd = pl.BlockSpec((b, None, cs, c), lambda s, o: (0, 1, o[1, s], 0))
    out_f = pl.BlockSpec((b, cs, c), lambda s, o: (0, o[0, s], 0))
    out_b = pl.BlockSpec((b, cs, c), lambda s, o: (0, o[1, s], 0))
    o_s = jax.ShapeDtypeStruct((b, t_all, c), F32)
    return pl.pallas_call(
        _wkv_state_kernel,
        grid_spec=pltpu.PrefetchScalarGridSpec(
            num_scalar_prefetch=1,
            grid=(nch,),
            in_specs=[fwd] * 4 + [bwd] * 4,
            out_specs=[out_f, out_b],
            scratch_shapes=[pltpu.VMEM((b, 2, cs, c), F32)]),
        out_shape=[o_s, o_s],
        compiler_params=_cparams(("arbitrary",)),
        name="wkv_state_pass",
    )(order, m, n, rr, yv, m, n, rr, yv)


def _ffn_pre(x1, gain_ref, sh_ref, sc_ref, rw_ref, rb_ref, hf_ref, idx_ref, gate_ref):
    hf = _rms(x1) * gain_ref[...]
    hf = hf * (1.0 + sc_ref[...]) + sh_ref[...]
    hf_ref[...] = hf.astype(BF16)
    hf_hi, hf_lo = _split(hf)
    t = _dot(hf_hi, rw_ref[...])
    u = _dot(hf_lo, rw_ref[:, :LOGIT_PAD])
    logits = t[:, :LOGIT_PAD] + t[:, LOGIT_PAD:] + u + rb_ref[...]
    lane = lax.broadcasted_iota(jnp.int32, logits.shape, 1)
    lane_f = lane.astype(F32)
    work = logits
    idx_out = jnp.zeros(logits.shape, F32)
    val_out = jnp.full(logits.shape, NEG_BIG, F32)
    for kth in range(TOP_K):
        m = jnp.max(work, axis=1, keepdims=True)
        idx = jnp.min(jnp.where(work == m, lane_f, float(LOGIT_PAD)), axis=1, keepdims=True)
        work = jnp.where(lane_f == idx, -jnp.inf, work)
        idx_out = jnp.where(lane == kth, idx, idx_out)
        val_out = jnp.where(lane == kth, m, val_out)
    e = jnp.exp(val_out - jnp.max(val_out, axis=1, keepdims=True))
    e = jnp.where(lane < TOP_K, e, 0.0)
    idx_ref[...] = idx_out.astype(jnp.int32)
    gate_ref[...] = e / jnp.sum(e, axis=1, keepdims=True)


def _router_pads(router_w, router_b):
    d = router_w.shape[0]
    rw = jnp.concatenate([router_w, jnp.zeros((d, LOGIT_PAD - N_EXPERTS), F32)], axis=1)
    rb = jnp.concatenate([router_b, jnp.full((LOGIT_PAD - N_EXPERTS,), NEG_BIG, F32)]).reshape(1, LOGIT_PAD)
    return jnp.concatenate(_split(rw), axis=1), rb


def _mix_out_kernel(yf_ref, yb_ref, r_ref, k_ref, v_ref, g_ref, o_ref, x_ref,
                    rk_ref, lnw_ref, lnb_ref, bdm_ref, bds_ref, wout_ref, npost_ref, g1_ref,
                    gpre_ref, sh_ref, sc_ref, rw_ref, rb_ref,
                    x1_ref, hf_ref, idx_ref, gate_ref):
    y = yf_ref[...] + yb_ref[...]
    mu = _seg_sum(y, bdm_ref)
    dlt = y - mu
    var = _seg_sum(dlt * dlt, bdm_ref)
    yn = dlt * lax.rsqrt(var + RWKV_LN_EPS) * lnw_ref[...] + lnb_ref[...]
    bonus = _seg_sum(r_ref[...] * k_ref[...] * rk_ref[...], bds_ref) * v_ref[...]
    rw = (yn + bonus) * g_ref[...]
    mix_in = jnp.concatenate([o_ref[...], rw.astype(BF16)], axis=1)
    mix = _dot(mix_in, wout_ref[...])
    x1 = x_ref[...] + g1_ref[...] * (_rms(mix) * npost_ref[...])
    x1_ref[...] = x1
    _ffn_pre(x1, gpre_ref, sh_ref, sc_ref, rw_ref, rb_ref, hf_ref, idx_ref, gate_ref)


def _token_out_specs(tm, d):
    row = lambda bi, i: (bi, i, 0)
    specs = [pl.BlockSpec((None, tm, d), row), pl.BlockSpec((None, tm, d), row),
             pl.BlockSpec((None, tm, LOGIT_PAD), row), pl.BlockSpec((None, tm, LOGIT_PAD), row)]
    return specs


def _token_out_shapes(b, s, d):
    return [jax.ShapeDtypeStruct((b, s, d), F32), jax.ShapeDtypeStruct((b, s, d), BF16),
            jax.ShapeDtypeStruct((b, s, LOGIT_PAD), jnp.int32), jax.ShapeDtypeStruct((b, s, LOGIT_PAD), F32)]


def _mix_out(yf, yb, r, k, v, g, o_attn, x, r_k, ln_w, ln_b, bd_mean, bd_ones, w_out, n_post, g1,
             n_pre, sh2, sc2, rw_pad, rb_pad):
    b, s, d = x.shape
    c = RWKV_DIM
    tm = ROW_TILE
    row = lambda bi, i: (bi, i, 0)
    const = lambda bi, i: (0, 0)
    per_b = lambda bi, i: (bi, 0, 0)
    rc = pl.BlockSpec((None, tm, c), row)
    vec_c = pl.BlockSpec((1, c), const)
    vec_d = pl.BlockSpec((1, d), const)
    mod_d = pl.BlockSpec((None, 1, d), per_b)
    return pl.pallas_call(
        _mix_out_kernel,
        grid=(b, s // tm),
        in_specs=[rc] * 7 + [pl.BlockSpec((None, tm, d), row),
                             vec_c, vec_c, vec_c, pl.BlockSpec((c, c), const), pl.BlockSpec((c, c), const),
                             pl.BlockSpec((d, d), const), vec_d, mod_d,
                             vec_d, mod_d, mod_d, pl.BlockSpec((d, 2 * LOGIT_PAD), const),
                             pl.BlockSpec((1, LOGIT_PAD), const)],
        out_specs=_token_out_specs(tm, d),
        out_shape=_token_out_shapes(b, s, d),
        compiler_params=_cparams(("parallel", "parallel")),
        name="mixer_out_router",
    )(yf, yb, r, k, v, g, o_attn, x, r_k.reshape(1, c), ln_w.reshape(1, c), ln_b.reshape(1, c), bd_mean, bd_ones,
      w_out, n_post.reshape(1, d), g1, n_pre.reshape(1, d), sh2, sc2, rw_pad, rb_pad)


def _gmlp_kernel(x_ref, gpre1_ref, sh1_ref, sc1_ref, win_ref, vnw_ref, vnb_ref, ws_ref, bs_ref, wout_ref,
                 npost_ref, g1_ref, gpre_ref, sh_ref, sc_ref, rw_ref, rb_ref,
                 x1_ref, hf_ref, idx_ref, gate_ref):
    x = x_ref[...]
    h = _rms(x) * gpre1_ref[...]
    h = h * (1.0 + sc1_ref[...]) + sh1_ref[...]
    z = _dot(h.astype(BF16), win_ref[...])
    z = 0.5 * z * (1.0 + lax.erf(z * float(2.0 ** -0.5)))
    u, v = z[:, :GM_WIDTH], z[:, GM_WIDTH:]
    mu = jnp.mean(v, axis=-1, keepdims=True)
    dv = v - mu
    var = jnp.mean(dv * dv, axis=-1, keepdims=True)
    v = (dv * lax.rsqrt(var + LN_EPS) * vnw_ref[...] + vnb_ref[...]).astype(BF16)
    gw = GM_WIDTH // GM_GROUPS
    rows = []
    for ci in range(x.shape[0] // CHUNK):
        cols = []
        for gi in range(GM_GROUPS):
            cols.append(_dot(ws_ref[gi], v[ci * CHUNK:(ci + 1) * CHUNK, gi * gw:(gi + 1) * gw]))
        rows.append(jnp.concatenate(cols, axis=1) + bs_ref[...])
    sp = jnp.concatenate(rows, axis=0)
    y = _dot((u * sp).astype(BF16), wout_ref[...])
    x1 = x + g1_ref[...] * (_rms(y) * npost_ref[...])
    x1_ref[...] = x1
    _ffn_pre(x1, gpre_ref, sh_ref, sc_ref, rw_ref, rb_ref, hf_ref, idx_ref, gate_ref)


def _gmlp(x, n_pre1, sh1, sc1, w_in, vn_w, vn_b, w_s, b_s, w_out, n_post, g1, n_pre, sh2, sc2, rw_pad, rb_pad):
    b, s, d = x.shape
    tm = ROW_TILE
    gw = GM_WIDTH // GM_GROUPS
    bs_full = jnp.repeat(b_s.T, gw, axis=1)
    row = lambda bi, i: (bi, i, 0)
    const = lambda bi, i: (0, 0)
    per_b = lambda bi, i: (bi, 0, 0)
    vec_d = pl.BlockSpec((1, d), const)
    vec_g = pl.BlockSpec((1, GM_WIDTH), const)
    mod_d = pl.BlockSpec((None, 1, d), per_b)
    return pl.pallas_call(
        _gmlp_kernel,
        grid=(b, s // tm),
        in_specs=[pl.BlockSpec((None, tm, d), row), vec_d, mod_d, mod_d,
                  pl.BlockSpec((d, 2 * GM_WIDTH), const), vec_g, vec_g,
                  pl.BlockSpec((GM_GROUPS, CHUNK, CHUNK), lambda bi, i: (0, 0, 0)),
                  pl.BlockSpec((CHUNK, GM_WIDTH), const), pl.BlockSpec((GM_WIDTH, d), const),
                  vec_d, mod_d, vec_d, mod_d, mod_d,
                  pl.BlockSpec((d, 2 * LOGIT_PAD), const), pl.BlockSpec((1, LOGIT_PAD), const)],
        out_specs=_token_out_specs(tm, d),
        out_shape=_token_out_shapes(b, s, d),
        compiler_params=_cparams(("parallel", "parallel")),
        name="gmlp_router",
    )(x, n_pre1.reshape(1, d), sh1, sc1, w_in.astype(BF16), vn_w.reshape(1, -1), vn_b.reshape(1, -1),
      w_s.astype(BF16), bs_full, w_out.astype(BF16), n_post.reshape(1, d), g1, n_pre.reshape(1, d), sh2, sc2,
      rw_pad, rb_pad)


MOE_ROWS = 256


def _expert_kernel(be_ref, na_ref, x_ref, wgu_ref, bgu_ref, wd_ref, bd_ref, y_ref, wgu_bf, wd_bf):
    i = pl.program_id(0)
    prev = be_ref[jnp.maximum(i - 1, 0)]
    changed = jnp.logical_or(i == 0, be_ref[i] != prev)

    @pl.when(changed)
    def _():
        wgu_bf[...] = wgu_ref[...].astype(BF16)
        wd_bf[...] = wd_ref[...].astype(BF16)

    @pl.when(i < na_ref[0])
    def _():
        gu = _dot(x_ref[...], wgu_bf[...]) + bgu_ref[...]
        x_glu = jnp.minimum(gu[:, :EXPERT_FF], SWIGLU_LIMIT)
        x_lin = jnp.clip(gu[:, EXPERT_FF:], -SWIGLU_LIMIT, SWIGLU_LIMIT)
        act = x_glu * jax.nn.sigmoid(SWIGLU_ALPHA * x_glu) * (x_lin + 1.0)
        y_ref[...] = (_dot(act.astype(BF16), wd_bf[...]) + bd_ref[...]).astype(BF16)

    @pl.when(i >= na_ref[0])
    def _():
        y_ref[...] = jnp.zeros(y_ref.shape, BF16)


def _experts(blk_expert, n_active, xg, layer, w_gu, b_gu, w_down, b_down):
    n_rows, d = xg.shape
    n_l, n_e, _, ff2 = w_gu.shape
    tb = MOE_ROWS
    n_blocks = n_rows // tb
    return pl.pallas_call(
        _expert_kernel,
        grid_spec=pltpu.PrefetchScalarGridSpec(
            num_scalar_prefetch=2,
            grid=(n_blocks,),
            in_specs=[pl.BlockSpec((tb, d), lambda i, be, na: (i, 0)),
                      pl.BlockSpec((None, None, d, ff2), lambda i, be, na: (layer, be[i], 0, 0)),
                      pl.BlockSpec((None, None, 1, ff2), lambda i, be, na: (layer, be[i], 0, 0)),
                      pl.BlockSpec((None, None, ff2 // 2, d), lambda i, be, na: (layer, be[i], 0, 0)),
                      pl.BlockSpec((None, None, 1, d), lambda i, be, na: (layer, be[i], 0, 0))],
            out_specs=pl.BlockSpec((tb, d), lambda i, be, na: (i, 0)),
            scratch_shapes=[pltpu.VMEM((d, ff2), BF16), pltpu.VMEM((ff2 // 2, d), BF16)]),
        out_shape=jax.ShapeDtypeStruct((n_rows, d), BF16),
        compiler_params=_cparams(("arbitrary",)),
        name="moe_experts",
    )(blk_expert, n_active, xg, w_gu, b_gu.reshape(n_l, n_e, 1, ff2), w_down, b_down.reshape(n_l, n_e, 1, d))


def _combine_kernel(x_ref, y_ref, gate_ref, npost_ref, g2_ref, o_ref):
    gate = gate_ref[...]
    f = jnp.zeros(x_ref.shape, F32)
    for kth in range(TOP_K):
        f = f + gate[:, kth:kth + 1] * y_ref[kth].astype(F32)
    o_ref[...] = x_ref[...] + g2_ref[...] * (_rms(f) * npost_ref[...])


def _combine(x1, yk, gates, n_post, g2):
    b, s, d = x1.shape
    tm = ROW_TILE
    row = lambda bi, i: (bi, i, 0)
    return pl.pallas_call(
        _combine_kernel,
        grid=(b, s // tm),
        in_specs=[pl.BlockSpec((None, tm, d), row),
                  pl.BlockSpec((TOP_K, None, tm, d), lambda bi, i: (0, bi, i, 0)),
                  pl.BlockSpec((None, tm, LOGIT_PAD), row),
                  pl.BlockSpec((1, d), lambda bi, i: (0, 0)),
                  pl.BlockSpec((None, 1, d), lambda bi, i: (bi, 0, 0))],
        out_specs=pl.BlockSpec((None, tm, d), row),
        out_shape=jax.ShapeDtypeStruct((b, s, d), F32),
        compiler_params=_cparams(("parallel", "parallel")),
        name="moe_combine_residual",
    )(x1, yk, gates, n_post.reshape(1, d), g2)


def _lookup(table, idx):
    n = table.shape[0]
    return jnp.sum(jnp.where(idx[:, None] == jnp.arange(n, dtype=jnp.int32)[None, :], table[None, :], 0), axis=1)


def _rows(a, idx):
    return a.at[idx].get(mode="promise_in_bounds")


def _moe(x1, hf, top_idx, gates, layer,w_gu, b_gu, w_down, b_down, n_post, g2):
    b, s, d = x1.shape
    n_tok = b * s
    tb = MOE_ROWS
    n_assign = n_tok * TOP_K
    assert N_EXPERTS * n_assign < 2 ** 31
    i32 = jnp.int32
    e_flat = top_idx[..., :TOP_K].reshape(-1)
    skey = jnp.sort(e_flat * n_assign + jnp.arange(n_assign, dtype=i32))
    order = skey % n_assign
    _, inv = lax.sort((order, jnp.arange(n_assign, dtype=i32)), num_keys=1)
    edges = jnp.arange(N_EXPERTS + 1, dtype=i32) * n_assign
    bounds = jnp.sum((skey[None, :] < edges[:, None]).astype(i32), axis=1)
    start = bounds[:-1]
    counts = bounds[1:] - start
    padded = (counts + tb - 1) // tb * tb
    pend = jnp.cumsum(padded)
    pstart = pend - padded
    n_rows = -(-n_assign // tb) * tb + N_EXPERTS * tb
    n_blocks = n_rows // tb
    blk_start = jnp.arange(n_blocks, dtype=i32) * tb
    blk_expert = jnp.minimum(jnp.sum((pend[None, :] <= blk_start[:, None]).astype(i32), axis=1), N_EXPERTS - 1)
    n_active = (pend[-1] // tb).astype(i32).reshape(1)
    j = jnp.arange(n_rows, dtype=i32) - jnp.repeat(_lookup(pstart, blk_expert), tb)
    src = jnp.repeat(_lookup(start, blk_expert), tb) + j
    valid = j < jnp.repeat(_lookup(counts, blk_expert), tb)
    row_tok = jnp.where(valid, _rows(order, jnp.clip(src, 0, n_assign - 1)) // TOP_K, 0)
    pos = _lookup(pstart - start, e_flat) + inv
    xg = _rows(hf.reshape(n_tok, d), row_tok)
    y = _experts(blk_expert, n_active, xg, layer, w_gu, b_gu, w_down, b_down)
    yk = _rows(y, pos.reshape(n_tok, TOP_K).T.reshape(-1)).reshape(TOP_K, b, s, d)
    return _combine(x1, yk, gates, n_post, g2)


def _block_diag(n, blk, val):
    return (jnp.kron(jnp.eye(n // blk, dtype=F32), jnp.ones((blk, blk), F32)) * val).astype(BF16)


def kernel(x, c, ctx, c_ctx, ada_w, ada_b, norm_mix_pre, norm_mix_post, norm_ffn_pre, norm_ffn_post, router_w, router_b, moe_w_gu, moe_b_gu, moe_w_down, moe_b_down, hy_w_in, mla_q_norm, mla_w_uq, mla_kv_norm, mla_w_ukv, rwkv_mu_prev, rwkv_mu_next, rwkv_w0, rwkv_w2, rwkv_a0, rwkv_a2, rwkv_g2, rwkv_k_k, rwkv_k_a, rwkv_r_k, rwkv_ln_w, rwkv_ln_b, hy_w_out, gm_w_in, gm_v_norm_w, gm_v_norm_b, gm_w_s, gm_b_s, gm_w_out):
    b, s, d = x.shape
    n_ctx = ctx.shape[1]
    t_all = s + n_ctx
    assert b + 1 <= 8 and s % ROW_TILE == 0 and n_ctx % ROW_TILE == 0

    cond_rows = jnp.concatenate([c, c_ctx[None], jnp.zeros((8 - b - 1, d), F32)], axis=0)
    mod = _modulation(cond_rows, ada_w, ada_b)

    def lat_mod(l, j):
        return mod[l, :b, j * d:(j + 1) * d].reshape(b, 1, d)

    sh_all = jnp.stack([mod[0, :b, 0:d], jnp.broadcast_to(mod[0, b, 0:d], (b, d))], axis=1).reshape(2 * b, 1, d)
    sc_all = jnp.stack([mod[0, :b, d:2 * d], jnp.broadcast_to(mod[0, b, d:2 * d], (b, d))], axis=1).reshape(2 * b, 1, d)
    x_all = jnp.concatenate([x, ctx], axis=1)
    cos, sin = _rope_tables(s, n_ctx)
    w1, wq2, wk2 = _inproj_weights(hy_w_in[0], mla_w_uq[0], mla_w_ukv[0])
    q, k, v, prw = _inproj(x_all, s, norm_mix_pre[0], sh_all, sc_all, cos, sin, w1, mla_q_norm[0], mla_kv_norm[0],
                           wq2, wk2)
    tq = 512 if s % 512 == 0 else ROW_TILE
    tk = next((t for t in (8320, 1280) if t_all % t == 0), ROW_TILE)
    o_attn = _attention(q, k, v, s, tq, tk)

    bd_ones = _block_diag(RWKV_DIM, RWKV_HEAD, 1.0)
    bd_mean = _block_diag(RWKV_DIM, RWKV_HEAD, 1.0 / RWKV_HEAD)
    r, kx, vx, g, kkn, lw, kd, bdir = _rwkv_prepare(prw, s, rwkv_mu_prev[0], rwkv_mu_next[0], rwkv_w0[0], rwkv_w2[0],
                                                    rwkv_a0[0], rwkv_a2[0], rwkv_g2[0], rwkv_k_k[0], rwkv_k_a[0],
                                                    bd_ones)
    m_c, n_c, r_c, y_c = _wkv_chunks(r, vx, kkn, lw, kd, bdir)
    n_lat_ch = s // WKV_CHUNK
    n_ctx_ch = n_ctx // WKV_CHUNK
    lat_ch = np.arange(n_lat_ch)
    ctx_ch = n_lat_ch + np.arange(n_ctx_ch)
    order = jnp.asarray(np.stack([np.concatenate([ctx_ch, lat_ch]),
                                  np.concatenate([ctx_ch[::-1], lat_ch[::-1]])]).astype(np.int32))
    yf, yb = _wkv_states(order, m_c, n_c, r_c, y_c, t_all)

    rw_pad, rb_pad = _router_pads(router_w[0], router_b[0])
    x1, hf, top_idx, gates = _mix_out(yf, yb, r, kx, vx, g, o_attn, x, rwkv_r_k[0].reshape(-1), rwkv_ln_w[0],
                                      rwkv_ln_b[0], bd_mean, bd_ones, hy_w_out[0].astype(BF16), norm_mix_post[0],
                                      lat_mod(0, 2), norm_ffn_pre[0], lat_mod(0, 3), lat_mod(0, 4), rw_pad, rb_pad)
    x2 = _moe(x1, hf, top_idx, gates, 0, moe_w_gu, moe_b_gu, moe_w_down, moe_b_down, norm_ffn_post[0], lat_mod(0, 5))

    rw_pad, rb_pad = _router_pads(router_w[1], router_b[1])
    x3, hf, top_idx, gates = _gmlp(x2, norm_mix_pre[1], lat_mod(1, 0), lat_mod(1, 1), gm_w_in[0], gm_v_norm_w[0],
                                   gm_v_norm_b[0], gm_w_s[0], gm_b_s[0], gm_w_out[0], norm_mix_post[1], lat_mod(1, 2),
                                   norm_ffn_pre[1], lat_mod(1, 3), lat_mod(1, 4), rw_pad, rb_pad)
    return _moe(x3, hf, top_idx, gates, 1, moe_w_gu, moe_b_gu, moe_w_down, moe_b_down, norm_ffn_post[1], lat_mod(1, 5))
```

```python
import functools

import jax
import jax.numpy as jnp
import numpy as np
from jax import lax
from jax.experimental import pallas as pl
from jax.experimental.pallas import tpu as pltpu

F32 = jnp.float32
BF16 = jnp.bfloat16
HIGHEST = lax.Precision.HIGHEST

D_MODEL = 1024
GRID_W = 64
EPS = 1e-6

MLA_HEADS = 8
MLA_NOPE = 64
MLA_ROPE = 32
MLA_V = 64
MLA_Q_RANK = 256
MLA_KV_RANK = 128
MLA_SCALE = (MLA_NOPE + MLA_ROPE) ** -0.5
ROPE_FREQS = MLA_ROPE // 4
ROPE_BASE = 10000.0
HEAD_PAD = 128
ATT_QSUB = 128
ATT_KSUB = 640
LOG2E = 1.4426950408889634
ATT_EXP_DTYPE = jnp.bfloat16
V_EXT = 80

RWKV_HEAD = 64
RWKV_DIM = 512
RWKV_HEADS = 8
DECAY_LORA = 32
AAA_LORA = 32
GATE_LORA = 96
RWKV_PAD_COLS = 3 * RWKV_DIM + 128 + 128
RWKV_LN_EPS = 64e-5
WKV_CHUNK = 64
WKV_CHUNKS_PER_STEP = 4

CHUNK = 128
GM_WIDTH = 1024
GM_GROUPS = 8
LN_EPS = 1e-5

N_EXPERTS = 32
TOP_K = 4
EXPERT_FF = 1024
SWIGLU_LIMIT = 7.0
SWIGLU_ALPHA = 1.702
LOGIT_PAD = 128
NEG_BIG = -1e30

ROW_TILE = 256
VMEM_LIMIT = 48 * 1024 * 1024


def _cparams(sem):
    return pltpu.CompilerParams(dimension_semantics=sem, vmem_limit_bytes=VMEM_LIMIT)


def _rms(x):
    return x * lax.rsqrt(jnp.mean(x * x, axis=-1, keepdims=True) + EPS)


def _dot(a, b, precision=None):
    return jnp.dot(a, b, preferred_element_type=F32, precision=precision)


def _dot_nt(a, b, precision=None):
    return lax.dot_general(a, b, (((1,), (1,)), ((), ())), preferred_element_type=F32, precision=precision)


def _dot_tn(a, b, precision=None):
    return lax.dot_general(a, b, (((0,), (0,)), ((), ())), preferred_element_type=F32, precision=precision)


def _split(x):
    hi = x.astype(BF16)
    return hi, (x - hi.astype(F32)).astype(BF16)


def _dot_split(a, b):
    a_hi, a_lo = _split(a)
    b_hi, b_lo = _split(b)
    return _dot(jnp.concatenate([a_hi, a_hi, a_lo], axis=1), jnp.concatenate([b_hi, b_lo, b_hi], axis=0))


def _mod_kernel(s_ref, w_ref, b_ref, o_ref):
    s = s_ref[...]
    s = s * jax.nn.sigmoid(s)
    o_ref[...] = _dot(s, w_ref[...], HIGHEST) + b_ref[...]


def _modulation(cond_rows, ada_w, ada_b):
    n_l, d, n6 = ada_w.shape
    tn = 1536
    return pl.pallas_call(
        _mod_kernel,
        grid=(n_l, n6 // tn),
        in_specs=[pl.BlockSpec((8, d), lambda l, j: (0, 0)),
                  pl.BlockSpec((None, d, tn), lambda l, j: (l, 0, j)),
                  pl.BlockSpec((None, 1, tn), lambda l, j: (l, 0, j))],
        out_specs=pl.BlockSpec((None, 8, tn), lambda l, j: (l, 0, j)),
        out_shape=jax.ShapeDtypeStruct((n_l, 8, n6), F32),
        compiler_params=_cparams(("arbitrary", "arbitrary")),
        name="adaln_mod",
    )(cond_rows, ada_w, ada_b.reshape(n_l, 1, n6))


def _inproj_kernel(x_ref, g_ref, sh_ref, sc_ref, cos_ref, sin_ref, w1_ref, qn_ref, kvn_ref, wq_ref, wk_ref,
                   q_ref, k_ref, vt_ref, prw_ref):
    h = _rms(x_ref[...]) * g_ref[...]
    h = h * (1.0 + sc_ref[...]) + sh_ref[...]
    p = _dot(h.astype(BF16), w1_ref[...])
    cos = jnp.concatenate([cos_ref[...]] * MLA_HEADS, axis=1)
    sin = jnp.concatenate([sin_ref[...]] * MLA_HEADS, axis=1)
    nq = MLA_HEADS * HEAD_PAD
    qn = _rms(p[:, :MLA_Q_RANK]) * qn_ref[...]
    qq = _dot(qn.astype(BF16), wq_ref[...])
    q_ref[...] = (qq[:, :nq] * cos + qq[:, nq:] * sin).astype(BF16)
    kvn = _rms(p[:, MLA_Q_RANK:MLA_Q_RANK + MLA_KV_RANK]) * kvn_ref[...]
    x2 = jnp.concatenate([kvn, p[:, MLA_Q_RANK + MLA_KV_RANK:512]], axis=1).astype(BF16)
    kk = _dot(x2, wk_ref[...])
    k_ref[...] = (kk[:, :nq] * cos + kk[:, nq:2 * nq] * sin).astype(BF16)
    vt = kk[:, 2 * nq:].T.astype(BF16)
    ones = jnp.ones((V_EXT - MLA_V, vt.shape[1]), BF16)
    for hd in range(MLA_HEADS):
        vt_ref[hd * V_EXT:hd * V_EXT + MLA_V, :] = vt[hd * MLA_V:(hd + 1) * MLA_V, :]
        vt_ref[hd * V_EXT + MLA_V:(hd + 1) * V_EXT, :] = ones
    prw_ref[...] = p[:, 512:]


def _rotate_half_cols(w):
    wr = w.reshape(w.shape[:-1] + (2, 2, ROPE_FREQS))
    return jnp.stack([-wr[..., 1, :], wr[..., 0, :]], axis=-2).reshape(w.shape)


def _inproj_weights(hy_w_in, w_uq, w_ukv):
    d = hy_w_in.shape[0]
    mla_cols = MLA_Q_RANK + MLA_KV_RANK + MLA_ROPE
    w_mla = hy_w_in[:, :mla_cols]
    w_rw = hy_w_in[:, mla_cols:]
    w1 = jnp.concatenate([w_mla, jnp.zeros((d, 512 - mla_cols), F32), _pad_rwkv_cols(w_rw)], axis=1).astype(BF16)
    wq = (w_uq * (MLA_SCALE * LOG2E)).reshape(MLA_Q_RANK, MLA_HEADS, MLA_NOPE + MLA_ROPE)
    z_n = jnp.zeros((MLA_Q_RANK, MLA_HEADS, MLA_NOPE), F32)
    z_p = jnp.zeros((MLA_Q_RANK, MLA_HEADS, HEAD_PAD - MLA_NOPE - MLA_ROPE), F32)
    wq_plain = jnp.concatenate([wq[..., :MLA_NOPE], wq[..., MLA_NOPE:], z_p], axis=-1)
    wq_rot = jnp.concatenate([z_n, _rotate_half_cols(wq[..., MLA_NOPE:]), z_p], axis=-1)
    wq2 = jnp.concatenate([wq_plain.reshape(MLA_Q_RANK, -1), wq_rot.reshape(MLA_Q_RANK, -1)], axis=1).astype(BF16)
    wkv = w_ukv.reshape(MLA_KV_RANK, MLA_HEADS, MLA_NOPE + MLA_V)
    eye = jnp.broadcast_to(jnp.eye(MLA_ROPE, dtype=F32)[:, None, :], (MLA_ROPE, MLA_HEADS, MLA_ROPE))
    pad_k = HEAD_PAD - MLA_NOPE - MLA_ROPE
    top_plain = jnp.concatenate([wkv[..., :MLA_NOPE], jnp.zeros((MLA_KV_RANK, MLA_HEADS, MLA_ROPE + pad_k), F32)], -1)
    mid_plain = jnp.concatenate([jnp.zeros((MLA_ROPE, MLA_HEADS, MLA_NOPE), F32), eye,
                                 jnp.zeros((MLA_ROPE, MLA_HEADS, pad_k), F32)], -1)
    mid_rot = jnp.concatenate([jnp.zeros((MLA_ROPE, MLA_HEADS, MLA_NOPE), F32), _rotate_half_cols(eye),
                               jnp.zeros((MLA_ROPE, MLA_HEADS, pad_k), F32)], -1)
    nq = MLA_HEADS * HEAD_PAD
    rows_c = jnp.concatenate([top_plain.reshape(MLA_KV_RANK, nq), jnp.zeros((MLA_KV_RANK, nq), F32),
                              wkv[..., MLA_NOPE:].reshape(MLA_KV_RANK, MLA_HEADS * MLA_V)], axis=1)
    rows_r = jnp.concatenate([mid_plain.reshape(MLA_ROPE, nq), mid_rot.reshape(MLA_ROPE, nq),
                              jnp.zeros((MLA_ROPE, MLA_HEADS * MLA_V), F32)], axis=1)
    rows_z = jnp.zeros((256 - MLA_KV_RANK - MLA_ROPE, rows_c.shape[1]), F32)
    wk2 = jnp.concatenate([rows_c, rows_r, rows_z], axis=0).astype(BF16)
    return w1, wq2, wk2


def _pad_rwkv_cols(w):
    pad = jnp.zeros(w.shape[:-1] + (RWKV_PAD_COLS - w.shape[-1],), w.dtype)
    return jnp.concatenate([w, pad], axis=-1)


def _rope_tables(n_lat, n_ctx):
    t = jnp.arange(n_lat)
    row = (t // GRID_W).astype(F32)
    col = (t % GRID_W).astype(F32)
    inv = ROPE_BASE ** (-jnp.arange(ROPE_FREQS, dtype=F32) / ROPE_FREQS)
    ang = jnp.stack([row[:, None] * inv, col[:, None] * inv], axis=1)
    ang = jnp.broadcast_to(ang[:, :, None, :], (n_lat, 2, 2, ROPE_FREQS)).reshape(n_lat, MLA_ROPE)
    cos = jnp.concatenate([jnp.ones((n_lat, MLA_NOPE), F32), jnp.cos(ang),
                           jnp.ones((n_lat, HEAD_PAD - MLA_NOPE - MLA_ROPE), F32)], axis=1)
    sin = jnp.concatenate([jnp.zeros((n_lat, MLA_NOPE), F32), jnp.sin(ang),
                           jnp.zeros((n_lat, HEAD_PAD - MLA_NOPE - MLA_ROPE), F32)], axis=1)
    cos = jnp.concatenate([cos, jnp.ones((n_ctx, HEAD_PAD), F32)], axis=0)
    sin = jnp.concatenate([sin, jnp.zeros((n_ctx, HEAD_PAD), F32)], axis=0)
    return cos, sin


def _inproj(x_all, n_lat, gain, shift, scale, cos, sin, w1, q_norm, kv_norm, wq2, wk2):
    b, t_all, d = x_all.shape
    tm = ROW_TILE
    n_lat_tiles = n_lat // tm
    nq = MLA_HEADS * HEAD_PAD

    def mod_map(bi, i):
        return (2 * bi + jnp.where(i >= n_lat_tiles, 1, 0), 0, 0)

    const = lambda bi, i: (0, 0)
    row = lambda bi, i: (bi, i, 0)
    return pl.pallas_call(
        _inproj_kernel,
        grid=(b, t_all // tm),
        in_specs=[pl.BlockSpec((None, tm, d), row),
                  pl.BlockSpec((1, d), const),
                  pl.BlockSpec((None, 1, d), mod_map),
                  pl.BlockSpec((None, 1, d), mod_map),
                  pl.BlockSpec((tm, HEAD_PAD), lambda bi, i: (i, 0)),
                  pl.BlockSpec((tm, HEAD_PAD), lambda bi, i: (i, 0)),
                  pl.BlockSpec(w1.shape, const),
                  pl.BlockSpec((1, MLA_Q_RANK), const),
                  pl.BlockSpec((1, MLA_KV_RANK), const),
                  pl.BlockSpec(wq2.shape, const),
                  pl.BlockSpec(wk2.shape, const)],
        out_specs=[pl.BlockSpec((None, tm, nq), row),
                   pl.BlockSpec((None, tm, nq), row),
                   pl.BlockSpec((None, MLA_HEADS * V_EXT, tm), lambda bi, i: (bi, 0, i)),
                   pl.BlockSpec((None, tm, RWKV_PAD_COLS), row)],
        out_shape=[jax.ShapeDtypeStruct((b, t_all, nq), BF16),
                   jax.ShapeDtypeStruct((b, t_all, nq), BF16),
                   jax.ShapeDtypeStruct((b, MLA_HEADS * V_EXT, t_all), BF16),
                   jax.ShapeDtypeStruct((b, t_all, RWKV_PAD_COLS), F32)],
        compiler_params=_cparams(("parallel", "parallel")),
        name="hybrid_inproj",
    )(x_all, gain.reshape(1, d), shift, scale, cos, sin, w1, q_norm.reshape(1, -1), kv_norm.reshape(1, -1), wq2, wk2)


def _attn_kernel(q_ref, k_ref, vt_ref, o_ref, m_sc, acc_sc):
    ki = pl.program_id(3)

    @pl.when(ki == 0)
    def _():
        m_sc[...] = jnp.full(m_sc.shape, -jnp.inf, F32)
        acc_sc[...] = jnp.zeros(acc_sc.shape, F32)

    tq, tk = q_ref.shape[0], k_ref.shape[0]
    chains = [(h, qs) for h in range(2) for qs in range(tq // ATT_QSUB)]
    hsl = [slice(h * HEAD_PAD, (h + 1) * HEAD_PAD) for h, _ in chains]
    vsl = [slice(h * V_EXT, (h + 1) * V_EXT) for h, _ in chains]
    qsl = [slice(qs * ATT_QSUB, (qs + 1) * ATT_QSUB) for _, qs in chains]
    nch = len(chains)
    q = [q_ref[qsl[c], hsl[c]] for c in range(nch)]
    m = [m_sc[chains[c][0], 0:1, qsl[c]] for c in range(nch)]
    acc = [acc_sc[chains[c][0], :, qsl[c]] for c in range(nch)]
    ksub = ATT_KSUB if tk % ATT_KSUB == 0 else tk
    for kb in range(tk // ksub):
        ksl = slice(kb * ksub, (kb + 1) * ksub)
        s = [_dot_nt(k_ref[ksl, hsl[c]], q[c]).astype(ATT_EXP_DTYPE) for c in range(nch)]
        m_new = [jnp.maximum(m[c], jnp.max(s[c], axis=0, keepdims=True).astype(F32)) for c in range(nch)]
        alpha = [jnp.exp2(m[c] - m_new[c]) for c in range(nch)]
        p = [jnp.exp2(s[c] - m_new[c].astype(ATT_EXP_DTYPE)).astype(BF16) for c in range(nch)]
        acc = [acc[c] * alpha[c] + _dot(vt_ref[vsl[c], ksl], p[c]) for c in range(nch)]
        m = m_new
    for c in range(nch):
        h = chains[c][0]
        m_sc[h, :, qsl[c]] = jnp.broadcast_to(m[c], (8, ATT_QSUB))
        acc_sc[h, :, qsl[c]] = acc[c]

    @pl.when(ki == pl.num_programs(3) - 1)
    def _():
        o0 = (acc_sc[0, :MLA_V, :] / acc_sc[0, MLA_V:MLA_V + 1, :]).T
        o1 = (acc_sc[1, :MLA_V, :] / acc_sc[1, MLA_V:MLA_V + 1, :]).T
        o_ref[...] = jnp.concatenate([o0, o1], axis=1).astype(BF16)


def _attention(q, k, vt, n_lat, tq, tk):
    b, t_all, _ = k.shape
    return pl.pallas_call(
        _attn_kernel,
        grid=(b, MLA_HEADS // 2, n_lat // tq, t_all // tk),
        in_specs=[pl.BlockSpec((None, tq, 2 * HEAD_PAD), lambda bi, hp, qi, ki: (bi, qi, hp)),
                  pl.BlockSpec((None, tk, 2 * HEAD_PAD), lambda bi, hp, qi, ki: (bi, ki, hp)),
                  pl.BlockSpec((None, 2 * V_EXT, tk), lambda bi, hp, qi, ki: (bi, hp, ki))],
        out_specs=pl.BlockSpec((None, tq, 2 * MLA_V), lambda bi, hp, qi, ki: (bi, qi, hp)),
        out_shape=jax.ShapeDtypeStruct((b, n_lat, MLA_HEADS * MLA_V), BF16),
        scratch_shapes=[pltpu.VMEM((2, 8, tq), F32), pltpu.VMEM((2, V_EXT, tq), F32)],
        compiler_params=_cparams(("parallel", "parallel", "parallel", "arbitrary")),
        name="mla_attention",
    )(q, k, vt)


def _seg_sum(x, bd_ref):
    hi = x.astype(BF16)
    lo = (x - hi.astype(F32)).astype(BF16)
    bd = bd_ref[...]
    return _dot(hi, bd) + _dot(lo, bd)


def _rwkv_prep_kernel(p_ref, prev_ref, next_ref, mup_ref, mun_ref, wl_ref, w0a0_ref, g2_ref, kk_ref, ka_ref, bd_ref,
                      r_ref, k_ref, v_ref, g_ref, kkn_ref, lw_ref, kd_ref, bdir_ref, *, n_lat_tiles, n_tiles):
    i = pl.program_id(1)
    p = p_ref[...]
    tm = p.shape[0]
    first = jnp.logical_or(i == 0, i == n_lat_tiles)
    last = jnp.logical_or(i == n_lat_tiles - 1, i == n_tiles - 1)
    prev_row = jnp.where(first, 0.0, prev_ref[7:8, :])
    next_row = jnp.where(last, 0.0, next_ref[0:1, :])
    ridx = lax.broadcasted_iota(jnp.int32, p.shape, 0)
    prev = jnp.where(ridx == 0, prev_row, pltpu.roll(p, 1, 0))
    nxt = jnp.where(ridx == tm - 1, next_row, pltpu.roll(p, tm - 1, 0))
    p = p + mup_ref[...] * (prev - p) + mun_ref[...] * (nxt - p)
    c = RWKV_DIM
    r, k, v = p[:, :c], p[:, c:2 * c], p[:, 2 * c:3 * c]
    lo = p[:, 3 * c:3 * c + 128]
    lane = lax.broadcasted_iota(jnp.int32, lo.shape, 1)
    lo = jnp.where(lane < 2 * DECAY_LORA, jnp.tanh(lo), lo)
    wa = _dot(lo.astype(BF16), wl_ref[...]) + w0a0_ref[...]
    gd = p[:, 3 * c + 128:]
    g_ref[...] = _dot(jax.nn.sigmoid(gd).astype(BF16), g2_ref[...])
    kk = k * kk_ref[...]
    kk = kk * lax.rsqrt(jnp.maximum(_seg_sum(kk * kk, bd_ref), 1e-24))
    r_ref[...] = r
    k_ref[...] = k
    v_ref[...] = v
    kkn_ref[...] = kk
    ka = ka_ref[...]
    for d in range(2):
        w = wa[:, d * c:(d + 1) * c]
        a = jax.nn.sigmoid(wa[:, (2 + d) * c:(3 + d) * c])
        lw_ref[:, d * c:(d + 1) * c] = -float(np.exp(-0.5)) * jax.nn.sigmoid(w)
        kd_ref[:, d * c:(d + 1) * c] = k * (1.0 + (a - 1.0) * ka)
        bdir_ref[:, d * c:(d + 1) * c] = kk * a


def _rwkv_prepare(prw, n_lat, mu_prev, mu_next, w0, w2, a0, a2, g2, k_k, k_a, bd_ones):
    b, t_all, pc = prw.shape
    tm = ROW_TILE
    n_tiles = t_all // tm
    n_lat_tiles = n_lat // tm
    c = RWKV_DIM
    z = jnp.zeros((DECAY_LORA, c), F32)
    wl = jnp.concatenate([
        jnp.concatenate([w2[0], z, z, z], axis=1), jnp.concatenate([z, w2[1], z, z], axis=1),
        jnp.concatenate([z, z, a2[0], z], axis=1), jnp.concatenate([z, z, z, a2[1]], axis=1)], axis=0).astype(BF16)
    w0a0 = jnp.concatenate([w0[0], w0[1], a0[0], a0[1]]).reshape(1, 4 * c)
    g2p = jnp.concatenate([g2, jnp.zeros((128 - GATE_LORA, c), F32)], axis=0).astype(BF16)
    row = lambda bi, i: (bi, i, 0)
    const = lambda bi, i: (0, 0)
    hb = tm // 8
    n_hb = t_all // 8
    kern = functools.partial(_rwkv_prep_kernel, n_lat_tiles=n_lat_tiles, n_tiles=n_tiles)
    o_c = jax.ShapeDtypeStruct((b, t_all, c), F32)
    o_2c = jax.ShapeDtypeStruct((b, t_all, 2 * c), F32)
    return pl.pallas_call(
        kern,
        grid=(b, n_tiles),
        in_specs=[pl.BlockSpec((None, tm, pc), row),
                  pl.BlockSpec((None, 8, pc), lambda bi, i: (bi, jnp.maximum(i * hb - 1, 0), 0)),
                  pl.BlockSpec((None, 8, pc), lambda bi, i: (bi, jnp.minimum((i + 1) * hb, n_hb - 1), 0)),
                  pl.BlockSpec((1, pc), const), pl.BlockSpec((1, pc), const),
                  pl.BlockSpec(wl.shape, const), pl.BlockSpec((1, 4 * c), const),
                  pl.BlockSpec(g2p.shape, const), pl.BlockSpec((1, c), const), pl.BlockSpec((1, c), const),
                  pl.BlockSpec((c, c), const)],
        out_specs=[pl.BlockSpec((None, tm, c), row)] * 5 + [pl.BlockSpec((None, tm, 2 * c), row)] * 3,
        out_shape=[o_c] * 5 + [o_2c] * 3,
        compiler_params=_cparams(("parallel", "parallel")),
        name="rwkv_prepare",
    )(prw, prw, prw, _pad_rwkv_cols(mu_prev).reshape(1, pc), _pad_rwkv_cols(mu_next).reshape(1, pc),
      wl, w0a0, g2p, k_k.reshape(1, c), k_a.reshape(1, c), bd_ones)


def _wkv_chunk_kernel(r_ref, v_ref, kk_ref, lw_ref, kd_ref, bd_ref, m_ref, n_ref, rr_ref, yv_ref):
    d = pl.program_id(1)
    cs = WKV_CHUNK
    nc = r_ref.shape[0] // cs
    ti = lax.broadcasted_iota(jnp.int32, (cs, cs), 0)
    si = lax.broadcasted_iota(jnp.int32, (cs, cs), 1)
    rel = (si - ti) * (1 - 2 * d)
    incl = rel <= 0
    strict = rel < 0
    eye = si == ti
    incl_f = incl.astype(F32)
    ops = []
    for ci in range(nc):
        rows = slice(ci * cs, (ci + 1) * cs)
        lw = lw_ref[rows, :]
        g = _dot(incl_f, lw, HIGHEST)
        total = jnp.sum(lw, axis=0, keepdims=True)
        e_inv = jnp.exp(-g)
        e_end = jnp.exp(total - g)
        gam = jnp.exp(total)
        kd = kd_ref[rows, :]
        bd = bd_ref[rows, :]
        at = -kk_ref[rows, :] * jnp.exp(g - lw)
        rt = r_ref[rows, :] * jnp.exp(g)
        kt = (kd * e_inv).astype(BF16)
        bt = (bd * e_inv).astype(BF16)
        ke = (kd * e_end).astype(BF16)
        be = (bd * e_end).astype(BF16)
        v32 = v_ref[rows, :]
        v = v32.astype(BF16)
        for h in range(RWKV_HEADS):
            sl = slice(h * RWKV_HEAD, (h + 1) * RWKV_HEAD)
            ops.append((at[:, sl], rt[:, sl], kt[:, sl], bt[:, sl], ke[:, sl], be[:, sl], v[:, sl], gam[:, sl], v32[:, sl]))
    n_it = len(ops)
    aa = [_dot_nt(jnp.concatenate([o[0], o[1]], axis=0).astype(BF16), jnp.concatenate([o[3], o[2]], axis=0))
          for o in ops]
    a_ab = [jnp.where(strict, x[:cs, :cs], 0.0) for x in aa]
    a_ak = [jnp.where(strict, x[:cs, cs:], 0.0).astype(BF16) for x in aa]
    ti2 = lax.broadcasted_iota(jnp.int32, (cs, 2 * cs), 0)
    si2 = lax.broadcasted_iota(jnp.int32, (cs, 2 * cs), 1)
    incl2 = (jnp.where(si2 >= cs, si2 - cs, si2) - ti2) * (1 - 2 * d) <= 0
    a_rbk = [jnp.where(incl2, x[cs:, :], 0.0).astype(BF16) for x in aa]
    z = [jnp.concatenate([ops[i][0], _dot(a_ak[i], ops[i][6])], axis=1) for i in range(n_it)]
    pw = a_ab
    n_sq = int(np.log2(cs))
    for it in range(n_sq):
        more = it + 1 < n_sq
        res = [_dot_split(pw[i], jnp.concatenate([z[i], pw[i]], axis=1) if more else z[i]) for i in range(n_it)]
        z = [z[i] + res[i][:, :2 * cs] for i in range(n_it)]
        if more:
            pw = [x[:, 2 * cs:] for x in res]
    zero = jnp.zeros((cs, cs), F32)
    wmat = [jnp.concatenate([z[i], jnp.concatenate([zero, ops[i][8]], axis=1)], axis=0).astype(BF16)
            for i in range(n_it)]
    w2 = [_dot(a_rbk[i], wmat[i]) for i in range(n_it)]
    w3 = [_dot_tn(jnp.concatenate([ops[i][5], ops[i][4]], axis=0), wmat[i]) for i in range(n_it)]
    nh = RWKV_HEADS
    for ci in range(nc):
        rows = slice(ci * cs, (ci + 1) * cs)
        ids = range(ci * nh, (ci + 1) * nh)
        m_ref[rows, :] = jnp.concatenate([jnp.where(eye, ops[i][7], 0.0) + w3[i][:, :cs] for i in ids], axis=1)
        n_ref[rows, :] = jnp.concatenate([w3[i][:, cs:] for i in ids], axis=1)
        rr_ref[rows, :] = jnp.concatenate([ops[i][1] + w2[i][:, :cs] for i in ids], axis=1)
        yv_ref[rows, :] = jnp.concatenate([w2[i][:, cs:] for i in ids], axis=1)


def _wkv_chunks(r, v, kkn, lw, kd, bdir):
    b, t_all, c = r.shape
    rows = WKV_CHUNK * WKV_CHUNKS_PER_STEP
    nst = t_all // rows
    shared = pl.BlockSpec((None, rows, c), lambda bi, d, ci: (bi, ci, 0))
    per_dir = pl.BlockSpec((None, rows, c), lambda bi, d, ci: (bi, ci, d))
    out = pl.BlockSpec((None, None, rows, c), lambda bi, d, ci: (bi, d, ci, 0))
    o_s = jax.ShapeDtypeStruct((b, 2, t_all, c), F32)
    return pl.pallas_call(
        _wkv_chunk_kernel,
        grid=(b, 2, nst),
        in_specs=[shared, shared, shared, per_dir, per_dir, per_dir],
        out_specs=[out] * 4,
        out_shape=[o_s] * 4,
        compiler_params=_cparams(("parallel", "parallel", "parallel")),
        name="wkv_chunk_summaries",
    )(r, v, kkn, lw, kd, bdir)


def _wkv_state_kernel(order_ref, mf_ref, nf_ref, rf_ref, yf_ref, mb_ref, nb_ref, rb_ref, yb_ref,
                      of_ref, ob_ref, st_sc):
    del order_ref
    s = pl.program_id(0)

    @pl.when(s == 0)
    def _():
        st_sc[...] = jnp.zeros(st_sc.shape, F32)

    n_b = st_sc.shape[0]
    ins = ((mf_ref, nf_ref, rf_ref, yf_ref, of_ref), (mb_ref, nb_ref, rb_ref, yb_ref, ob_ref))
    for bi in range(n_b):
        for d in range(2):
            m_r, n_r, r_r, y_r, o_r = ins[d]
            st = st_sc[bi, d]
            m, n, rr, yv = m_r[bi], n_r[bi], r_r[bi], y_r[bi]
            st_hi, st_lo = _split(st)
            m_hi, m_lo = _split(m)
            rr_b = rr.astype(BF16)
            zpad = jnp.zeros((rr.shape[0], 2 * RWKV_HEAD), BF16)
            ys, sts = [], []
            for h in range(RWKV_HEADS):
                sl = slice(h * RWKV_HEAD, (h + 1) * RWKV_HEAD)
                lhs = jnp.concatenate([jnp.concatenate([m_hi[:, sl], m_hi[:, sl], m_lo[:, sl]], axis=1),
                                       jnp.concatenate([rr_b[:, sl], zpad], axis=1)], axis=0)
                rhs = jnp.concatenate([st_hi[:, sl], st_lo[:, sl], st_hi[:, sl]], axis=0)
                res = _dot(lhs, rhs)
                cs = rr.shape[0]
                sts.append(res[:cs] + n[:, sl])
                ys.append(res[cs:] + yv[:, sl])
            o_r[bi] = jnp.concatenate(ys, axis=1)
            st_sc[bi, d] = jnp.concatenate(sts, axis=1)


def _wkv_states(order, m, n, rr, yv, t_all):
    b, _, _, c = m.shape
    cs = WKV_CHUNK
    nch = t_all // cs
    fwd = pl.BlockSpec((b, None, cs, c), lambda s, o: (0, 0, o[0, s], 0))
    bwd = pl.BlockSpec((b, None, cs, c), lambda s, o: (0, 1, o[1, s], 0))
    out_f = pl.BlockSpec((b, cs, c), lambda s, o: (0, o[0, s], 0))
    out_b = pl.BlockSpec((b, cs, c), lambda s, o: (0, o[1, s], 0))
    o_s = jax.ShapeDtypeStruct((b, t_all, c), F32)
    return pl.pallas_call(
        _wkv_state_kernel,
        grid_spec=pltpu.PrefetchScalarGridSpec(
            num_scalar_prefetch=1,
            grid=(nch,),
            in_specs=[fwd] * 4 + [bwd] * 4,
            out_specs=[out_f, out_b],
            scratch_shapes=[pltpu.VMEM((b, 2, cs, c), F32)]),
        out_shape=[o_s, o_s],
        compiler_params=_cparams(("arbitrary",)),
        name="wkv_state_pass",
    )(order, m, n, rr, yv, m, n, rr, yv)


def _ffn_pre(x1, gain_ref, sh_ref, sc_ref, rw_ref, rb_ref, hf_ref, idx_ref, gate_ref):
    hf = _rms(x1) * gain_ref[...]
    hf = hf * (1.0 + sc_ref[...]) + sh_ref[...]
    hf_ref[...] = hf.astype(BF16)
    hf_hi, hf_lo = _split(hf)
    t = _dot(hf_hi, rw_ref[...])
    u = _dot(hf_lo, rw_ref[:, :LOGIT_PAD])
    logits = t[:, :LOGIT_PAD] + t[:, LOGIT_PAD:] + u + rb_ref[...]
    lane = lax.broadcasted_iota(jnp.int32, logits.shape, 1)
    lane_f = lane.astype(F32)
    work = logits
    idx_out = jnp.zeros(logits.shape, F32)
    val_out = jnp.full(logits.shape, NEG_BIG, F32)
    for kth in range(TOP_K):
        m = jnp.max(work, axis=1, keepdims=True)
        idx = jnp.min(jnp.where(work == m, lane_f, float(LOGIT_PAD)), axis=1, keepdims=True)
        work = jnp.where(lane_f == idx, -jnp.inf, work)
        idx_out = jnp.where(lane == kth, idx, idx_out)
        val_out = jnp.where(lane == kth, m, val_out)
    e = jnp.exp(val_out - jnp.max(val_out, axis=1, keepdims=True))
    e = jnp.where(lane < TOP_K, e, 0.0)
    idx_ref[...] = idx_out.astype(jnp.int32)
    gate_ref[...] = e / jnp.sum(e, axis=1, keepdims=True)


def _router_pads(router_w, router_b):
    d = router_w.shape[0]
    rw = jnp.concatenate([router_w, jnp.zeros((d, LOGIT_PAD - N_EXPERTS), F32)], axis=1)
    rb = jnp.concatenate([router_b, jnp.full((LOGIT_PAD - N_EXPERTS,), NEG_BIG, F32)]).reshape(1, LOGIT_PAD)
    return jnp.concatenate(_split(rw), axis=1), rb


def _mix_out_kernel(yf_ref, yb_ref, r_ref, k_ref, v_ref, g_ref, o_ref, x_ref,
                    rk_ref, lnw_ref, lnb_ref, bdm_ref, bds_ref, wout_ref, npost_ref, g1_ref,
                    gpre_ref, sh_ref, sc_ref, rw_ref, rb_ref,
                    x1_ref, hf_ref, idx_ref, gate_ref):
    y = yf_ref[...] + yb_ref[...]
    mu = _seg_sum(y, bdm_ref)
    dlt = y - mu
    var = _seg_sum(dlt * dlt, bdm_ref)
    yn = dlt * lax.rsqrt(var + RWKV_LN_EPS) * lnw_ref[...] + lnb_ref[...]
    bonus = _seg_sum(r_ref[...] * k_ref[...] * rk_ref[...], bds_ref) * v_ref[...]
    rw = (yn + bonus) * g_ref[...]
    mix_in = jnp.concatenate([o_ref[...], rw.astype(BF16)], axis=1)
    mix = _dot(mix_in, wout_ref[...])
    x1 = x_ref[...] + g1_ref[...] * (_rms(mix) * npost_ref[...])
    x1_ref[...] = x1
    _ffn_pre(x1, gpre_ref, sh_ref, sc_ref, rw_ref, rb_ref, hf_ref, idx_ref, gate_ref)


def _token_out_specs(tm, d):
    row = lambda bi, i: (bi, i, 0)
    specs = [pl.BlockSpec((None, tm, d), row), pl.BlockSpec((None, tm, d), row),
             pl.BlockSpec((None, tm, LOGIT_PAD), row), pl.BlockSpec((None, tm, LOGIT_PAD), row)]
    return specs


def _token_out_shapes(b, s, d):
    return [jax.ShapeDtypeStruct((b, s, d), F32), jax.ShapeDtypeStruct((b, s, d), BF16),
            jax.ShapeDtypeStruct((b, s, LOGIT_PAD), jnp.int32), jax.ShapeDtypeStruct((b, s, LOGIT_PAD), F32)]


def _mix_out(yf, yb, r, k, v, g, o_attn, x, r_k, ln_w, ln_b, bd_mean, bd_ones, w_out, n_post, g1,
             n_pre, sh2, sc2, rw_pad, rb_pad):
    b, s, d = x.shape
    c = RWKV_DIM
    tm = ROW_TILE
    row = lambda bi, i: (bi, i, 0)
    const = lambda bi, i: (0, 0)
    per_b = lambda bi, i: (bi, 0, 0)
    rc = pl.BlockSpec((None, tm, c), row)
    vec_c = pl.BlockSpec((1, c), const)
    vec_d = pl.BlockSpec((1, d), const)
    mod_d = pl.BlockSpec((None, 1, d), per_b)
    return pl.pallas_call(
        _mix_out_kernel,
        grid=(b, s // tm),
        in_specs=[rc] * 7 + [pl.BlockSpec((None, tm, d), row),
                             vec_c, vec_c, vec_c, pl.BlockSpec((c, c), const), pl.BlockSpec((c, c), const),
                             pl.BlockSpec((d, d), const), vec_d, mod_d,
                             vec_d, mod_d, mod_d, pl.BlockSpec((d, 2 * LOGIT_PAD), const),
                             pl.BlockSpec((1, LOGIT_PAD), const)],
        out_specs=_token_out_specs(tm, d),
        out_shape=_token_out_shapes(b, s, d),
        compiler_params=_cparams(("parallel", "parallel")),
        name="mixer_out_router",
    )(yf, yb, r, k, v, g, o_attn, x, r_k.reshape(1, c), ln_w.reshape(1, c), ln_b.reshape(1, c), bd_mean, bd_ones,
      w_out, n_post.reshape(1, d), g1, n_pre.reshape(1, d), sh2, sc2, rw_pad, rb_pad)


def _gmlp_kernel(x_ref, gpre1_ref, sh1_ref, sc1_ref, win_ref, vnw_ref, vnb_ref, ws_ref, bs_ref, wout_ref,
                 npost_ref, g1_ref, gpre_ref, sh_ref, sc_ref, rw_ref, rb_ref,
                 x1_ref, hf_ref, idx_ref, gate_ref):
    x = x_ref[...]
    h = _rms(x) * gpre1_ref[...]
    h = h * (1.0 + sc1_ref[...]) + sh1_ref[...]
    z = _dot(h.astype(BF16), win_ref[...])
    z = 0.5 * z * (1.0 + lax.erf(z * float(2.0 ** -0.5)))
    u, v = z[:, :GM_WIDTH], z[:, GM_WIDTH:]
    mu = jnp.mean(v, axis=-1, keepdims=True)
    dv = v - mu
    var = jnp.mean(dv * dv, axis=-1, keepdims=True)
    v = (dv * lax.rsqrt(var + LN_EPS) * vnw_ref[...] + vnb_ref[...]).astype(BF16)
    gw = GM_WIDTH // GM_GROUPS
    rows = []
    for ci in range(x.shape[0] // CHUNK):
        cols = []
        for gi in range(GM_GROUPS):
            cols.append(_dot(ws_ref[gi], v[ci * CHUNK:(ci + 1) * CHUNK, gi * gw:(gi + 1) * gw]))
        rows.append(jnp.concatenate(cols, axis=1) + bs_ref[...])
    sp = jnp.concatenate(rows, axis=0)
    y = _dot((u * sp).astype(BF16), wout_ref[...])
    x1 = x + g1_ref[...] * (_rms(y) * npost_ref[...])
    x1_ref[...] = x1
    _ffn_pre(x1, gpre_ref, sh_ref, sc_ref, rw_ref, rb_ref, hf_ref, idx_ref, gate_ref)


def _gmlp(x, n_pre1, sh1, sc1, w_in, vn_w, vn_b, w_s, b_s, w_out, n_post, g1, n_pre, sh2, sc2, rw_pad, rb_pad):
    b, s, d = x.shape
    tm = ROW_TILE
    gw = GM_WIDTH // GM_GROUPS
    bs_full = jnp.repeat(b_s.T, gw, axis=1)
    row = lambda bi, i: (bi, i, 0)
    const = lambda bi, i: (0, 0)
    per_b = lambda bi, i: (bi, 0, 0)
    vec_d = pl.BlockSpec((1, d), const)
    vec_g = pl.BlockSpec((1, GM_WIDTH), const)
    mod_d = pl.BlockSpec((None, 1, d), per_b)
    return pl.pallas_call(
        _gmlp_kernel,
        grid=(b, s // tm),
        in_specs=[pl.BlockSpec((None, tm, d), row), vec_d, mod_d, mod_d,
                  pl.BlockSpec((d, 2 * GM_WIDTH), const), vec_g, vec_g,
                  pl.BlockSpec((GM_GROUPS, CHUNK, CHUNK), lambda bi, i: (0, 0, 0)),
                  pl.BlockSpec((CHUNK, GM_WIDTH), const), pl.BlockSpec((GM_WIDTH, d), const),
                  vec_d, mod_d, vec_d, mod_d, mod_d,
                  pl.BlockSpec((d, 2 * LOGIT_PAD), const), pl.BlockSpec((1, LOGIT_PAD), const)],
        out_specs=_token_out_specs(tm, d),
        out_shape=_token_out_shapes(b, s, d),
        compiler_params=_cparams(("parallel", "parallel")),
        name="gmlp_router",
    )(x, n_pre1.reshape(1, d), sh1, sc1, w_in.astype(BF16), vn_w.reshape(1, -1), vn_b.reshape(1, -1),
      w_s.astype(BF16), bs_full, w_out.astype(BF16), n_post.reshape(1, d), g1, n_pre.reshape(1, d), sh2, sc2,
      rw_pad, rb_pad)


MOE_ROWS = 512


def _expert_kernel(be_ref, na_ref, x_ref, wgu_ref, bgu_ref, wd_ref, bd_ref, y_ref, wgu_bf, wd_bf):
    i = pl.program_id(0)
    prev = be_ref[jnp.maximum(i - 1, 0)]
    changed = jnp.logical_or(i == 0, be_ref[i] != prev)

    @pl.when(changed)
    def _():
        wgu_bf[...] = wgu_ref[...].astype(BF16)
        wd_bf[...] = wd_ref[...].astype(BF16)

    @pl.when(i < na_ref[0])
    def _():
        gu = _dot(x_ref[...], wgu_bf[...]) + bgu_ref[...]
        x_glu = jnp.minimum(gu[:, :EXPERT_FF], SWIGLU_LIMIT)
        x_lin = jnp.clip(gu[:, EXPERT_FF:], -SWIGLU_LIMIT, SWIGLU_LIMIT)
        act = x_glu * jax.nn.sigmoid(SWIGLU_ALPHA * x_glu) * (x_lin + 1.0)
        y_ref[...] = (_dot(act.astype(BF16), wd_bf[...]) + bd_ref[...]).astype(BF16)

    @pl.when(i >= na_ref[0])
    def _():
        y_ref[...] = jnp.zeros(y_ref.shape, BF16)


def _experts(blk_expert, n_active, xg, layer, w_gu, b_gu, w_down, b_down):
    n_rows, d = xg.shape
    n_l, n_e, _, ff2 = w_gu.shape
    tb = MOE_ROWS
    n_blocks = n_rows // tb
    return pl.pallas_call(
        _expert_kernel,
        grid_spec=pltpu.PrefetchScalarGridSpec(
            num_scalar_prefetch=2,
            grid=(n_blocks,),
            in_specs=[pl.BlockSpec((tb, d), lambda i, be, na: (i, 0)),
                      pl.BlockSpec((None, None, d, ff2), lambda i, be, na: (layer, be[i], 0, 0)),
                      pl.BlockSpec((None, None, 1, ff2), lambda i, be, na: (layer, be[i], 0, 0)),
                      pl.BlockSpec((None, None, ff2 // 2, d), lambda i, be, na: (layer, be[i], 0, 0)),
                      pl.BlockSpec((None, None, 1, d), lambda i, be, na: (layer, be[i], 0, 0))],
            out_specs=pl.BlockSpec((tb, d), lambda i, be, na: (i, 0)),
            scratch_shapes=[pltpu.VMEM((d, ff2), BF16), pltpu.VMEM((ff2 // 2, d), BF16)]),
        out_shape=jax.ShapeDtypeStruct((n_rows, d), BF16),
        compiler_params=_cparams(("arbitrary",)),
        name="moe_experts",
    )(blk_expert, n_active, xg, w_gu, b_gu.reshape(n_l, n_e, 1, ff2), w_down, b_down.reshape(n_l, n_e, 1, d))


def _combine_kernel(x_ref, y_ref, gate_ref, npost_ref, g2_ref, o_ref):
    gate = gate_ref[...]
    f = jnp.zeros(x_ref.shape, F32)
    for kth in range(TOP_K):
        f = f + gate[:, kth:kth + 1] * y_ref[kth].astype(F32)
    o_ref[...] = x_ref[...] + g2_ref[...] * (_rms(f) * npost_ref[...])


def _combine(x1, yk, gates, n_post, g2):
    b, s, d = x1.shape
    tm = ROW_TILE
    row = lambda bi, i: (bi, i, 0)
    return pl.pallas_call(
        _combine_kernel,
        grid=(b, s // tm),
        in_specs=[pl.BlockSpec((None, tm, d), row),
                  pl.BlockSpec((TOP_K, None, tm, d), lambda bi, i: (0, bi, i, 0)),
                  pl.BlockSpec((None, tm, LOGIT_PAD), row),
                  pl.BlockSpec((1, d), lambda bi, i: (0, 0)),
                  pl.BlockSpec((None, 1, d), lambda bi, i: (bi, 0, 0))],
        out_specs=pl.BlockSpec((None, tm, d), row),
        out_shape=jax.ShapeDtypeStruct((b, s, d), F32),
        compiler_params=_cparams(("parallel", "parallel")),
        name="moe_combine_residual",
    )(x1, yk, gates, n_post.reshape(1, d), g2)


def _lookup(table, idx):
    n = table.shape[0]
    return jnp.sum(jnp.where(idx[:, None] == jnp.arange(n, dtype=jnp.int32)[None, :], table[None, :], 0), axis=1)


def _rows(a, idx):
    return a.at[idx].get(mode="promise_in_bounds")


def _moe(x1, hf, top_idx, gates, layer,w_gu, b_gu, w_down, b_down, n_post, g2):
    b, s, d = x1.shape
    n_tok = b * s
    tb = MOE_ROWS
    n_assign = n_tok * TOP_K
    assert N_EXPERTS * n_assign < 2 ** 31
    i32 = jnp.int32
    e_flat = top_idx[..., :TOP_K].reshape(-1)
    skey = jnp.sort(e_flat * n_assign + jnp.arange(n_assign, dtype=i32))
    order = skey % n_assign
    _, inv = lax.sort((order, jnp.arange(n_assign, dtype=i32)), num_keys=1)
    edges = jnp.arange(N_EXPERTS + 1, dtype=i32) * n_assign
    bounds = jnp.sum((skey[None, :] < edges[:, None]).astype(i32), axis=1)
    start = bounds[:-1]
    counts = bounds[1:] - start
    padded = (counts + tb - 1) // tb * tb
    pend = jnp.cumsum(padded)
    pstart = pend - padded
    n_rows = -(-n_assign // tb) * tb + N_EXPERTS * tb
    n_blocks = n_rows // tb
    blk_start = jnp.arange(n_blocks, dtype=i32) * tb
    blk_expert = jnp.minimum(jnp.sum((pend[None, :] <= blk_start[:, None]).astype(i32), axis=1), N_EXPERTS - 1)
    n_active = (pend[-1] // tb).astype(i32).reshape(1)
    j = jnp.arange(n_rows, dtype=i32) - jnp.repeat(_lookup(pstart, blk_expert), tb)
    src = jnp.repeat(_lookup(start, blk_expert), tb) + j
    valid = j < jnp.repeat(_lookup(counts, blk_expert), tb)
    row_tok = jnp.where(valid, _rows(order, jnp.clip(src, 0, n_assign - 1)) // TOP_K, 0)
    pos = _lookup(pstart - start, e_flat) + inv
    xg = _rows(hf.reshape(n_tok, d), row_tok)
    y = _experts(blk_expert, n_active, xg, layer, w_gu, b_gu, w_down, b_down)
    yk = _rows(y, pos.reshape(n_tok, TOP_K).T.reshape(-1)).reshape(TOP_K, b, s, d)
    return _combine(x1, yk, gates, n_post, g2)


def _block_diag(n, blk, val):
    return (jnp.kron(jnp.eye(n // blk, dtype=F32), jnp.ones((blk, blk), F32)) * val).astype(BF16)


def kernel(x, c, ctx, c_ctx, ada_w, ada_b, norm_mix_pre, norm_mix_post, norm_ffn_pre, norm_ffn_post, router_w, router_b, moe_w_gu, moe_b_gu, moe_w_down, moe_b_down, hy_w_in, mla_q_norm, mla_w_uq, mla_kv_norm, mla_w_ukv, rwkv_mu_prev, rwkv_mu_next, rwkv_w0, rwkv_w2, rwkv_a0, rwkv_a2, rwkv_g2, rwkv_k_k, rwkv_k_a, rwkv_r_k, rwkv_ln_w, rwkv_ln_b, hy_w_out, gm_w_in, gm_v_norm_w, gm_v_norm_b, gm_w_s, gm_b_s, gm_w_out):
    b, s, d = x.shape
    n_ctx = ctx.shape[1]
    t_all = s + n_ctx
    assert b + 1 <= 8 and s % ROW_TILE == 0 and n_ctx % ROW_TILE == 0

    cond_rows = jnp.concatenate([c, c_ctx[None], jnp.zeros((8 - b - 1, d), F32)], axis=0)
    mod = _modulation(cond_rows, ada_w, ada_b)

    def lat_mod(l, j):
        return mod[l, :b, j * d:(j + 1) * d].reshape(b, 1, d)

    sh_all = jnp.stack([mod[0, :b, 0:d], jnp.broadcast_to(mod[0, b, 0:d], (b, d))], axis=1).reshape(2 * b, 1, d)
    sc_all = jnp.stack([mod[0, :b, d:2 * d], jnp.broadcast_to(mod[0, b, d:2 * d], (b, d))], axis=1).reshape(2 * b, 1, d)
    x_all = jnp.concatenate([x, ctx], axis=1)
    cos, sin = _rope_tables(s, n_ctx)
    w1, wq2, wk2 = _inproj_weights(hy_w_in[0], mla_w_uq[0], mla_w_ukv[0])
    q, k, v, prw = _inproj(x_all, s, norm_mix_pre[0], sh_all, sc_all, cos, sin, w1, mla_q_norm[0], mla_kv_norm[0],
                           wq2, wk2)
    tq = 512 if s % 512 == 0 else ROW_TILE
    tk = next((t for t in (8320, 1280) if t_all % t == 0), ROW_TILE)
    o_attn = _attention(q, k, v, s, tq, tk)

    bd_ones = _block_diag(RWKV_DIM, RWKV_HEAD, 1.0)
    bd_mean = _block_diag(RWKV_DIM, RWKV_HEAD, 1.0 / RWKV_HEAD)
    r, kx, vx, g, kkn, lw, kd, bdir = _rwkv_prepare(prw, s, rwkv_mu_prev[0], rwkv_mu_next[0], rwkv_w0[0], rwkv_w2[0],
                                                    rwkv_a0[0], rwkv_a2[0], rwkv_g2[0], rwkv_k_k[0], rwkv_k_a[0],
                                                    bd_ones)
    m_c, n_c, r_c, y_c = _wkv_chunks(r, vx, kkn, lw, kd, bdir)
    n_lat_ch = s // WKV_CHUNK
    n_ctx_ch = n_ctx // WKV_CHUNK
    lat_ch = np.arange(n_lat_ch)
    ctx_ch = n_lat_ch + np.arange(n_ctx_ch)
    order = jnp.asarray(np.stack([np.concatenate([ctx_ch, lat_ch]),
                                  np.concatenate([ctx_ch[::-1], lat_ch[::-1]])]).astype(np.int32))
    yf, yb = _wkv_states(order, m_c, n_c, r_c, y_c, t_all)

    rw_pad, rb_pad = _router_pads(router_w[0], router_b[0])
    x1, hf, top_idx, gates = _mix_out(yf, yb, r, kx, vx, g, o_attn, x, rwkv_r_k[0].reshape(-1), rwkv_ln_w[0],
                                      rwkv_ln_b[0], bd_mean, bd_ones, hy_w_out[0].astype(BF16), norm_mix_post[0],
                                      lat_mod(0, 2), norm_ffn_pre[0], lat_mod(0, 3), lat_mod(0, 4), rw_pad, rb_pad)
    x2 = _moe(x1, hf, top_idx, gates, 0, moe_w_gu, moe_b_gu, moe_w_down, moe_b_down, norm_ffn_post[0], lat_mod(0, 5))

    rw_pad, rb_pad = _router_pads(router_w[1], router_b[1])
    x3, hf, top_idx, gates = _gmlp(x2, norm_mix_pre[1], lat_mod(1, 0), lat_mod(1, 1), gm_w_in[0], gm_v_norm_w[0],
                                   gm_v_norm_b[0], gm_w_s[0], gm_b_s[0], gm_w_out[0], norm_mix_post[1], lat_mod(1, 2),
                                   norm_ffn_pre[1], lat_mod(1, 3), lat_mod(1, 4), rw_pad, rb_pad)
    return _moe(x3, hf, top_idx, gates, 1, moe_w_gu, moe_b_gu, moe_w_down, moe_b_down, norm_ffn_post[1], lat_mod(1, 5))
```

```python
import functools

import jax
import jax.numpy as jnp
import numpy as np
from jax import lax
from jax.experimental import pallas as pl
from jax.experimental.pallas import tpu as pltpu

F32 = jnp.float32
BF16 = jnp.bfloat16
HIGHEST = lax.Precision.HIGHEST

D_MODEL = 1024
GRID_W = 64
EPS = 1e-6

MLA_HEADS = 8
MLA_NOPE = 64
MLA_ROPE = 32
MLA_V = 64
MLA_Q_RANK = 256
MLA_KV_RANK = 128
MLA_SCALE = (MLA_NOPE + MLA_ROPE) ** -0.5
ROPE_FREQS = MLA_ROPE // 4
ROPE_BASE = 10000.0
HEAD_PAD = 128
ATT_QSUB = 128
ATT_KSUB = 640
LOG2E = 1.4426950408889634
ATT_EXP_DTYPE = jnp.bfloat16
V_EXT = 80

RWKV_HEAD = 64
RWKV_DIM = 512
RWKV_HEADS = 8
DECAY_LORA = 32
AAA_LORA = 32
GATE_LORA = 96
RWKV_PAD_COLS = 3 * RWKV_DIM + 128 + 128
RWKV_LN_EPS = 64e-5
WKV_CHUNK = 64
WKV_CHUNKS_PER_STEP = 4

CHUNK = 128
GM_WIDTH = 1024
GM_GROUPS = 8
LN_EPS = 1e-5

N_EXPERTS = 32
TOP_K = 4
EXPERT_FF = 1024
SWIGLU_LIMIT = 7.0
SWIGLU_ALPHA = 1.702
LOGIT_PAD = 128
NEG_BIG = -1e30

ROW_TILE = 256
VMEM_LIMIT = 48 * 1024 * 1024


def _cparams(sem):
    return pltpu.CompilerParams(dimension_semantics=sem, vmem_limit_bytes=VMEM_LIMIT)


def _rms(x):
    return x * lax.rsqrt(jnp.mean(x * x, axis=-1, keepdims=True) + EPS)


def _dot(a, b, precision=None):
    return jnp.dot(a, b, preferred_element_type=F32, precision=precision)


def _dot_nt(a, b, precision=None):
    return lax.dot_general(a, b, (((1,), (1,)), ((), ())), preferred_element_type=F32, precision=precision)


def _dot_tn(a, b, precision=None):
    return lax.dot_general(a, b, (((0,), (0,)), ((), ())), preferred_element_type=F32, precision=precision)


def _split(x):
    hi = x.astype(BF16)
    return hi, (x - hi.astype(F32)).astype(BF16)


def _dot_split(a, b):
    a_hi, a_lo = _split(a)
    b_hi, b_lo = _split(b)
    return _dot(jnp.concatenate([a_hi, a_hi, a_lo], axis=1), jnp.concatenate([b_hi, b_lo, b_hi], axis=0))


def _mod_kernel(s_ref, w_ref, b_ref, o_ref):
    s = s_ref[...]
    s = s * jax.nn.sigmoid(s)
    o_ref[...] = _dot(s, w_ref[...], HIGHEST) + b_ref[...]


def _modulation(cond_rows, ada_w, ada_b):
    n_l, d, n6 = ada_w.shape
    tn = 1536
    return pl.pallas_call(
        _mod_kernel,
        grid=(n_l, n6 // tn),
        in_specs=[pl.BlockSpec((8, d), lambda l, j: (0, 0)),
                  pl.BlockSpec((None, d, tn), lambda l, j: (l, 0, j)),
                  pl.BlockSpec((None, 1, tn), lambda l, j: (l, 0, j))],
        out_specs=pl.BlockSpec((None, 8, tn), lambda l, j: (l, 0, j)),
        out_shape=jax.ShapeDtypeStruct((n_l, 8, n6), F32),
        compiler_params=_cparams(("arbitrary", "arbitrary")),
        name="adaln_mod",
    )(cond_rows, ada_w, ada_b.reshape(n_l, 1, n6))


def _inproj_kernel(x_ref, ctx_ref, g_ref, sh_ref, sc_ref, cos_ref, sin_ref, w1_ref, qn_ref, kvn_ref, wq_ref, wk_ref,
                   q_ref, k_ref, vt_ref, prw_ref, *, n_lat_tiles):
    x_in = jnp.where(pl.program_id(1) >= n_lat_tiles, ctx_ref[...], x_ref[...])
    h = _rms(x_in) * g_ref[...]
    h = h * (1.0 + sc_ref[...]) + sh_ref[...]
    p = _dot(h.astype(BF16), w1_ref[...])
    cos = jnp.concatenate([cos_ref[...]] * MLA_HEADS, axis=1)
    sin = jnp.concatenate([sin_ref[...]] * MLA_HEADS, axis=1)
    nq = MLA_HEADS * HEAD_PAD
    qn = _rms(p[:, :MLA_Q_RANK]) * qn_ref[...]
    qq = _dot(qn.astype(BF16), wq_ref[...])
    q_ref[...] = (qq[:, :nq] * cos + qq[:, nq:] * sin).astype(BF16)
    kvn = _rms(p[:, MLA_Q_RANK:MLA_Q_RANK + MLA_KV_RANK]) * kvn_ref[...]
    x2 = jnp.concatenate([kvn, p[:, MLA_Q_RANK + MLA_KV_RANK:512]], axis=1).astype(BF16)
    kk = _dot(x2, wk_ref[...])
    k_ref[...] = (kk[:, :nq] * cos + kk[:, nq:2 * nq] * sin).astype(BF16)
    vt = kk[:, 2 * nq:].T.astype(BF16)
    ones = jnp.ones((V_EXT - MLA_V, vt.shape[1]), BF16)
    for hd in range(MLA_HEADS):
        vt_ref[hd * V_EXT:hd * V_EXT + MLA_V, :] = vt[hd * MLA_V:(hd + 1) * MLA_V, :]
        vt_ref[hd * V_EXT + MLA_V:(hd + 1) * V_EXT, :] = ones
    prw_ref[...] = p[:, 512:]


def _rotate_half_cols(w):
    wr = w.reshape(w.shape[:-1] + (2, 2, ROPE_FREQS))
    return jnp.stack([-wr[..., 1, :], wr[..., 0, :]], axis=-2).reshape(w.shape)


def _inproj_weights(hy_w_in, w_uq, w_ukv):
    d = hy_w_in.shape[0]
    mla_cols = MLA_Q_RANK + MLA_KV_RANK + MLA_ROPE
    w_mla = hy_w_in[:, :mla_cols]
    w_rw = hy_w_in[:, mla_cols:]
    w1 = jnp.concatenate([w_mla, jnp.zeros((d, 512 - mla_cols), F32), _pad_rwkv_cols(w_rw)], axis=1).astype(BF16)
    wq = (w_uq * (MLA_SCALE * LOG2E)).reshape(MLA_Q_RANK, MLA_HEADS, MLA_NOPE + MLA_ROPE)
    z_n = jnp.zeros((MLA_Q_RANK, MLA_HEADS, MLA_NOPE), F32)
    z_p = jnp.zeros((MLA_Q_RANK, MLA_HEADS, HEAD_PAD - MLA_NOPE - MLA_ROPE), F32)
    wq_plain = jnp.concatenate([wq[..., :MLA_NOPE], wq[..., MLA_NOPE:], z_p], axis=-1)
    wq_rot = jnp.concatenate([z_n, _rotate_half_cols(wq[..., MLA_NOPE:]), z_p], axis=-1)
    wq2 = jnp.concatenate([wq_plain.reshape(MLA_Q_RANK, -1), wq_rot.reshape(MLA_Q_RANK, -1)], axis=1).astype(BF16)
    wkv = w_ukv.reshape(MLA_KV_RANK, MLA_HEADS, MLA_NOPE + MLA_V)
    eye = jnp.broadcast_to(jnp.eye(MLA_ROPE, dtype=F32)[:, None, :], (MLA_ROPE, MLA_HEADS, MLA_ROPE))
    pad_k = HEAD_PAD - MLA_NOPE - MLA_ROPE
    top_plain = jnp.concatenate([wkv[..., :MLA_NOPE], jnp.zeros((MLA_KV_RANK, MLA_HEADS, MLA_ROPE + pad_k), F32)], -1)
    mid_plain = jnp.concatenate([jnp.zeros((MLA_ROPE, MLA_HEADS, MLA_NOPE), F32), eye,
                                 jnp.zeros((MLA_ROPE, MLA_HEADS, pad_k), F32)], -1)
    mid_rot = jnp.concatenate([jnp.zeros((MLA_ROPE, MLA_HEADS, MLA_NOPE), F32), _rotate_half_cols(eye),
                               jnp.zeros((MLA_ROPE, MLA_HEADS, pad_k), F32)], -1)
    nq = MLA_HEADS * HEAD_PAD
    rows_c = jnp.concatenate([top_plain.reshape(MLA_KV_RANK, nq), jnp.zeros((MLA_KV_RANK, nq), F32),
                              wkv[..., MLA_NOPE:].reshape(MLA_KV_RANK, MLA_HEADS * MLA_V)], axis=1)
    rows_r = jnp.concatenate([mid_plain.reshape(MLA_ROPE, nq), mid_rot.reshape(MLA_ROPE, nq),
                              jnp.zeros((MLA_ROPE, MLA_HEADS * MLA_V), F32)], axis=1)
    rows_z = jnp.zeros((256 - MLA_KV_RANK - MLA_ROPE, rows_c.shape[1]), F32)
    wk2 = jnp.concatenate([rows_c, rows_r, rows_z], axis=0).astype(BF16)
    return w1, wq2, wk2


def _pad_rwkv_cols(w):
    pad = jnp.zeros(w.shape[:-1] + (RWKV_PAD_COLS - w.shape[-1],), w.dtype)
    return jnp.concatenate([w, pad], axis=-1)


def _rope_tables(n_lat, n_ctx):
    t = jnp.arange(n_lat)
    row = (t // GRID_W).astype(F32)
    col = (t % GRID_W).astype(F32)
    inv = ROPE_BASE ** (-jnp.arange(ROPE_FREQS, dtype=F32) / ROPE_FREQS)
    ang = jnp.stack([row[:, None] * inv, col[:, None] * inv], axis=1)
    ang = jnp.broadcast_to(ang[:, :, None, :], (n_lat, 2, 2, ROPE_FREQS)).reshape(n_lat, MLA_ROPE)
    cos = jnp.concatenate([jnp.ones((n_lat, MLA_NOPE), F32), jnp.cos(ang),
                           jnp.ones((n_lat, HEAD_PAD - MLA_NOPE - MLA_ROPE), F32)], axis=1)
    sin = jnp.concatenate([jnp.zeros((n_lat, MLA_NOPE), F32), jnp.sin(ang),
                           jnp.zeros((n_lat, HEAD_PAD - MLA_NOPE - MLA_ROPE), F32)], axis=1)
    cos = jnp.concatenate([cos, jnp.ones((n_ctx, HEAD_PAD), F32)], axis=0)
    sin = jnp.concatenate([sin, jnp.zeros((n_ctx, HEAD_PAD), F32)], axis=0)
    return cos, sin


def _inproj(x, ctx, gain, shift, scale, cos, sin, w1, q_norm, kv_norm, wq2, wk2):
    b, n_lat, d = x.shape
    t_all = n_lat + ctx.shape[1]
    tm = ROW_TILE
    n_lat_tiles = n_lat // tm
    nq = MLA_HEADS * HEAD_PAD

    def mod_map(bi, i):
        return (2 * bi + jnp.where(i >= n_lat_tiles, 1, 0), 0, 0)

    const = lambda bi, i: (0, 0)
    row = lambda bi, i: (bi, i, 0)
    return pl.pallas_call(
        functools.partial(_inproj_kernel, n_lat_tiles=n_lat_tiles),
        grid=(b, t_all // tm),
        in_specs=[pl.BlockSpec((None, tm, d), lambda bi, i: (bi, jnp.minimum(i, n_lat_tiles - 1), 0)),
                  pl.BlockSpec((None, tm, d), lambda bi, i: (bi, jnp.maximum(i - n_lat_tiles, 0), 0)),
                  pl.BlockSpec((1, d), const),
                  pl.BlockSpec((None, 1, d), mod_map),
                  pl.BlockSpec((None, 1, d), mod_map),
                  pl.BlockSpec((tm, HEAD_PAD), lambda bi, i: (i, 0)),
                  pl.BlockSpec((tm, HEAD_PAD), lambda bi, i: (i, 0)),
                  pl.BlockSpec(w1.shape, const),
                  pl.BlockSpec((1, MLA_Q_RANK), const),
                  pl.BlockSpec((1, MLA_KV_RANK), const),
                  pl.BlockSpec(wq2.shape, const),
                  pl.BlockSpec(wk2.shape, const)],
        out_specs=[pl.BlockSpec((None, tm, nq), row),
                   pl.BlockSpec((None, tm, nq), row),
                   pl.BlockSpec((None, MLA_HEADS * V_EXT, tm), lambda bi, i: (bi, 0, i)),
                   pl.BlockSpec((None, tm, RWKV_PAD_COLS), row)],
        out_shape=[jax.ShapeDtypeStruct((b, t_all, nq), BF16),
                   jax.ShapeDtypeStruct((b, t_all, nq), BF16),
                   jax.ShapeDtypeStruct((b, MLA_HEADS * V_EXT, t_all), BF16),
                   jax.ShapeDtypeStruct((b, t_all, RWKV_PAD_COLS), F32)],
        compiler_params=_cparams(("parallel", "parallel")),
        name="hybrid_inproj",
    )(x, ctx, gain.reshape(1, d), shift, scale, cos, sin, w1, q_norm.reshape(1, -1), kv_norm.reshape(1, -1), wq2, wk2)


def _attn_kernel(q_ref, k_ref, vt_ref, o_ref, m_sc, acc_sc):
    ki = pl.program_id(3)

    @pl.when(ki == 0)
    def _():
        m_sc[...] = jnp.full(m_sc.shape, -jnp.inf, F32)
        acc_sc[...] = jnp.zeros(acc_sc.shape, F32)

    tq, tk = q_ref.shape[0], k_ref.shape[0]
    chains = [(h, qs) for h in range(2) for qs in range(tq // ATT_QSUB)]
    hsl = [slice(h * HEAD_PAD, (h + 1) * HEAD_PAD) for h, _ in chains]
    vsl = [slice(h * V_EXT, (h + 1) * V_EXT) for h, _ in chains]
    qsl = [slice(qs * ATT_QSUB, (qs + 1) * ATT_QSUB) for _, qs in chains]
    nch = len(chains)
    q = [q_ref[qsl[c], hsl[c]] for c in range(nch)]
    m = [m_sc[chains[c][0], 0:1, qsl[c]] for c in range(nch)]
    acc = [acc_sc[chains[c][0], :, qsl[c]] for c in range(nch)]
    ksub = ATT_KSUB if tk % ATT_KSUB == 0 else tk
    for kb in range(tk // ksub):
        ksl = slice(kb * ksub, (kb + 1) * ksub)
        s = [_dot_nt(k_ref[ksl, hsl[c]], q[c]).astype(ATT_EXP_DTYPE) for c in range(nch)]
        m_new = [jnp.maximum(m[c], jnp.max(s[c], axis=0, keepdims=True).astype(F32)) for c in range(nch)]
        alpha = [jnp.exp2(m[c] - m_new[c]) for c in range(nch)]
        p = [jnp.exp2(s[c] - m_new[c].astype(ATT_EXP_DTYPE)).astype(BF16) for c in range(nch)]
        acc = [acc[c] * alpha[c] + _dot(vt_ref[vsl[c], ksl], p[c]) for c in range(nch)]
        m = m_new
    for c in range(nch):
        h = chains[c][0]
        m_sc[h, :, qsl[c]] = jnp.broadcast_to(m[c], (8, ATT_QSUB))
        acc_sc[h, :, qsl[c]] = acc[c]

    @pl.when(ki == pl.num_programs(3) - 1)
    def _():
        o0 = (acc_sc[0, :MLA_V, :] / acc_sc[0, MLA_V:MLA_V + 1, :]).T
        o1 = (acc_sc[1, :MLA_V, :] / acc_sc[1, MLA_V:MLA_V + 1, :]).T
        o_ref[...] = jnp.concatenate([o0, o1], axis=1).astype(BF16)


def _attention(q, k, vt, n_lat, tq, tk):
    b, t_all, _ = k.shape
    return pl.pallas_call(
        _attn_kernel,
        grid=(b, MLA_HEADS // 2, n_lat // tq, t_all // tk),
        in_specs=[pl.BlockSpec((None, tq, 2 * HEAD_PAD), lambda bi, hp, qi, ki: (bi, qi, hp)),
                  pl.BlockSpec((None, tk, 2 * HEAD_PAD), lambda bi, hp, qi, ki: (bi, ki, hp)),
                  pl.BlockSpec((None, 2 * V_EXT, tk), lambda bi, hp, qi, ki: (bi, hp, ki))],
        out_specs=pl.BlockSpec((None, tq, 2 * MLA_V), lambda bi, hp, qi, ki: (bi, qi, hp)),
        out_shape=jax.ShapeDtypeStruct((b, n_lat, MLA_HEADS * MLA_V), BF16),
        scratch_shapes=[pltpu.VMEM((2, 8, tq), F32), pltpu.VMEM((2, V_EXT, tq), F32)],
        compiler_params=_cparams(("parallel", "parallel", "parallel", "arbitrary")),
        name="mla_attention",
    )(q, k, vt)


def _seg_sum(x, bd_ref):
    hi = x.astype(BF16)
    lo = (x - hi.astype(F32)).astype(BF16)
    bd = bd_ref[...]
    return _dot(hi, bd) + _dot(lo, bd)


def _rwkv_prep_kernel(p_ref, prev_ref, next_ref, mup_ref, mun_ref, wl_ref, w0a0_ref, g2_ref, kk_ref, ka_ref, bd_ref,
                      r_ref, k_ref, v_ref, g_ref, kkn_ref, lw_ref, kd_ref, bdir_ref, *, n_lat_tiles, n_tiles):
    i = pl.program_id(1)
    p = p_ref[...]
    tm = p.shape[0]
    first = jnp.logical_or(i == 0, i == n_lat_tiles)
    last = jnp.logical_or(i == n_lat_tiles - 1, i == n_tiles - 1)
    prev_row = jnp.where(first, 0.0, prev_ref[7:8, :])
    next_row = jnp.where(last, 0.0, next_ref[0:1, :])
    ridx = lax.broadcasted_iota(jnp.int32, p.shape, 0)
    prev = jnp.where(ridx == 0, prev_row, pltpu.roll(p, 1, 0))
    nxt = jnp.where(ridx == tm - 1, next_row, pltpu.roll(p, tm - 1, 0))
    p = p + mup_ref[...] * (prev - p) + mun_ref[...] * (nxt - p)
    c = RWKV_DIM
    r, k, v = p[:, :c], p[:, c:2 * c], p[:, 2 * c:3 * c]
    lo = p[:, 3 * c:3 * c + 128]
    lane = lax.broadcasted_iota(jnp.int32, lo.shape, 1)
    lo = jnp.where(lane < 2 * DECAY_LORA, jnp.tanh(lo), lo)
    wa = _dot(lo.astype(BF16), wl_ref[...]) + w0a0_ref[...]
    gd = p[:, 3 * c + 128:]
    g_ref[...] = _dot(jax.nn.sigmoid(gd).astype(BF16), g2_ref[...]).astype(BF16)
    kk = k * kk_ref[...]
    kk = kk * lax.rsqrt(jnp.maximum(_seg_sum(kk * kk, bd_ref), 1e-24))
    r_ref[...] = r.astype(BF16)
    k_ref[...] = k.astype(BF16)
    v_ref[...] = v.astype(BF16)
    kkn_ref[...] = kk.astype(BF16)
    ka = ka_ref[...]
    for d in range(2):
        w = wa[:, d * c:(d + 1) * c]
        a = jax.nn.sigmoid(wa[:, (2 + d) * c:(3 + d) * c])
        lw_ref[:, d * c:(d + 1) * c] = -float(np.exp(-0.5)) * jax.nn.sigmoid(w)
        kd_ref[:, d * c:(d + 1) * c] = (k * (1.0 + (a - 1.0) * ka)).astype(BF16)
        bdir_ref[:, d * c:(d + 1) * c] = (kk * a).astype(BF16)


def _rwkv_prepare(prw, n_lat, mu_prev, mu_next, w0, w2, a0, a2, g2, k_k, k_a, bd_ones):
    b, t_all, pc = prw.shape
    tm = ROW_TILE
    n_tiles = t_all // tm
    n_lat_tiles = n_lat // tm
    c = RWKV_DIM
    z = jnp.zeros((DECAY_LORA, c), F32)
    wl = jnp.concatenate([
        jnp.concatenate([w2[0], z, z, z], axis=1), jnp.concatenate([z, w2[1], z, z], axis=1),
        jnp.concatenate([z, z, a2[0], z], axis=1), jnp.concatenate([z, z, z, a2[1]], axis=1)], axis=0).astype(BF16)
    w0a0 = jnp.concatenate([w0[0], w0[1], a0[0], a0[1]]).reshape(1, 4 * c)
    g2p = jnp.concatenate([g2, jnp.zeros((128 - GATE_LORA, c), F32)], axis=0).astype(BF16)
    row = lambda bi, i: (bi, i, 0)
    const = lambda bi, i: (0, 0)
    hb = tm // 8
    n_hb = t_all // 8
    kern = functools.partial(_rwkv_prep_kernel, n_lat_tiles=n_lat_tiles, n_tiles=n_tiles)
    o_c = jax.ShapeDtypeStruct((b, t_all, c), BF16)
    o_2c = jax.ShapeDtypeStruct((b, t_all, 2 * c), BF16)
    o_lw = jax.ShapeDtypeStruct((b, t_all, 2 * c), F32)
    return pl.pallas_call(
        kern,
        grid=(b, n_tiles),
        in_specs=[pl.BlockSpec((None, tm, pc), row),
                  pl.BlockSpec((None, 8, pc), lambda bi, i: (bi, jnp.maximum(i * hb - 1, 0), 0)),
                  pl.BlockSpec((None, 8, pc), lambda bi, i: (bi, jnp.minimum((i + 1) * hb, n_hb - 1), 0)),
                  pl.BlockSpec((1, pc), const), pl.BlockSpec((1, pc), const),
                  pl.BlockSpec(wl.shape, const), pl.BlockSpec((1, 4 * c), const),
                  pl.BlockSpec(g2p.shape, const), pl.BlockSpec((1, c), const), pl.BlockSpec((1, c), const),
                  pl.BlockSpec((c, c), const)],
        out_specs=[pl.BlockSpec((None, tm, c), row)] * 5 + [pl.BlockSpec((None, tm, 2 * c), row)] * 3,
        out_shape=[o_c] * 5 + [o_lw, o_2c, o_2c],
        compiler_params=_cparams(("parallel", "parallel")),
        name="rwkv_prepare",
    )(prw, prw, prw, _pad_rwkv_cols(mu_prev).reshape(1, pc), _pad_rwkv_cols(mu_next).reshape(1, pc),
      wl, w0a0, g2p, k_k.reshape(1, c), k_a.reshape(1, c), bd_ones)


def _wkv_chunk_kernel(r_ref, v_ref, kk_ref, lw_ref, kd_ref, bd_ref, m_ref, n_ref, rr_ref, yv_ref):
    d = pl.program_id(1)
    cs = WKV_CHUNK
    nc = r_ref.shape[0] // cs
    ti = lax.broadcasted_iota(jnp.int32, (cs, cs), 0)
    si = lax.broadcasted_iota(jnp.int32, (cs, cs), 1)
    rel = (si - ti) * (1 - 2 * d)
    incl = rel <= 0
    strict = rel < 0
    eye = si == ti
    incl_f = incl.astype(F32)
    ops = []
    for ci in range(nc):
        rows = slice(ci * cs, (ci + 1) * cs)
        lw = lw_ref[rows, :]
        g = _dot(incl_f, lw, HIGHEST)
        total = jnp.sum(lw, axis=0, keepdims=True)
        e_inv = jnp.exp(-g)
        e_end = jnp.exp(total - g)
        gam = jnp.exp(total)
        kd = kd_ref[rows, :]
        bd = bd_ref[rows, :]
        at = -kk_ref[rows, :] * jnp.exp(g - lw)
        rt = r_ref[rows, :] * jnp.exp(g)
        kt = (kd * e_inv).astype(BF16)
        bt = (bd * e_inv).astype(BF16)
        ke = (kd * e_end).astype(BF16)
        be = (bd * e_end).astype(BF16)
        v32 = v_ref[rows, :].astype(F32)
        v = v32.astype(BF16)
        for h in range(RWKV_HEADS):
            sl = slice(h * RWKV_HEAD, (h + 1) * RWKV_HEAD)
            ops.append((at[:, sl], rt[:, sl], kt[:, sl], bt[:, sl], ke[:, sl], be[:, sl], v[:, sl], gam[:, sl], v32[:, sl]))
    n_it = len(ops)
    aa = [_dot_nt(jnp.concatenate([o[0], o[1]], axis=0).astype(BF16), jnp.concatenate([o[3], o[2]], axis=0))
          for o in ops]
    a_ab = [jnp.where(strict, x[:cs, :cs], 0.0) for x in aa]
    a_ak = [jnp.where(strict, x[:cs, cs:], 0.0).astype(BF16) for x in aa]
    ti2 = lax.broadcasted_iota(jnp.int32, (cs, 2 * cs), 0)
    si2 = lax.broadcasted_iota(jnp.int32, (cs, 2 * cs), 1)
    incl2 = (jnp.where(si2 >= cs, si2 - cs, si2) - ti2) * (1 - 2 * d) <= 0
    a_rbk = [jnp.where(incl2, x[cs:, :], 0.0).astype(BF16) for x in aa]
    z = [jnp.concatenate([ops[i][0], _dot(a_ak[i], ops[i][6])], axis=1) for i in range(n_it)]
    pw = a_ab
    n_sq = int(np.log2(cs))
    for it in range(n_sq):
        more = it + 1 < n_sq
        res = [_dot_split(pw[i], jnp.concatenate([z[i], pw[i]], axis=1) if more else z[i]) for i in range(n_it)]
        z = [z[i] + res[i][:, :2 * cs] for i in range(n_it)]
        if more:
            pw = [x[:, 2 * cs:] for x in res]
    zero = jnp.zeros((cs, cs), F32)
    wmat = [jnp.concatenate([z[i], jnp.concatenate([zero, ops[i][8]], axis=1)], axis=0).astype(BF16)
            for i in range(n_it)]
    w2 = [_dot(a_rbk[i], wmat[i]) for i in range(n_it)]
    w3 = [_dot_tn(jnp.concatenate([ops[i][5], ops[i][4]], axis=0), wmat[i]) for i in range(n_it)]
    nh = RWKV_HEADS
    for ci in range(nc):
        rows = slice(ci * cs, (ci + 1) * cs)
        ids = range(ci * nh, (ci + 1) * nh)
        m_ref[rows, :] = jnp.concatenate([jnp.where(eye, ops[i][7], 0.0) + w3[i][:, :cs] for i in ids], axis=1)
        n_ref[rows, :] = jnp.concatenate([w3[i][:, cs:] for i in ids], axis=1)
        rr_ref[rows, :] = jnp.concatenate([ops[i][1] + w2[i][:, :cs] for i in ids], axis=1)
        yv_ref[rows, :] = jnp.concatenate([w2[i][:, cs:] for i in ids], axis=1)


def _wkv_chunks(r, v, kkn, lw, kd, bdir):
    b, t_all, c = r.shape
    rows = WKV_CHUNK * WKV_CHUNKS_PER_STEP
    nst = t_all // rows
    shared = pl.BlockSpec((None, rows, c), lambda bi, d, ci: (bi, ci, 0))
    per_dir = pl.BlockSpec((None, rows, c), lambda bi, d, ci: (bi, ci, d))
    out = pl.BlockSpec((None, None, rows, c), lambda bi, d, ci: (bi, d, ci, 0))
    o_s = jax.ShapeDtypeStruct((b, 2, t_all, c), F32)
    return pl.pallas_call(
        _wkv_chunk_kernel,
        grid=(b, 2, nst),
        in_specs=[shared, shared, shared, per_dir, per_dir, per_dir],
        out_specs=[out] * 4,
        out_shape=[o_s] * 4,
        compiler_params=_cparams(("parallel", "parallel", "parallel")),
        name="wkv_chunk_summaries",
    )(r, v, kkn, lw, kd, bdir)


def _wkv_state_kernel(order_ref, mf_ref, nf_ref, rf_ref, yf_ref, mb_ref, nb_ref, rb_ref, yb_ref,
                      of_ref, ob_ref, st_sc):
    del order_ref
    s = pl.program_id(0)

    @pl.when(s == 0)
    def _():
        st_sc[...] = jnp.zeros(st_sc.shape, F32)

    n_b = st_sc.shape[0]
    ins = ((mf_ref, nf_ref, rf_ref, yf_ref, of_ref), (mb_ref, nb_ref, rb_ref, yb_ref, ob_ref))
    for bi in range(n_b):
        for d in range(2):
            m_r, n_r, r_r, y_r, o_r = ins[d]
            st = st_sc[bi, d]
            m, n, rr, yv = m_r[bi], n_r[bi], r_r[bi], y_r[bi]
            st_hi, st_lo = _split(st)
            m_hi, m_lo = _split(m)
            rr_b = rr.astype(BF16)
            zpad = jnp.zeros((rr.shape[0], 2 * RWKV_HEAD), BF16)
            ys, sts = [], []
            for h in range(RWKV_HEADS):
                sl = slice(h * RWKV_HEAD, (h + 1) * RWKV_HEAD)
                lhs = jnp.concatenate([jnp.concatenate([m_hi[:, sl], m_hi[:, sl], m_lo[:, sl]], axis=1),
                                       jnp.concatenate([rr_b[:, sl], zpad], axis=1)], axis=0)
                rhs = jnp.concatenate([st_hi[:, sl], st_lo[:, sl], st_hi[:, sl]], axis=0)
                res = _dot(lhs, rhs)
                cs = rr.shape[0]
                sts.append(res[:cs] + n[:, sl])
                ys.append(res[cs:] + yv[:, sl])
            o_r[bi] = jnp.concatenate(ys, axis=1)
            st_sc[bi, d] = jnp.concatenate(sts, axis=1)


def _wkv_states(order, m, n, rr, yv, t_all):
    b, _, _, c = m.shape
    cs = WKV_CHUNK
    nch = t_all // cs
    fwd = pl.BlockSpec((b, None, cs, c), lambda s, o: (0, 0, o[0, s], 0))
    bwd = pl.BlockSpec((b, None, cs, c), lambda s, o: (0, 1, o[1, s], 0))
    out_f = pl.BlockSpec((b, cs, c), lambda s, o: (0, o[0, s], 0))
    out_b = pl.BlockSpec((b, cs, c), lambda s, o: (0, o[1, s], 0))
    o_s = jax.ShapeDtypeStruct((b, t_all, c), F32)
    return pl.pallas_call(
        _wkv_state_kernel,
        grid_spec=pltpu.PrefetchScalarGridSpec(
            num_scalar_prefetch=1,
            grid=(nch,),
            in_specs=[fwd] * 4 + [bwd] * 4,
            out_specs=[out_f, out_b],
            scratch_shapes=[pltpu.VMEM((b, 2, cs, c), F32)]),
        out_shape=[o_s, o_s],
        compiler_params=_cparams(("arbitrary",)),
        name="wkv_state_pass",
    )(order, m, n, rr, yv, m, n, rr, yv)


def _ffn_pre(x1, gain_ref, sh_ref, sc_ref, rw_ref, rb_ref, hf_ref, idx_ref, gate_ref):
    hf = _rms(x1) * gain_ref[...]
    hf = hf * (1.0 + sc_ref[...]) + sh_ref[...]
    hf_ref[...] = hf.astype(BF16)
    hf_hi, hf_lo = _split(hf)
    t = _dot(hf_hi, rw_ref[...])
    u = _dot(hf_lo, rw_ref[:, :LOGIT_PAD])
    logits = t[:, :LOGIT_PAD] + t[:, LOGIT_PAD:] + u + rb_ref[...]
    lane = lax.broadcasted_iota(jnp.int32, logits.shape, 1)
    lane_f = lane.astype(F32)
    work = logits
    idx_out = jnp.zeros(logits.shape, F32)
    val_out = jnp.full(logits.shape, NEG_BIG, F32)
    for kth in range(TOP_K):
        m = jnp.max(work, axis=1, keepdims=True)
        idx = jnp.min(jnp.where(work == m, lane_f, float(LOGIT_PAD)), axis=1, keepdims=True)
        work = jnp.where(lane_f == idx, -jnp.inf, work)
        idx_out = jnp.where(lane == kth, idx, idx_out)
        val_out = jnp.where(lane == kth, m, val_out)
    e = jnp.exp(val_out - jnp.max(val_out, axis=1, keepdims=True))
    e = jnp.where(lane < TOP_K, e, 0.0)
    idx_ref[...] = idx_out.astype(jnp.int32)
    gate_ref[...] = e / jnp.sum(e, axis=1, keepdims=True)


def _router_pads(router_w, router_b):
    d = router_w.shape[0]
    rw = jnp.concatenate([router_w, jnp.zeros((d, LOGIT_PAD - N_EXPERTS), F32)], axis=1)
    rb = jnp.concatenate([router_b, jnp.full((LOGIT_PAD - N_EXPERTS,), NEG_BIG, F32)]).reshape(1, LOGIT_PAD)
    return jnp.concatenate(_split(rw), axis=1), rb


def _mix_out_kernel(yf_ref, yb_ref, r_ref, k_ref, v_ref, g_ref, o_ref, x_ref,
                    rk_ref, lnw_ref, lnb_ref, bdm_ref, bds_ref, wout_ref, npost_ref, g1_ref,
                    gpre_ref, sh_ref, sc_ref, rw_ref, rb_ref,
                    x1_ref, hf_ref, idx_ref, gate_ref):
    y = yf_ref[...] + yb_ref[...]
    mu = _seg_sum(y, bdm_ref)
    dlt = y - mu
    var = _seg_sum(dlt * dlt, bdm_ref)
    yn = dlt * lax.rsqrt(var + RWKV_LN_EPS) * lnw_ref[...] + lnb_ref[...]
    bonus = _seg_sum(r_ref[...].astype(F32) * k_ref[...] * rk_ref[...], bds_ref) * v_ref[...]
    rw = (yn + bonus) * g_ref[...]
    mix_in = jnp.concatenate([o_ref[...], rw.astype(BF16)], axis=1)
    mix = _dot(mix_in, wout_ref[...])
    x1 = x_ref[...] + g1_ref[...] * (_rms(mix) * npost_ref[...])
    x1_ref[...] = x1
    _ffn_pre(x1, gpre_ref, sh_ref, sc_ref, rw_ref, rb_ref, hf_ref, idx_ref, gate_ref)


def _token_out_specs(tm, d):
    row = lambda bi, i: (bi, i, 0)
    specs = [pl.BlockSpec((None, tm, d), row), pl.BlockSpec((None, tm, d), row),
             pl.BlockSpec((None, tm, LOGIT_PAD), row), pl.BlockSpec((None, tm, LOGIT_PAD), row)]
    return specs


def _token_out_shapes(b, s, d):
    return [jax.ShapeDtypeStruct((b, s, d), F32), jax.ShapeDtypeStruct((b, s, d), BF16),
            jax.ShapeDtypeStruct((b, s, LOGIT_PAD), jnp.int32), jax.ShapeDtypeStruct((b, s, LOGIT_PAD), F32)]


def _mix_out(yf, yb, r, k, v, g, o_attn, x, r_k, ln_w, ln_b, bd_mean, bd_ones, w_out, n_post, g1,
             n_pre, sh2, sc2, rw_pad, rb_pad):
    b, s, d = x.shape
    c = RWKV_DIM
    tm = ROW_TILE
    row = lambda bi, i: (bi, i, 0)
    const = lambda bi, i: (0, 0)
    per_b = lambda bi, i: (bi, 0, 0)
    rc = pl.BlockSpec((None, tm, c), row)
    vec_c = pl.BlockSpec((1, c), const)
    vec_d = pl.BlockSpec((1, d), const)
    mod_d = pl.BlockSpec((None, 1, d), per_b)
    return pl.pallas_call(
        _mix_out_kernel,
        grid=(b, s // tm),
        in_specs=[rc] * 7 + [pl.BlockSpec((None, tm, d), row),
                             vec_c, vec_c, vec_c, pl.BlockSpec((c, c), const), pl.BlockSpec((c, c), const),
                             pl.BlockSpec((d, d), const), vec_d, mod_d,
                             vec_d, mod_d, mod_d, pl.BlockSpec((d, 2 * LOGIT_PAD), const),
                             pl.BlockSpec((1, LOGIT_PAD), const)],
        out_specs=_token_out_specs(tm, d),
        out_shape=_token_out_shapes(b, s, d),
        compiler_params=_cparams(("parallel", "parallel")),
        name="mixer_out_router",
    )(yf, yb, r, k, v, g, o_attn, x, r_k.reshape(1, c), ln_w.reshape(1, c), ln_b.reshape(1, c), bd_mean, bd_ones,
      w_out, n_post.reshape(1, d), g1, n_pre.reshape(1, d), sh2, sc2, rw_pad, rb_pad)


def _moe_residual(x_ref, y_ref, gate_ref, npost_ref, g2_ref):
    gate = gate_ref[...]
    f = jnp.zeros(x_ref.shape, F32)
    for kth in range(TOP_K):
        f = f + gate[:, kth:kth + 1] * y_ref[kth].astype(F32)
    return x_ref[...] + g2_ref[...] * (_rms(f) * npost_ref[...])


def _gmlp_kernel(xin_ref, y_ref, gatein_ref, npostin_ref, g2in_ref,
                 gpre1_ref, sh1_ref, sc1_ref, win_ref, vnw_ref, vnb_ref, ws_ref, bs_ref, wout_ref,
                 npost_ref, g1_ref, gpre_ref, sh_ref, sc_ref, rw_ref, rb_ref,
                 x1_ref, hf_ref, idx_ref, gate_ref):
    x = _moe_residual(xin_ref, y_ref, gatein_ref, npostin_ref, g2in_ref)
    h = _rms(x) * gpre1_ref[...]
    h = h * (1.0 + sc1_ref[...]) + sh1_ref[...]
    z = _dot(h.astype(BF16), win_ref[...])
    z = 0.5 * z * (1.0 + lax.erf(z * float(2.0 ** -0.5)))
    u, v = z[:, :GM_WIDTH], z[:, GM_WIDTH:]
    mu = jnp.mean(v, axis=-1, keepdims=True)
    dv = v - mu
    var = jnp.mean(dv * dv, axis=-1, keepdims=True)
    v = (dv * lax.rsqrt(var + LN_EPS) * vnw_ref[...] + vnb_ref[...]).astype(BF16)
    gw = GM_WIDTH // GM_GROUPS
    rows = []
    for ci in range(x.shape[0] // CHUNK):
        cols = []
        for gi in range(GM_GROUPS):
            cols.append(_dot(ws_ref[gi], v[ci * CHUNK:(ci + 1) * CHUNK, gi * gw:(gi + 1) * gw]))
        rows.append(jnp.concatenate(cols, axis=1) + bs_ref[...])
    sp = jnp.concatenate(rows, axis=0)
    y = _dot((u * sp).astype(BF16), wout_ref[...])
    x1 = x + g1_ref[...] * (_rms(y) * npost_ref[...])
    x1_ref[...] = x1
    _ffn_pre(x1, gpre_ref, sh_ref, sc_ref, rw_ref, rb_ref, hf_ref, idx_ref, gate_ref)


def _gmlp(x, yk, gates_in, n_post_in, g2_in, n_pre1, sh1, sc1, w_in, vn_w, vn_b, w_s, b_s, w_out, n_post, g1, n_pre, sh2, sc2, rw_pad, rb_pad):
    b, s, d = x.shape
    tm = ROW_TILE
    gw = GM_WIDTH // GM_GROUPS
    bs_full = jnp.repeat(b_s.T, gw, axis=1)
    row = lambda bi, i: (bi, i, 0)
    const = lambda bi, i: (0, 0)
    per_b = lambda bi, i: (bi, 0, 0)
    vec_d = pl.BlockSpec((1, d), const)
    vec_g = pl.BlockSpec((1, GM_WIDTH), const)
    mod_d = pl.BlockSpec((None, 1, d), per_b)
    return pl.pallas_call(
        _gmlp_kernel,
        grid=(b, s // tm),
        in_specs=[pl.BlockSpec((None, tm, d), row),
                  pl.BlockSpec((TOP_K, None, tm, d), lambda bi, i: (0, bi, i, 0)),
                  pl.BlockSpec((None, tm, LOGIT_PAD), row), vec_d, mod_d,
                  vec_d, mod_d, mod_d,
                  pl.BlockSpec((d, 2 * GM_WIDTH), const), vec_g, vec_g,
                  pl.BlockSpec((GM_GROUPS, CHUNK, CHUNK), lambda bi, i: (0, 0, 0)),
                  pl.BlockSpec((CHUNK, GM_WIDTH), const), pl.BlockSpec((GM_WIDTH, d), const),
                  vec_d, mod_d, vec_d, mod_d, mod_d,
                  pl.BlockSpec((d, 2 * LOGIT_PAD), const), pl.BlockSpec((1, LOGIT_PAD), const)],
        out_specs=_token_out_specs(tm, d),
        out_shape=_token_out_shapes(b, s, d),
        compiler_params=_cparams(("parallel", "parallel")),
        name="gmlp_router",
    )(x, yk, gates_in, n_post_in.reshape(1, d), g2_in, n_pre1.reshape(1, d), sh1, sc1, w_in.astype(BF16), vn_w.reshape(1, -1), vn_b.reshape(1, -1),
      w_s.astype(BF16), bs_full, w_out.astype(BF16), n_post.reshape(1, d), g1, n_pre.reshape(1, d), sh2, sc2,
      rw_pad, rb_pad)


MOE_ROWS = 256


def _expert_kernel(be_ref, na_ref, x_ref, wgu_ref, bgu_ref, wd_ref, bd_ref, y_ref, wgu_bf, wd_bf):
    i = pl.program_id(0)
    prev = be_ref[jnp.maximum(i - 1, 0)]
    changed = jnp.logical_or(i == 0, be_ref[i] != prev)

    @pl.when(changed)
    def _():
        wgu_bf[...] = wgu_ref[...].astype(BF16)
        wd_bf[...] = wd_ref[...].astype(BF16)

    @pl.when(i < na_ref[0])
    def _():
        gu = _dot(x_ref[...], wgu_bf[...]) + bgu_ref[...]
        x_glu = jnp.minimum(gu[:, :EXPERT_FF], SWIGLU_LIMIT)
        x_lin = jnp.clip(gu[:, EXPERT_FF:], -SWIGLU_LIMIT, SWIGLU_LIMIT)
        act = x_glu * jax.nn.sigmoid(SWIGLU_ALPHA * x_glu) * (x_lin + 1.0)
        y_ref[...] = (_dot(act.astype(BF16), wd_bf[...]) + bd_ref[...]).astype(BF16)

    @pl.when(i >= na_ref[0])
    def _():
        y_ref[...] = jnp.zeros(y_ref.shape, BF16)


def _experts(blk_expert, n_active, xg, layer, w_gu, b_gu, w_down, b_down):
    n_rows, d = xg.shape
    n_l, n_e, _, ff2 = w_gu.shape
    tb = MOE_ROWS
    n_blocks = n_rows // tb
    return pl.pallas_call(
        _expert_kernel,
        grid_spec=pltpu.PrefetchScalarGridSpec(
            num_scalar_prefetch=2,
            grid=(n_blocks,),
            in_specs=[pl.BlockSpec((tb, d), lambda i, be, na: (i, 0)),
                      pl.BlockSpec((None, None, d, ff2), lambda i, be, na: (layer, be[i], 0, 0)),
                      pl.BlockSpec((None, None, 1, ff2), lambda i, be, na: (layer, be[i], 0, 0)),
                      pl.BlockSpec((None, None, ff2 // 2, d), lambda i, be, na: (layer, be[i], 0, 0)),
                      pl.BlockSpec((None, None, 1, d), lambda i, be, na: (layer, be[i], 0, 0))],
            out_specs=pl.BlockSpec((tb, d), lambda i, be, na: (i, 0)),
            scratch_shapes=[pltpu.VMEM((d, ff2), BF16), pltpu.VMEM((ff2 // 2, d), BF16)]),
        out_shape=jax.ShapeDtypeStruct((n_rows, d), BF16),
        compiler_params=_cparams(("arbitrary",)),
        name="moe_experts",
    )(blk_expert, n_active, xg, w_gu, b_gu.reshape(n_l, n_e, 1, ff2), w_down, b_down.reshape(n_l, n_e, 1, d))


def _combine_kernel(x_ref, y_ref, gate_ref, npost_ref, g2_ref, o_ref):
    o_ref[...] = _moe_residual(x_ref, y_ref, gate_ref, npost_ref, g2_ref)


def _combine(x1, yk, gates, n_post, g2):
    b, s, d = x1.shape
    tm = ROW_TILE
    row = lambda bi, i: (bi, i, 0)
    return pl.pallas_call(
        _combine_kernel,
        grid=(b, s // tm),
        in_specs=[pl.BlockSpec((None, tm, d), row),
                  pl.BlockSpec((TOP_K, None, tm, d), lambda bi, i: (0, bi, i, 0)),
                  pl.BlockSpec((None, tm, LOGIT_PAD), row),
                  pl.BlockSpec((1, d), lambda bi, i: (0, 0)),
                  pl.BlockSpec((None, 1, d), lambda bi, i: (bi, 0, 0))],
        out_specs=pl.BlockSpec((None, tm, d), row),
        out_shape=jax.ShapeDtypeStruct((b, s, d), F32),
        compiler_params=_cparams(("parallel", "parallel")),
        name="moe_combine_residual",
    )(x1, yk, gates, n_post.reshape(1, d), g2)


def _lookup(table, idx):
    n = table.shape[0]
    return jnp.sum(jnp.where(idx[:, None] == jnp.arange(n, dtype=jnp.int32)[None, :], table[None, :], 0), axis=1)


def _rows(a, idx):
    return a.at[idx].get(mode="promise_in_bounds")


def _moe_rows(hf, top_idx, layer, w_gu, b_gu, w_down, b_down):
    b, s, d = hf.shape
    n_tok = b * s
    tb = MOE_ROWS
    n_assign = n_tok * TOP_K
    assert N_EXPERTS * n_assign < 2 ** 31
    i32 = jnp.int32
    e_flat = top_idx[..., :TOP_K].reshape(-1)
    skey = jnp.sort(e_flat * n_assign + jnp.arange(n_assign, dtype=i32))
    order = skey % n_assign
    _, inv = lax.sort((order, jnp.arange(n_assign, dtype=i32)), num_keys=1)
    edges = jnp.arange(N_EXPERTS + 1, dtype=i32) * n_assign
    bounds = jnp.sum((skey[None, :] < edges[:, None]).astype(i32), axis=1)
    start = bounds[:-1]
    counts = bounds[1:] - start
    padded = (counts + tb - 1) // tb * tb
    pend = jnp.cumsum(padded)
    pstart = pend - padded
    n_rows = -(-n_assign // tb) * tb + N_EXPERTS * tb
    n_blocks = n_rows // tb
    blk_start = jnp.arange(n_blocks, dtype=i32) * tb
    blk_expert = jnp.minimum(jnp.sum((pend[None, :] <= blk_start[:, None]).astype(i32), axis=1), N_EXPERTS - 1)
    n_active = (pend[-1] // tb).astype(i32).reshape(1)
    j = jnp.arange(n_rows, dtype=i32) - jnp.repeat(_lookup(pstart, blk_expert), tb)
    src = jnp.repeat(_lookup(start, blk_expert), tb) + j
    valid = j < jnp.repeat(_lookup(counts, blk_expert), tb)
    row_tok = jnp.where(valid, _rows(order, jnp.clip(src, 0, n_assign - 1)) // TOP_K, 0)
    pos = _lookup(pstart - start, e_flat) + inv
    xg = _rows(hf.reshape(n_tok, d), row_tok)
    y = _experts(blk_expert, n_active, xg, layer, w_gu, b_gu, w_down, b_down)
    return _rows(y, pos.reshape(n_tok, TOP_K).T.reshape(-1)).reshape(TOP_K, b, s, d)


def _block_diag(n, blk, val):
    return (jnp.kron(jnp.eye(n // blk, dtype=F32), jnp.ones((blk, blk), F32)) * val).astype(BF16)


def kernel(x, c, ctx, c_ctx, ada_w, ada_b, norm_mix_pre, norm_mix_post, norm_ffn_pre, norm_ffn_post, router_w, router_b, moe_w_gu, moe_b_gu, moe_w_down, moe_b_down, hy_w_in, mla_q_norm, mla_w_uq, mla_kv_norm, mla_w_ukv, rwkv_mu_prev, rwkv_mu_next, rwkv_w0, rwkv_w2, rwkv_a0, rwkv_a2, rwkv_g2, rwkv_k_k, rwkv_k_a, rwkv_r_k, rwkv_ln_w, rwkv_ln_b, hy_w_out, gm_w_in, gm_v_norm_w, gm_v_norm_b, gm_w_s, gm_b_s, gm_w_out):
    b, s, d = x.shape
    n_ctx = ctx.shape[1]
    t_all = s + n_ctx
    assert b + 1 <= 8 and s % ROW_TILE == 0 and n_ctx % ROW_TILE == 0

    cond_rows = jnp.concatenate([c, c_ctx[None], jnp.zeros((8 - b - 1, d), F32)], axis=0)
    mod = _modulation(cond_rows, ada_w, ada_b)

    def lat_mod(l, j):
        return mod[l, :b, j * d:(j + 1) * d].reshape(b, 1, d)

    sh_all = jnp.stack([mod[0, :b, 0:d], jnp.broadcast_to(mod[0, b, 0:d], (b, d))], axis=1).reshape(2 * b, 1, d)
    sc_all = jnp.stack([mod[0, :b, d:2 * d], jnp.broadcast_to(mod[0, b, d:2 * d], (b, d))], axis=1).reshape(2 * b, 1, d)
    cos, sin = _rope_tables(s, n_ctx)
    w1, wq2, wk2 = _inproj_weights(hy_w_in[0], mla_w_uq[0], mla_w_ukv[0])
    q, k, v, prw = _inproj(x, ctx, norm_mix_pre[0], sh_all, sc_all, cos, sin, w1, mla_q_norm[0], mla_kv_norm[0],
                           wq2, wk2)
    tq = 512 if s % 512 == 0 else ROW_TILE
    tk = next((t for t in (8320, 1280) if t_all % t == 0), ROW_TILE)
    o_attn = _attention(q, k, v, s, tq, tk)

    bd_ones = _block_diag(RWKV_DIM, RWKV_HEAD, 1.0)
    bd_mean = _block_diag(RWKV_DIM, RWKV_HEAD, 1.0 / RWKV_HEAD)
    r, kx, vx, g, kkn, lw, kd, bdir = _rwkv_prepare(prw, s, rwkv_mu_prev[0], rwkv_mu_next[0], rwkv_w0[0], rwkv_w2[0],
                                                    rwkv_a0[0], rwkv_a2[0], rwkv_g2[0], rwkv_k_k[0], rwkv_k_a[0],
                                                    bd_ones)
    m_c, n_c, r_c, y_c = _wkv_chunks(r, vx, kkn, lw, kd, bdir)
    n_lat_ch = s // WKV_CHUNK
    n_ctx_ch = n_ctx // WKV_CHUNK
    lat_ch = np.arange(n_lat_ch)
    ctx_ch = n_lat_ch + np.arange(n_ctx_ch)
    order = jnp.asarray(np.stack([np.concatenate([ctx_ch, lat_ch]),
                                  np.concatenate([ctx_ch[::-1], lat_ch[::-1]])]).astype(np.int32))
    yf, yb = _wkv_states(order, m_c, n_c, r_c, y_c, t_all)

    rw_pad, rb_pad = _router_pads(router_w[0], router_b[0])
    x1, hf, top_idx, gates = _mix_out(yf, yb, r, kx, vx, g, o_attn, x, rwkv_r_k[0].reshape(-1), rwkv_ln_w[0],
                                      rwkv_ln_b[0], bd_mean, bd_ones, hy_w_out[0].astype(BF16), norm_mix_post[0],
                                      lat_mod(0, 2), norm_ffn_pre[0], lat_mod(0, 3), lat_mod(0, 4), rw_pad, rb_pad)
    yk = _moe_rows(hf, top_idx, 0, moe_w_gu, moe_b_gu, moe_w_down, moe_b_down)

    rw_pad, rb_pad = _router_pads(router_w[1], router_b[1])
    x3, hf, top_idx, gates = _gmlp(x1, yk, gates, norm_ffn_post[0], lat_mod(0, 5), norm_mix_pre[1], lat_mod(1, 0), lat_mod(1, 1), gm_w_in[0], gm_v_norm_w[0],
                                   gm_v_norm_b[0], gm_w_s[0], gm_b_s[0], gm_w_out[0], norm_mix_post[1], lat_mod(1, 2),
                                   norm_ffn_pre[1], lat_mod(1, 3), lat_mod(1, 4), rw_pad, rb_pad)
    yk = _moe_rows(hf, top_idx, 1, moe_w_gu, moe_b_gu, moe_w_down, moe_b_down)
    return _combine(x3, yk, gates, norm_ffn_post[1], lat_mod(1, 5))
```

```python
import functools

import jax
import jax.numpy as jnp
import numpy as np
from jax import lax
from jax.experimental import pallas as pl
from jax.experimental.pallas import tpu as pltpu

F32 = jnp.float32
BF16 = jnp.bfloat16
HIGHEST = lax.Precision.HIGHEST

D_MODEL = 1024
GRID_W = 64
EPS = 1e-6

MLA_HEADS = 8
MLA_NOPE = 64
MLA_ROPE = 32
MLA_V = 64
MLA_Q_RANK = 256
MLA_KV_RANK = 128
MLA_SCALE = (MLA_NOPE + MLA_ROPE) ** -0.5
ROPE_FREQS = MLA_ROPE // 4
ROPE_BASE = 10000.0
HEAD_PAD = 128
ATT_QSUB = 128
ATT_KSUB = 640
LOG2E = 1.4426950408889634
ATT_EXP_DTYPE = jnp.bfloat16
V_EXT = 80

RWKV_HEAD = 64
RWKV_DIM = 512
RWKV_HEADS = 8
DECAY_LORA = 32
AAA_LORA = 32
GATE_LORA = 96
RWKV_PAD_COLS = 3 * RWKV_DIM + 128 + 128
RWKV_LN_EPS = 64e-5
WKV_CHUNK = 64
WKV_CHUNKS_PER_STEP = 4

CHUNK = 128
GM_WIDTH = 1024
GM_GROUPS = 8
LN_EPS = 1e-5

N_EXPERTS = 32
TOP_K = 4
EXPERT_FF = 1024
SWIGLU_LIMIT = 7.0
SWIGLU_ALPHA = 1.702
LOGIT_PAD = 128
NEG_BIG = -1e30

ROW_TILE = 256
VMEM_LIMIT = 48 * 1024 * 1024


def _cparams(sem):
    return pltpu.CompilerParams(dimension_semantics=sem, vmem_limit_bytes=VMEM_LIMIT)


def _rms(x):
    return x * lax.rsqrt(jnp.mean(x * x, axis=-1, keepdims=True) + EPS)


def _dot(a, b, precision=None):
    return jnp.dot(a, b, preferred_element_type=F32, precision=precision)


def _dot_nt(a, b, precision=None):
    return lax.dot_general(a, b, (((1,), (1,)), ((), ())), preferred_element_type=F32, precision=precision)


def _dot_tn(a, b, precision=None):
    return lax.dot_general(a, b, (((0,), (0,)), ((), ())), preferred_element_type=F32, precision=precision)


def _split(x):
    hi = x.astype(BF16)
    return hi, (x - hi.astype(F32)).astype(BF16)


def _dot_split(a, b):
    a_hi, a_lo = _split(a)
    b_hi, b_lo = _split(b)
    return _dot(jnp.concatenate([a_hi, a_hi, a_lo], axis=1), jnp.concatenate([b_hi, b_lo, b_hi], axis=0))


def _mod_kernel(s_ref, w_ref, b_ref, o_ref):
    s = s_ref[...]
    s = s * jax.nn.sigmoid(s)
    o_ref[...] = _dot(s, w_ref[...], HIGHEST) + b_ref[...]


def _modulation(cond_rows, ada_w, ada_b):
    n_l, d, n6 = ada_w.shape
    tn = 1536
    return pl.pallas_call(
        _mod_kernel,
        grid=(n_l, n6 // tn),
        in_specs=[pl.BlockSpec((8, d), lambda l, j: (0, 0)),
                  pl.BlockSpec((None, d, tn), lambda l, j: (l, 0, j)),
                  pl.BlockSpec((None, 1, tn), lambda l, j: (l, 0, j))],
        out_specs=pl.BlockSpec((None, 8, tn), lambda l, j: (l, 0, j)),
        out_shape=jax.ShapeDtypeStruct((n_l, 8, n6), F32),
        compiler_params=_cparams(("arbitrary", "arbitrary")),
        name="adaln_mod",
    )(cond_rows, ada_w, ada_b.reshape(n_l, 1, n6))


def _inproj_kernel(x_ref, ctx_ref, g_ref, sh_ref, sc_ref, cos_ref, sin_ref, w1_ref, qn_ref, kvn_ref, wq_ref, wk_ref,
                   q_ref, k_ref, vt_ref, prw_ref, *, n_lat_tiles):
    x_in = jnp.where(pl.program_id(1) >= n_lat_tiles, ctx_ref[...], x_ref[...])
    h = _rms(x_in) * g_ref[...]
    h = h * (1.0 + sc_ref[...]) + sh_ref[...]
    p = _dot(h.astype(BF16), w1_ref[...])
    cos = jnp.concatenate([cos_ref[...]] * MLA_HEADS, axis=1)
    sin = jnp.concatenate([sin_ref[...]] * MLA_HEADS, axis=1)
    nq = MLA_HEADS * HEAD_PAD
    qn = _rms(p[:, :MLA_Q_RANK]) * qn_ref[...]
    qq = _dot(qn.astype(BF16), wq_ref[...])
    q_ref[...] = (qq[:, :nq] * cos + qq[:, nq:] * sin).astype(BF16)
    kvn = _rms(p[:, MLA_Q_RANK:MLA_Q_RANK + MLA_KV_RANK]) * kvn_ref[...]
    x2 = jnp.concatenate([kvn, p[:, MLA_Q_RANK + MLA_KV_RANK:512]], axis=1).astype(BF16)
    kk = _dot(x2, wk_ref[...])
    k_ref[...] = (kk[:, :nq] * cos + kk[:, nq:2 * nq] * sin).astype(BF16)
    vt = kk[:, 2 * nq:].T.astype(BF16)
    ones = jnp.ones((V_EXT - MLA_V, vt.shape[1]), BF16)
    for hd in range(MLA_HEADS):
        vt_ref[hd * V_EXT:hd * V_EXT + MLA_V, :] = vt[hd * MLA_V:(hd + 1) * MLA_V, :]
        vt_ref[hd * V_EXT + MLA_V:(hd + 1) * V_EXT, :] = ones
    prw_ref[...] = p[:, 512:]


def _rotate_half_cols(w):
    wr = w.reshape(w.shape[:-1] + (2, 2, ROPE_FREQS))
    return jnp.stack([-wr[..., 1, :], wr[..., 0, :]], axis=-2).reshape(w.shape)


def _inproj_weights(hy_w_in, w_uq, w_ukv):
    d = hy_w_in.shape[0]
    mla_cols = MLA_Q_RANK + MLA_KV_RANK + MLA_ROPE
    w_mla = hy_w_in[:, :mla_cols]
    w_rw = hy_w_in[:, mla_cols:]
    w1 = jnp.concatenate([w_mla, jnp.zeros((d, 512 - mla_cols), F32), _pad_rwkv_cols(w_rw)], axis=1).astype(BF16)
    wq = (w_uq * (MLA_SCALE * LOG2E)).reshape(MLA_Q_RANK, MLA_HEADS, MLA_NOPE + MLA_ROPE)
    z_n = jnp.zeros((MLA_Q_RANK, MLA_HEADS, MLA_NOPE), F32)
    z_p = jnp.zeros((MLA_Q_RANK, MLA_HEADS, HEAD_PAD - MLA_NOPE - MLA_ROPE), F32)
    wq_plain = jnp.concatenate([wq[..., :MLA_NOPE], wq[..., MLA_NOPE:], z_p], axis=-1)
    wq_rot = jnp.concatenate([z_n, _rotate_half_cols(wq[..., MLA_NOPE:]), z_p], axis=-1)
    wq2 = jnp.concatenate([wq_plain.reshape(MLA_Q_RANK, -1), wq_rot.reshape(MLA_Q_RANK, -1)], axis=1).astype(BF16)
    wkv = w_ukv.reshape(MLA_KV_RANK, MLA_HEADS, MLA_NOPE + MLA_V)
    eye = jnp.broadcast_to(jnp.eye(MLA_ROPE, dtype=F32)[:, None, :], (MLA_ROPE, MLA_HEADS, MLA_ROPE))
    pad_k = HEAD_PAD - MLA_NOPE - MLA_ROPE
    top_plain = jnp.concatenate([wkv[..., :MLA_NOPE], jnp.zeros((MLA_KV_RANK, MLA_HEADS, MLA_ROPE + pad_k), F32)], -1)
    mid_plain = jnp.concatenate([jnp.zeros((MLA_ROPE, MLA_HEADS, MLA_NOPE), F32), eye,
                                 jnp.zeros((MLA_ROPE, MLA_HEADS, pad_k), F32)], -1)
    mid_rot = jnp.concatenate([jnp.zeros((MLA_ROPE, MLA_HEADS, MLA_NOPE), F32), _rotate_half_cols(eye),
                               jnp.zeros((MLA_ROPE, MLA_HEADS, pad_k), F32)], -1)
    nq = MLA_HEADS * HEAD_PAD
    rows_c = jnp.concatenate([top_plain.reshape(MLA_KV_RANK, nq), jnp.zeros((MLA_KV_RANK, nq), F32),
                              wkv[..., MLA_NOPE:].reshape(MLA_KV_RANK, MLA_HEADS * MLA_V)], axis=1)
    rows_r = jnp.concatenate([mid_plain.reshape(MLA_ROPE, nq), mid_rot.reshape(MLA_ROPE, nq),
                              jnp.zeros((MLA_ROPE, MLA_HEADS * MLA_V), F32)], axis=1)
    rows_z = jnp.zeros((256 - MLA_KV_RANK - MLA_ROPE, rows_c.shape[1]), F32)
    wk2 = jnp.concatenate([rows_c, rows_r, rows_z], axis=0).astype(BF16)
    return w1, wq2, wk2


def _pad_rwkv_cols(w):
    pad = jnp.zeros(w.shape[:-1] + (RWKV_PAD_COLS - w.shape[-1],), w.dtype)
    return jnp.concatenate([w, pad], axis=-1)


def _rope_tables(n_lat, n_ctx):
    t = jnp.arange(n_lat)
    row = (t // GRID_W).astype(F32)
    col = (t % GRID_W).astype(F32)
    inv = ROPE_BASE ** (-jnp.arange(ROPE_FREQS, dtype=F32) / ROPE_FREQS)
    ang = jnp.stack([row[:, None] * inv, col[:, None] * inv], axis=1)
    ang = jnp.broadcast_to(ang[:, :, None, :], (n_lat, 2, 2, ROPE_FREQS)).reshape(n_lat, MLA_ROPE)
    cos = jnp.concatenate([jnp.ones((n_lat, MLA_NOPE), F32), jnp.cos(ang),
                           jnp.ones((n_lat, HEAD_PAD - MLA_NOPE - MLA_ROPE), F32)], axis=1)
    sin = jnp.concatenate([jnp.zeros((n_lat, MLA_NOPE), F32), jnp.sin(ang),
                           jnp.zeros((n_lat, HEAD_PAD - MLA_NOPE - MLA_ROPE), F32)], axis=1)
    cos = jnp.concatenate([cos, jnp.ones((n_ctx, HEAD_PAD), F32)], axis=0)
    sin = jnp.concatenate([sin, jnp.zeros((n_ctx, HEAD_PAD), F32)], axis=0)
    return cos, sin


def _inproj(x, ctx, gain, shift, scale, cos, sin, w1, q_norm, kv_norm, wq2, wk2):
    b, n_lat, d = x.shape
    t_all = n_lat + ctx.shape[1]
    tm = ROW_TILE
    n_lat_tiles = n_lat // tm
    nq = MLA_HEADS * HEAD_PAD

    def mod_map(bi, i):
        return (2 * bi + jnp.where(i >= n_lat_tiles, 1, 0), 0, 0)

    const = lambda bi, i: (0, 0)
    row = lambda bi, i: (bi, i, 0)
    return pl.pallas_call(
        functools.partial(_inproj_kernel, n_lat_tiles=n_lat_tiles),
        grid=(b, t_all // tm),
        in_specs=[pl.BlockSpec((None, tm, d), lambda bi, i: (bi, jnp.minimum(i, n_lat_tiles - 1), 0)),
                  pl.BlockSpec((None, tm, d), lambda bi, i: (bi, jnp.maximum(i - n_lat_tiles, 0), 0)),
                  pl.BlockSpec((1, d), const),
                  pl.BlockSpec((None, 1, d), mod_map),
                  pl.BlockSpec((None, 1, d), mod_map),
                  pl.BlockSpec((tm, HEAD_PAD), lambda bi, i: (i, 0)),
                  pl.BlockSpec((tm, HEAD_PAD), lambda bi, i: (i, 0)),
                  pl.BlockSpec(w1.shape, const),
                  pl.BlockSpec((1, MLA_Q_RANK), const),
                  pl.BlockSpec((1, MLA_KV_RANK), const),
                  pl.BlockSpec(wq2.shape, const),
                  pl.BlockSpec(wk2.shape, const)],
        out_specs=[pl.BlockSpec((None, tm, nq), row),
                   pl.BlockSpec((None, tm, nq), row),
                   pl.BlockSpec((None, MLA_HEADS * V_EXT, tm), lambda bi, i: (bi, 0, i)),
                   pl.BlockSpec((None, tm, RWKV_PAD_COLS), row)],
        out_shape=[jax.ShapeDtypeStruct((b, t_all, nq), BF16),
                   jax.ShapeDtypeStruct((b, t_all, nq), BF16),
                   jax.ShapeDtypeStruct((b, MLA_HEADS * V_EXT, t_all), BF16),
                   jax.ShapeDtypeStruct((b, t_all, RWKV_PAD_COLS), F32)],
        compiler_params=_cparams(("parallel", "parallel")),
        name="hybrid_inproj",
    )(x, ctx, gain.reshape(1, d), shift, scale, cos, sin, w1, q_norm.reshape(1, -1), kv_norm.reshape(1, -1), wq2, wk2)


def _attn_kernel(q_ref, k_ref, vt_ref, o_ref, m_sc, acc_sc):
    ki = pl.program_id(3)

    @pl.when(ki == 0)
    def _():
        m_sc[...] = jnp.full(m_sc.shape, -jnp.inf, F32)
        acc_sc[...] = jnp.zeros(acc_sc.shape, F32)

    tq, tk = q_ref.shape[0], k_ref.shape[0]
    chains = [(h, qs) for h in range(2) for qs in range(tq // ATT_QSUB)]
    hsl = [slice(h * HEAD_PAD, (h + 1) * HEAD_PAD) for h, _ in chains]
    vsl = [slice(h * V_EXT, (h + 1) * V_EXT) for h, _ in chains]
    qsl = [slice(qs * ATT_QSUB, (qs + 1) * ATT_QSUB) for _, qs in chains]
    nch = len(chains)
    q = [q_ref[qsl[c], hsl[c]] for c in range(nch)]
    m = [m_sc[chains[c][0], 0:1, qsl[c]] for c in range(nch)]
    acc = [acc_sc[chains[c][0], :, qsl[c]] for c in range(nch)]
    ksub = ATT_KSUB if tk % ATT_KSUB == 0 else tk
    for kb in range(tk // ksub):
        ksl = slice(kb * ksub, (kb + 1) * ksub)
        s = [_dot_nt(k_ref[ksl, hsl[c]], q[c]).astype(ATT_EXP_DTYPE) for c in range(nch)]
        m_new = [jnp.maximum(m[c], jnp.max(s[c], axis=0, keepdims=True).astype(F32)) for c in range(nch)]
        alpha = [jnp.exp2(m[c] - m_new[c]) for c in range(nch)]
        p = [jnp.exp2(s[c] - m_new[c].astype(ATT_EXP_DTYPE)).astype(BF16) for c in range(nch)]
        acc = [acc[c] * alpha[c] + _dot(vt_ref[vsl[c], ksl], p[c]) for c in range(nch)]
        m = m_new
    for c in range(nch):
        h = chains[c][0]
        m_sc[h, :, qsl[c]] = jnp.broadcast_to(m[c], (8, ATT_QSUB))
        acc_sc[h, :, qsl[c]] = acc[c]

    @pl.when(ki == pl.num_programs(3) - 1)
    def _():
        o0 = (acc_sc[0, :MLA_V, :] / acc_sc[0, MLA_V:MLA_V + 1, :]).T
        o1 = (acc_sc[1, :MLA_V, :] / acc_sc[1, MLA_V:MLA_V + 1, :]).T
        o_ref[...] = jnp.concatenate([o0, o1], axis=1).astype(BF16)


def _attention(q, k, vt, n_lat, tq, tk):
    b, t_all, _ = k.shape
    return pl.pallas_call(
        _attn_kernel,
        grid=(b, MLA_HEADS // 2, n_lat // tq, t_all // tk),
        in_specs=[pl.BlockSpec((None, tq, 2 * HEAD_PAD), lambda bi, hp, qi, ki: (bi, qi, hp)),
                  pl.BlockSpec((None, tk, 2 * HEAD_PAD), lambda bi, hp, qi, ki: (bi, ki, hp)),
                  pl.BlockSpec((None, 2 * V_EXT, tk), lambda bi, hp, qi, ki: (bi, hp, ki))],
        out_specs=pl.BlockSpec((None, tq, 2 * MLA_V), lambda bi, hp, qi, ki: (bi, qi, hp)),
        out_shape=jax.ShapeDtypeStruct((b, n_lat, MLA_HEADS * MLA_V), BF16),
        scratch_shapes=[pltpu.VMEM((2, 8, tq), F32), pltpu.VMEM((2, V_EXT, tq), F32)],
        compiler_params=_cparams(("parallel", "parallel", "parallel", "arbitrary")),
        name="mla_attention",
    )(q, k, vt)


def _seg_sum(x, bd_ref):
    hi = x.astype(BF16)
    lo = (x - hi.astype(F32)).astype(BF16)
    bd = bd_ref[...]
    return _dot(hi, bd) + _dot(lo, bd)


def _rwkv_prep_kernel(p_ref, prev_ref, next_ref, mup_ref, mun_ref, wl_ref, w0a0_ref, g2_ref, kk_ref, ka_ref, bd_ref,
                      r_ref, k_ref, v_ref, g_ref, kkn_ref, lw_ref, kd_ref, bdir_ref, *, n_lat_tiles, n_tiles):
    i = pl.program_id(1)
    p = p_ref[...]
    tm = p.shape[0]
    first = jnp.logical_or(i == 0, i == n_lat_tiles)
    last = jnp.logical_or(i == n_lat_tiles - 1, i == n_tiles - 1)
    prev_row = jnp.where(first, 0.0, prev_ref[7:8, :])
    next_row = jnp.where(last, 0.0, next_ref[0:1, :])
    ridx = lax.broadcasted_iota(jnp.int32, p.shape, 0)
    prev = jnp.where(ridx == 0, prev_row, pltpu.roll(p, 1, 0))
    nxt = jnp.where(ridx == tm - 1, next_row, pltpu.roll(p, tm - 1, 0))
    p = p + mup_ref[...] * (prev - p) + mun_ref[...] * (nxt - p)
    c = RWKV_DIM
    r, k, v = p[:, :c], p[:, c:2 * c], p[:, 2 * c:3 * c]
    lo = p[:, 3 * c:3 * c + 128]
    lane = lax.broadcasted_iota(jnp.int32, lo.shape, 1)
    lo = jnp.where(lane < 2 * DECAY_LORA, jnp.tanh(lo), lo)
    wa = _dot(lo.astype(BF16), wl_ref[...]) + w0a0_ref[...]
    gd = p[:, 3 * c + 128:]
    g_ref[...] = _dot(jax.nn.sigmoid(gd).astype(BF16), g2_ref[...]).astype(BF16)
    kk = k * kk_ref[...]
    kk = kk * lax.rsqrt(jnp.maximum(_seg_sum(kk * kk, bd_ref), 1e-24))
    r_ref[...] = r.astype(BF16)
    k_ref[...] = k.astype(BF16)
    v_ref[...] = v.astype(BF16)
    kkn_ref[...] = kk.astype(BF16)
    ka = ka_ref[...]
    for d in range(2):
        w = wa[:, d * c:(d + 1) * c]
        a = jax.nn.sigmoid(wa[:, (2 + d) * c:(3 + d) * c])
        lw_ref[:, d * c:(d + 1) * c] = -float(np.exp(-0.5)) * jax.nn.sigmoid(w)
        kd_ref[:, d * c:(d + 1) * c] = (k * (1.0 + (a - 1.0) * ka)).astype(BF16)
        bdir_ref[:, d * c:(d + 1) * c] = (kk * a).astype(BF16)


def _rwkv_prepare(prw, n_lat, mu_prev, mu_next, w0, w2, a0, a2, g2, k_k, k_a, bd_ones):
    b, t_all, pc = prw.shape
    tm = ROW_TILE
    n_tiles = t_all // tm
    n_lat_tiles = n_lat // tm
    c = RWKV_DIM
    z = jnp.zeros((DECAY_LORA, c), F32)
    wl = jnp.concatenate([
        jnp.concatenate([w2[0], z, z, z], axis=1), jnp.concatenate([z, w2[1], z, z], axis=1),
        jnp.concatenate([z, z, a2[0], z], axis=1), jnp.concatenate([z, z, z, a2[1]], axis=1)], axis=0).astype(BF16)
    w0a0 = jnp.concatenate([w0[0], w0[1], a0[0], a0[1]]).reshape(1, 4 * c)
    g2p = jnp.concatenate([g2, jnp.zeros((128 - GATE_LORA, c), F32)], axis=0).astype(BF16)
    row = lambda bi, i: (bi, i, 0)
    const = lambda bi, i: (0, 0)
    hb = tm // 8
    n_hb = t_all // 8
    kern = functools.partial(_rwkv_prep_kernel, n_lat_tiles=n_lat_tiles, n_tiles=n_tiles)
    o_c = jax.ShapeDtypeStruct((b, t_all, c), BF16)
    o_2c = jax.ShapeDtypeStruct((b, t_all, 2 * c), BF16)
    o_lw = jax.ShapeDtypeStruct((b, t_all, 2 * c), F32)
    return pl.pallas_call(
        kern,
        grid=(b, n_tiles),
        in_specs=[pl.BlockSpec((None, tm, pc), row),
                  pl.BlockSpec((None, 8, pc), lambda bi, i: (bi, jnp.maximum(i * hb - 1, 0), 0)),
                  pl.BlockSpec((None, 8, pc), lambda bi, i: (bi, jnp.minimum((i + 1) * hb, n_hb - 1), 0)),
                  pl.BlockSpec((1, pc), const), pl.BlockSpec((1, pc), const),
                  pl.BlockSpec(wl.shape, const), pl.BlockSpec((1, 4 * c), const),
                  pl.BlockSpec(g2p.shape, const), pl.BlockSpec((1, c), const), pl.BlockSpec((1, c), const),
                  pl.BlockSpec((c, c), const)],
        out_specs=[pl.BlockSpec((None, tm, c), row)] * 5 + [pl.BlockSpec((None, tm, 2 * c), row)] * 3,
        out_shape=[o_c] * 5 + [o_lw, o_2c, o_2c],
        compiler_params=_cparams(("parallel", "parallel")),
        name="rwkv_prepare",
    )(prw, prw, prw, _pad_rwkv_cols(mu_prev).reshape(1, pc), _pad_rwkv_cols(mu_next).reshape(1, pc),
      wl, w0a0, g2p, k_k.reshape(1, c), k_a.reshape(1, c), bd_ones)


def _wkv_chunk_kernel(r_ref, v_ref, kk_ref, lw_ref, kd_ref, bd_ref, m_ref, n_ref, rr_ref, yv_ref):
    d = pl.program_id(1)
    cs = WKV_CHUNK
    nc = r_ref.shape[0] // cs
    ti = lax.broadcasted_iota(jnp.int32, (cs, cs), 0)
    si = lax.broadcasted_iota(jnp.int32, (cs, cs), 1)
    rel = (si - ti) * (1 - 2 * d)
    incl = rel <= 0
    strict = rel < 0
    eye = si == ti
    incl_f = incl.astype(F32)
    ops = []
    for ci in range(nc):
        rows = slice(ci * cs, (ci + 1) * cs)
        lw = lw_ref[rows, :]
        g = _dot(incl_f, lw, HIGHEST)
        total = jnp.sum(lw, axis=0, keepdims=True)
        e_inv = jnp.exp(-g)
        e_end = jnp.exp(total - g)
        gam = jnp.exp(total)
        kd = kd_ref[rows, :]
        bd = bd_ref[rows, :]
        at = -kk_ref[rows, :] * jnp.exp(g - lw)
        rt = r_ref[rows, :] * jnp.exp(g)
        kt = (kd * e_inv).astype(BF16)
        bt = (bd * e_inv).astype(BF16)
        ke = (kd * e_end).astype(BF16)
        be = (bd * e_end).astype(BF16)
        v32 = v_ref[rows, :].astype(F32)
        v = v32.astype(BF16)
        for h in range(RWKV_HEADS):
            sl = slice(h * RWKV_HEAD, (h + 1) * RWKV_HEAD)
            ops.append((at[:, sl], rt[:, sl], kt[:, sl], bt[:, sl], ke[:, sl], be[:, sl], v[:, sl], gam[:, sl], v32[:, sl]))
    n_it = len(ops)
    aa = [_dot_nt(jnp.concatenate([o[0], o[1]], axis=0).astype(BF16), jnp.concatenate([o[3], o[2]], axis=0))
          for o in ops]
    a_ab = [jnp.where(strict, x[:cs, :cs], 0.0) for x in aa]
    a_ak = [jnp.where(strict, x[:cs, cs:], 0.0).astype(BF16) for x in aa]
    ti2 = lax.broadcasted_iota(jnp.int32, (cs, 2 * cs), 0)
    si2 = lax.broadcasted_iota(jnp.int32, (cs, 2 * cs), 1)
    incl2 = (jnp.where(si2 >= cs, si2 - cs, si2) - ti2) * (1 - 2 * d) <= 0
    a_rbk = [jnp.where(incl2, x[cs:, :], 0.0).astype(BF16) for x in aa]
    z = [jnp.concatenate([ops[i][0], _dot(a_ak[i], ops[i][6])], axis=1) for i in range(n_it)]
    pw = a_ab
    n_sq = int(np.log2(cs))
    for it in range(n_sq):
        more = it + 1 < n_sq
        res = [_dot_split(pw[i], jnp.concatenate([z[i], pw[i]], axis=1) if more else z[i]) for i in range(n_it)]
        z = [z[i] + res[i][:, :2 * cs] for i in range(n_it)]
        if more:
            pw = [x[:, 2 * cs:] for x in res]
    zero = jnp.zeros((cs, cs), F32)
    wmat = [jnp.concatenate([z[i], jnp.concatenate([zero, ops[i][8]], axis=1)], axis=0).astype(BF16)
            for i in range(n_it)]
    w2 = [_dot(a_rbk[i], wmat[i]) for i in range(n_it)]
    w3 = [_dot_tn(jnp.concatenate([ops[i][5], ops[i][4]], axis=0), wmat[i]) for i in range(n_it)]
    nh = RWKV_HEADS
    for ci in range(nc):
        rows = slice(ci * cs, (ci + 1) * cs)
        ids = range(ci * nh, (ci + 1) * nh)
        m_ref[rows, :] = jnp.concatenate([jnp.where(eye, ops[i][7], 0.0) + w3[i][:, :cs] for i in ids], axis=1)
        n_ref[rows, :] = jnp.concatenate([w3[i][:, cs:] for i in ids], axis=1)
        rr_ref[rows, :] = jnp.concatenate([ops[i][1] + w2[i][:, :cs] for i in ids], axis=1)
        yv_ref[rows, :] = jnp.concatenate([w2[i][:, cs:] for i in ids], axis=1)


def _wkv_chunks(r, v, kkn, lw, kd, bdir):
    b, t_all, c = r.shape
    rows = WKV_CHUNK * WKV_CHUNKS_PER_STEP
    nst = t_all // rows
    shared = pl.BlockSpec((None, rows, c), lambda bi, d, ci: (bi, ci, 0))
    per_dir = pl.BlockSpec((None, rows, c), lambda bi, d, ci: (bi, ci, d))
    out = pl.BlockSpec((None, None, rows, c), lambda bi, d, ci: (bi, d, ci, 0))
    o_s = jax.ShapeDtypeStruct((b, 2, t_all, c), F32)
    return pl.pallas_call(
        _wkv_chunk_kernel,
        grid=(b, 2, nst),
        in_specs=[shared, shared, shared, per_dir, per_dir, per_dir],
        out_specs=[out] * 4,
        out_shape=[o_s] * 4,
        compiler_params=_cparams(("parallel", "parallel", "parallel")),
        name="wkv_chunk_summaries",
    )(r, v, kkn, lw, kd, bdir)


def _wkv_state_kernel(order_ref, mf_ref, nf_ref, rf_ref, yf_ref, mb_ref, nb_ref, rb_ref, yb_ref,
                      of_ref, ob_ref, st_sc):
    del order_ref
    s = pl.program_id(0)

    @pl.when(s == 0)
    def _():
        st_sc[...] = jnp.zeros(st_sc.shape, F32)

    n_b = st_sc.shape[0]
    ins = ((mf_ref, nf_ref, rf_ref, yf_ref, of_ref), (mb_ref, nb_ref, rb_ref, yb_ref, ob_ref))
    for bi in range(n_b):
        for d in range(2):
            m_r, n_r, r_r, y_r, o_r = ins[d]
            st = st_sc[bi, d]
            m, n, rr, yv = m_r[bi], n_r[bi], r_r[bi], y_r[bi]
            st_hi, st_lo = _split(st)
            m_hi, m_lo = _split(m)
            rr_b = rr.astype(BF16)
            zpad = jnp.zeros((rr.shape[0], 2 * RWKV_HEAD), BF16)
            ys, sts = [], []
            for h in range(RWKV_HEADS):
                sl = slice(h * RWKV_HEAD, (h + 1) * RWKV_HEAD)
                lhs = jnp.concatenate([jnp.concatenate([m_hi[:, sl], m_hi[:, sl], m_lo[:, sl]], axis=1),
                                       jnp.concatenate([rr_b[:, sl], zpad], axis=1)], axis=0)
                rhs = jnp.concatenate([st_hi[:, sl], st_lo[:, sl], st_hi[:, sl]], axis=0)
                res = _dot(lhs, rhs)
                cs = rr.shape[0]
                sts.append(res[:cs] + n[:, sl])
                ys.append(res[cs:] + yv[:, sl])
            o_r[bi] = jnp.concatenate(ys, axis=1)
            st_sc[bi, d] = jnp.concatenate(sts, axis=1)


def _wkv_states(order, m, n, rr, yv, t_all):
    b, _, _, c = m.shape
    cs = WKV_CHUNK
    nch = t_all // cs
    fwd = pl.BlockSpec((b, None, cs, c), lambda s, o: (0, 0, o[0, s], 0))
    bwd = pl.BlockSpec((b, None, cs, c), lambda s, o: (0, 1, o[1, s], 0))
    out_f = pl.BlockSpec((b, cs, c), lambda s, o: (0, o[0, s], 0))
    out_b = pl.BlockSpec((b, cs, c), lambda s, o: (0, o[1, s], 0))
    o_s = jax.ShapeDtypeStruct((b, t_all, c), F32)
    return pl.pallas_call(
        _wkv_state_kernel,
        grid_spec=pltpu.PrefetchScalarGridSpec(
            num_scalar_prefetch=1,
            grid=(nch,),
            in_specs=[fwd] * 4 + [bwd] * 4,
            out_specs=[out_f, out_b],
            scratch_shapes=[pltpu.VMEM((b, 2, cs, c), F32)]),
        out_shape=[o_s, o_s],
        compiler_params=_cparams(("arbitrary",)),
        name="wkv_state_pass",
    )(order, m, n, rr, yv, m, n, rr, yv)


def _ffn_pre(x1, gain_ref, sh_ref, sc_ref, rw_ref, rb_ref, hf_ref, idx_ref, gate_ref, rank_ref, hist_ref):
    hf = _rms(x1) * gain_ref[...]
    hf = hf * (1.0 + sc_ref[...]) + sh_ref[...]
    hf_ref[...] = hf.astype(BF16)
    hf_hi, hf_lo = _split(hf)
    t = _dot(hf_hi, rw_ref[...])
    u = _dot(hf_lo, rw_ref[:, :LOGIT_PAD])
    logits = t[:, :LOGIT_PAD] + t[:, LOGIT_PAD:] + u + rb_ref[...]
    lane = lax.broadcasted_iota(jnp.int32, logits.shape, 1)
    lane_f = lane.astype(F32)
    work = logits
    idx_out = jnp.zeros(logits.shape, F32)
    val_out = jnp.full(logits.shape, NEG_BIG, F32)
    sels = []
    for kth in range(TOP_K):
        m = jnp.max(work, axis=1, keepdims=True)
        idx = jnp.min(jnp.where(work == m, lane_f, float(LOGIT_PAD)), axis=1, keepdims=True)
        sel = lane_f == idx
        sels.append(sel)
        work = jnp.where(sel, -jnp.inf, work)
        idx_out = jnp.where(lane == kth, idx, idx_out)
        val_out = jnp.where(lane == kth, m, val_out)
    e = jnp.exp(val_out - jnp.max(val_out, axis=1, keepdims=True))
    e = jnp.where(lane < TOP_K, e, 0.0)
    idx_ref[...] = idx_out.astype(jnp.int32)
    gate_ref[...] = e / jnp.sum(e, axis=1, keepdims=True)
    tm = logits.shape[0]
    cnt = jnp.zeros(logits.shape, F32)
    for sel in sels:
        cnt = cnt + jnp.where(sel, 1.0, 0.0)
    ti = lax.broadcasted_iota(jnp.int32, (tm, tm), 0)
    si = lax.broadcasted_iota(jnp.int32, (tm, tm), 1)
    before = _dot(jnp.where(si < ti, 1.0, 0.0).astype(BF16), cnt.astype(BF16))
    rank_out = jnp.zeros(logits.shape, F32)
    for kth, sel in enumerate(sels):
        rank_out = jnp.where(lane == kth, jnp.sum(jnp.where(sel, before, 0.0), axis=1, keepdims=True), rank_out)
    rank_ref[...] = rank_out.astype(jnp.int32)
    hist_ref[...] = jnp.broadcast_to(jnp.sum(cnt, axis=0, keepdims=True), hist_ref.shape)


def _router_pads(router_w, router_b):
    d = router_w.shape[0]
    rw = jnp.concatenate([router_w, jnp.zeros((d, LOGIT_PAD - N_EXPERTS), F32)], axis=1)
    rb = jnp.concatenate([router_b, jnp.full((LOGIT_PAD - N_EXPERTS,), NEG_BIG, F32)]).reshape(1, LOGIT_PAD)
    return jnp.concatenate(_split(rw), axis=1), rb


def _mix_out_kernel(yf_ref, yb_ref, r_ref, k_ref, v_ref, g_ref, o_ref, x_ref,
                    rk_ref, lnw_ref, lnb_ref, bdm_ref, bds_ref, wout_ref, npost_ref, g1_ref,
                    gpre_ref, sh_ref, sc_ref, rw_ref, rb_ref,
                    x1_ref, hf_ref, idx_ref, gate_ref, rank_ref, hist_ref):
    y = yf_ref[...] + yb_ref[...]
    mu = _seg_sum(y, bdm_ref)
    dlt = y - mu
    var = _seg_sum(dlt * dlt, bdm_ref)
    yn = dlt * lax.rsqrt(var + RWKV_LN_EPS) * lnw_ref[...] + lnb_ref[...]
    bonus = _seg_sum(r_ref[...].astype(F32) * k_ref[...] * rk_ref[...], bds_ref) * v_ref[...]
    rw = (yn + bonus) * g_ref[...]
    mix_in = jnp.concatenate([o_ref[...], rw.astype(BF16)], axis=1)
    mix = _dot(mix_in, wout_ref[...])
    x1 = x_ref[...] + g1_ref[...] * (_rms(mix) * npost_ref[...])
    x1_ref[...] = x1
    _ffn_pre(x1, gpre_ref, sh_ref, sc_ref, rw_ref, rb_ref, hf_ref, idx_ref, gate_ref, rank_ref, hist_ref)


def _token_out_specs(tm, d):
    row = lambda bi, i: (bi, i, 0)
    lanes = pl.BlockSpec((None, tm, LOGIT_PAD), row)
    return [pl.BlockSpec((None, tm, d), row), pl.BlockSpec((None, tm, d), row), lanes, lanes, lanes,
            pl.BlockSpec((None, None, 8, LOGIT_PAD), lambda bi, i: (bi, i, 0, 0))]


def _token_out_shapes(b, s, d):
    return [jax.ShapeDtypeStruct((b, s, d), F32), jax.ShapeDtypeStruct((b, s, d), BF16),
            jax.ShapeDtypeStruct((b, s, LOGIT_PAD), jnp.int32), jax.ShapeDtypeStruct((b, s, LOGIT_PAD), F32),
            jax.ShapeDtypeStruct((b, s, LOGIT_PAD), jnp.int32),
            jax.ShapeDtypeStruct((b, s // ROW_TILE, 8, LOGIT_PAD), F32)]


def _mix_out(yf, yb, r, k, v, g, o_attn, x, r_k, ln_w, ln_b, bd_mean, bd_ones, w_out, n_post, g1,
             n_pre, sh2, sc2, rw_pad, rb_pad):
    b, s, d = x.shape
    c = RWKV_DIM
    tm = ROW_TILE
    row = lambda bi, i: (bi, i, 0)
    const = lambda bi, i: (0, 0)
    per_b = lambda bi, i: (bi, 0, 0)
    rc = pl.BlockSpec((None, tm, c), row)
    vec_c = pl.BlockSpec((1, c), const)
    vec_d = pl.BlockSpec((1, d), const)
    mod_d = pl.BlockSpec((None, 1, d), per_b)
    return pl.pallas_call(
        _mix_out_kernel,
        grid=(b, s // tm),
        in_specs=[rc] * 7 + [pl.BlockSpec((None, tm, d), row),
                             vec_c, vec_c, vec_c, pl.BlockSpec((c, c), const), pl.BlockSpec((c, c), const),
                             pl.BlockSpec((d, d), const), vec_d, mod_d,
                             vec_d, mod_d, mod_d, pl.BlockSpec((d, 2 * LOGIT_PAD), const),
                             pl.BlockSpec((1, LOGIT_PAD), const)],
        out_specs=_token_out_specs(tm, d),
        out_shape=_token_out_shapes(b, s, d),
        compiler_params=_cparams(("parallel", "parallel")),
        name="mixer_out_router",
    )(yf, yb, r, k, v, g, o_attn, x, r_k.reshape(1, c), ln_w.reshape(1, c), ln_b.reshape(1, c), bd_mean, bd_ones,
      w_out, n_post.reshape(1, d), g1, n_pre.reshape(1, d), sh2, sc2, rw_pad, rb_pad)


def _moe_residual(x_ref, y_ref, gate_ref, npost_ref, g2_ref):
    gate = gate_ref[...]
    f = jnp.zeros(x_ref.shape, F32)
    for kth in range(TOP_K):
        f = f + gate[:, kth:kth + 1] * y_ref[kth].astype(F32)
    return x_ref[...] + g2_ref[...] * (_rms(f) * npost_ref[...])


def _gmlp_kernel(xin_ref, y_ref, gatein_ref, npostin_ref, g2in_ref,
                 gpre1_ref, sh1_ref, sc1_ref, win_ref, vnw_ref, vnb_ref, ws_ref, bs_ref, wout_ref,
                 npost_ref, g1_ref, gpre_ref, sh_ref, sc_ref, rw_ref, rb_ref,
                 x1_ref, hf_ref, idx_ref, gate_ref, rank_ref, hist_ref):
    x = _moe_residual(xin_ref, y_ref, gatein_ref, npostin_ref, g2in_ref)
    h = _rms(x) * gpre1_ref[...]
    h = h * (1.0 + sc1_ref[...]) + sh1_ref[...]
    z = _dot(h.astype(BF16), win_ref[...])
    z = 0.5 * z * (1.0 + lax.erf(z * float(2.0 ** -0.5)))
    u, v = z[:, :GM_WIDTH], z[:, GM_WIDTH:]
    mu = jnp.mean(v, axis=-1, keepdims=True)
    dv = v - mu
    var = jnp.mean(dv * dv, axis=-1, keepdims=True)
    v = (dv * lax.rsqrt(var + LN_EPS) * vnw_ref[...] + vnb_ref[...]).astype(BF16)
    gw = GM_WIDTH // GM_GROUPS
    rows = []
    for ci in range(x.shape[0] // CHUNK):
        cols = []
        for gi in range(GM_GROUPS):
            cols.append(_dot(ws_ref[gi], v[ci * CHUNK:(ci + 1) * CHUNK, gi * gw:(gi + 1) * gw]))
        rows.append(jnp.concatenate(cols, axis=1) + bs_ref[...])
    sp = jnp.concatenate(rows, axis=0)
    y = _dot((u * sp).astype(BF16), wout_ref[...])
    x1 = x + g1_ref[...] * (_rms(y) * npost_ref[...])
    x1_ref[...] = x1
    _ffn_pre(x1, gpre_ref, sh_ref, sc_ref, rw_ref, rb_ref, hf_ref, idx_ref, gate_ref, rank_ref, hist_ref)


def _gmlp(x, yk, gates_in, n_post_in, g2_in, n_pre1, sh1, sc1, w_in, vn_w, vn_b, w_s, b_s, w_out, n_post, g1, n_pre, sh2, sc2, rw_pad, rb_pad):
    b, s, d = x.shape
    tm = ROW_TILE
    gw = GM_WIDTH // GM_GROUPS
    bs_full = jnp.repeat(b_s.T, gw, axis=1)
    row = lambda bi, i: (bi, i, 0)
    const = lambda bi, i: (0, 0)
    per_b = lambda bi, i: (bi, 0, 0)
    vec_d = pl.BlockSpec((1, d), const)
    vec_g = pl.BlockSpec((1, GM_WIDTH), const)
    mod_d = pl.BlockSpec((None, 1, d), per_b)
    return pl.pallas_call(
        _gmlp_kernel,
        grid=(b, s // tm),
        in_specs=[pl.BlockSpec((None, tm, d), row),
                  pl.BlockSpec((TOP_K, None, tm, d), lambda bi, i: (0, bi, i, 0)),
                  pl.BlockSpec((None, tm, LOGIT_PAD), row), vec_d, mod_d,
                  vec_d, mod_d, mod_d,
                  pl.BlockSpec((d, 2 * GM_WIDTH), const), vec_g, vec_g,
                  pl.BlockSpec((GM_GROUPS, CHUNK, CHUNK), lambda bi, i: (0, 0, 0)),
                  pl.BlockSpec((CHUNK, GM_WIDTH), const), pl.BlockSpec((GM_WIDTH, d), const),
                  vec_d, mod_d, vec_d, mod_d, mod_d,
                  pl.BlockSpec((d, 2 * LOGIT_PAD), const), pl.BlockSpec((1, LOGIT_PAD), const)],
        out_specs=_token_out_specs(tm, d),
        out_shape=_token_out_shapes(b, s, d),
        compiler_params=_cparams(("parallel", "parallel")),
        name="gmlp_router",
    )(x, yk, gates_in, n_post_in.reshape(1, d), g2_in, n_pre1.reshape(1, d), sh1, sc1, w_in.astype(BF16), vn_w.reshape(1, -1), vn_b.reshape(1, -1),
      w_s.astype(BF16), bs_full, w_out.astype(BF16), n_post.reshape(1, d), g1, n_pre.reshape(1, d), sh2, sc2,
      rw_pad, rb_pad)


MOE_ROWS = 256


def _expert_kernel(be_ref, na_ref, x_ref, wgu_ref, bgu_ref, wd_ref, bd_ref, y_ref, wgu_bf, wd_bf):
    i = pl.program_id(0)
    prev = be_ref[jnp.maximum(i - 1, 0)]
    changed = jnp.logical_or(i == 0, be_ref[i] != prev)

    @pl.when(changed)
    def _():
        wgu_bf[...] = wgu_ref[...].astype(BF16)
        wd_bf[...] = wd_ref[...].astype(BF16)

    @pl.when(i < na_ref[0])
    def _():
        gu = _dot(x_ref[...], wgu_bf[...]) + bgu_ref[...]
        x_glu = jnp.minimum(gu[:, :EXPERT_FF], SWIGLU_LIMIT)
        x_lin = jnp.clip(gu[:, EXPERT_FF:], -SWIGLU_LIMIT, SWIGLU_LIMIT)
        act = x_glu * jax.nn.sigmoid(SWIGLU_ALPHA * x_glu) * (x_lin + 1.0)
        y_ref[...] = (_dot(act.astype(BF16), wd_bf[...]) + bd_ref[...]).astype(BF16)

    @pl.when(i >= na_ref[0])
    def _():
        y_ref[...] = jnp.zeros(y_ref.shape, BF16)


def _experts(blk_expert, n_active, xg, layer, w_gu, b_gu, w_down, b_down):
    n_rows, d = xg.shape
    n_l, n_e, _, ff2 = w_gu.shape
    tb = MOE_ROWS
    n_blocks = n_rows // tb
    return pl.pallas_call(
        _expert_kernel,
        grid_spec=pltpu.PrefetchScalarGridSpec(
            num_scalar_prefetch=2,
            grid=(n_blocks,),
            in_specs=[pl.BlockSpec((tb, d), lambda i, be, na: (i, 0)),
                      pl.BlockSpec((None, None, d, ff2), lambda i, be, na: (layer, be[i], 0, 0)),
                      pl.BlockSpec((None, None, 1, ff2), lambda i, be, na: (layer, be[i], 0, 0)),
                      pl.BlockSpec((None, None, ff2 // 2, d), lambda i, be, na: (layer, be[i], 0, 0)),
                      pl.BlockSpec((None, None, 1, d), lambda i, be, na: (layer, be[i], 0, 0))],
            out_specs=pl.BlockSpec((tb, d), lambda i, be, na: (i, 0)),
            scratch_shapes=[pltpu.VMEM((d, ff2), BF16), pltpu.VMEM((ff2 // 2, d), BF16)]),
        out_shape=jax.ShapeDtypeStruct((n_rows, d), BF16),
        compiler_params=_cparams(("arbitrary",)),
        name="moe_experts",
    )(blk_expert, n_active, xg, w_gu, b_gu.reshape(n_l, n_e, 1, ff2), w_down, b_down.reshape(n_l, n_e, 1, d))


def _combine_kernel(x_ref, y_ref, gate_ref, npost_ref, g2_ref, o_ref):
    o_ref[...] = _moe_residual(x_ref, y_ref, gate_ref, npost_ref, g2_ref)


def _combine(x1, yk, gates, n_post, g2):
    b, s, d = x1.shape
    tm = ROW_TILE
    row = lambda bi, i: (bi, i, 0)
    return pl.pallas_call(
        _combine_kernel,
        grid=(b, s // tm),
        in_specs=[pl.BlockSpec((None, tm, d), row),
                  pl.BlockSpec((TOP_K, None, tm, d), lambda bi, i: (0, bi, i, 0)),
                  pl.BlockSpec((None, tm, LOGIT_PAD), row),
                  pl.BlockSpec((1, d), lambda bi, i: (0, 0)),
                  pl.BlockSpec((None, 1, d), lambda bi, i: (bi, 0, 0))],
        out_specs=pl.BlockSpec((None, tm, d), row),
        out_shape=jax.ShapeDtypeStruct((b, s, d), F32),
        compiler_params=_cparams(("parallel", "parallel")),
        name="moe_combine_residual",
    )(x1, yk, gates, n_post.reshape(1, d), g2)


def _lookup(table, idx):
    n = table.shape[0]
    return jnp.sum(jnp.where(idx[:, None] == jnp.arange(n, dtype=jnp.int32)[None, :], table[None, :], 0), axis=1)


def _rows(a, idx):
    return a.at[idx].get(mode="promise_in_bounds")


def _moe_rows(hf, top_idx, rank, hist, layer, w_gu, b_gu, w_down, b_down):
    b, s, d = hf.shape
    n_tok = b * s
    tb = MOE_ROWS
    n_assign = n_tok * TOP_K
    assert N_EXPERTS * n_assign < 2 ** 31
    i32 = jnp.int32
    e_flat = top_idx[..., :TOP_K].reshape(-1)
    skey = jnp.sort(e_flat * n_assign + jnp.arange(n_assign, dtype=i32))
    order = skey % n_assign
    tile_hist = hist[:, :, 0, :N_EXPERTS].reshape(-1, N_EXPERTS).astype(i32)
    tile_base = jnp.cumsum(tile_hist, axis=0) - tile_hist
    counts = jnp.sum(tile_hist, axis=0)
    start = jnp.cumsum(counts) - counts
    padded = (counts + tb - 1) // tb * tb
    pend = jnp.cumsum(padded)
    pstart = pend - padded
    n_rows = -(-n_assign // tb) * tb + N_EXPERTS * tb
    n_blocks = n_rows // tb
    blk_start = jnp.arange(n_blocks, dtype=i32) * tb
    blk_expert = jnp.minimum(jnp.sum((pend[None, :] <= blk_start[:, None]).astype(i32), axis=1), N_EXPERTS - 1)
    n_active = (pend[-1] // tb).astype(i32).reshape(1)
    j = jnp.arange(n_rows, dtype=i32) - jnp.repeat(_lookup(pstart, blk_expert), tb)
    src = jnp.repeat(_lookup(start, blk_expert), tb) + j
    valid = j < jnp.repeat(_lookup(counts, blk_expert), tb)
    row_tok = jnp.where(valid, _rows(order, jnp.clip(src, 0, n_assign - 1)) // TOP_K, 0)
    base_tok = jnp.repeat(pstart[None, :] + tile_base, ROW_TILE * TOP_K, axis=0)
    onehot = e_flat[:, None] == jnp.arange(N_EXPERTS, dtype=i32)[None, :]
    pos = jnp.sum(jnp.where(onehot, base_tok, 0), axis=1) + rank[..., :TOP_K].reshape(-1)
    xg = _rows(hf.reshape(n_tok, d), row_tok)
    y = _experts(blk_expert, n_active, xg, layer, w_gu, b_gu, w_down, b_down)
    return _rows(y, pos.reshape(n_tok, TOP_K).T.reshape(-1)).reshape(TOP_K, b, s, d)


def _block_diag(n, blk, val):
    return (jnp.kron(jnp.eye(n // blk, dtype=F32), jnp.ones((blk, blk), F32)) * val).astype(BF16)


def kernel(x, c, ctx, c_ctx, ada_w, ada_b, norm_mix_pre, norm_mix_post, norm_ffn_pre, norm_ffn_post, router_w, router_b, moe_w_gu, moe_b_gu, moe_w_down, moe_b_down, hy_w_in, mla_q_norm, mla_w_uq, mla_kv_norm, mla_w_ukv, rwkv_mu_prev, rwkv_mu_next, rwkv_w0, rwkv_w2, rwkv_a0, rwkv_a2, rwkv_g2, rwkv_k_k, rwkv_k_a, rwkv_r_k, rwkv_ln_w, rwkv_ln_b, hy_w_out, gm_w_in, gm_v_norm_w, gm_v_norm_b, gm_w_s, gm_b_s, gm_w_out):
    b, s, d = x.shape
    n_ctx = ctx.shape[1]
    t_all = s + n_ctx
    assert b + 1 <= 8 and s % ROW_TILE == 0 and n_ctx % ROW_TILE == 0

    cond_rows = jnp.concatenate([c, c_ctx[None], jnp.zeros((8 - b - 1, d), F32)], axis=0)
    mod = _modulation(cond_rows, ada_w, ada_b)

    def lat_mod(l, j):
        return mod[l, :b, j * d:(j + 1) * d].reshape(b, 1, d)

    sh_all = jnp.stack([mod[0, :b, 0:d], jnp.broadcast_to(mod[0, b, 0:d], (b, d))], axis=1).reshape(2 * b, 1, d)
    sc_all = jnp.stack([mod[0, :b, d:2 * d], jnp.broadcast_to(mod[0, b, d:2 * d], (b, d))], axis=1).reshape(2 * b, 1, d)
    cos, sin = _rope_tables(s, n_ctx)
    w1, wq2, wk2 = _inproj_weights(hy_w_in[0], mla_w_uq[0], mla_w_ukv[0])
    q, k, v, prw = _inproj(x, ctx, norm_mix_pre[0], sh_all, sc_all, cos, sin, w1, mla_q_norm[0], mla_kv_norm[0],
                           wq2, wk2)
    tq = 512 if s % 512 == 0 else ROW_TILE
    tk = next((t for t in (8320, 1280) if t_all % t == 0), ROW_TILE)
    o_attn = _attention(q, k, v, s, tq, tk)

    bd_ones = _block_diag(RWKV_DIM, RWKV_HEAD, 1.0)
    bd_mean = _block_diag(RWKV_DIM, RWKV_HEAD, 1.0 / RWKV_HEAD)
    r, kx, vx, g, kkn, lw, kd, bdir = _rwkv_prepare(prw, s, rwkv_mu_prev[0], rwkv_mu_next[0], rwkv_w0[0], rwkv_w2[0],
                                                    rwkv_a0[0], rwkv_a2[0], rwkv_g2[0], rwkv_k_k[0], rwkv_k_a[0],
                                                    bd_ones)
    m_c, n_c, r_c, y_c = _wkv_chunks(r, vx, kkn, lw, kd, bdir)
    n_lat_ch = s // WKV_CHUNK
    n_ctx_ch = n_ctx // WKV_CHUNK
    lat_ch = np.arange(n_lat_ch)
    ctx_ch = n_lat_ch + np.arange(n_ctx_ch)
    order = jnp.asarray(np.stack([np.concatenate([ctx_ch, lat_ch]),
                                  np.concatenate([ctx_ch[::-1], lat_ch[::-1]])]).astype(np.int32))
    yf, yb = _wkv_states(order, m_c, n_c, r_c, y_c, t_all)

    rw_pad, rb_pad = _router_pads(router_w[0], router_b[0])
    x1, hf, top_idx, gates, rank, hist = _mix_out(yf, yb, r, kx, vx, g, o_attn, x, rwkv_r_k[0].reshape(-1), rwkv_ln_w[0],
                                      rwkv_ln_b[0], bd_mean, bd_ones, hy_w_out[0].astype(BF16), norm_mix_post[0],
                                      lat_mod(0, 2), norm_ffn_pre[0], lat_mod(0, 3), lat_mod(0, 4), rw_pad, rb_pad)
    yk = _moe_rows(hf, top_idx, rank, hist, 0, moe_w_gu, moe_b_gu, moe_w_down, moe_b_down)

    rw_pad, rb_pad = _router_pads(router_w[1], router_b[1])
    x3, hf, top_idx, gates, rank, hist = _gmlp(x1, yk, gates, norm_ffn_post[0], lat_mod(0, 5), norm_mix_pre[1], lat_mod(1, 0), lat_mod(1, 1), gm_w_in[0], gm_v_norm_w[0],
                                   gm_v_norm_b[0], gm_w_s[0], gm_b_s[0], gm_w_out[0], norm_mix_post[1], lat_mod(1, 2),
                                   norm_ffn_pre[1], lat_mod(1, 3), lat_mod(1, 4), rw_pad, rb_pad)
    yk = _moe_rows(hf, top_idx, rank, hist, 1, moe_w_gu, moe_b_gu, moe_w_down, moe_b_down)
    return _combine(x3, yk, gates, norm_ffn_post[1], lat_mod(1, 5))
```

```python
import functools

import jax
import jax.numpy as jnp
import numpy as np
from jax import lax
from jax.experimental import pallas as pl
from jax.experimental.pallas import tpu as pltpu

F32 = jnp.float32
BF16 = jnp.bfloat16
HIGHEST = lax.Precision.HIGHEST

D_MODEL = 1024
GRID_W = 64
EPS = 1e-6

MLA_HEADS = 8
MLA_NOPE = 64
MLA_ROPE = 32
MLA_V = 64
MLA_Q_RANK = 256
MLA_KV_RANK = 128
MLA_SCALE = (MLA_NOPE + MLA_ROPE) ** -0.5
ROPE_FREQS = MLA_ROPE // 4
ROPE_BASE = 10000.0
HEAD_PAD = 128
ATT_QSUB = 128
ATT_KSUB = 640
LOG2E = 1.4426950408889634
ATT_EXP_DTYPE = jnp.bfloat16
V_EXT = 80

RWKV_HEAD = 64
RWKV_DIM = 512
RWKV_HEADS = 8
DECAY_LORA = 32
AAA_LORA = 32
GATE_LORA = 96
RWKV_PAD_COLS = 3 * RWKV_DIM + 128 + 128
RWKV_LN_EPS = 64e-5
WKV_CHUNK = 64
WKV_CHUNKS_PER_STEP = 4

CHUNK = 128
GM_WIDTH = 1024
GM_GROUPS = 8
LN_EPS = 1e-5

N_EXPERTS = 32
TOP_K = 4
EXPERT_FF = 1024
SWIGLU_LIMIT = 7.0
SWIGLU_ALPHA = 1.702
LOGIT_PAD = 128
NEG_BIG = -1e30

ROW_TILE = 256
TOKEN_TILE = 512
VMEM_LIMIT = 48 * 1024 * 1024


def _cparams(sem):
    return pltpu.CompilerParams(dimension_semantics=sem, vmem_limit_bytes=VMEM_LIMIT)


def _token_tile(n_rows):
    return TOKEN_TILE if n_rows % TOKEN_TILE == 0 else ROW_TILE


def _rms(x):
    return x * lax.rsqrt(jnp.mean(x * x, axis=-1, keepdims=True) + EPS)


def _dot(a, b, precision=None):
    return jnp.dot(a, b, preferred_element_type=F32, precision=precision)


def _dot_nt(a, b, precision=None):
    return lax.dot_general(a, b, (((1,), (1,)), ((), ())), preferred_element_type=F32, precision=precision)


def _dot_tn(a, b, precision=None):
    return lax.dot_general(a, b, (((0,), (0,)), ((), ())), preferred_element_type=F32, precision=precision)


def _split(x):
    hi = x.astype(BF16)
    return hi, (x - hi.astype(F32)).astype(BF16)


def _dot_split(a, b):
    a_hi, a_lo = _split(a)
    b_hi, b_lo = _split(b)
    return _dot(jnp.concatenate([a_hi, a_hi, a_lo], axis=1), jnp.concatenate([b_hi, b_lo, b_hi], axis=0))


def _mod_kernel(s_ref, w_ref, b_ref, o_ref):
    s = s_ref[...]
    s = s * jax.nn.sigmoid(s)
    o_ref[...] = _dot(s, w_ref[...], HIGHEST) + b_ref[...]


def _modulation(cond_rows, ada_w, ada_b):
    n_l, d, n6 = ada_w.shape
    tn = 1536
    return pl.pallas_call(
        _mod_kernel,
        grid=(n_l, n6 // tn),
        in_specs=[pl.BlockSpec((8, d), lambda l, j: (0, 0)),
                  pl.BlockSpec((None, d, tn), lambda l, j: (l, 0, j)),
                  pl.BlockSpec((None, 1, tn), lambda l, j: (l, 0, j))],
        out_specs=pl.BlockSpec((None, 8, tn), lambda l, j: (l, 0, j)),
        out_shape=jax.ShapeDtypeStruct((n_l, 8, n6), F32),
        compiler_params=_cparams(("arbitrary", "arbitrary")),
        name="adaln_mod",
    )(cond_rows, ada_w, ada_b.reshape(n_l, 1, n6))


def _inproj_kernel(x_ref, ctx_ref, g_ref, sh_ref, sc_ref, cos_ref, sin_ref, w1_ref, qn_ref, kvn_ref, wq_ref, wk_ref,
                   q_ref, k_ref, vt_ref, prw_ref, *, n_lat_tiles):
    x_in = jnp.where(pl.program_id(1) >= n_lat_tiles, ctx_ref[...], x_ref[...])
    h = _rms(x_in) * g_ref[...]
    h = h * (1.0 + sc_ref[...]) + sh_ref[...]
    p = _dot(h.astype(BF16), w1_ref[...])
    cos = jnp.concatenate([cos_ref[...]] * MLA_HEADS, axis=1)
    sin = jnp.concatenate([sin_ref[...]] * MLA_HEADS, axis=1)
    nq = MLA_HEADS * HEAD_PAD
    qn = _rms(p[:, :MLA_Q_RANK]) * qn_ref[...]
    qq = _dot(qn.astype(BF16), wq_ref[...])
    q_ref[...] = (qq[:, :nq] * cos + qq[:, nq:] * sin).astype(BF16)
    kvn = _rms(p[:, MLA_Q_RANK:MLA_Q_RANK + MLA_KV_RANK]) * kvn_ref[...]
    x2 = jnp.concatenate([kvn, p[:, MLA_Q_RANK + MLA_KV_RANK:512]], axis=1).astype(BF16)
    kk = _dot(x2, wk_ref[...])
    k_ref[...] = (kk[:, :nq] * cos + kk[:, nq:2 * nq] * sin).astype(BF16)
    vt = kk[:, 2 * nq:].T.astype(BF16)
    ones = jnp.ones((V_EXT - MLA_V, vt.shape[1]), BF16)
    for hd in range(MLA_HEADS):
        vt_ref[hd * V_EXT:hd * V_EXT + MLA_V, :] = vt[hd * MLA_V:(hd + 1) * MLA_V, :]
        vt_ref[hd * V_EXT + MLA_V:(hd + 1) * V_EXT, :] = ones
    prw_ref[...] = p[:, 512:]


def _rotate_half_cols(w):
    wr = w.reshape(w.shape[:-1] + (2, 2, ROPE_FREQS))
    return jnp.stack([-wr[..., 1, :], wr[..., 0, :]], axis=-2).reshape(w.shape)


def _inproj_weights(hy_w_in, w_uq, w_ukv):
    d = hy_w_in.shape[0]
    mla_cols = MLA_Q_RANK + MLA_KV_RANK + MLA_ROPE
    w_mla = hy_w_in[:, :mla_cols]
    w_rw = hy_w_in[:, mla_cols:]
    w1 = jnp.concatenate([w_mla, jnp.zeros((d, 512 - mla_cols), F32), _pad_rwkv_cols(w_rw)], axis=1).astype(BF16)
    wq = (w_uq * (MLA_SCALE * LOG2E)).reshape(MLA_Q_RANK, MLA_HEADS, MLA_NOPE + MLA_ROPE)
    z_n = jnp.zeros((MLA_Q_RANK, MLA_HEADS, MLA_NOPE), F32)
    z_p = jnp.zeros((MLA_Q_RANK, MLA_HEADS, HEAD_PAD - MLA_NOPE - MLA_ROPE), F32)
    wq_plain = jnp.concatenate([wq[..., :MLA_NOPE], wq[..., MLA_NOPE:], z_p], axis=-1)
    wq_rot = jnp.concatenate([z_n, _rotate_half_cols(wq[..., MLA_NOPE:]), z_p], axis=-1)
    wq2 = jnp.concatenate([wq_plain.reshape(MLA_Q_RANK, -1), wq_rot.reshape(MLA_Q_RANK, -1)], axis=1).astype(BF16)
    wkv = w_ukv.reshape(MLA_KV_RANK, MLA_HEADS, MLA_NOPE + MLA_V)
    eye = jnp.broadcast_to(jnp.eye(MLA_ROPE, dtype=F32)[:, None, :], (MLA_ROPE, MLA_HEADS, MLA_ROPE))
    pad_k = HEAD_PAD - MLA_NOPE - MLA_ROPE
    top_plain = jnp.concatenate([wkv[..., :MLA_NOPE], jnp.zeros((MLA_KV_RANK, MLA_HEADS, MLA_ROPE + pad_k), F32)], -1)
    mid_plain = jnp.concatenate([jnp.zeros((MLA_ROPE, MLA_HEADS, MLA_NOPE), F32), eye,
                                 jnp.zeros((MLA_ROPE, MLA_HEADS, pad_k), F32)], -1)
    mid_rot = jnp.concatenate([jnp.zeros((MLA_ROPE, MLA_HEADS, MLA_NOPE), F32), _rotate_half_cols(eye),
                               jnp.zeros((MLA_ROPE, MLA_HEADS, pad_k), F32)], -1)
    nq = MLA_HEADS * HEAD_PAD
    rows_c = jnp.concatenate([top_plain.reshape(MLA_KV_RANK, nq), jnp.zeros((MLA_KV_RANK, nq), F32),
                              wkv[..., MLA_NOPE:].reshape(MLA_KV_RANK, MLA_HEADS * MLA_V)], axis=1)
    rows_r = jnp.concatenate([mid_plain.reshape(MLA_ROPE, nq), mid_rot.reshape(MLA_ROPE, nq),
                              jnp.zeros((MLA_ROPE, MLA_HEADS * MLA_V), F32)], axis=1)
    rows_z = jnp.zeros((256 - MLA_KV_RANK - MLA_ROPE, rows_c.shape[1]), F32)
    wk2 = jnp.concatenate([rows_c, rows_r, rows_z], axis=0).astype(BF16)
    return w1, wq2, wk2


def _pad_rwkv_cols(w):
    pad = jnp.zeros(w.shape[:-1] + (RWKV_PAD_COLS - w.shape[-1],), w.dtype)
    return jnp.concatenate([w, pad], axis=-1)


def _rope_tables(n_lat, n_ctx):
    t = jnp.arange(n_lat)
    row = (t // GRID_W).astype(F32)
    col = (t % GRID_W).astype(F32)
    inv = ROPE_BASE ** (-jnp.arange(ROPE_FREQS, dtype=F32) / ROPE_FREQS)
    ang = jnp.stack([row[:, None] * inv, col[:, None] * inv], axis=1)
    ang = jnp.broadcast_to(ang[:, :, None, :], (n_lat, 2, 2, ROPE_FREQS)).reshape(n_lat, MLA_ROPE)
    cos = jnp.concatenate([jnp.ones((n_lat, MLA_NOPE), F32), jnp.cos(ang),
                           jnp.ones((n_lat, HEAD_PAD - MLA_NOPE - MLA_ROPE), F32)], axis=1)
    sin = jnp.concatenate([jnp.zeros((n_lat, MLA_NOPE), F32), jnp.sin(ang),
                           jnp.zeros((n_lat, HEAD_PAD - MLA_NOPE - MLA_ROPE), F32)], axis=1)
    cos = jnp.concatenate([cos, jnp.ones((n_ctx, HEAD_PAD), F32)], axis=0)
    sin = jnp.concatenate([sin, jnp.zeros((n_ctx, HEAD_PAD), F32)], axis=0)
    return cos, sin


def _inproj(x, ctx, gain, shift, scale, cos, sin, w1, q_norm, kv_norm, wq2, wk2):
    b, n_lat, d = x.shape
    t_all = n_lat + ctx.shape[1]
    tm = ROW_TILE
    n_lat_tiles = n_lat // tm
    nq = MLA_HEADS * HEAD_PAD

    def mod_map(bi, i):
        return (2 * bi + jnp.where(i >= n_lat_tiles, 1, 0), 0, 0)

    const = lambda bi, i: (0, 0)
    row = lambda bi, i: (bi, i, 0)
    return pl.pallas_call(
        functools.partial(_inproj_kernel, n_lat_tiles=n_lat_tiles),
        grid=(b, t_all // tm),
        in_specs=[pl.BlockSpec((None, tm, d), lambda bi, i: (bi, jnp.minimum(i, n_lat_tiles - 1), 0)),
                  pl.BlockSpec((None, tm, d), lambda bi, i: (bi, jnp.maximum(i - n_lat_tiles, 0), 0)),
                  pl.BlockSpec((1, d), const),
                  pl.BlockSpec((None, 1, d), mod_map),
                  pl.BlockSpec((None, 1, d), mod_map),
                  pl.BlockSpec((tm, HEAD_PAD), lambda bi, i: (i, 0)),
                  pl.BlockSpec((tm, HEAD_PAD), lambda bi, i: (i, 0)),
                  pl.BlockSpec(w1.shape, const),
                  pl.BlockSpec((1, MLA_Q_RANK), const),
                  pl.BlockSpec((1, MLA_KV_RANK), const),
                  pl.BlockSpec(wq2.shape, const),
                  pl.BlockSpec(wk2.shape, const)],
        out_specs=[pl.BlockSpec((None, tm, nq), row),
                   pl.BlockSpec((None, tm, nq), row),
                   pl.BlockSpec((None, MLA_HEADS * V_EXT, tm), lambda bi, i: (bi, 0, i)),
                   pl.BlockSpec((None, tm, RWKV_PAD_COLS), row)],
        out_shape=[jax.ShapeDtypeStruct((b, t_all, nq), BF16),
                   jax.ShapeDtypeStruct((b, t_all, nq), BF16),
                   jax.ShapeDtypeStruct((b, MLA_HEADS * V_EXT, t_all), BF16),
                   jax.ShapeDtypeStruct((b, t_all, RWKV_PAD_COLS), F32)],
        compiler_params=_cparams(("parallel", "parallel")),
        name="hybrid_inproj",
    )(x, ctx, gain.reshape(1, d), shift, scale, cos, sin, w1, q_norm.reshape(1, -1), kv_norm.reshape(1, -1), wq2, wk2)


def _attn_kernel(q_ref, k_ref, vt_ref, o_ref, m_sc, acc_sc):
    ki = pl.program_id(3)

    @pl.when(ki == 0)
    def _():
        m_sc[...] = jnp.full(m_sc.shape, -jnp.inf, F32)
        acc_sc[...] = jnp.zeros(acc_sc.shape, F32)

    tq, tk = q_ref.shape[0], k_ref.shape[0]
    chains = [(h, qs) for h in range(2) for qs in range(tq // ATT_QSUB)]
    hsl = [slice(h * HEAD_PAD, (h + 1) * HEAD_PAD) for h, _ in chains]
    vsl = [slice(h * V_EXT, (h + 1) * V_EXT) for h, _ in chains]
    qsl = [slice(qs * ATT_QSUB, (qs + 1) * ATT_QSUB) for _, qs in chains]
    nch = len(chains)
    q = [q_ref[qsl[c], hsl[c]] for c in range(nch)]
    m = [m_sc[chains[c][0], 0:1, qsl[c]] for c in range(nch)]
    acc = [acc_sc[chains[c][0], :, qsl[c]] for c in range(nch)]
    ksub = ATT_KSUB if tk % ATT_KSUB == 0 else tk
    for kb in range(tk // ksub):
        ksl = slice(kb * ksub, (kb + 1) * ksub)
        s = [_dot_nt(k_ref[ksl, hsl[c]], q[c]).astype(ATT_EXP_DTYPE) for c in range(nch)]
        m_new = [jnp.maximum(m[c], jnp.max(s[c], axis=0, keepdims=True).astype(F32)) for c in range(nch)]
        alpha = [jnp.exp2(m[c] - m_new[c]) for c in range(nch)]
        p = [jnp.exp2(s[c] - m_new[c].astype(ATT_EXP_DTYPE)).astype(BF16) for c in range(nch)]
        acc = [acc[c] * alpha[c] + _dot(vt_ref[vsl[c], ksl], p[c]) for c in range(nch)]
        m = m_new
    for c in range(nch):
        h = chains[c][0]
        m_sc[h, :, qsl[c]] = jnp.broadcast_to(m[c], (8, ATT_QSUB))
        acc_sc[h, :, qsl[c]] = acc[c]

    @pl.when(ki == pl.num_programs(3) - 1)
    def _():
        o0 = (acc_sc[0, :MLA_V, :] / acc_sc[0, MLA_V:MLA_V + 1, :]).T
        o1 = (acc_sc[1, :MLA_V, :] / acc_sc[1, MLA_V:MLA_V + 1, :]).T
        o_ref[...] = jnp.concatenate([o0, o1], axis=1).astype(BF16)


def _attention(q, k, vt, n_lat, tq, tk):
    b, t_all, _ = k.shape
    return pl.pallas_call(
        _attn_kernel,
        grid=(b, MLA_HEADS // 2, n_lat // tq, t_all // tk),
        in_specs=[pl.BlockSpec((None, tq, 2 * HEAD_PAD), lambda bi, hp, qi, ki: (bi, qi, hp)),
                  pl.BlockSpec((None, tk, 2 * HEAD_PAD), lambda bi, hp, qi, ki: (bi, ki, hp)),
                  pl.BlockSpec((None, 2 * V_EXT, tk), lambda bi, hp, qi, ki: (bi, hp, ki))],
        out_specs=pl.BlockSpec((None, tq, 2 * MLA_V), lambda bi, hp, qi, ki: (bi, qi, hp)),
        out_shape=jax.ShapeDtypeStruct((b, n_lat, MLA_HEADS * MLA_V), BF16),
        scratch_shapes=[pltpu.VMEM((2, 8, tq), F32), pltpu.VMEM((2, V_EXT, tq), F32)],
        compiler_params=_cparams(("parallel", "parallel", "parallel", "arbitrary")),
        name="mla_attention",
    )(q, k, vt)


def _seg_sum(x, bd_ref):
    hi = x.astype(BF16)
    lo = (x - hi.astype(F32)).astype(BF16)
    bd = bd_ref[...]
    return _dot(hi, bd) + _dot(lo, bd)


def _rwkv_prep_kernel(p_ref, prev_ref, next_ref, mup_ref, mun_ref, wl_ref, w0a0_ref, g2_ref, kk_ref, ka_ref, bd_ref,
                      r_ref, k_ref, v_ref, g_ref, kkn_ref, lw_ref, kd_ref, bdir_ref, *, n_lat_tiles, n_tiles):
    i = pl.program_id(1)
    p = p_ref[...]
    tm = p.shape[0]
    first = jnp.logical_or(i == 0, i == n_lat_tiles)
    last = jnp.logical_or(i == n_lat_tiles - 1, i == n_tiles - 1)
    prev_row = jnp.where(first, 0.0, prev_ref[7:8, :])
    next_row = jnp.where(last, 0.0, next_ref[0:1, :])
    ridx = lax.broadcasted_iota(jnp.int32, p.shape, 0)
    prev = jnp.where(ridx == 0, prev_row, pltpu.roll(p, 1, 0))
    nxt = jnp.where(ridx == tm - 1, next_row, pltpu.roll(p, tm - 1, 0))
    p = p + mup_ref[...] * (prev - p) + mun_ref[...] * (nxt - p)
    c = RWKV_DIM
    r, k, v = p[:, :c], p[:, c:2 * c], p[:, 2 * c:3 * c]
    lo = p[:, 3 * c:3 * c + 128]
    lane = lax.broadcasted_iota(jnp.int32, lo.shape, 1)
    lo = jnp.where(lane < 2 * DECAY_LORA, jnp.tanh(lo), lo)
    wa = _dot(lo.astype(BF16), wl_ref[...]) + w0a0_ref[...]
    gd = p[:, 3 * c + 128:]
    g_ref[...] = _dot(jax.nn.sigmoid(gd).astype(BF16), g2_ref[...]).astype(BF16)
    kk = k * kk_ref[...]
    kk = kk * lax.rsqrt(jnp.maximum(_seg_sum(kk * kk, bd_ref), 1e-24))
    r_ref[...] = r.astype(BF16)
    k_ref[...] = k.astype(BF16)
    v_ref[...] = v.astype(BF16)
    kkn_ref[...] = kk.astype(BF16)
    ka = ka_ref[...]
    for d in range(2):
        w = wa[:, d * c:(d + 1) * c]
        a = jax.nn.sigmoid(wa[:, (2 + d) * c:(3 + d) * c])
        lw_ref[:, d * c:(d + 1) * c] = -float(np.exp(-0.5)) * jax.nn.sigmoid(w)
        kd_ref[:, d * c:(d + 1) * c] = (k * (1.0 + (a - 1.0) * ka)).astype(BF16)
        bdir_ref[:, d * c:(d + 1) * c] = (kk * a).astype(BF16)


def _rwkv_prepare(prw, n_lat, mu_prev, mu_next, w0, w2, a0, a2, g2, k_k, k_a, bd_ones):
    b, t_all, pc = prw.shape
    tm = ROW_TILE
    n_tiles = t_all // tm
    n_lat_tiles = n_lat // tm
    c = RWKV_DIM
    z = jnp.zeros((DECAY_LORA, c), F32)
    wl = jnp.concatenate([
        jnp.concatenate([w2[0], z, z, z], axis=1), jnp.concatenate([z, w2[1], z, z], axis=1),
        jnp.concatenate([z, z, a2[0], z], axis=1), jnp.concatenate([z, z, z, a2[1]], axis=1)], axis=0).astype(BF16)
    w0a0 = jnp.concatenate([w0[0], w0[1], a0[0], a0[1]]).reshape(1, 4 * c)
    g2p = jnp.concatenate([g2, jnp.zeros((128 - GATE_LORA, c), F32)], axis=0).astype(BF16)
    row = lambda bi, i: (bi, i, 0)
    const = lambda bi, i: (0, 0)
    hb = tm // 8
    n_hb = t_all // 8
    kern = functools.partial(_rwkv_prep_kernel, n_lat_tiles=n_lat_tiles, n_tiles=n_tiles)
    o_c = jax.ShapeDtypeStruct((b, t_all, c), BF16)
    o_2c = jax.ShapeDtypeStruct((b, t_all, 2 * c), BF16)
    o_lw = jax.ShapeDtypeStruct((b, t_all, 2 * c), F32)
    return pl.pallas_call(
        kern,
        grid=(b, n_tiles),
        in_specs=[pl.BlockSpec((None, tm, pc), row),
                  pl.BlockSpec((None, 8, pc), lambda bi, i: (bi, jnp.maximum(i * hb - 1, 0), 0)),
                  pl.BlockSpec((None, 8, pc), lambda bi, i: (bi, jnp.minimum((i + 1) * hb, n_hb - 1), 0)),
                  pl.BlockSpec((1, pc), const), pl.BlockSpec((1, pc), const),
                  pl.BlockSpec(wl.shape, const), pl.BlockSpec((1, 4 * c), const),
                  pl.BlockSpec(g2p.shape, const), pl.BlockSpec((1, c), const), pl.BlockSpec((1, c), const),
                  pl.BlockSpec((c, c), const)],
        out_specs=[pl.BlockSpec((None, tm, c), row)] * 5 + [pl.BlockSpec((None, tm, 2 * c), row)] * 3,
        out_shape=[o_c] * 5 + [o_lw, o_2c, o_2c],
        compiler_params=_cparams(("parallel", "parallel")),
        name="rwkv_prepare",
    )(prw, prw, prw, _pad_rwkv_cols(mu_prev).reshape(1, pc), _pad_rwkv_cols(mu_next).reshape(1, pc),
      wl, w0a0, g2p, k_k.reshape(1, c), k_a.reshape(1, c), bd_ones)


def _wkv_chunk_kernel(r_ref, v_ref, kk_ref, lw_ref, kd_ref, bd_ref, m_ref, n_ref, rr_ref, yv_ref):
    d = pl.program_id(1)
    cs = WKV_CHUNK
    nc = r_ref.shape[0] // cs
    ti = lax.broadcasted_iota(jnp.int32, (cs, cs), 0)
    si = lax.broadcasted_iota(jnp.int32, (cs, cs), 1)
    rel = (si - ti) * (1 - 2 * d)
    incl = rel <= 0
    strict = rel < 0
    eye = si == ti
    incl_f = incl.astype(F32)
    ops = []
    for ci in range(nc):
        rows = slice(ci * cs, (ci + 1) * cs)
        lw = lw_ref[rows, :]
        g = _dot(incl_f, lw, HIGHEST)
        total = jnp.sum(lw, axis=0, keepdims=True)
        e_inv = jnp.exp(-g)
        e_end = jnp.exp(total - g)
        gam = jnp.exp(total)
        kd = kd_ref[rows, :]
        bd = bd_ref[rows, :]
        at = -kk_ref[rows, :] * jnp.exp(g - lw)
        rt = r_ref[rows, :] * jnp.exp(g)
        kt = (kd * e_inv).astype(BF16)
        bt = (bd * e_inv).astype(BF16)
        ke = (kd * e_end).astype(BF16)
        be = (bd * e_end).astype(BF16)
        v32 = v_ref[rows, :].astype(F32)
        v = v32.astype(BF16)
        for h in range(RWKV_HEADS):
            sl = slice(h * RWKV_HEAD, (h + 1) * RWKV_HEAD)
            ops.append((at[:, sl], rt[:, sl], kt[:, sl], bt[:, sl], ke[:, sl], be[:, sl], v[:, sl], gam[:, sl], v32[:, sl]))
    n_it = len(ops)
    aa = [_dot_nt(jnp.concatenate([o[0], o[1]], axis=0).astype(BF16), jnp.concatenate([o[3], o[2]], axis=0))
          for o in ops]
    a_ab = [jnp.where(strict, x[:cs, :cs], 0.0) for x in aa]
    a_ak = [jnp.where(strict, x[:cs, cs:], 0.0).astype(BF16) for x in aa]
    ti2 = lax.broadcasted_iota(jnp.int32, (cs, 2 * cs), 0)
    si2 = lax.broadcasted_iota(jnp.int32, (cs, 2 * cs), 1)
    incl2 = (jnp.where(si2 >= cs, si2 - cs, si2) - ti2) * (1 - 2 * d) <= 0
    a_rbk = [jnp.where(incl2, x[cs:, :], 0.0).astype(BF16) for x in aa]
    z = [jnp.concatenate([ops[i][0], _dot(a_ak[i], ops[i][6])], axis=1) for i in range(n_it)]
    pw = a_ab
    n_sq = int(np.log2(cs))
    for it in range(n_sq):
        more = it + 1 < n_sq
        res = [_dot_split(pw[i], jnp.concatenate([z[i], pw[i]], axis=1) if more else z[i]) for i in range(n_it)]
        z = [z[i] + res[i][:, :2 * cs] for i in range(n_it)]
        if more:
            pw = [x[:, 2 * cs:] for x in res]
    zero = jnp.zeros((cs, cs), F32)
    wmat = [jnp.concatenate([z[i], jnp.concatenate([zero, ops[i][8]], axis=1)], axis=0).astype(BF16)
            for i in range(n_it)]
    w2 = [_dot(a_rbk[i], wmat[i]) for i in range(n_it)]
    w3 = [_dot_tn(jnp.concatenate([ops[i][5], ops[i][4]], axis=0), wmat[i]) for i in range(n_it)]
    nh = RWKV_HEADS
    for ci in range(nc):
        rows = slice(ci * cs, (ci + 1) * cs)
        ids = range(ci * nh, (ci + 1) * nh)
        m_ref[rows, :] = jnp.concatenate([jnp.where(eye, ops[i][7], 0.0) + w3[i][:, :cs] for i in ids], axis=1)
        n_ref[rows, :] = jnp.concatenate([w3[i][:, cs:] for i in ids], axis=1)
        rr_ref[rows, :] = jnp.concatenate([ops[i][1] + w2[i][:, :cs] for i in ids], axis=1)
        yv_ref[rows, :] = jnp.concatenate([w2[i][:, cs:] for i in ids], axis=1)


def _wkv_chunks(r, v, kkn, lw, kd, bdir):
    b, t_all, c = r.shape
    rows = WKV_CHUNK * WKV_CHUNKS_PER_STEP
    nst = t_all // rows
    shared = pl.BlockSpec((None, rows, c), lambda bi, d, ci: (bi, ci, 0))
    per_dir = pl.BlockSpec((None, rows, c), lambda bi, d, ci: (bi, ci, d))
    out = pl.BlockSpec((None, None, rows, c), lambda bi, d, ci: (bi, d, ci, 0))
    o_s = jax.ShapeDtypeStruct((b, 2, t_all, c), F32)
    return pl.pallas_call(
        _wkv_chunk_kernel,
        grid=(b, 2, nst),
        in_specs=[shared, shared, shared, per_dir, per_dir, per_dir],
        out_specs=[out] * 4,
        out_shape=[o_s] * 4,
        compiler_params=_cparams(("parallel", "parallel", "parallel")),
        name="wkv_chunk_summaries",
    )(r, v, kkn, lw, kd, bdir)


def _wkv_state_kernel(order_ref, mf_ref, nf_ref, rf_ref, yf_ref, mb_ref, nb_ref, rb_ref, yb_ref,
                      of_ref, ob_ref, st_sc):
    del order_ref
    s = pl.program_id(0)

    @pl.when(s == 0)
    def _():
        st_sc[...] = jnp.zeros(st_sc.shape, F32)

    n_b = st_sc.shape[0]
    ins = ((mf_ref, nf_ref, rf_ref, yf_ref, of_ref), (mb_ref, nb_ref, rb_ref, yb_ref, ob_ref))
    for bi in range(n_b):
        for d in range(2):
            m_r, n_r, r_r, y_r, o_r = ins[d]
            st = st_sc[bi, d]
            m, n, rr, yv = m_r[bi], n_r[bi], r_r[bi], y_r[bi]
            st_hi, st_lo = _split(st)
            m_hi, m_lo = _split(m)
            rr_b = rr.astype(BF16)
            zpad = jnp.zeros((rr.shape[0], 2 * RWKV_HEAD), BF16)
            ys, sts = [], []
            for h in range(RWKV_HEADS):
                sl = slice(h * RWKV_HEAD, (h + 1) * RWKV_HEAD)
                lhs = jnp.concatenate([jnp.concatenate([m_hi[:, sl], m_hi[:, sl], m_lo[:, sl]], axis=1),
                                       jnp.concatenate([rr_b[:, sl], zpad], axis=1)], axis=0)
                rhs = jnp.concatenate([st_hi[:, sl], st_lo[:, sl], st_hi[:, sl]], axis=0)
                res = _dot(lhs, rhs)
                cs = rr.shape[0]
                sts.append(res[:cs] + n[:, sl])
                ys.append(res[cs:] + yv[:, sl])
            o_r[bi] = jnp.concatenate(ys, axis=1)
            st_sc[bi, d] = jnp.concatenate(sts, axis=1)


def _wkv_states(order, m, n, rr, yv, t_all):
    b, _, _, c = m.shape
    cs = WKV_CHUNK
    nch = t_all // cs
    fwd = pl.BlockSpec((b, None, cs, c), lambda s, o: (0, 0, o[0, s], 0))
    bwd = pl.BlockSpec((b, None, cs, c), lambda s, o: (0, 1, o[1, s], 0))
    out_f = pl.BlockSpec((b, cs, c), lambda s, o: (0, o[0, s], 0))
    out_b = pl.BlockSpec((b, cs, c), lambda s, o: (0, o[1, s], 0))
    o_s = jax.ShapeDtypeStruct((b, t_all, c), F32)
    return pl.pallas_call(
        _wkv_state_kernel,
        grid_spec=pltpu.PrefetchScalarGridSpec(
            num_scalar_prefetch=1,
            grid=(nch,),
            in_specs=[fwd] * 4 + [bwd] * 4,
            out_specs=[out_f, out_b],
            scratch_shapes=[pltpu.VMEM((b, 2, cs, c), F32)]),
        out_shape=[o_s, o_s],
        compiler_params=_cparams(("arbitrary",)),
        name="wkv_state_pass",
    )(order, m, n, rr, yv, m, n, rr, yv)


def _ffn_pre(x1, gain_ref, sh_ref, sc_ref, rw_ref, rb_ref, hf_ref, idx_ref, gate_ref):
    hf = _rms(x1) * gain_ref[...]
    hf = hf * (1.0 + sc_ref[...]) + sh_ref[...]
    hf_ref[...] = hf.astype(BF16)
    hf_hi, hf_lo = _split(hf)
    t = _dot(hf_hi, rw_ref[...])
    u = _dot(hf_lo, rw_ref[:, :LOGIT_PAD])
    logits = t[:, :LOGIT_PAD] + t[:, LOGIT_PAD:] + u + rb_ref[...]
    lane = lax.broadcasted_iota(jnp.int32, logits.shape, 1)
    lane_f = lane.astype(F32)
    work = logits
    idx_out = jnp.zeros(logits.shape, F32)
    val_out = jnp.full(logits.shape, NEG_BIG, F32)
    for kth in range(TOP_K):
        m = jnp.max(work, axis=1, keepdims=True)
        idx = jnp.min(jnp.where(work == m, lane_f, float(LOGIT_PAD)), axis=1, keepdims=True)
        work = jnp.where(lane_f == idx, -jnp.inf, work)
        idx_out = jnp.where(lane == kth, idx, idx_out)
        val_out = jnp.where(lane == kth, m, val_out)
    e = jnp.exp(val_out - jnp.max(val_out, axis=1, keepdims=True))
    e = jnp.where(lane < TOP_K, e, 0.0)
    idx_ref[...] = idx_out.astype(jnp.int32)
    gate_ref[...] = e / jnp.sum(e, axis=1, keepdims=True)


def _router_pads(router_w, router_b):
    d = router_w.shape[0]
    rw = jnp.concatenate([router_w, jnp.zeros((d, LOGIT_PAD - N_EXPERTS), F32)], axis=1)
    rb = jnp.concatenate([router_b, jnp.full((LOGIT_PAD - N_EXPERTS,), NEG_BIG, F32)]).reshape(1, LOGIT_PAD)
    return jnp.concatenate(_split(rw), axis=1), rb


def _mix_out_kernel(yf_ref, yb_ref, r_ref, k_ref, v_ref, g_ref, o_ref, x_ref,
                    rk_ref, lnw_ref, lnb_ref, bdm_ref, bds_ref, wout_ref, npost_ref, g1_ref,
                    gpre_ref, sh_ref, sc_ref, rw_ref, rb_ref,
                    x1_ref, hf_ref, idx_ref, gate_ref):
    y = yf_ref[...] + yb_ref[...]
    mu = _seg_sum(y, bdm_ref)
    dlt = y - mu
    var = _seg_sum(dlt * dlt, bdm_ref)
    yn = dlt * lax.rsqrt(var + RWKV_LN_EPS) * lnw_ref[...] + lnb_ref[...]
    bonus = _seg_sum(r_ref[...].astype(F32) * k_ref[...] * rk_ref[...], bds_ref) * v_ref[...]
    rw = (yn + bonus) * g_ref[...]
    mix_in = jnp.concatenate([o_ref[...], rw.astype(BF16)], axis=1)
    mix = _dot(mix_in, wout_ref[...])
    x1 = x_ref[...] + g1_ref[...] * (_rms(mix) * npost_ref[...])
    x1_ref[...] = x1
    _ffn_pre(x1, gpre_ref, sh_ref, sc_ref, rw_ref, rb_ref, hf_ref, idx_ref, gate_ref)


def _token_out_specs(tm, d):
    row = lambda bi, i: (bi, i, 0)
    specs = [pl.BlockSpec((None, tm, d), row), pl.BlockSpec((None, tm, d), row),
             pl.BlockSpec((None, tm, LOGIT_PAD), row), pl.BlockSpec((None, tm, LOGIT_PAD), row)]
    return specs


def _token_out_shapes(b, s, d):
    return [jax.ShapeDtypeStruct((b, s, d), F32), jax.ShapeDtypeStruct((b, s, d), BF16),
            jax.ShapeDtypeStruct((b, s, LOGIT_PAD), jnp.int32), jax.ShapeDtypeStruct((b, s, LOGIT_PAD), F32)]


def _mix_out(yf, yb, r, k, v, g, o_attn, x, r_k, ln_w, ln_b, bd_mean, bd_ones, w_out, n_post, g1,
             n_pre, sh2, sc2, rw_pad, rb_pad):
    b, s, d = x.shape
    c = RWKV_DIM
    tm = _token_tile(s)
    row = lambda bi, i: (bi, i, 0)
    const = lambda bi, i: (0, 0)
    per_b = lambda bi, i: (bi, 0, 0)
    rc = pl.BlockSpec((None, tm, c), row)
    vec_c = pl.BlockSpec((1, c), const)
    vec_d = pl.BlockSpec((1, d), const)
    mod_d = pl.BlockSpec((None, 1, d), per_b)
    return pl.pallas_call(
        _mix_out_kernel,
        grid=(b, s // tm),
        in_specs=[rc] * 7 + [pl.BlockSpec((None, tm, d), row),
                             vec_c, vec_c, vec_c, pl.BlockSpec((c, c), const), pl.BlockSpec((c, c), const),
                             pl.BlockSpec((d, d), const), vec_d, mod_d,
                             vec_d, mod_d, mod_d, pl.BlockSpec((d, 2 * LOGIT_PAD), const),
                             pl.BlockSpec((1, LOGIT_PAD), const)],
        out_specs=_token_out_specs(tm, d),
        out_shape=_token_out_shapes(b, s, d),
        compiler_params=_cparams(("parallel", "parallel")),
        name="mixer_out_router",
    )(yf, yb, r, k, v, g, o_attn, x, r_k.reshape(1, c), ln_w.reshape(1, c), ln_b.reshape(1, c), bd_mean, bd_ones,
      w_out, n_post.reshape(1, d), g1, n_pre.reshape(1, d), sh2, sc2, rw_pad, rb_pad)


def _moe_residual(x_ref, y_ref, gate_ref, npost_ref, g2_ref):
    gate = gate_ref[...]
    f = jnp.zeros(x_ref.shape, F32)
    for kth in range(TOP_K):
        f = f + gate[:, kth:kth + 1] * y_ref[kth].astype(F32)
    return x_ref[...] + g2_ref[...] * (_rms(f) * npost_ref[...])


def _gmlp_kernel(xin_ref, y_ref, gatein_ref, npostin_ref, g2in_ref,
                 gpre1_ref, sh1_ref, sc1_ref, win_ref, vnw_ref, vnb_ref, ws_ref, bs_ref, wout_ref,
                 npost_ref, g1_ref, gpre_ref, sh_ref, sc_ref, rw_ref, rb_ref,
                 x1_ref, hf_ref, idx_ref, gate_ref):
    x = _moe_residual(xin_ref, y_ref, gatein_ref, npostin_ref, g2in_ref)
    h = _rms(x) * gpre1_ref[...]
    h = h * (1.0 + sc1_ref[...]) + sh1_ref[...]
    z = _dot(h.astype(BF16), win_ref[...])
    z = 0.5 * z * (1.0 + lax.erf(z * float(2.0 ** -0.5)))
    u, v = z[:, :GM_WIDTH], z[:, GM_WIDTH:]
    mu = jnp.mean(v, axis=-1, keepdims=True)
    dv = v - mu
    var = jnp.mean(dv * dv, axis=-1, keepdims=True)
    v = (dv * lax.rsqrt(var + LN_EPS) * vnw_ref[...] + vnb_ref[...]).astype(BF16)
    gw = GM_WIDTH // GM_GROUPS
    rows = []
    for ci in range(x.shape[0] // CHUNK):
        cols = []
        for gi in range(GM_GROUPS):
            cols.append(_dot(ws_ref[gi], v[ci * CHUNK:(ci + 1) * CHUNK, gi * gw:(gi + 1) * gw]))
        rows.append(jnp.concatenate(cols, axis=1) + bs_ref[...])
    sp = jnp.concatenate(rows, axis=0)
    y = _dot((u * sp).astype(BF16), wout_ref[...])
    x1 = x + g1_ref[...] * (_rms(y) * npost_ref[...])
    x1_ref[...] = x1
    _ffn_pre(x1, gpre_ref, sh_ref, sc_ref, rw_ref, rb_ref, hf_ref, idx_ref, gate_ref)


def _gmlp(x, yk, gates_in, n_post_in, g2_in, n_pre1, sh1, sc1, w_in, vn_w, vn_b, w_s, b_s, w_out, n_post, g1, n_pre, sh2, sc2, rw_pad, rb_pad):
    b, s, d = x.shape
    tm = _token_tile(s)
    gw = GM_WIDTH // GM_GROUPS
    bs_full = jnp.repeat(b_s.T, gw, axis=1)
    row = lambda bi, i: (bi, i, 0)
    const = lambda bi, i: (0, 0)
    per_b = lambda bi, i: (bi, 0, 0)
    vec_d = pl.BlockSpec((1, d), const)
    vec_g = pl.BlockSpec((1, GM_WIDTH), const)
    mod_d = pl.BlockSpec((None, 1, d), per_b)
    return pl.pallas_call(
        _gmlp_kernel,
        grid=(b, s // tm),
        in_specs=[pl.BlockSpec((None, tm, d), row),
                  pl.BlockSpec((TOP_K, None, tm, d), lambda bi, i: (0, bi, i, 0)),
                  pl.BlockSpec((None, tm, LOGIT_PAD), row), vec_d, mod_d,
                  vec_d, mod_d, mod_d,
                  pl.BlockSpec((d, 2 * GM_WIDTH), const), vec_g, vec_g,
                  pl.BlockSpec((GM_GROUPS, CHUNK, CHUNK), lambda bi, i: (0, 0, 0)),
                  pl.BlockSpec((CHUNK, GM_WIDTH), const), pl.BlockSpec((GM_WIDTH, d), const),
                  vec_d, mod_d, vec_d, mod_d, mod_d,
                  pl.BlockSpec((d, 2 * LOGIT_PAD), const), pl.BlockSpec((1, LOGIT_PAD), const)],
        out_specs=_token_out_specs(tm, d),
        out_shape=_token_out_shapes(b, s, d),
        compiler_params=_cparams(("parallel", "parallel")),
        name="gmlp_router",
    )(x, yk, gates_in, n_post_in.reshape(1, d), g2_in, n_pre1.reshape(1, d), sh1, sc1, w_in.astype(BF16), vn_w.reshape(1, -1), vn_b.reshape(1, -1),
      w_s.astype(BF16), bs_full, w_out.astype(BF16), n_post.reshape(1, d), g1, n_pre.reshape(1, d), sh2, sc2,
      rw_pad, rb_pad)


MOE_ROWS = 256


def _expert_kernel(be_ref, na_ref, x_ref, wgu_ref, bgu_ref, wd_ref, bd_ref, y_ref, wgu_bf, wd_bf):
    i = pl.program_id(0)
    prev = be_ref[jnp.maximum(i - 1, 0)]
    changed = jnp.logical_or(i == 0, be_ref[i] != prev)

    @pl.when(changed)
    def _():
        wgu_bf[...] = wgu_ref[...].astype(BF16)
        wd_bf[...] = wd_ref[...].astype(BF16)

    @pl.when(i < na_ref[0])
    def _():
        gu = _dot(x_ref[...], wgu_bf[...]) + bgu_ref[...]
        x_glu = jnp.minimum(gu[:, :EXPERT_FF], SWIGLU_LIMIT)
        x_lin = jnp.clip(gu[:, EXPERT_FF:], -SWIGLU_LIMIT, SWIGLU_LIMIT)
        act = x_glu * jax.nn.sigmoid(SWIGLU_ALPHA * x_glu) * (x_lin + 1.0)
        y_ref[...] = (_dot(act.astype(BF16), wd_bf[...]) + bd_ref[...]).astype(BF16)

    @pl.when(i >= na_ref[0])
    def _():
        y_ref[...] = jnp.zeros(y_ref.shape, BF16)


def _experts(blk_expert, n_active, xg, layer, w_gu, b_gu, w_down, b_down):
    n_rows, d = xg.shape
    n_l, n_e, _, ff2 = w_gu.shape
    tb = MOE_ROWS
    n_blocks = n_rows // tb
    return pl.pallas_call(
        _expert_kernel,
        grid_spec=pltpu.PrefetchScalarGridSpec(
            num_scalar_prefetch=2,
            grid=(n_blocks,),
            in_specs=[pl.BlockSpec((tb, d), lambda i, be, na: (i, 0)),
                      pl.BlockSpec((None, None, d, ff2), lambda i, be, na: (layer, be[i], 0, 0)),
                      pl.BlockSpec((None, None, 1, ff2), lambda i, be, na: (layer, be[i], 0, 0)),
                      pl.BlockSpec((None, None, ff2 // 2, d), lambda i, be, na: (layer, be[i], 0, 0)),
                      pl.BlockSpec((None, None, 1, d), lambda i, be, na: (layer, be[i], 0, 0))],
            out_specs=pl.BlockSpec((tb, d), lambda i, be, na: (i, 0)),
            scratch_shapes=[pltpu.VMEM((d, ff2), BF16), pltpu.VMEM((ff2 // 2, d), BF16)]),
        out_shape=jax.ShapeDtypeStruct((n_rows, d), BF16),
        compiler_params=_cparams(("arbitrary",)),
        name="moe_experts",
    )(blk_expert, n_active, xg, w_gu, b_gu.reshape(n_l, n_e, 1, ff2), w_down, b_down.reshape(n_l, n_e, 1, d))


def _combine_kernel(x_ref, y_ref, gate_ref, npost_ref, g2_ref, o_ref):
    o_ref[...] = _moe_residual(x_ref, y_ref, gate_ref, npost_ref, g2_ref)


def _combine(x1, yk, gates, n_post, g2):
    b, s, d = x1.shape
    tm = _token_tile(s)
    row = lambda bi, i: (bi, i, 0)
    return pl.pallas_call(
        _combine_kernel,
        grid=(b, s // tm),
        in_specs=[pl.BlockSpec((None, tm, d), row),
                  pl.BlockSpec((TOP_K, None, tm, d), lambda bi, i: (0, bi, i, 0)),
                  pl.BlockSpec((None, tm, LOGIT_PAD), row),
                  pl.BlockSpec((1, d), lambda bi, i: (0, 0)),
                  pl.BlockSpec((None, 1, d), lambda bi, i: (bi, 0, 0))],
        out_specs=pl.BlockSpec((None, tm, d), row),
        out_shape=jax.ShapeDtypeStruct((b, s, d), F32),
        compiler_params=_cparams(("parallel", "parallel")),
        name="moe_combine_residual",
    )(x1, yk, gates, n_post.reshape(1, d), g2)


def _lookup(table, idx):
    n = table.shape[0]
    return jnp.sum(jnp.where(idx[:, None] == jnp.arange(n, dtype=jnp.int32)[None, :], table[None, :], 0), axis=1)


def _rows(a, idx):
    return a.at[idx].get(mode="promise_in_bounds")


def _moe_rows(hf, top_idx, layer, w_gu, b_gu, w_down, b_down):
    b, s, d = hf.shape
    n_tok = b * s
    tb = MOE_ROWS
    n_assign = n_tok * TOP_K
    assert N_EXPERTS * n_assign < 2 ** 31
    i32 = jnp.int32
    e_flat = top_idx[..., :TOP_K].reshape(-1)
    skey = jnp.sort(e_flat * n_assign + jnp.arange(n_assign, dtype=i32))
    order = skey % n_assign
    _, inv = lax.sort((order, jnp.arange(n_assign, dtype=i32)), num_keys=1)
    edges = jnp.arange(N_EXPERTS + 1, dtype=i32) * n_assign
    bounds = jnp.sum((skey[None, :] < edges[:, None]).astype(i32), axis=1)
    start = bounds[:-1]
    counts = bounds[1:] - start
    padded = (counts + tb - 1) // tb * tb
    pend = jnp.cumsum(padded)
    pstart = pend - padded
    n_rows = -(-n_assign // tb) * tb + N_EXPERTS * tb
    n_blocks = n_rows // tb
    blk_start = jnp.arange(n_blocks, dtype=i32) * tb
    blk_expert = jnp.minimum(jnp.sum((pend[None, :] <= blk_start[:, None]).astype(i32), axis=1), N_EXPERTS - 1)
    n_active = (pend[-1] // tb).astype(i32).reshape(1)
    j = jnp.arange(n_rows, dtype=i32) - jnp.repeat(_lookup(pstart, blk_expert), tb)
    src = jnp.repeat(_lookup(start, blk_expert), tb) + j
    valid = j < jnp.repeat(_lookup(counts, blk_expert), tb)
    row_tok = jnp.where(valid, _rows(order, jnp.clip(src, 0, n_assign - 1)) // TOP_K, 0)
    pos = _lookup(pstart - start, e_flat) + inv
    xg = _rows(hf.reshape(n_tok, d), row_tok)
    y = _experts(blk_expert, n_active, xg, layer, w_gu, b_gu, w_down, b_down)
    return _rows(y, pos.reshape(n_tok, TOP_K).T.reshape(-1)).reshape(TOP_K, b, s, d)


def _block_diag(n, blk, val):
    return (jnp.kron(jnp.eye(n // blk, dtype=F32), jnp.ones((blk, blk), F32)) * val).astype(BF16)


def kernel(x, c, ctx, c_ctx, ada_w, ada_b, norm_mix_pre, norm_mix_post, norm_ffn_pre, norm_ffn_post, router_w, router_b, moe_w_gu, moe_b_gu, moe_w_down, moe_b_down, hy_w_in, mla_q_norm, mla_w_uq, mla_kv_norm, mla_w_ukv, rwkv_mu_prev, rwkv_mu_next, rwkv_w0, rwkv_w2, rwkv_a0, rwkv_a2, rwkv_g2, rwkv_k_k, rwkv_k_a, rwkv_r_k, rwkv_ln_w, rwkv_ln_b, hy_w_out, gm_w_in, gm_v_norm_w, gm_v_norm_b, gm_w_s, gm_b_s, gm_w_out):
    b, s, d = x.shape
    n_ctx = ctx.shape[1]
    t_all = s + n_ctx
    assert b + 1 <= 8 and s % ROW_TILE == 0 and n_ctx % ROW_TILE == 0

    cond_rows = jnp.concatenate([c, c_ctx[None], jnp.zeros((8 - b - 1, d), F32)], axis=0)
    mod = _modulation(cond_rows, ada_w, ada_b)

    def lat_mod(l, j):
        return mod[l, :b, j * d:(j + 1) * d].reshape(b, 1, d)

    sh_all = jnp.stack([mod[0, :b, 0:d], jnp.broadcast_to(mod[0, b, 0:d], (b, d))], axis=1).reshape(2 * b, 1, d)
    sc_all = jnp.stack([mod[0, :b, d:2 * d], jnp.broadcast_to(mod[0, b, d:2 * d], (b, d))], axis=1).reshape(2 * b, 1, d)
    cos, sin = _rope_tables(s, n_ctx)
    w1, wq2, wk2 = _inproj_weights(hy_w_in[0], mla_w_uq[0], mla_w_ukv[0])
    q, k, v, prw = _inproj(x, ctx, norm_mix_pre[0], sh_all, sc_all, cos, sin, w1, mla_q_norm[0], mla_kv_norm[0],
                           wq2, wk2)
    tq = 512 if s % 512 == 0 else ROW_TILE
    tk = next((t for t in (8320, 1280) if t_all % t == 0), ROW_TILE)
    o_attn = _attention(q, k, v, s, tq, tk)

    bd_ones = _block_diag(RWKV_DIM, RWKV_HEAD, 1.0)
    bd_mean = _block_diag(RWKV_DIM, RWKV_HEAD, 1.0 / RWKV_HEAD)
    r, kx, vx, g, kkn, lw, kd, bdir = _rwkv_prepare(prw, s, rwkv_mu_prev[0], rwkv_mu_next[0], rwkv_w0[0], rwkv_w2[0],
                                                    rwkv_a0[0], rwkv_a2[0], rwkv_g2[0], rwkv_k_k[0], rwkv_k_a[0],
                                                    bd_ones)
    m_c, n_c, r_c, y_c = _wkv_chunks(r, vx, kkn, lw, kd, bdir)
    n_lat_ch = s // WKV_CHUNK
    n_ctx_ch = n_ctx // WKV_CHUNK
    lat_ch = np.arange(n_lat_ch)
    ctx_ch = n_lat_ch + np.arange(n_ctx_ch)
    order = jnp.asarray(np.stack([np.concatenate([ctx_ch, lat_ch]),
                                  np.concatenate([ctx_ch[::-1], lat_ch[::-1]])]).astype(np.int32))
    yf, yb = _wkv_states(order, m_c, n_c, r_c, y_c, t_all)

    rw_pad, rb_pad = _router_pads(router_w[0], router_b[0])
    x1, hf, top_idx, gates = _mix_out(yf, yb, r, kx, vx, g, o_attn, x, rwkv_r_k[0].reshape(-1), rwkv_ln_w[0],
                                      rwkv_ln_b[0], bd_mean, bd_ones, hy_w_out[0].astype(BF16), norm_mix_post[0],
                                      lat_mod(0, 2), norm_ffn_pre[0], lat_mod(0, 3), lat_mod(0, 4), rw_pad, rb_pad)
    yk = _moe_rows(hf, top_idx, 0, moe_w_gu, moe_b_gu, moe_w_down, moe_b_down)

    rw_pad, rb_pad = _router_pads(router_w[1], router_b[1])
    x3, hf, top_idx, gates = _gmlp(x1, yk, gates, norm_ffn_post[0], lat_mod(0, 5), norm_mix_pre[1], lat_mod(1, 0), lat_mod(1, 1), gm_w_in[0], gm_v_norm_w[0],
                                   gm_v_norm_b[0], gm_w_s[0], gm_b_s[0], gm_w_out[0], norm_mix_post[1], lat_mod(1, 2),
                                   norm_ffn_pre[1], lat_mod(1, 3), lat_mod(1, 4), rw_pad, rb_pad)
    yk = _moe_rows(hf, top_idx, 1, moe_w_gu, moe_b_gu, moe_w_down, moe_b_down)
    return _combine(x3, yk, gates, norm_ffn_post[1], lat_mod(1, 5))
```

```python
import functools

import jax
import jax.numpy as jnp
import numpy as np
from jax import lax
from jax.experimental import pallas as pl
from jax.experimental.pallas import tpu as pltpu

F32 = jnp.float32
BF16 = jnp.bfloat16
HIGHEST = lax.Precision.HIGHEST

D_MODEL = 1024
GRID_W = 64
EPS = 1e-6

MLA_HEADS = 8
MLA_NOPE = 64
MLA_ROPE = 32
MLA_V = 64
MLA_Q_RANK = 256
MLA_KV_RANK = 128
MLA_SCALE = (MLA_NOPE + MLA_ROPE) ** -0.5
ROPE_FREQS = MLA_ROPE // 4
ROPE_BASE = 10000.0
HEAD_PAD = 128
ATT_QSUB = 128
ATT_KSUB = 640
LOG2E = 1.4426950408889634
ATT_EXP_DTYPE = jnp.bfloat16
V_EXT = 80

RWKV_HEAD = 64
RWKV_DIM = 512
RWKV_HEADS = 8
DECAY_LORA = 32
AAA_LORA = 32
GATE_LORA = 96
RWKV_PAD_COLS = 3 * RWKV_DIM + 128 + 128
RWKV_LN_EPS = 64e-5
WKV_CHUNK = 64
WKV_CHUNKS_PER_STEP = (10, 5, 4, 2, 1)

CHUNK = 128
GM_WIDTH = 1024
GM_GROUPS = 8
LN_EPS = 1e-5

N_EXPERTS = 32
TOP_K = 4
EXPERT_FF = 1024
SWIGLU_LIMIT = 7.0
SWIGLU_ALPHA = 1.702
LOGIT_PAD = 128
NEG_BIG = -1e30

ROW_TILE = 256
TOKEN_TILE = 512
MIXOUT_TILE = 1024
VMEM_LIMIT = 48 * 1024 * 1024


def _cparams(sem):
    return pltpu.CompilerParams(dimension_semantics=sem, vmem_limit_bytes=VMEM_LIMIT)


def _token_tile(n_rows):
    return TOKEN_TILE if n_rows % TOKEN_TILE == 0 else ROW_TILE


def _rms(x):
    return x * lax.rsqrt(jnp.mean(x * x, axis=-1, keepdims=True) + EPS)


def _dot(a, b, precision=None):
    return jnp.dot(a, b, preferred_element_type=F32, precision=precision)


def _dot_nt(a, b, precision=None):
    return lax.dot_general(a, b, (((1,), (1,)), ((), ())), preferred_element_type=F32, precision=precision)


def _dot_tn(a, b, precision=None):
    return lax.dot_general(a, b, (((0,), (0,)), ((), ())), preferred_element_type=F32, precision=precision)


def _split(x):
    hi = x.astype(BF16)
    return hi, (x - hi.astype(F32)).astype(BF16)


def _dot_split(a, b):
    a_hi, a_lo = _split(a)
    b_hi, b_lo = _split(b)
    return _dot(jnp.concatenate([a_hi, a_hi, a_lo], axis=1), jnp.concatenate([b_hi, b_lo, b_hi], axis=0))


def _mod_kernel(s_ref, w_ref, b_ref, o_ref):
    s = s_ref[...]
    s = s * jax.nn.sigmoid(s)
    o_ref[...] = _dot(s, w_ref[...], HIGHEST) + b_ref[...]


def _modulation(cond_rows, ada_w, ada_b):
    n_l, d, n6 = ada_w.shape
    tn = 1536
    return pl.pallas_call(
        _mod_kernel,
        grid=(n_l, n6 // tn),
        in_specs=[pl.BlockSpec((8, d), lambda l, j: (0, 0)),
                  pl.BlockSpec((None, d, tn), lambda l, j: (l, 0, j)),
                  pl.BlockSpec((None, 1, tn), lambda l, j: (l, 0, j))],
        out_specs=pl.BlockSpec((None, 8, tn), lambda l, j: (l, 0, j)),
        out_shape=jax.ShapeDtypeStruct((n_l, 8, n6), F32),
        compiler_params=_cparams(("arbitrary", "arbitrary")),
        name="adaln_mod",
    )(cond_rows, ada_w, ada_b.reshape(n_l, 1, n6))


def _inproj_kernel(x_ref, ctx_ref, g_ref, sh_ref, sc_ref, cos_ref, sin_ref, w1_ref, qn_ref, kvn_ref, wq_ref, wk_ref,
                   q_ref, k_ref, vt_ref, prw_ref, *, n_lat_tiles):
    x_in = jnp.where(pl.program_id(1) >= n_lat_tiles, ctx_ref[...], x_ref[...])
    h = _rms(x_in) * g_ref[...]
    h = h * (1.0 + sc_ref[...]) + sh_ref[...]
    p = _dot(h.astype(BF16), w1_ref[...])
    cos = jnp.concatenate([cos_ref[...]] * MLA_HEADS, axis=1)
    sin = jnp.concatenate([sin_ref[...]] * MLA_HEADS, axis=1)
    nq = MLA_HEADS * HEAD_PAD
    qn = _rms(p[:, :MLA_Q_RANK]) * qn_ref[...]
    qq = _dot(qn.astype(BF16), wq_ref[...])
    q_ref[...] = (qq[:, :nq] * cos + qq[:, nq:] * sin).astype(BF16)
    kvn = _rms(p[:, MLA_Q_RANK:MLA_Q_RANK + MLA_KV_RANK]) * kvn_ref[...]
    x2 = jnp.concatenate([kvn, p[:, MLA_Q_RANK + MLA_KV_RANK:512]], axis=1).astype(BF16)
    kk = _dot(x2, wk_ref[...])
    k_ref[...] = (kk[:, :nq] * cos + kk[:, nq:2 * nq] * sin).astype(BF16)
    vt = kk[:, 2 * nq:].T.astype(BF16)
    ones = jnp.ones((V_EXT - MLA_V, vt.shape[1]), BF16)
    for hd in range(MLA_HEADS):
        vt_ref[hd * V_EXT:hd * V_EXT + MLA_V, :] = vt[hd * MLA_V:(hd + 1) * MLA_V, :]
        vt_ref[hd * V_EXT + MLA_V:(hd + 1) * V_EXT, :] = ones
    prw_ref[...] = p[:, 512:]


def _rotate_half_cols(w):
    wr = w.reshape(w.shape[:-1] + (2, 2, ROPE_FREQS))
    return jnp.stack([-wr[..., 1, :], wr[..., 0, :]], axis=-2).reshape(w.shape)


def _inproj_weights(hy_w_in, w_uq, w_ukv):
    d = hy_w_in.shape[0]
    mla_cols = MLA_Q_RANK + MLA_KV_RANK + MLA_ROPE
    w_mla = hy_w_in[:, :mla_cols]
    w_rw = hy_w_in[:, mla_cols:]
    w1 = jnp.concatenate([w_mla, jnp.zeros((d, 512 - mla_cols), F32), _pad_rwkv_cols(w_rw)], axis=1).astype(BF16)
    wq = (w_uq * (MLA_SCALE * LOG2E)).reshape(MLA_Q_RANK, MLA_HEADS, MLA_NOPE + MLA_ROPE)
    z_n = jnp.zeros((MLA_Q_RANK, MLA_HEADS, MLA_NOPE), F32)
    z_p = jnp.zeros((MLA_Q_RANK, MLA_HEADS, HEAD_PAD - MLA_NOPE - MLA_ROPE), F32)
    wq_plain = jnp.concatenate([wq[..., :MLA_NOPE], wq[..., MLA_NOPE:], z_p], axis=-1)
    wq_rot = jnp.concatenate([z_n, _rotate_half_cols(wq[..., MLA_NOPE:]), z_p], axis=-1)
    wq2 = jnp.concatenate([wq_plain.reshape(MLA_Q_RANK, -1), wq_rot.reshape(MLA_Q_RANK, -1)], axis=1).astype(BF16)
    wkv = w_ukv.reshape(MLA_KV_RANK, MLA_HEADS, MLA_NOPE + MLA_V)
    eye = jnp.broadcast_to(jnp.eye(MLA_ROPE, dtype=F32)[:, None, :], (MLA_ROPE, MLA_HEADS, MLA_ROPE))
    pad_k = HEAD_PAD - MLA_NOPE - MLA_ROPE
    top_plain = jnp.concatenate([wkv[..., :MLA_NOPE], jnp.zeros((MLA_KV_RANK, MLA_HEADS, MLA_ROPE + pad_k), F32)], -1)
    mid_plain = jnp.concatenate([jnp.zeros((MLA_ROPE, MLA_HEADS, MLA_NOPE), F32), eye,
                                 jnp.zeros((MLA_ROPE, MLA_HEADS, pad_k), F32)], -1)
    mid_rot = jnp.concatenate([jnp.zeros((MLA_ROPE, MLA_HEADS, MLA_NOPE), F32), _rotate_half_cols(eye),
                               jnp.zeros((MLA_ROPE, MLA_HEADS, pad_k), F32)], -1)
    nq = MLA_HEADS * HEAD_PAD
    rows_c = jnp.concatenate([top_plain.reshape(MLA_KV_RANK, nq), jnp.zeros((MLA_KV_RANK, nq), F32),
                              wkv[..., MLA_NOPE:].reshape(MLA_KV_RANK, MLA_HEADS * MLA_V)], axis=1)
    rows_r = jnp.concatenate([mid_plain.reshape(MLA_ROPE, nq), mid_rot.reshape(MLA_ROPE, nq),
                              jnp.zeros((MLA_ROPE, MLA_HEADS * MLA_V), F32)], axis=1)
    rows_z = jnp.zeros((256 - MLA_KV_RANK - MLA_ROPE, rows_c.shape[1]), F32)
    wk2 = jnp.concatenate([rows_c, rows_r, rows_z], axis=0).astype(BF16)
    return w1, wq2, wk2


def _pad_rwkv_cols(w):
    pad = jnp.zeros(w.shape[:-1] + (RWKV_PAD_COLS - w.shape[-1],), w.dtype)
    return jnp.concatenate([w, pad], axis=-1)


def _rope_tables(n_lat, n_ctx):
    t = jnp.arange(n_lat)
    row = (t // GRID_W).astype(F32)
    col = (t % GRID_W).astype(F32)
    inv = ROPE_BASE ** (-jnp.arange(ROPE_FREQS, dtype=F32) / ROPE_FREQS)
    ang = jnp.stack([row[:, None] * inv, col[:, None] * inv], axis=1)
    ang = jnp.broadcast_to(ang[:, :, None, :], (n_lat, 2, 2, ROPE_FREQS)).reshape(n_lat, MLA_ROPE)
    cos = jnp.concatenate([jnp.ones((n_lat, MLA_NOPE), F32), jnp.cos(ang),
                           jnp.ones((n_lat, HEAD_PAD - MLA_NOPE - MLA_ROPE), F32)], axis=1)
    sin = jnp.concatenate([jnp.zeros((n_lat, MLA_NOPE), F32), jnp.sin(ang),
                           jnp.zeros((n_lat, HEAD_PAD - MLA_NOPE - MLA_ROPE), F32)], axis=1)
    cos = jnp.concatenate([cos, jnp.ones((n_ctx, HEAD_PAD), F32)], axis=0)
    sin = jnp.concatenate([sin, jnp.zeros((n_ctx, HEAD_PAD), F32)], axis=0)
    return cos, sin


def _inproj(x, ctx, gain, shift, scale, cos, sin, w1, q_norm, kv_norm, wq2, wk2):
    b, n_lat, d = x.shape
    t_all = n_lat + ctx.shape[1]
    tm = ROW_TILE
    n_lat_tiles = n_lat // tm
    nq = MLA_HEADS * HEAD_PAD

    def mod_map(bi, i):
        return (2 * bi + jnp.where(i >= n_lat_tiles, 1, 0), 0, 0)

    const = lambda bi, i: (0, 0)
    row = lambda bi, i: (bi, i, 0)
    return pl.pallas_call(
        functools.partial(_inproj_kernel, n_lat_tiles=n_lat_tiles),
        grid=(b, t_all // tm),
        in_specs=[pl.BlockSpec((None, tm, d), lambda bi, i: (bi, jnp.minimum(i, n_lat_tiles - 1), 0)),
                  pl.BlockSpec((None, tm, d), lambda bi, i: (bi, jnp.maximum(i - n_lat_tiles, 0), 0)),
                  pl.BlockSpec((1, d), const),
                  pl.BlockSpec((None, 1, d), mod_map),
                  pl.BlockSpec((None, 1, d), mod_map),
                  pl.BlockSpec((tm, HEAD_PAD), lambda bi, i: (i, 0)),
                  pl.BlockSpec((tm, HEAD_PAD), lambda bi, i: (i, 0)),
                  pl.BlockSpec(w1.shape, const),
                  pl.BlockSpec((1, MLA_Q_RANK), const),
                  pl.BlockSpec((1, MLA_KV_RANK), const),
                  pl.BlockSpec(wq2.shape, const),
                  pl.BlockSpec(wk2.shape, const)],
        out_specs=[pl.BlockSpec((None, tm, nq), row),
                   pl.BlockSpec((None, tm, nq), row),
                   pl.BlockSpec((None, MLA_HEADS * V_EXT, tm), lambda bi, i: (bi, 0, i)),
                   pl.BlockSpec((None, tm, RWKV_PAD_COLS), row)],
        out_shape=[jax.ShapeDtypeStruct((b, t_all, nq), BF16),
                   jax.ShapeDtypeStruct((b, t_all, nq), BF16),
                   jax.ShapeDtypeStruct((b, MLA_HEADS * V_EXT, t_all), BF16),
                   jax.ShapeDtypeStruct((b, t_all, RWKV_PAD_COLS), F32)],
        compiler_params=_cparams(("parallel", "parallel")),
        name="hybrid_inproj",
    )(x, ctx, gain.reshape(1, d), shift, scale, cos, sin, w1, q_norm.reshape(1, -1), kv_norm.reshape(1, -1), wq2, wk2)


def _attn_kernel(q_ref, k_ref, vt_ref, o_ref, m_sc, acc_sc):
    ki = pl.program_id(3)

    @pl.when(ki == 0)
    def _():
        m_sc[...] = jnp.full(m_sc.shape, -jnp.inf, F32)
        acc_sc[...] = jnp.zeros(acc_sc.shape, F32)

    tq, tk = q_ref.shape[0], k_ref.shape[0]
    chains = [(h, qs) for h in range(2) for qs in range(tq // ATT_QSUB)]
    hsl = [slice(h * HEAD_PAD, (h + 1) * HEAD_PAD) for h, _ in chains]
    vsl = [slice(h * V_EXT, (h + 1) * V_EXT) for h, _ in chains]
    qsl = [slice(qs * ATT_QSUB, (qs + 1) * ATT_QSUB) for _, qs in chains]
    nch = len(chains)
    q = [q_ref[qsl[c], hsl[c]] for c in range(nch)]
    m = [m_sc[chains[c][0], 0:1, qsl[c]] for c in range(nch)]
    acc = [acc_sc[chains[c][0], :, qsl[c]] for c in range(nch)]
    ksub = ATT_KSUB if tk % ATT_KSUB == 0 else tk
    for kb in range(tk // ksub):
        ksl = slice(kb * ksub, (kb + 1) * ksub)
        s = [_dot_nt(k_ref[ksl, hsl[c]], q[c]).astype(ATT_EXP_DTYPE) for c in range(nch)]
        m_new = [jnp.maximum(m[c], jnp.max(s[c], axis=0, keepdims=True).astype(F32)) for c in range(nch)]
        alpha = [jnp.exp2(m[c] - m_new[c]) for c in range(nch)]
        p = [jnp.exp2(s[c] - m_new[c].astype(ATT_EXP_DTYPE)).astype(BF16) for c in range(nch)]
        acc = [acc[c] * alpha[c] + _dot(vt_ref[vsl[c], ksl], p[c]) for c in range(nch)]
        m = m_new
    for c in range(nch):
        h = chains[c][0]
        m_sc[h, :, qsl[c]] = jnp.broadcast_to(m[c], (8, ATT_QSUB))
        acc_sc[h, :, qsl[c]] = acc[c]

    @pl.when(ki == pl.num_programs(3) - 1)
    def _():
        o0 = (acc_sc[0, :MLA_V, :] / acc_sc[0, MLA_V:MLA_V + 1, :]).T
        o1 = (acc_sc[1, :MLA_V, :] / acc_sc[1, MLA_V:MLA_V + 1, :]).T
        o_ref[...] = jnp.concatenate([o0, o1], axis=1).astype(BF16)


def _attention(q, k, vt, n_lat, tq, tk):
    b, t_all, _ = k.shape
    return pl.pallas_call(
        _attn_kernel,
        grid=(b, MLA_HEADS // 2, n_lat // tq, t_all // tk),
        in_specs=[pl.BlockSpec((None, tq, 2 * HEAD_PAD), lambda bi, hp, qi, ki: (bi, qi, hp)),
                  pl.BlockSpec((None, tk, 2 * HEAD_PAD), lambda bi, hp, qi, ki: (bi, ki, hp)),
                  pl.BlockSpec((None, 2 * V_EXT, tk), lambda bi, hp, qi, ki: (bi, hp, ki))],
        out_specs=pl.BlockSpec((None, tq, 2 * MLA_V), lambda bi, hp, qi, ki: (bi, qi, hp)),
        out_shape=jax.ShapeDtypeStruct((b, n_lat, MLA_HEADS * MLA_V), BF16),
        scratch_shapes=[pltpu.VMEM((2, 8, tq), F32), pltpu.VMEM((2, V_EXT, tq), F32)],
        compiler_params=_cparams(("parallel", "parallel", "parallel", "arbitrary")),
        name="mla_attention",
    )(q, k, vt)


def _seg_sum(x, bd_ref):
    hi = x.astype(BF16)
    lo = (x - hi.astype(F32)).astype(BF16)
    bd = bd_ref[...]
    return _dot(hi, bd) + _dot(lo, bd)


def _rwkv_prep_kernel(p_ref, prev_ref, next_ref, mup_ref, mun_ref, wl_ref, w0a0_ref, g2_ref, kk_ref, ka_ref, bd_ref,
                      r_ref, k_ref, v_ref, g_ref, kkn_ref, lw_ref, kd_ref, bdir_ref, *, n_lat_tiles, n_tiles):
    i = pl.program_id(1)
    p = p_ref[...]
    tm = p.shape[0]
    first = jnp.logical_or(i == 0, i == n_lat_tiles)
    last = jnp.logical_or(i == n_lat_tiles - 1, i == n_tiles - 1)
    prev_row = jnp.where(first, 0.0, prev_ref[7:8, :])
    next_row = jnp.where(last, 0.0, next_ref[0:1, :])
    ridx = lax.broadcasted_iota(jnp.int32, p.shape, 0)
    prev = jnp.where(ridx == 0, prev_row, pltpu.roll(p, 1, 0))
    nxt = jnp.where(ridx == tm - 1, next_row, pltpu.roll(p, tm - 1, 0))
    p = p + mup_ref[...] * (prev - p) + mun_ref[...] * (nxt - p)
    c = RWKV_DIM
    r, k, v = p[:, :c], p[:, c:2 * c], p[:, 2 * c:3 * c]
    lo = p[:, 3 * c:3 * c + 128]
    lane = lax.broadcasted_iota(jnp.int32, lo.shape, 1)
    lo = jnp.where(lane < 2 * DECAY_LORA, jnp.tanh(lo), lo)
    wa = _dot(lo.astype(BF16), wl_ref[...]) + w0a0_ref[...]
    gd = p[:, 3 * c + 128:]
    g_ref[...] = _dot(jax.nn.sigmoid(gd).astype(BF16), g2_ref[...]).astype(BF16)
    kk = k * kk_ref[...]
    kk = kk * lax.rsqrt(jnp.maximum(_seg_sum(kk * kk, bd_ref), 1e-24))
    r_ref[...] = r.astype(BF16)
    k_ref[...] = k.astype(BF16)
    v_ref[...] = v.astype(BF16)
    kkn_ref[...] = kk.astype(BF16)
    ka = ka_ref[...]
    for d in range(2):
        w = wa[:, d * c:(d + 1) * c]
        a = jax.nn.sigmoid(wa[:, (2 + d) * c:(3 + d) * c])
        lw_ref[:, d * c:(d + 1) * c] = -float(np.exp(-0.5)) * jax.nn.sigmoid(w)
        kd_ref[:, d * c:(d + 1) * c] = (k * (1.0 + (a - 1.0) * ka)).astype(BF16)
        bdir_ref[:, d * c:(d + 1) * c] = (kk * a).astype(BF16)


def _rwkv_prepare(prw, n_lat, mu_prev, mu_next, w0, w2, a0, a2, g2, k_k, k_a, bd_ones):
    b, t_all, pc = prw.shape
    tm = ROW_TILE
    n_tiles = t_all // tm
    n_lat_tiles = n_lat // tm
    c = RWKV_DIM
    z = jnp.zeros((DECAY_LORA, c), F32)
    wl = jnp.concatenate([
        jnp.concatenate([w2[0], z, z, z], axis=1), jnp.concatenate([z, w2[1], z, z], axis=1),
        jnp.concatenate([z, z, a2[0], z], axis=1), jnp.concatenate([z, z, z, a2[1]], axis=1)], axis=0).astype(BF16)
    w0a0 = jnp.concatenate([w0[0], w0[1], a0[0], a0[1]]).reshape(1, 4 * c)
    g2p = jnp.concatenate([g2, jnp.zeros((128 - GATE_LORA, c), F32)], axis=0).astype(BF16)
    row = lambda bi, i: (bi, i, 0)
    const = lambda bi, i: (0, 0)
    hb = tm // 8
    n_hb = t_all // 8
    kern = functools.partial(_rwkv_prep_kernel, n_lat_tiles=n_lat_tiles, n_tiles=n_tiles)
    o_c = jax.ShapeDtypeStruct((b, t_all, c), BF16)
    o_2c = jax.ShapeDtypeStruct((b, t_all, 2 * c), BF16)
    o_lw = jax.ShapeDtypeStruct((b, t_all, 2 * c), F32)
    return pl.pallas_call(
        kern,
        grid=(b, n_tiles),
        in_specs=[pl.BlockSpec((None, tm, pc), row),
                  pl.BlockSpec((None, 8, pc), lambda bi, i: (bi, jnp.maximum(i * hb - 1, 0), 0)),
                  pl.BlockSpec((None, 8, pc), lambda bi, i: (bi, jnp.minimum((i + 1) * hb, n_hb - 1), 0)),
                  pl.BlockSpec((1, pc), const), pl.BlockSpec((1, pc), const),
                  pl.BlockSpec(wl.shape, const), pl.BlockSpec((1, 4 * c), const),
                  pl.BlockSpec(g2p.shape, const), pl.BlockSpec((1, c), const), pl.BlockSpec((1, c), const),
                  pl.BlockSpec((c, c), const)],
        out_specs=[pl.BlockSpec((None, tm, c), row)] * 5 + [pl.BlockSpec((None, tm, 2 * c), row)] * 3,
        out_shape=[o_c] * 5 + [o_lw, o_2c, o_2c],
        compiler_params=_cparams(("parallel", "parallel")),
        name="rwkv_prepare",
    )(prw, prw, prw, _pad_rwkv_cols(mu_prev).reshape(1, pc), _pad_rwkv_cols(mu_next).reshape(1, pc),
      wl, w0a0, g2p, k_k.reshape(1, c), k_a.reshape(1, c), bd_ones)


def _wkv_chunk_kernel(r_ref, v_ref, kk_ref, lw_ref, kd_ref, bd_ref, m_ref, n_ref, rr_ref, yv_ref):
    d = pl.program_id(1)
    cs = WKV_CHUNK
    nc = r_ref.shape[0] // cs
    ti = lax.broadcasted_iota(jnp.int32, (cs, cs), 0)
    si = lax.broadcasted_iota(jnp.int32, (cs, cs), 1)
    rel = (si - ti) * (1 - 2 * d)
    incl = rel <= 0
    strict = rel < 0
    eye = si == ti
    ti2 = lax.broadcasted_iota(jnp.int32, (cs, 2 * cs), 0)
    si2 = lax.broadcasted_iota(jnp.int32, (cs, 2 * cs), 1)
    incl2 = (jnp.where(si2 >= cs, si2 - cs, si2) - ti2) * (1 - 2 * d) <= 0
    incl_2 = jnp.where(incl2, 1.0, 0.0).astype(BF16)
    ops = []
    for ci in range(nc):
        rows = slice(ci * cs, (ci + 1) * cs)
        lw = lw_ref[rows, :]
        lw_hi, lw_lo = _split(lw)
        g = _dot(incl_2, jnp.concatenate([lw_hi, lw_lo], axis=0))
        total = jnp.sum(lw, axis=0, keepdims=True)
        e_inv = jnp.exp(-g)
        e_end = jnp.exp(total - g)
        gam = jnp.exp(total)
        kd = kd_ref[rows, :]
        bd = bd_ref[rows, :]
        at = -kk_ref[rows, :] * jnp.exp(g - lw)
        rt = r_ref[rows, :] * jnp.exp(g)
        kt = (kd * e_inv).astype(BF16)
        bt = (bd * e_inv).astype(BF16)
        ke = (kd * e_end).astype(BF16)
        be = (bd * e_end).astype(BF16)
        v32 = v_ref[rows, :].astype(F32)
        v = v32.astype(BF16)
        for h in range(RWKV_HEADS):
            sl = slice(h * RWKV_HEAD, (h + 1) * RWKV_HEAD)
            ops.append((at[:, sl], rt[:, sl], kt[:, sl], bt[:, sl], ke[:, sl], be[:, sl], v[:, sl], gam[:, sl], v32[:, sl]))
    n_it = len(ops)
    aa = [_dot_nt(jnp.concatenate([o[0], o[1]], axis=0).astype(BF16), jnp.concatenate([o[3], o[2]], axis=0))
          for o in ops]
    a_ab = [jnp.where(strict, x[:cs, :cs], 0.0) for x in aa]
    a_ak = [jnp.where(strict, x[:cs, cs:], 0.0).astype(BF16) for x in aa]
    a_rbk = [jnp.where(incl2, x[cs:, :], 0.0).astype(BF16) for x in aa]
    z = [jnp.concatenate([ops[i][0], _dot(a_ak[i], ops[i][6])], axis=1) for i in range(n_it)]
    pw = a_ab
    n_sq = int(np.log2(cs))
    for it in range(n_sq):
        more = it + 1 < n_sq
        res = [_dot_split(pw[i], jnp.concatenate([z[i], pw[i]], axis=1) if more else z[i]) for i in range(n_it)]
        z = [z[i] + res[i][:, :2 * cs] for i in range(n_it)]
        if more:
            pw = [x[:, 2 * cs:] for x in res]
    zero = jnp.zeros((cs, cs), F32)
    wmat = [jnp.concatenate([z[i], jnp.concatenate([zero, ops[i][8]], axis=1)], axis=0).astype(BF16)
            for i in range(n_it)]
    w2 = [_dot(a_rbk[i], wmat[i]) for i in range(n_it)]
    w3 = [_dot_tn(jnp.concatenate([ops[i][5], ops[i][4]], axis=0), wmat[i]) for i in range(n_it)]
    nh = RWKV_HEADS
    for ci in range(nc):
        rows = slice(ci * cs, (ci + 1) * cs)
        ids = range(ci * nh, (ci + 1) * nh)
        m_ref[rows, :] = jnp.concatenate([jnp.where(eye, ops[i][7], 0.0) + w3[i][:, :cs] for i in ids], axis=1)
        n_ref[rows, :] = jnp.concatenate([w3[i][:, cs:] for i in ids], axis=1)
        rr_ref[rows, :] = jnp.concatenate([ops[i][1] + w2[i][:, :cs] for i in ids], axis=1).astype(BF16)
        yv_ref[rows, :] = jnp.concatenate([w2[i][:, cs:] for i in ids], axis=1)


def _wkv_chunks(r, v, kkn, lw, kd, bdir):
    b, t_all, c = r.shape
    per_step = next(n for n in WKV_CHUNKS_PER_STEP if (t_all // WKV_CHUNK) % n == 0)
    rows = WKV_CHUNK * per_step
    nst = t_all // rows
    shared = pl.BlockSpec((None, rows, c), lambda bi, d, ci: (bi, ci, 0))
    per_dir = pl.BlockSpec((None, rows, c), lambda bi, d, ci: (bi, ci, d))
    out = pl.BlockSpec((None, None, rows, c), lambda bi, d, ci: (bi, d, ci, 0))
    o_s = jax.ShapeDtypeStruct((b, 2, t_all, c), F32)
    o_r = jax.ShapeDtypeStruct((b, 2, t_all, c), BF16)
    return pl.pallas_call(
        _wkv_chunk_kernel,
        grid=(b, 2, nst),
        in_specs=[shared, shared, shared, per_dir, per_dir, per_dir],
        out_specs=[out] * 4,
        out_shape=[o_s, o_s, o_r, o_s],
        compiler_params=_cparams(("parallel", "parallel", "parallel")),
        name="wkv_chunk_summaries",
    )(r, v, kkn, lw, kd, bdir)


def _wkv_state_kernel(order_ref, mf_ref, nf_ref, rf_ref, yf_ref, mb_ref, nb_ref, rb_ref, yb_ref,
                      of_ref, ob_ref, st_sc):
    del order_ref
    s = pl.program_id(0)

    @pl.when(s == 0)
    def _():
        st_sc[...] = jnp.zeros(st_sc.shape, F32)

    n_b = st_sc.shape[0]
    cs = WKV_CHUNK
    cpb = mf_ref.shape[1] // cs
    zpad = jnp.zeros((cs, 2 * RWKV_HEAD), BF16)
    ins = ((mf_ref, nf_ref, rf_ref, yf_ref, of_ref), (mb_ref, nb_ref, rb_ref, yb_ref, ob_ref))
    for bi in range(n_b):
        for d in range(2):
            m_r, n_r, r_r, y_r, o_r = ins[d]
            st = st_sc[bi, d]
            for jj in range(cpb):
                j = jj if d == 0 else cpb - 1 - jj
                rows = slice(j * cs, (j + 1) * cs)
                n, yv = n_r[bi, rows, :], y_r[bi, rows, :]
                st_hi, st_lo = _split(st)
                m_hi, m_lo = _split(m_r[bi, rows, :])
                rr_b = r_r[bi, rows, :].astype(BF16)
                ys, sts = [], []
                for h in range(RWKV_HEADS):
                    sl = slice(h * RWKV_HEAD, (h + 1) * RWKV_HEAD)
                    lhs = jnp.concatenate([jnp.concatenate([m_hi[:, sl], m_hi[:, sl], m_lo[:, sl]], axis=1),
                                           jnp.concatenate([rr_b[:, sl], zpad], axis=1)], axis=0)
                    rhs = jnp.concatenate([st_hi[:, sl], st_lo[:, sl], st_hi[:, sl]], axis=0)
                    res = _dot(lhs, rhs)
                    sts.append(res[:cs] + n[:, sl])
                    ys.append(res[cs:] + yv[:, sl])
                o_r[bi, rows, :] = jnp.concatenate(ys, axis=1)
                st = jnp.concatenate(sts, axis=1)
            st_sc[bi, d] = st


def _wkv_states(order, m, n, rr, yv, t_all, cpb):
    b, _, _, c = m.shape
    cs = WKV_CHUNK * cpb
    nch = t_all // cs
    fwd = pl.BlockSpec((b, None, cs, c), lambda s, o: (0, 0, o[0, s], 0))
    bwd = pl.BlockSpec((b, None, cs, c), lambda s, o: (0, 1, o[1, s], 0))
    out_f = pl.BlockSpec((b, cs, c), lambda s, o: (0, o[0, s], 0))
    out_b = pl.BlockSpec((b, cs, c), lambda s, o: (0, o[1, s], 0))
    o_s = jax.ShapeDtypeStruct((b, t_all, c), F32)
    return pl.pallas_call(
        _wkv_state_kernel,
        grid_spec=pltpu.PrefetchScalarGridSpec(
            num_scalar_prefetch=1,
            grid=(nch,),
            in_specs=[fwd] * 4 + [bwd] * 4,
            out_specs=[out_f, out_b],
            scratch_shapes=[pltpu.VMEM((b, 2, RWKV_HEAD, c), F32)]),
        out_shape=[o_s, o_s],
        compiler_params=_cparams(("arbitrary",)),
        name="wkv_state_pass",
    )(order, m, n, rr, yv, m, n, rr, yv)


def _ffn_pre(x1, gain_ref, sh_ref, sc_ref, rw_ref, rb_ref, hf_ref, idx_ref, gate_ref):
    hf = _rms(x1) * gain_ref[...]
    hf = hf * (1.0 + sc_ref[...]) + sh_ref[...]
    hf_ref[...] = hf.astype(BF16)
    hf_hi, hf_lo = _split(hf)
    t = _dot(hf_hi, rw_ref[...])
    u = _dot(hf_lo, rw_ref[:, :LOGIT_PAD])
    logits = t[:, :LOGIT_PAD] + t[:, LOGIT_PAD:] + u + rb_ref[...]
    lane = lax.broadcasted_iota(jnp.int32, logits.shape, 1)
    lane_f = lane.astype(F32)
    work = logits
    idx_out = jnp.zeros(logits.shape, F32)
    val_out = jnp.full(logits.shape, NEG_BIG, F32)
    for kth in range(TOP_K):
        m = jnp.max(work, axis=1, keepdims=True)
        idx = jnp.min(jnp.where(work == m, lane_f, float(LOGIT_PAD)), axis=1, keepdims=True)
        work = jnp.where(lane_f == idx, -jnp.inf, work)
        idx_out = jnp.where(lane == kth, idx, idx_out)
        val_out = jnp.where(lane == kth, m, val_out)
    e = jnp.exp(val_out - jnp.max(val_out, axis=1, keepdims=True))
    e = jnp.where(lane < TOP_K, e, 0.0)
    idx_ref[...] = idx_out.astype(jnp.int32)
    gate_ref[...] = e / jnp.sum(e, axis=1, keepdims=True)


def _router_pads(router_w, router_b):
    d = router_w.shape[0]
    rw = jnp.concatenate([router_w, jnp.zeros((d, LOGIT_PAD - N_EXPERTS), F32)], axis=1)
    rb = jnp.concatenate([router_b, jnp.full((LOGIT_PAD - N_EXPERTS,), NEG_BIG, F32)]).reshape(1, LOGIT_PAD)
    return jnp.concatenate(_split(rw), axis=1), rb


def _mix_out_kernel(yf_ref, yb_ref, r_ref, k_ref, v_ref, g_ref, o_ref, x_ref,
                    rk_ref, lnw_ref, lnb_ref, bdm_ref, bds_ref, wout_ref, npost_ref, g1_ref,
                    gpre_ref, sh_ref, sc_ref, rw_ref, rb_ref,
                    x1_ref, hf_ref, idx_ref, gate_ref):
    y = yf_ref[...] + yb_ref[...]
    mu = _seg_sum(y, bdm_ref)
    dlt = y - mu
    var = _seg_sum(dlt * dlt, bdm_ref)
    yn = dlt * lax.rsqrt(var + RWKV_LN_EPS) * lnw_ref[...] + lnb_ref[...]
    bonus = _seg_sum(r_ref[...].astype(F32) * k_ref[...] * rk_ref[...], bds_ref) * v_ref[...]
    rw = (yn + bonus) * g_ref[...]
    mix_in = jnp.concatenate([o_ref[...], rw.astype(BF16)], axis=1)
    mix = _dot(mix_in, wout_ref[...])
    x1 = x_ref[...] + g1_ref[...] * (_rms(mix) * npost_ref[...])
    x1_ref[...] = x1
    _ffn_pre(x1, gpre_ref, sh_ref, sc_ref, rw_ref, rb_ref, hf_ref, idx_ref, gate_ref)


def _token_out_specs(tm, d):
    row = lambda bi, i: (bi, i, 0)
    specs = [pl.BlockSpec((None, tm, d), row), pl.BlockSpec((None, tm, d), row),
             pl.BlockSpec((None, tm, LOGIT_PAD), row), pl.BlockSpec((None, tm, LOGIT_PAD), row)]
    return specs


def _token_out_shapes(b, s, d):
    return [jax.ShapeDtypeStruct((b, s, d), F32), jax.ShapeDtypeStruct((b, s, d), BF16),
            jax.ShapeDtypeStruct((b, s, LOGIT_PAD), jnp.int32), jax.ShapeDtypeStruct((b, s, LOGIT_PAD), F32)]


def _mix_out(yf, yb, r, k, v, g, o_attn, x, r_k, ln_w, ln_b, bd_mean, bd_ones, w_out, n_post, g1,
             n_pre, sh2, sc2, rw_pad, rb_pad):
    b, s, d = x.shape
    c = RWKV_DIM
    tm = MIXOUT_TILE if s % MIXOUT_TILE == 0 else _token_tile(s)
    row = lambda bi, i: (bi, i, 0)
    const = lambda bi, i: (0, 0)
    per_b = lambda bi, i: (bi, 0, 0)
    rc = pl.BlockSpec((None, tm, c), row)
    vec_c = pl.BlockSpec((1, c), const)
    vec_d = pl.BlockSpec((1, d), const)
    mod_d = pl.BlockSpec((None, 1, d), per_b)
    return pl.pallas_call(
        _mix_out_kernel,
        grid=(b, s // tm),
        in_specs=[rc] * 7 + [pl.BlockSpec((None, tm, d), row),
                             vec_c, vec_c, vec_c, pl.BlockSpec((c, c), const), pl.BlockSpec((c, c), const),
                             pl.BlockSpec((d, d), const), vec_d, mod_d,
                             vec_d, mod_d, mod_d, pl.BlockSpec((d, 2 * LOGIT_PAD), const),
                             pl.BlockSpec((1, LOGIT_PAD), const)],
        out_specs=_token_out_specs(tm, d),
        out_shape=_token_out_shapes(b, s, d),
        compiler_params=_cparams(("parallel", "parallel")),
        name="mixer_out_router",
    )(yf, yb, r, k, v, g, o_attn, x, r_k.reshape(1, c), ln_w.reshape(1, c), ln_b.reshape(1, c), bd_mean, bd_ones,
      w_out, n_post.reshape(1, d), g1, n_pre.reshape(1, d), sh2, sc2, rw_pad, rb_pad)


def _moe_residual(x_ref, y_ref, gate_ref, npost_ref, g2_ref):
    gate = gate_ref[...]
    f = jnp.zeros(x_ref.shape, F32)
    for kth in range(TOP_K):
        f = f + gate[:, kth:kth + 1] * y_ref[kth].astype(F32)
    return x_ref[...] + g2_ref[...] * (_rms(f) * npost_ref[...])


def _gmlp_kernel(xin_ref, y_ref, gatein_ref, npostin_ref, g2in_ref,
                 gpre1_ref, sh1_ref, sc1_ref, win_ref, vnw_ref, vnb_ref, ws_ref, bs_ref, wout_ref,
                 npost_ref, g1_ref, gpre_ref, sh_ref, sc_ref, rw_ref, rb_ref,
                 x1_ref, hf_ref, idx_ref, gate_ref):
    x = _moe_residual(xin_ref, y_ref, gatein_ref, npostin_ref, g2in_ref)
    h = _rms(x) * gpre1_ref[...]
    h = h * (1.0 + sc1_ref[...]) + sh1_ref[...]
    z = _dot(h.astype(BF16), win_ref[...])
    z = 0.5 * z * (1.0 + lax.erf(z * float(2.0 ** -0.5)))
    u, v = z[:, :GM_WIDTH], z[:, GM_WIDTH:]
    mu = jnp.mean(v, axis=-1, keepdims=True)
    dv = v - mu
    var = jnp.mean(dv * dv, axis=-1, keepdims=True)
    v = (dv * lax.rsqrt(var + LN_EPS) * vnw_ref[...] + vnb_ref[...]).astype(BF16)
    gw = GM_WIDTH // GM_GROUPS
    rows = []
    for ci in range(x.shape[0] // CHUNK):
        cols = []
        for gi in range(GM_GROUPS):
            cols.append(_dot(ws_ref[gi], v[ci * CHUNK:(ci + 1) * CHUNK, gi * gw:(gi + 1) * gw]))
        rows.append(jnp.concatenate(cols, axis=1) + bs_ref[...])
    sp = jnp.concatenate(rows, axis=0)
    y = _dot((u * sp).astype(BF16), wout_ref[...])
    x1 = x + g1_ref[...] * (_rms(y) * npost_ref[...])
    x1_ref[...] = x1
    _ffn_pre(x1, gpre_ref, sh_ref, sc_ref, rw_ref, rb_ref, hf_ref, idx_ref, gate_ref)


def _gmlp(x, yk, gates_in, n_post_in, g2_in, n_pre1, sh1, sc1, w_in, vn_w, vn_b, w_s, b_s, w_out, n_post, g1, n_pre, sh2, sc2, rw_pad, rb_pad):
    b, s, d = x.shape
    tm = _token_tile(s)
    gw = GM_WIDTH // GM_GROUPS
    bs_full = jnp.repeat(b_s.T, gw, axis=1)
    row = lambda bi, i: (bi, i, 0)
    const = lambda bi, i: (0, 0)
    per_b = lambda bi, i: (bi, 0, 0)
    vec_d = pl.BlockSpec((1, d), const)
    vec_g = pl.BlockSpec((1, GM_WIDTH), const)
    mod_d = pl.BlockSpec((None, 1, d), per_b)
    return pl.pallas_call(
        _gmlp_kernel,
        grid=(b, s // tm),
        in_specs=[pl.BlockSpec((None, tm, d), row),
                  pl.BlockSpec((TOP_K, None, tm, d), lambda bi, i: (0, bi, i, 0)),
                  pl.BlockSpec((None, tm, LOGIT_PAD), row), vec_d, mod_d,
                  vec_d, mod_d, mod_d,
                  pl.BlockSpec((d, 2 * GM_WIDTH), const), vec_g, vec_g,
                  pl.BlockSpec((GM_GROUPS, CHUNK, CHUNK), lambda bi, i: (0, 0, 0)),
                  pl.BlockSpec((CHUNK, GM_WIDTH), const), pl.BlockSpec((GM_WIDTH, d), const),
                  vec_d, mod_d, vec_d, mod_d, mod_d,
                  pl.BlockSpec((d, 2 * LOGIT_PAD), const), pl.BlockSpec((1, LOGIT_PAD), const)],
        out_specs=_token_out_specs(tm, d),
        out_shape=_token_out_shapes(b, s, d),
        compiler_params=_cparams(("parallel", "parallel")),
        name="gmlp_router",
    )(x, yk, gates_in, n_post_in.reshape(1, d), g2_in, n_pre1.reshape(1, d), sh1, sc1, w_in.astype(BF16), vn_w.reshape(1, -1), vn_b.reshape(1, -1),
      w_s.astype(BF16), bs_full, w_out.astype(BF16), n_post.reshape(1, d), g1, n_pre.reshape(1, d), sh2, sc2,
      rw_pad, rb_pad)


MOE_ROWS = 256


def _expert_kernel(be_ref, na_ref, x_ref, wgu_ref, bgu_ref, wd_ref, bd_ref, y_ref, wgu_bf, wd_bf):
    i = pl.program_id(0)
    prev = be_ref[jnp.maximum(i - 1, 0)]
    changed = jnp.logical_or(i == 0, be_ref[i] != prev)

    @pl.when(changed)
    def _():
        wgu_bf[...] = wgu_ref[...].astype(BF16)
        wd_bf[...] = wd_ref[...].astype(BF16)

    @pl.when(i < na_ref[0])
    def _():
        gu = _dot(x_ref[...], wgu_bf[...]) + bgu_ref[...]
        x_glu = jnp.minimum(gu[:, :EXPERT_FF], SWIGLU_LIMIT)
        x_lin = jnp.clip(gu[:, EXPERT_FF:], -SWIGLU_LIMIT, SWIGLU_LIMIT)
        act = x_glu * jax.nn.sigmoid(SWIGLU_ALPHA * x_glu) * (x_lin + 1.0)
        y_ref[...] = (_dot(act.astype(BF16), wd_bf[...]) + bd_ref[...]).astype(BF16)

    @pl.when(i >= na_ref[0])
    def _():
        y_ref[...] = jnp.zeros(y_ref.shape, BF16)


def _experts(blk_expert, n_active, xg, layer, w_gu, b_gu, w_down, b_down):
    n_rows, d = xg.shape
    n_l, n_e, _, ff2 = w_gu.shape
    tb = MOE_ROWS
    n_blocks = n_rows // tb
    return pl.pallas_call(
        _expert_kernel,
        grid_spec=pltpu.PrefetchScalarGridSpec(
            num_scalar_prefetch=2,
            grid=(n_blocks,),
            in_specs=[pl.BlockSpec((tb, d), lambda i, be, na: (i, 0)),
                      pl.BlockSpec((None, None, d, ff2), lambda i, be, na: (layer, be[i], 0, 0)),
                      pl.BlockSpec((None, None, 1, ff2), lambda i, be, na: (layer, be[i], 0, 0)),
                      pl.BlockSpec((None, None, ff2 // 2, d), lambda i, be, na: (layer, be[i], 0, 0)),
                      pl.BlockSpec((None, None, 1, d), lambda i, be, na: (layer, be[i], 0, 0))],
            out_specs=pl.BlockSpec((tb, d), lambda i, be, na: (i, 0)),
            scratch_shapes=[pltpu.VMEM((d, ff2), BF16), pltpu.VMEM((ff2 // 2, d), BF16)]),
        out_shape=jax.ShapeDtypeStruct((n_rows, d), BF16),
        compiler_params=_cparams(("arbitrary",)),
        name="moe_experts",
    )(blk_expert, n_active, xg, w_gu, b_gu.reshape(n_l, n_e, 1, ff2), w_down, b_down.reshape(n_l, n_e, 1, d))


def _combine_kernel(x_ref, y_ref, gate_ref, npost_ref, g2_ref, o_ref):
    o_ref[...] = _moe_residual(x_ref, y_ref, gate_ref, npost_ref, g2_ref)


def _combine(x1, yk, gates, n_post, g2):
    b, s, d = x1.shape
    tm = _token_tile(s)
    row = lambda bi, i: (bi, i, 0)
    return pl.pallas_call(
        _combine_kernel,
        grid=(b, s // tm),
        in_specs=[pl.BlockSpec((None, tm, d), row),
                  pl.BlockSpec((TOP_K, None, tm, d), lambda bi, i: (0, bi, i, 0)),
                  pl.BlockSpec((None, tm, LOGIT_PAD), row),
                  pl.BlockSpec((1, d), lambda bi, i: (0, 0)),
                  pl.BlockSpec((None, 1, d), lambda bi, i: (bi, 0, 0))],
        out_specs=pl.BlockSpec((None, tm, d), row),
        out_shape=jax.ShapeDtypeStruct((b, s, d), F32),
        compiler_params=_cparams(("parallel", "parallel")),
        name="moe_combine_residual",
    )(x1, yk, gates, n_post.reshape(1, d), g2)


def _lookup(table, idx):
    n = table.shape[0]
    return jnp.sum(jnp.where(idx[:, None] == jnp.arange(n, dtype=jnp.int32)[None, :], table[None, :], 0), axis=1)


def _rows(a, idx):
    return a.at[idx].get(mode="promise_in_bounds")


def _moe_rows(hf, top_idx, layer, w_gu, b_gu, w_down, b_down):
    b, s, d = hf.shape
    n_tok = b * s
    tb = MOE_ROWS
    n_assign = n_tok * TOP_K
    assert N_EXPERTS * n_assign < 2 ** 31
    i32 = jnp.int32
    e_flat = top_idx[..., :TOP_K].reshape(-1)
    skey = jnp.sort(e_flat * n_assign + jnp.arange(n_assign, dtype=i32))
    order = skey % n_assign
    _, inv = lax.sort((order, jnp.arange(n_assign, dtype=i32)), num_keys=1)
    edges = jnp.arange(N_EXPERTS + 1, dtype=i32) * n_assign
    bounds = jnp.sum((skey[None, :] < edges[:, None]).astype(i32), axis=1)
    start = bounds[:-1]
    counts = bounds[1:] - start
    padded = (counts + tb - 1) // tb * tb
    pend = jnp.cumsum(padded)
    pstart = pend - padded
    n_rows = -(-n_assign // tb) * tb + N_EXPERTS * tb
    n_blocks = n_rows // tb
    blk_start = jnp.arange(n_blocks, dtype=i32) * tb
    blk_expert = jnp.minimum(jnp.sum((pend[None, :] <= blk_start[:, None]).astype(i32), axis=1), N_EXPERTS - 1)
    n_active = (pend[-1] // tb).astype(i32).reshape(1)
    j = jnp.arange(n_rows, dtype=i32) - jnp.repeat(_lookup(pstart, blk_expert), tb)
    src = jnp.repeat(_lookup(start, blk_expert), tb) + j
    valid = j < jnp.repeat(_lookup(counts, blk_expert), tb)
    row_tok = jnp.where(valid, _rows(order, jnp.clip(src, 0, n_assign - 1)) // TOP_K, 0)
    pos = _lookup(pstart - start, e_flat) + inv
    xg = _rows(hf.reshape(n_tok, d), row_tok)
    y = _experts(blk_expert, n_active, xg, layer, w_gu, b_gu, w_down, b_down)
    return _rows(y, pos.reshape(n_tok, TOP_K).T.reshape(-1)).reshape(TOP_K, b, s, d)


def _block_diag(n, blk, val):
    return (jnp.kron(jnp.eye(n // blk, dtype=F32), jnp.ones((blk, blk), F32)) * val).astype(BF16)


def kernel(x, c, ctx, c_ctx, ada_w, ada_b, norm_mix_pre, norm_mix_post, norm_ffn_pre, norm_ffn_post, router_w, router_b, moe_w_gu, moe_b_gu, moe_w_down, moe_b_down, hy_w_in, mla_q_norm, mla_w_uq, mla_kv_norm, mla_w_ukv, rwkv_mu_prev, rwkv_mu_next, rwkv_w0, rwkv_w2, rwkv_a0, rwkv_a2, rwkv_g2, rwkv_k_k, rwkv_k_a, rwkv_r_k, rwkv_ln_w, rwkv_ln_b, hy_w_out, gm_w_in, gm_v_norm_w, gm_v_norm_b, gm_w_s, gm_b_s, gm_w_out):
    b, s, d = x.shape
    n_ctx = ctx.shape[1]
    t_all = s + n_ctx
    assert b + 1 <= 8 and s % ROW_TILE == 0 and n_ctx % ROW_TILE == 0

    cond_rows = jnp.concatenate([c, c_ctx[None], jnp.zeros((8 - b - 1, d), F32)], axis=0)
    mod = _modulation(cond_rows, ada_w, ada_b)

    def lat_mod(l, j):
        return mod[l, :b, j * d:(j + 1) * d].reshape(b, 1, d)

    sh_all = jnp.stack([mod[0, :b, 0:d], jnp.broadcast_to(mod[0, b, 0:d], (b, d))], axis=1).reshape(2 * b, 1, d)
    sc_all = jnp.stack([mod[0, :b, d:2 * d], jnp.broadcast_to(mod[0, b, d:2 * d], (b, d))], axis=1).reshape(2 * b, 1, d)
    cos, sin = _rope_tables(s, n_ctx)
    w1, wq2, wk2 = _inproj_weights(hy_w_in[0], mla_w_uq[0], mla_w_ukv[0])
    q, k, v, prw = _inproj(x, ctx, norm_mix_pre[0], sh_all, sc_all, cos, sin, w1, mla_q_norm[0], mla_kv_norm[0],
                           wq2, wk2)
    tq = 512 if s % 512 == 0 else ROW_TILE
    tk = next((t for t in (8320, 1280) if t_all % t == 0), ROW_TILE)
    o_attn = _attention(q, k, v, s, tq, tk)

    bd_ones = _block_diag(RWKV_DIM, RWKV_HEAD, 1.0)
    bd_mean = _block_diag(RWKV_DIM, RWKV_HEAD, 1.0 / RWKV_HEAD)
    r, kx, vx, g, kkn, lw, kd, bdir = _rwkv_prepare(prw, s, rwkv_mu_prev[0], rwkv_mu_next[0], rwkv_w0[0], rwkv_w2[0],
                                                    rwkv_a0[0], rwkv_a2[0], rwkv_g2[0], rwkv_k_k[0], rwkv_k_a[0],
                                                    bd_ones)
    m_c, n_c, r_c, y_c = _wkv_chunks(r, vx, kkn, lw, kd, bdir)
    cpb = 2 if (s // WKV_CHUNK) % 2 == 0 and (n_ctx // WKV_CHUNK) % 2 == 0 else 1
    n_lat_blk = s // (WKV_CHUNK * cpb)
    n_ctx_blk = n_ctx // (WKV_CHUNK * cpb)
    lat_blk = np.arange(n_lat_blk)
    ctx_blk = n_lat_blk + np.arange(n_ctx_blk)
    order = jnp.asarray(np.stack([np.concatenate([ctx_blk, lat_blk]),
                                  np.concatenate([ctx_blk[::-1], lat_blk[::-1]])]).astype(np.int32))
    yf, yb = _wkv_states(order, m_c, n_c, r_c, y_c, t_all, cpb)

    rw_pad, rb_pad = _router_pads(router_w[0], router_b[0])
    x1, hf, top_idx, gates = _mix_out(yf, yb, r, kx, vx, g, o_attn, x, rwkv_r_k[0].reshape(-1), rwkv_ln_w[0],
                                      rwkv_ln_b[0], bd_mean, bd_ones, hy_w_out[0].astype(BF16), norm_mix_post[0],
                                      lat_mod(0, 2), norm_ffn_pre[0], lat_mod(0, 3), lat_mod(0, 4), rw_pad, rb_pad)
    yk = _moe_rows(hf, top_idx, 0, moe_w_gu, moe_b_gu, moe_w_down, moe_b_down)

    rw_pad, rb_pad = _router_pads(router_w[1], router_b[1])
    x3, hf, top_idx, gates = _gmlp(x1, yk, gates, norm_ffn_post[0], lat_mod(0, 5), norm_mix_pre[1], lat_mod(1, 0), lat_mod(1, 1), gm_w_in[0], gm_v_norm_w[0],
                                   gm_v_norm_b[0], gm_w_s[0], gm_b_s[0], gm_w_out[0], norm_mix_post[1], lat_mod(1, 2),
                                   norm_ffn_pre[1], lat_mod(1, 3), lat_mod(1, 4), rw_pad, rb_pad)
    yk = _moe_rows(hf, top_idx, 1, moe_w_gu, moe_b_gu, moe_w_down, moe_b_down)
    return _combine(x3, yk, gates, norm_ffn_post[1], lat_mod(1, 5))
```

```python
import functools

import jax
import jax.numpy as jnp
import numpy as np
from jax import lax
from jax.experimental import pallas as pl
from jax.experimental.pallas import tpu as pltpu

F32 = jnp.float32
BF16 = jnp.bfloat16
HIGHEST = lax.Precision.HIGHEST

D_MODEL = 1024
GRID_W = 64
EPS = 1e-6

MLA_HEADS = 8
MLA_NOPE = 64
MLA_ROPE = 32
MLA_V = 64
MLA_Q_RANK = 256
MLA_KV_RANK = 128
MLA_SCALE = (MLA_NOPE + MLA_ROPE) ** -0.5
ROPE_FREQS = MLA_ROPE // 4
ROPE_BASE = 10000.0
HEAD_PAD = 128
ATT_QSUB = 128
ATT_KSUB = 640
LOG2E = 1.4426950408889634
ATT_EXP_DTYPE = jnp.bfloat16
V_EXT = 80

RWKV_HEAD = 64
RWKV_DIM = 512
RWKV_HEADS = 8
DECAY_LORA = 32
AAA_LORA = 32
GATE_LORA = 96
RWKV_PAD_COLS = 3 * RWKV_DIM + 128 + 128
RWKV_LN_EPS = 64e-5
WKV_CHUNK = 64
WKV_CHUNKS_PER_STEP = (13, 10, 5, 4, 2, 1)

CHUNK = 128
GM_WIDTH = 1024
GM_GROUPS = 8
LN_EPS = 1e-5

N_EXPERTS = 32
TOP_K = 4
EXPERT_FF = 1024
SWIGLU_LIMIT = 7.0
SWIGLU_ALPHA = 1.702
LOGIT_PAD = 128
NEG_BIG = -1e30

ROW_TILE = 256
TOKEN_TILE = 512
MIXOUT_TILE = 1024
VMEM_LIMIT = 48 * 1024 * 1024


def _cparams(sem):
    return pltpu.CompilerParams(dimension_semantics=sem, vmem_limit_bytes=VMEM_LIMIT)


def _token_tile(n_rows):
    return TOKEN_TILE if n_rows % TOKEN_TILE == 0 else ROW_TILE


def _rms(x):
    return x * lax.rsqrt(jnp.mean(x * x, axis=-1, keepdims=True) + EPS)


def _dot(a, b, precision=None):
    return jnp.dot(a, b, preferred_element_type=F32, precision=precision)


def _dot_nt(a, b, precision=None):
    return lax.dot_general(a, b, (((1,), (1,)), ((), ())), preferred_element_type=F32, precision=precision)


def _dot_tn(a, b, precision=None):
    return lax.dot_general(a, b, (((0,), (0,)), ((), ())), preferred_element_type=F32, precision=precision)


def _split(x):
    hi = x.astype(BF16)
    return hi, (x - hi.astype(F32)).astype(BF16)


def _dot_split(a, b):
    a_hi, a_lo = _split(a)
    b_hi, b_lo = _split(b)
    return _dot(jnp.concatenate([a_hi, a_hi, a_lo], axis=1), jnp.concatenate([b_hi, b_lo, b_hi], axis=0))


def _mod_kernel(s_ref, w_ref, b_ref, o_ref):
    s = s_ref[...]
    s = s * jax.nn.sigmoid(s)
    o_ref[...] = _dot(s, w_ref[...], HIGHEST) + b_ref[...]


def _modulation(cond_rows, ada_w, ada_b):
    n_l, d, n6 = ada_w.shape
    tn = 1536
    return pl.pallas_call(
        _mod_kernel,
        grid=(n_l, n6 // tn),
        in_specs=[pl.BlockSpec((8, d), lambda l, j: (0, 0)),
                  pl.BlockSpec((None, d, tn), lambda l, j: (l, 0, j)),
                  pl.BlockSpec((None, 1, tn), lambda l, j: (l, 0, j))],
        out_specs=pl.BlockSpec((None, 8, tn), lambda l, j: (l, 0, j)),
        out_shape=jax.ShapeDtypeStruct((n_l, 8, n6), F32),
        compiler_params=_cparams(("arbitrary", "arbitrary")),
        name="adaln_mod",
    )(cond_rows, ada_w, ada_b.reshape(n_l, 1, n6))


def _inproj_kernel(x_ref, ctx_ref, g_ref, sh_ref, sc_ref, cos_ref, sin_ref, w1_ref, qn_ref, kvn_ref, wq_ref, wk_ref,
                   q_ref, k_ref, vt_ref, prw_ref, *, n_lat_tiles):
    x_in = jnp.where(pl.program_id(1) >= n_lat_tiles, ctx_ref[...], x_ref[...])
    h = _rms(x_in) * g_ref[...]
    h = h * (1.0 + sc_ref[...]) + sh_ref[...]
    p = _dot(h.astype(BF16), w1_ref[...])
    cos = jnp.concatenate([cos_ref[...]] * MLA_HEADS, axis=1)
    sin = jnp.concatenate([sin_ref[...]] * MLA_HEADS, axis=1)
    nq = MLA_HEADS * HEAD_PAD
    qn = _rms(p[:, :MLA_Q_RANK]) * qn_ref[...]
    qq = _dot(qn.astype(BF16), wq_ref[...])
    q_ref[...] = (qq[:, :nq] * cos + qq[:, nq:] * sin).astype(BF16)
    kvn = _rms(p[:, MLA_Q_RANK:MLA_Q_RANK + MLA_KV_RANK]) * kvn_ref[...]
    x2 = jnp.concatenate([kvn, p[:, MLA_Q_RANK + MLA_KV_RANK:512]], axis=1).astype(BF16)
    kk = _dot(x2, wk_ref[...])
    k_ref[...] = (kk[:, :nq] * cos + kk[:, nq:2 * nq] * sin).astype(BF16)
    vt = kk[:, 2 * nq:].T.astype(BF16)
    ones = jnp.ones((V_EXT - MLA_V, vt.shape[1]), BF16)
    for hd in range(MLA_HEADS):
        vt_ref[hd * V_EXT:hd * V_EXT + MLA_V, :] = vt[hd * MLA_V:(hd + 1) * MLA_V, :]
        vt_ref[hd * V_EXT + MLA_V:(hd + 1) * V_EXT, :] = ones
    prw_ref[...] = p[:, 512:]


def _rotate_half_cols(w):
    wr = w.reshape(w.shape[:-1] + (2, 2, ROPE_FREQS))
    return jnp.stack([-wr[..., 1, :], wr[..., 0, :]], axis=-2).reshape(w.shape)


def _inproj_weights(hy_w_in, w_uq, w_ukv):
    d = hy_w_in.shape[0]
    mla_cols = MLA_Q_RANK + MLA_KV_RANK + MLA_ROPE
    w_mla = hy_w_in[:, :mla_cols]
    w_rw = hy_w_in[:, mla_cols:]
    w1 = jnp.concatenate([w_mla, jnp.zeros((d, 512 - mla_cols), F32), _pad_rwkv_cols(w_rw)], axis=1).astype(BF16)
    wq = (w_uq * (MLA_SCALE * LOG2E)).reshape(MLA_Q_RANK, MLA_HEADS, MLA_NOPE + MLA_ROPE)
    z_n = jnp.zeros((MLA_Q_RANK, MLA_HEADS, MLA_NOPE), F32)
    z_p = jnp.zeros((MLA_Q_RANK, MLA_HEADS, HEAD_PAD - MLA_NOPE - MLA_ROPE), F32)
    wq_plain = jnp.concatenate([wq[..., :MLA_NOPE], wq[..., MLA_NOPE:], z_p], axis=-1)
    wq_rot = jnp.concatenate([z_n, _rotate_half_cols(wq[..., MLA_NOPE:]), z_p], axis=-1)
    wq2 = jnp.concatenate([wq_plain.reshape(MLA_Q_RANK, -1), wq_rot.reshape(MLA_Q_RANK, -1)], axis=1).astype(BF16)
    wkv = w_ukv.reshape(MLA_KV_RANK, MLA_HEADS, MLA_NOPE + MLA_V)
    eye = jnp.broadcast_to(jnp.eye(MLA_ROPE, dtype=F32)[:, None, :], (MLA_ROPE, MLA_HEADS, MLA_ROPE))
    pad_k = HEAD_PAD - MLA_NOPE - MLA_ROPE
    top_plain = jnp.concatenate([wkv[..., :MLA_NOPE], jnp.zeros((MLA_KV_RANK, MLA_HEADS, MLA_ROPE + pad_k), F32)], -1)
    mid_plain = jnp.concatenate([jnp.zeros((MLA_ROPE, MLA_HEADS, MLA_NOPE), F32), eye,
                                 jnp.zeros((MLA_ROPE, MLA_HEADS, pad_k), F32)], -1)
    mid_rot = jnp.concatenate([jnp.zeros((MLA_ROPE, MLA_HEADS, MLA_NOPE), F32), _rotate_half_cols(eye),
                               jnp.zeros((MLA_ROPE, MLA_HEADS, pad_k), F32)], -1)
    nq = MLA_HEADS * HEAD_PAD
    rows_c = jnp.concatenate([top_plain.reshape(MLA_KV_RANK, nq), jnp.zeros((MLA_KV_RANK, nq), F32),
                              wkv[..., MLA_NOPE:].reshape(MLA_KV_RANK, MLA_HEADS * MLA_V)], axis=1)
    rows_r = jnp.concatenate([mid_plain.reshape(MLA_ROPE, nq), mid_rot.reshape(MLA_ROPE, nq),
                              jnp.zeros((MLA_ROPE, MLA_HEADS * MLA_V), F32)], axis=1)
    rows_z = jnp.zeros((256 - MLA_KV_RANK - MLA_ROPE, rows_c.shape[1]), F32)
    wk2 = jnp.concatenate([rows_c, rows_r, rows_z], axis=0).astype(BF16)
    return w1, wq2, wk2


def _pad_rwkv_cols(w):
    pad = jnp.zeros(w.shape[:-1] + (RWKV_PAD_COLS - w.shape[-1],), w.dtype)
    return jnp.concatenate([w, pad], axis=-1)


def _rope_tables(n_lat, n_ctx):
    t = jnp.arange(n_lat)
    row = (t // GRID_W).astype(F32)
    col = (t % GRID_W).astype(F32)
    inv = ROPE_BASE ** (-jnp.arange(ROPE_FREQS, dtype=F32) / ROPE_FREQS)
    ang = jnp.stack([row[:, None] * inv, col[:, None] * inv], axis=1)
    ang = jnp.broadcast_to(ang[:, :, None, :], (n_lat, 2, 2, ROPE_FREQS)).reshape(n_lat, MLA_ROPE)
    cos = jnp.concatenate([jnp.ones((n_lat, MLA_NOPE), F32), jnp.cos(ang),
                           jnp.ones((n_lat, HEAD_PAD - MLA_NOPE - MLA_ROPE), F32)], axis=1)
    sin = jnp.concatenate([jnp.zeros((n_lat, MLA_NOPE), F32), jnp.sin(ang),
                           jnp.zeros((n_lat, HEAD_PAD - MLA_NOPE - MLA_ROPE), F32)], axis=1)
    cos = jnp.concatenate([cos, jnp.ones((n_ctx, HEAD_PAD), F32)], axis=0)
    sin = jnp.concatenate([sin, jnp.zeros((n_ctx, HEAD_PAD), F32)], axis=0)
    return cos, sin


def _inproj(x, ctx, gain, shift, scale, cos, sin, w1, q_norm, kv_norm, wq2, wk2):
    b, n_lat, d = x.shape
    t_all = n_lat + ctx.shape[1]
    tm = ROW_TILE
    n_lat_tiles = n_lat // tm
    nq = MLA_HEADS * HEAD_PAD

    def mod_map(bi, i):
        return (2 * bi + jnp.where(i >= n_lat_tiles, 1, 0), 0, 0)

    const = lambda bi, i: (0, 0)
    row = lambda bi, i: (bi, i, 0)
    return pl.pallas_call(
        functools.partial(_inproj_kernel, n_lat_tiles=n_lat_tiles),
        grid=(b, t_all // tm),
        in_specs=[pl.BlockSpec((None, tm, d), lambda bi, i: (bi, jnp.minimum(i, n_lat_tiles - 1), 0)),
                  pl.BlockSpec((None, tm, d), lambda bi, i: (bi, jnp.maximum(i - n_lat_tiles, 0), 0)),
                  pl.BlockSpec((1, d), const),
                  pl.BlockSpec((None, 1, d), mod_map),
                  pl.BlockSpec((None, 1, d), mod_map),
                  pl.BlockSpec((tm, HEAD_PAD), lambda bi, i: (i, 0)),
                  pl.BlockSpec((tm, HEAD_PAD), lambda bi, i: (i, 0)),
                  pl.BlockSpec(w1.shape, const),
                  pl.BlockSpec((1, MLA_Q_RANK), const),
                  pl.BlockSpec((1, MLA_KV_RANK), const),
                  pl.BlockSpec(wq2.shape, const),
                  pl.BlockSpec(wk2.shape, const)],
        out_specs=[pl.BlockSpec((None, tm, nq), row),
                   pl.BlockSpec((None, tm, nq), row),
                   pl.BlockSpec((None, MLA_HEADS * V_EXT, tm), lambda bi, i: (bi, 0, i)),
                   pl.BlockSpec((None, tm, RWKV_PAD_COLS), row)],
        out_shape=[jax.ShapeDtypeStruct((b, t_all, nq), BF16),
                   jax.ShapeDtypeStruct((b, t_all, nq), BF16),
                   jax.ShapeDtypeStruct((b, MLA_HEADS * V_EXT, t_all), BF16),
                   jax.ShapeDtypeStruct((b, t_all, RWKV_PAD_COLS), F32)],
        compiler_params=_cparams(("parallel", "parallel")),
        name="hybrid_inproj",
    )(x, ctx, gain.reshape(1, d), shift, scale, cos, sin, w1, q_norm.reshape(1, -1), kv_norm.reshape(1, -1), wq2, wk2)


def _attn_kernel(q_ref, k_ref, vt_ref, o_ref, m_sc, acc_sc):
    ki = pl.program_id(3)

    @pl.when(ki == 0)
    def _():
        m_sc[...] = jnp.full(m_sc.shape, -jnp.inf, F32)
        acc_sc[...] = jnp.zeros(acc_sc.shape, F32)

    tq, tk = q_ref.shape[0], k_ref.shape[0]
    chains = [(h, qs) for h in range(2) for qs in range(tq // ATT_QSUB)]
    hsl = [slice(h * HEAD_PAD, (h + 1) * HEAD_PAD) for h, _ in chains]
    vsl = [slice(h * V_EXT, (h + 1) * V_EXT) for h, _ in chains]
    qsl = [slice(qs * ATT_QSUB, (qs + 1) * ATT_QSUB) for _, qs in chains]
    nch = len(chains)
    q = [q_ref[qsl[c], hsl[c]] for c in range(nch)]
    m = [m_sc[chains[c][0], 0:1, qsl[c]] for c in range(nch)]
    acc = [acc_sc[chains[c][0], :, qsl[c]] for c in range(nch)]
    ksub = ATT_KSUB if tk % ATT_KSUB == 0 else tk
    for kb in range(tk // ksub):
        ksl = slice(kb * ksub, (kb + 1) * ksub)
        s = [_dot_nt(k_ref[ksl, hsl[c]], q[c]).astype(ATT_EXP_DTYPE) for c in range(nch)]
        m_new = [jnp.maximum(m[c], jnp.max(s[c], axis=0, keepdims=True).astype(F32)) for c in range(nch)]
        alpha = [jnp.exp2(m[c] - m_new[c]) for c in range(nch)]
        p = [jnp.exp2(s[c] - m_new[c].astype(ATT_EXP_DTYPE)).astype(BF16) for c in range(nch)]
        acc = [acc[c] * alpha[c] + _dot(vt_ref[vsl[c], ksl], p[c]) for c in range(nch)]
        m = m_new
    for c in range(nch):
        h = chains[c][0]
        m_sc[h, :, qsl[c]] = jnp.broadcast_to(m[c], (8, ATT_QSUB))
        acc_sc[h, :, qsl[c]] = acc[c]

    @pl.when(ki == pl.num_programs(3) - 1)
    def _():
        o0 = (acc_sc[0, :MLA_V, :] / acc_sc[0, MLA_V:MLA_V + 1, :]).T
        o1 = (acc_sc[1, :MLA_V, :] / acc_sc[1, MLA_V:MLA_V + 1, :]).T
        o_ref[...] = jnp.concatenate([o0, o1], axis=1).astype(BF16)


def _attention(q, k, vt, n_lat, tq, tk):
    b, t_all, _ = k.shape
    return pl.pallas_call(
        _attn_kernel,
        grid=(b, MLA_HEADS // 2, n_lat // tq, t_all // tk),
        in_specs=[pl.BlockSpec((None, tq, 2 * HEAD_PAD), lambda bi, hp, qi, ki: (bi, qi, hp)),
                  pl.BlockSpec((None, tk, 2 * HEAD_PAD), lambda bi, hp, qi, ki: (bi, ki, hp)),
                  pl.BlockSpec((None, 2 * V_EXT, tk), lambda bi, hp, qi, ki: (bi, hp, ki))],
        out_specs=pl.BlockSpec((None, tq, 2 * MLA_V), lambda bi, hp, qi, ki: (bi, qi, hp)),
        out_shape=jax.ShapeDtypeStruct((b, n_lat, MLA_HEADS * MLA_V), BF16),
        scratch_shapes=[pltpu.VMEM((2, 8, tq), F32), pltpu.VMEM((2, V_EXT, tq), F32)],
        compiler_params=_cparams(("parallel", "parallel", "parallel", "arbitrary")),
        name="mla_attention",
    )(q, k, vt)


def _seg_sum(x, bd_ref):
    hi = x.astype(BF16)
    lo = (x - hi.astype(F32)).astype(BF16)
    bd = bd_ref[...]
    return _dot(hi, bd) + _dot(lo, bd)


def _rwkv_prep_kernel(p_ref, prev_ref, next_ref, mup_ref, mun_ref, wl_ref, w0a0_ref, g2_ref, kk_ref, ka_ref, bd_ref,
                      r_ref, k_ref, v_ref, g_ref, kkn_ref, lw_ref, kd_ref, bdir_ref, *, n_lat_tiles, n_tiles):
    i = pl.program_id(1)
    p = p_ref[...]
    tm = p.shape[0]
    first = jnp.logical_or(i == 0, i == n_lat_tiles)
    last = jnp.logical_or(i == n_lat_tiles - 1, i == n_tiles - 1)
    prev_row = jnp.where(first, 0.0, prev_ref[7:8, :])
    next_row = jnp.where(last, 0.0, next_ref[0:1, :])
    ridx = lax.broadcasted_iota(jnp.int32, p.shape, 0)
    prev = jnp.where(ridx == 0, prev_row, pltpu.roll(p, 1, 0))
    nxt = jnp.where(ridx == tm - 1, next_row, pltpu.roll(p, tm - 1, 0))
    p = p + mup_ref[...] * (prev - p) + mun_ref[...] * (nxt - p)
    c = RWKV_DIM
    r, k, v = p[:, :c], p[:, c:2 * c], p[:, 2 * c:3 * c]
    lo = p[:, 3 * c:3 * c + 128]
    lane = lax.broadcasted_iota(jnp.int32, lo.shape, 1)
    lo = jnp.where(lane < 2 * DECAY_LORA, jnp.tanh(lo), lo)
    wa = _dot(lo.astype(BF16), wl_ref[...]) + w0a0_ref[...]
    gd = p[:, 3 * c + 128:]
    g_ref[...] = _dot(jax.nn.sigmoid(gd).astype(BF16), g2_ref[...]).astype(BF16)
    kk = k * kk_ref[...]
    kk = kk * lax.rsqrt(jnp.maximum(_seg_sum(kk * kk, bd_ref), 1e-24))
    r_ref[...] = r.astype(BF16)
    k_ref[...] = k.astype(BF16)
    v_ref[...] = v.astype(BF16)
    kkn_ref[...] = kk.astype(BF16)
    ka = ka_ref[...]
    for d in range(2):
        w = wa[:, d * c:(d + 1) * c]
        a = jax.nn.sigmoid(wa[:, (2 + d) * c:(3 + d) * c])
        lw_ref[:, d * c:(d + 1) * c] = -float(np.exp(-0.5)) * jax.nn.sigmoid(w)
        kd_ref[:, d * c:(d + 1) * c] = (k * (1.0 + (a - 1.0) * ka)).astype(BF16)
        bdir_ref[:, d * c:(d + 1) * c] = (kk * a).astype(BF16)


def _rwkv_prepare(prw, n_lat, mu_prev, mu_next, w0, w2, a0, a2, g2, k_k, k_a, bd_ones):
    b, t_all, pc = prw.shape
    tm = ROW_TILE
    n_tiles = t_all // tm
    n_lat_tiles = n_lat // tm
    c = RWKV_DIM
    z = jnp.zeros((DECAY_LORA, c), F32)
    wl = jnp.concatenate([
        jnp.concatenate([w2[0], z, z, z], axis=1), jnp.concatenate([z, w2[1], z, z], axis=1),
        jnp.concatenate([z, z, a2[0], z], axis=1), jnp.concatenate([z, z, z, a2[1]], axis=1)], axis=0).astype(BF16)
    w0a0 = jnp.concatenate([w0[0], w0[1], a0[0], a0[1]]).reshape(1, 4 * c)
    g2p = jnp.concatenate([g2, jnp.zeros((128 - GATE_LORA, c), F32)], axis=0).astype(BF16)
    row = lambda bi, i: (bi, i, 0)
    const = lambda bi, i: (0, 0)
    hb = tm // 8
    n_hb = t_all // 8
    kern = functools.partial(_rwkv_prep_kernel, n_lat_tiles=n_lat_tiles, n_tiles=n_tiles)
    o_c = jax.ShapeDtypeStruct((b, t_all, c), BF16)
    o_2c = jax.ShapeDtypeStruct((b, t_all, 2 * c), BF16)
    o_lw = jax.ShapeDtypeStruct((b, t_all, 2 * c), F32)
    return pl.pallas_call(
        kern,
        grid=(b, n_tiles),
        in_specs=[pl.BlockSpec((None, tm, pc), row),
                  pl.BlockSpec((None, 8, pc), lambda bi, i: (bi, jnp.maximum(i * hb - 1, 0), 0)),
                  pl.BlockSpec((None, 8, pc), lambda bi, i: (bi, jnp.minimum((i + 1) * hb, n_hb - 1), 0)),
                  pl.BlockSpec((1, pc), const), pl.BlockSpec((1, pc), const),
                  pl.BlockSpec(wl.shape, const), pl.BlockSpec((1, 4 * c), const),
                  pl.BlockSpec(g2p.shape, const), pl.BlockSpec((1, c), const), pl.BlockSpec((1, c), const),
                  pl.BlockSpec((c, c), const)],
        out_specs=[pl.BlockSpec((None, tm, c), row)] * 5 + [pl.BlockSpec((None, tm, 2 * c), row)] * 3,
        out_shape=[o_c] * 5 + [o_lw, o_2c, o_2c],
        compiler_params=_cparams(("parallel", "parallel")),
        name="rwkv_prepare",
    )(prw, prw, prw, _pad_rwkv_cols(mu_prev).reshape(1, pc), _pad_rwkv_cols(mu_next).reshape(1, pc),
      wl, w0a0, g2p, k_k.reshape(1, c), k_a.reshape(1, c), bd_ones)


def _wkv_chunk_kernel(r_ref, v_ref, kk_ref, lw_ref, kd_ref, bd_ref, m_ref, n_ref, rr_ref, yv_ref):
    d = pl.program_id(1)
    cs = WKV_CHUNK
    nc = r_ref.shape[0] // cs
    ti = lax.broadcasted_iota(jnp.int32, (cs, cs), 0)
    si = lax.broadcasted_iota(jnp.int32, (cs, cs), 1)
    rel = (si - ti) * (1 - 2 * d)
    incl = rel <= 0
    strict = rel < 0
    eye = si == ti
    ti2 = lax.broadcasted_iota(jnp.int32, (cs, 2 * cs), 0)
    si2 = lax.broadcasted_iota(jnp.int32, (cs, 2 * cs), 1)
    incl2 = (jnp.where(si2 >= cs, si2 - cs, si2) - ti2) * (1 - 2 * d) <= 0
    incl_2 = jnp.where(incl2, 1.0, 0.0).astype(BF16)
    ops = []
    for ci in range(nc):
        rows = slice(ci * cs, (ci + 1) * cs)
        lw = lw_ref[rows, :]
        lw_hi, lw_lo = _split(lw)
        g = _dot(incl_2, jnp.concatenate([lw_hi, lw_lo], axis=0))
        total = jnp.sum(lw, axis=0, keepdims=True)
        e_inv = jnp.exp(-g)
        e_end = jnp.exp(total - g)
        gam = jnp.exp(total)
        kd = kd_ref[rows, :]
        bd = bd_ref[rows, :]
        at = -kk_ref[rows, :] * jnp.exp(g - lw)
        rt = r_ref[rows, :] * jnp.exp(g)
        kt = (kd * e_inv).astype(BF16)
        bt = (bd * e_inv).astype(BF16)
        ke = (kd * e_end).astype(BF16)
        be = (bd * e_end).astype(BF16)
        v32 = v_ref[rows, :].astype(F32)
        v = v32.astype(BF16)
        for h in range(RWKV_HEADS):
            sl = slice(h * RWKV_HEAD, (h + 1) * RWKV_HEAD)
            ops.append((at[:, sl], rt[:, sl], kt[:, sl], bt[:, sl], ke[:, sl], be[:, sl], v[:, sl], gam[:, sl], v32[:, sl]))
    n_it = len(ops)
    aa = [_dot_nt(jnp.concatenate([o[0], o[1]], axis=0).astype(BF16), jnp.concatenate([o[3], o[2]], axis=0))
          for o in ops]
    a_ab = [jnp.where(strict, x[:cs, :cs], 0.0) for x in aa]
    a_ak = [jnp.where(strict, x[:cs, cs:], 0.0).astype(BF16) for x in aa]
    a_rbk = [jnp.where(incl2, x[cs:, :], 0.0).astype(BF16) for x in aa]
    z = [jnp.concatenate([ops[i][0], _dot(a_ak[i], ops[i][6])], axis=1) for i in range(n_it)]
    pw = a_ab
    n_sq = int(np.log2(cs))
    for it in range(n_sq):
        more = it + 1 < n_sq
        res = [_dot_split(pw[i], jnp.concatenate([z[i], pw[i]], axis=1) if more else z[i]) for i in range(n_it)]
        z = [z[i] + res[i][:, :2 * cs] for i in range(n_it)]
        if more:
            pw = [x[:, 2 * cs:] for x in res]
    zero = jnp.zeros((cs, cs), F32)
    wmat = [jnp.concatenate([z[i], jnp.concatenate([zero, ops[i][8]], axis=1)], axis=0).astype(BF16)
            for i in range(n_it)]
    w2 = [_dot(a_rbk[i], wmat[i]) for i in range(n_it)]
    w3 = [_dot_tn(jnp.concatenate([ops[i][5], ops[i][4]], axis=0), wmat[i]) for i in range(n_it)]
    nh = RWKV_HEADS
    for ci in range(nc):
        rows = slice(ci * cs, (ci + 1) * cs)
        ids = range(ci * nh, (ci + 1) * nh)
        m_ref[rows, :] = jnp.concatenate([jnp.where(eye, ops[i][7], 0.0) + w3[i][:, :cs] for i in ids], axis=1)
        n_ref[rows, :] = jnp.concatenate([w3[i][:, cs:] for i in ids], axis=1)
        rr_ref[rows, :] = jnp.concatenate([ops[i][1] + w2[i][:, :cs] for i in ids], axis=1).astype(BF16)
        yv_ref[rows, :] = jnp.concatenate([w2[i][:, cs:] for i in ids], axis=1)


def _wkv_chunks(r, v, kkn, lw, kd, bdir):
    b, t_all, c = r.shape
    per_step = next(n for n in WKV_CHUNKS_PER_STEP if (t_all // WKV_CHUNK) % n == 0)
    rows = WKV_CHUNK * per_step
    nst = t_all // rows
    shared = pl.BlockSpec((None, rows, c), lambda bi, d, ci: (bi, ci, 0))
    per_dir = pl.BlockSpec((None, rows, c), lambda bi, d, ci: (bi, ci, d))
    out = pl.BlockSpec((None, None, rows, c), lambda bi, d, ci: (bi, d, ci, 0))
    o_s = jax.ShapeDtypeStruct((b, 2, t_all, c), F32)
    o_r = jax.ShapeDtypeStruct((b, 2, t_all, c), BF16)
    return pl.pallas_call(
        _wkv_chunk_kernel,
        grid=(b, 2, nst),
        in_specs=[shared, shared, shared, per_dir, per_dir, per_dir],
        out_specs=[out] * 4,
        out_shape=[o_s, o_s, o_r, o_s],
        compiler_params=_cparams(("parallel", "parallel", "parallel")),
        name="wkv_chunk_summaries",
    )(r, v, kkn, lw, kd, bdir)


def _wkv_state_kernel(order_ref, mf_ref, nf_ref, rf_ref, yf_ref, mb_ref, nb_ref, rb_ref, yb_ref,
                      of_ref, ob_ref, st_sc):
    del order_ref
    s = pl.program_id(0)

    @pl.when(s == 0)
    def _():
        st_sc[...] = jnp.zeros(st_sc.shape, F32)

    n_b = st_sc.shape[0]
    cs = WKV_CHUNK
    cpb = mf_ref.shape[1] // cs
    zpad = jnp.zeros((cs, 2 * RWKV_HEAD), BF16)
    ins = ((mf_ref, nf_ref, rf_ref, yf_ref, of_ref), (mb_ref, nb_ref, rb_ref, yb_ref, ob_ref))
    for bi in range(n_b):
        for d in range(2):
            m_r, n_r, r_r, y_r, o_r = ins[d]
            st = st_sc[bi, d]
            for jj in range(cpb):
                j = jj if d == 0 else cpb - 1 - jj
                rows = slice(j * cs, (j + 1) * cs)
                n, yv = n_r[bi, rows, :], y_r[bi, rows, :]
                st_hi, st_lo = _split(st)
                m_hi, m_lo = _split(m_r[bi, rows, :])
                rr_b = r_r[bi, rows, :].astype(BF16)
                ys, sts = [], []
                for h in range(RWKV_HEADS):
                    sl = slice(h * RWKV_HEAD, (h + 1) * RWKV_HEAD)
                    lhs = jnp.concatenate([jnp.concatenate([m_hi[:, sl], m_hi[:, sl], m_lo[:, sl]], axis=1),
                                           jnp.concatenate([rr_b[:, sl], zpad], axis=1)], axis=0)
                    rhs = jnp.concatenate([st_hi[:, sl], st_lo[:, sl], st_hi[:, sl]], axis=0)
                    res = _dot(lhs, rhs)
                    sts.append(res[:cs] + n[:, sl])
                    ys.append(res[cs:] + yv[:, sl])
                o_r[bi, rows, :] = jnp.concatenate(ys, axis=1)
                st = jnp.concatenate(sts, axis=1)
            st_sc[bi, d] = st


def _wkv_states(order, m, n, rr, yv, t_all, cpb):
    b, _, _, c = m.shape
    cs = WKV_CHUNK * cpb
    nch = t_all // cs
    fwd = pl.BlockSpec((b, None, cs, c), lambda s, o: (0, 0, o[0, s], 0))
    bwd = pl.BlockSpec((b, None, cs, c), lambda s, o: (0, 1, o[1, s], 0))
    out_f = pl.BlockSpec((b, cs, c), lambda s, o: (0, o[0, s], 0))
    out_b = pl.BlockSpec((b, cs, c), lambda s, o: (0, o[1, s], 0))
    o_s = jax.ShapeDtypeStruct((b, t_all, c), F32)
    return pl.pallas_call(
        _wkv_state_kernel,
        grid_spec=pltpu.PrefetchScalarGridSpec(
            num_scalar_prefetch=1,
            grid=(nch,),
            in_specs=[fwd] * 4 + [bwd] * 4,
            out_specs=[out_f, out_b],
            scratch_shapes=[pltpu.VMEM((b, 2, RWKV_HEAD, c), F32)]),
        out_shape=[o_s, o_s],
        compiler_params=_cparams(("arbitrary",)),
        name="wkv_state_pass",
    )(order, m, n, rr, yv, m, n, rr, yv)


def _ffn_pre(x1, gain_ref, sh_ref, sc_ref, rw_ref, rb_ref, hf_ref, idx_ref, gate_ref):
    hf = _rms(x1) * gain_ref[...]
    hf = hf * (1.0 + sc_ref[...]) + sh_ref[...]
    hf_ref[...] = hf.astype(BF16)
    hf_hi, hf_lo = _split(hf)
    t = _dot(hf_hi, rw_ref[...])
    u = _dot(hf_lo, rw_ref[:, :LOGIT_PAD])
    logits = t[:, :LOGIT_PAD] + t[:, LOGIT_PAD:] + u + rb_ref[...]
    lane = lax.broadcasted_iota(jnp.int32, logits.shape, 1)
    lane_f = lane.astype(F32)
    work = logits
    idx_out = jnp.zeros(logits.shape, F32)
    val_out = jnp.full(logits.shape, NEG_BIG, F32)
    for kth in range(TOP_K):
        m = jnp.max(work, axis=1, keepdims=True)
        idx = jnp.min(jnp.where(work == m, lane_f, float(LOGIT_PAD)), axis=1, keepdims=True)
        work = jnp.where(lane_f == idx, -jnp.inf, work)
        idx_out = jnp.where(lane == kth, idx, idx_out)
        val_out = jnp.where(lane == kth, m, val_out)
    e = jnp.exp(val_out - jnp.max(val_out, axis=1, keepdims=True))
    e = jnp.where(lane < TOP_K, e, 0.0)
    idx_ref[...] = idx_out.astype(jnp.int32)
    gate_ref[...] = e / jnp.sum(e, axis=1, keepdims=True)


def _router_pads(router_w, router_b):
    d = router_w.shape[0]
    rw = jnp.concatenate([router_w, jnp.zeros((d, LOGIT_PAD - N_EXPERTS), F32)], axis=1)
    rb = jnp.concatenate([router_b, jnp.full((LOGIT_PAD - N_EXPERTS,), NEG_BIG, F32)]).reshape(1, LOGIT_PAD)
    return jnp.concatenate(_split(rw), axis=1), rb


def _mix_out_kernel(yf_ref, yb_ref, r_ref, k_ref, v_ref, g_ref, o_ref, x_ref,
                    rk_ref, lnw_ref, lnb_ref, bdm_ref, bds_ref, wout_ref, npost_ref, g1_ref,
                    gpre_ref, sh_ref, sc_ref, rw_ref, rb_ref,
                    x1_ref, hf_ref, idx_ref, gate_ref):
    y = yf_ref[...] + yb_ref[...]
    mu = _seg_sum(y, bdm_ref)
    dlt = y - mu
    var = _seg_sum(dlt * dlt, bdm_ref)
    yn = dlt * lax.rsqrt(var + RWKV_LN_EPS) * lnw_ref[...] + lnb_ref[...]
    bonus = _seg_sum(r_ref[...].astype(F32) * k_ref[...] * rk_ref[...], bds_ref) * v_ref[...]
    rw = (yn + bonus) * g_ref[...]
    mix_in = jnp.concatenate([o_ref[...], rw.astype(BF16)], axis=1)
    mix = _dot(mix_in, wout_ref[...])
    x1 = x_ref[...] + g1_ref[...] * (_rms(mix) * npost_ref[...])
    x1_ref[...] = x1
    _ffn_pre(x1, gpre_ref, sh_ref, sc_ref, rw_ref, rb_ref, hf_ref, idx_ref, gate_ref)


def _token_out_specs(tm, d):
    row = lambda bi, i: (bi, i, 0)
    specs = [pl.BlockSpec((None, tm, d), row), pl.BlockSpec((None, tm, d), row),
             pl.BlockSpec((None, tm, LOGIT_PAD), row), pl.BlockSpec((None, tm, LOGIT_PAD), row)]
    return specs


def _token_out_shapes(b, s, d):
    return [jax.ShapeDtypeStruct((b, s, d), F32), jax.ShapeDtypeStruct((b, s, d), BF16),
            jax.ShapeDtypeStruct((b, s, LOGIT_PAD), jnp.int32), jax.ShapeDtypeStruct((b, s, LOGIT_PAD), F32)]


def _mix_out(yf, yb, r, k, v, g, o_attn, x, r_k, ln_w, ln_b, bd_mean, bd_ones, w_out, n_post, g1,
             n_pre, sh2, sc2, rw_pad, rb_pad):
    b, s, d = x.shape
    c = RWKV_DIM
    tm = MIXOUT_TILE if s % MIXOUT_TILE == 0 else _token_tile(s)
    row = lambda bi, i: (bi, i, 0)
    const = lambda bi, i: (0, 0)
    per_b = lambda bi, i: (bi, 0, 0)
    rc = pl.BlockSpec((None, tm, c), row)
    vec_c = pl.BlockSpec((1, c), const)
    vec_d = pl.BlockSpec((1, d), const)
    mod_d = pl.BlockSpec((None, 1, d), per_b)
    return pl.pallas_call(
        _mix_out_kernel,
        grid=(b, s // tm),
        in_specs=[rc] * 7 + [pl.BlockSpec((None, tm, d), row),
                             vec_c, vec_c, vec_c, pl.BlockSpec((c, c), const), pl.BlockSpec((c, c), const),
                             pl.BlockSpec((d, d), const), vec_d, mod_d,
                             vec_d, mod_d, mod_d, pl.BlockSpec((d, 2 * LOGIT_PAD), const),
                             pl.BlockSpec((1, LOGIT_PAD), const)],
        out_specs=_token_out_specs(tm, d),
        out_shape=_token_out_shapes(b, s, d),
        compiler_params=_cparams(("parallel", "parallel")),
        name="mixer_out_router",
    )(yf, yb, r, k, v, g, o_attn, x, r_k.reshape(1, c), ln_w.reshape(1, c), ln_b.reshape(1, c), bd_mean, bd_ones,
      w_out, n_post.reshape(1, d), g1, n_pre.reshape(1, d), sh2, sc2, rw_pad, rb_pad)


def _moe_residual(x_ref, y_ref, gate_ref, npost_ref, g2_ref):
    gate = gate_ref[...]
    f = jnp.zeros(x_ref.shape, F32)
    for kth in range(TOP_K):
        f = f + gate[:, kth:kth + 1] * y_ref[kth].astype(F32)
    return x_ref[...] + g2_ref[...] * (_rms(f) * npost_ref[...])


def _gmlp_kernel(xin_ref, y_ref, gatein_ref, npostin_ref, g2in_ref,
                 gpre1_ref, sh1_ref, sc1_ref, win_ref, vnw_ref, vnb_ref, ws_ref, bs_ref, wout_ref,
                 npost_ref, g1_ref, gpre_ref, sh_ref, sc_ref, rw_ref, rb_ref,
                 x1_ref, hf_ref, idx_ref, gate_ref):
    x = _moe_residual(xin_ref, y_ref, gatein_ref, npostin_ref, g2in_ref)
    h = _rms(x) * gpre1_ref[...]
    h = h * (1.0 + sc1_ref[...]) + sh1_ref[...]
    z = _dot(h.astype(BF16), win_ref[...])
    z = 0.5 * z * (1.0 + lax.erf(z * float(2.0 ** -0.5)))
    u, v = z[:, :GM_WIDTH], z[:, GM_WIDTH:]
    mu = jnp.mean(v, axis=-1, keepdims=True)
    dv = v - mu
    var = jnp.mean(dv * dv, axis=-1, keepdims=True)
    v = (dv * lax.rsqrt(var + LN_EPS) * vnw_ref[...] + vnb_ref[...]).astype(BF16)
    gw = GM_WIDTH // GM_GROUPS
    rows = []
    for ci in range(x.shape[0] // CHUNK):
        cols = []
        for gi in range(GM_GROUPS):
            cols.append(_dot(ws_ref[gi], v[ci * CHUNK:(ci + 1) * CHUNK, gi * gw:(gi + 1) * gw]))
        rows.append(jnp.concatenate(cols, axis=1) + bs_ref[...])
    sp = jnp.concatenate(rows, axis=0)
    y = _dot((u * sp).astype(BF16), wout_ref[...])
    x1 = x + g1_ref[...] * (_rms(y) * npost_ref[...])
    x1_ref[...] = x1
    _ffn_pre(x1, gpre_ref, sh_ref, sc_ref, rw_ref, rb_ref, hf_ref, idx_ref, gate_ref)


def _gmlp(x, yk, gates_in, n_post_in, g2_in, n_pre1, sh1, sc1, w_in, vn_w, vn_b, w_s, b_s, w_out, n_post, g1, n_pre, sh2, sc2, rw_pad, rb_pad):
    b, s, d = x.shape
    tm = _token_tile(s)
    gw = GM_WIDTH // GM_GROUPS
    bs_full = jnp.repeat(b_s.T, gw, axis=1)
    row = lambda bi, i: (bi, i, 0)
    const = lambda bi, i: (0, 0)
    per_b = lambda bi, i: (bi, 0, 0)
    vec_d = pl.BlockSpec((1, d), const)
    vec_g = pl.BlockSpec((1, GM_WIDTH), const)
    mod_d = pl.BlockSpec((None, 1, d), per_b)
    return pl.pallas_call(
        _gmlp_kernel,
        grid=(b, s // tm),
        in_specs=[pl.BlockSpec((None, tm, d), row),
                  pl.BlockSpec((TOP_K, None, tm, d), lambda bi, i: (0, bi, i, 0)),
                  pl.BlockSpec((None, tm, LOGIT_PAD), row), vec_d, mod_d,
                  vec_d, mod_d, mod_d,
                  pl.BlockSpec((d, 2 * GM_WIDTH), const), vec_g, vec_g,
                  pl.BlockSpec((GM_GROUPS, CHUNK, CHUNK), lambda bi, i: (0, 0, 0)),
                  pl.BlockSpec((CHUNK, GM_WIDTH), const), pl.BlockSpec((GM_WIDTH, d), const),
                  vec_d, mod_d, vec_d, mod_d, mod_d,
                  pl.BlockSpec((d, 2 * LOGIT_PAD), const), pl.BlockSpec((1, LOGIT_PAD), const)],
        out_specs=_token_out_specs(tm, d),
        out_shape=_token_out_shapes(b, s, d),
        compiler_params=_cparams(("parallel", "parallel")),
        name="gmlp_router",
    )(x, yk, gates_in, n_post_in.reshape(1, d), g2_in, n_pre1.reshape(1, d), sh1, sc1, w_in.astype(BF16), vn_w.reshape(1, -1), vn_b.reshape(1, -1),
      w_s.astype(BF16), bs_full, w_out.astype(BF16), n_post.reshape(1, d), g1, n_pre.reshape(1, d), sh2, sc2,
      rw_pad, rb_pad)


MOE_ROWS = 256


def _expert_kernel(be_ref, na_ref, x_ref, wgu_ref, bgu_ref, wd_ref, bd_ref, y_ref, wgu_bf, wd_bf):
    i = pl.program_id(0)
    prev = be_ref[jnp.maximum(i - 1, 0)]
    changed = jnp.logical_or(i == 0, be_ref[i] != prev)

    @pl.when(changed)
    def _():
        wgu_bf[...] = wgu_ref[...].astype(BF16)
        wd_bf[...] = wd_ref[...].astype(BF16)

    @pl.when(i < na_ref[0])
    def _():
        gu = _dot(x_ref[...], wgu_bf[...]) + bgu_ref[...]
        x_glu = jnp.minimum(gu[:, :EXPERT_FF], SWIGLU_LIMIT)
        x_lin = jnp.clip(gu[:, EXPERT_FF:], -SWIGLU_LIMIT, SWIGLU_LIMIT)
        act = x_glu * jax.nn.sigmoid(SWIGLU_ALPHA * x_glu) * (x_lin + 1.0)
        y_ref[...] = (_dot(act.astype(BF16), wd_bf[...]) + bd_ref[...]).astype(BF16)

    @pl.when(i >= na_ref[0])
    def _():
        y_ref[...] = jnp.zeros(y_ref.shape, BF16)


def _experts(blk_expert, n_active, xg, layer, w_gu, b_gu, w_down, b_down):
    n_rows, d = xg.shape
    n_l, n_e, _, ff2 = w_gu.shape
    tb = MOE_ROWS
    n_blocks = n_rows // tb
    return pl.pallas_call(
        _expert_kernel,
        grid_spec=pltpu.PrefetchScalarGridSpec(
            num_scalar_prefetch=2,
            grid=(n_blocks,),
            in_specs=[pl.BlockSpec((tb, d), lambda i, be, na: (i, 0)),
                      pl.BlockSpec((None, None, d, ff2), lambda i, be, na: (layer, be[i], 0, 0)),
                      pl.BlockSpec((None, None, 1, ff2), lambda i, be, na: (layer, be[i], 0, 0)),
                      pl.BlockSpec((None, None, ff2 // 2, d), lambda i, be, na: (layer, be[i], 0, 0)),
                      pl.BlockSpec((None, None, 1, d), lambda i, be, na: (layer, be[i], 0, 0))],
            out_specs=pl.BlockSpec((tb, d), lambda i, be, na: (i, 0)),
            scratch_shapes=[pltpu.VMEM((d, ff2), BF16), pltpu.VMEM((ff2 // 2, d), BF16)]),
        out_shape=jax.ShapeDtypeStruct((n_rows, d), BF16),
        compiler_params=_cparams(("arbitrary",)),
        name="moe_experts",
    )(blk_expert, n_active, xg, w_gu, b_gu.reshape(n_l, n_e, 1, ff2), w_down, b_down.reshape(n_l, n_e, 1, d))


def _combine_kernel(x_ref, y_ref, gate_ref, npost_ref, g2_ref, o_ref):
    o_ref[...] = _moe_residual(x_ref, y_ref, gate_ref, npost_ref, g2_ref)


def _combine(x1, yk, gates, n_post, g2):
    b, s, d = x1.shape
    tm = _token_tile(s)
    row = lambda bi, i: (bi, i, 0)
    return pl.pallas_call(
        _combine_kernel,
        grid=(b, s // tm),
        in_specs=[pl.BlockSpec((None, tm, d), row),
                  pl.BlockSpec((TOP_K, None, tm, d), lambda bi, i: (0, bi, i, 0)),
                  pl.BlockSpec((None, tm, LOGIT_PAD), row),
                  pl.BlockSpec((1, d), lambda bi, i: (0, 0)),
                  pl.BlockSpec((None, 1, d), lambda bi, i: (bi, 0, 0))],
        out_specs=pl.BlockSpec((None, tm, d), row),
        out_shape=jax.ShapeDtypeStruct((b, s, d), F32),
        compiler_params=_cparams(("parallel", "parallel")),
        name="moe_combine_residual",
    )(x1, yk, gates, n_post.reshape(1, d), g2)


def _lookup(table, idx):
    n = table.shape[0]
    return jnp.sum(jnp.where(idx[:, None] == jnp.arange(n, dtype=jnp.int32)[None, :], table[None, :], 0), axis=1)


def _rows(a, idx):
    return a.at[idx].get(mode="promise_in_bounds")


def _moe_rows(hf, top_idx, layer, w_gu, b_gu, w_down, b_down):
    b, s, d = hf.shape
    n_tok = b * s
    tb = MOE_ROWS
    n_assign = n_tok * TOP_K
    assert N_EXPERTS * n_assign < 2 ** 31
    i32 = jnp.int32
    e_flat = top_idx[..., :TOP_K].reshape(-1)
    skey = jnp.sort(e_flat * n_assign + jnp.arange(n_assign, dtype=i32))
    order = skey % n_assign
    _, inv = lax.sort((order, jnp.arange(n_assign, dtype=i32)), num_keys=1)
    edges = jnp.arange(N_EXPERTS + 1, dtype=i32) * n_assign
    bounds = jnp.sum((skey[None, :] < edges[:, None]).astype(i32), axis=1)
    start = bounds[:-1]
    counts = bounds[1:] - start
    padded = (counts + tb - 1) // tb * tb
    pend = jnp.cumsum(padded)
    pstart = pend - padded
    n_rows = -(-n_assign // tb) * tb + N_EXPERTS * tb
    n_blocks = n_rows // tb
    blk_start = jnp.arange(n_blocks, dtype=i32) * tb
    blk_expert = jnp.minimum(jnp.sum((pend[None, :] <= blk_start[:, None]).astype(i32), axis=1), N_EXPERTS - 1)
    n_active = (pend[-1] // tb).astype(i32).reshape(1)
    j = jnp.arange(n_rows, dtype=i32) - jnp.repeat(_lookup(pstart, blk_expert), tb)
    src = jnp.repeat(_lookup(start, blk_expert), tb) + j
    valid = j < jnp.repeat(_lookup(counts, blk_expert), tb)
    row_tok = jnp.where(valid, _rows(order, jnp.clip(src, 0, n_assign - 1)) // TOP_K, 0)
    pos = _lookup(pstart - start, e_flat) + inv
    xg = _rows(hf.reshape(n_tok, d), row_tok)
    y = _experts(blk_expert, n_active, xg, layer, w_gu, b_gu, w_down, b_down)
    return _rows(y, pos.reshape(n_tok, TOP_K).T.reshape(-1)).reshape(TOP_K, b, s, d)


def _block_diag(n, blk, val):
    return (jnp.kron(jnp.eye(n // blk, dtype=F32), jnp.ones((blk, blk), F32)) * val).astype(BF16)


def kernel(x, c, ctx, c_ctx, ada_w, ada_b, norm_mix_pre, norm_mix_post, norm_ffn_pre, norm_ffn_post, router_w, router_b, moe_w_gu, moe_b_gu, moe_w_down, moe_b_down, hy_w_in, mla_q_norm, mla_w_uq, mla_kv_norm, mla_w_ukv, rwkv_mu_prev, rwkv_mu_next, rwkv_w0, rwkv_w2, rwkv_a0, rwkv_a2, rwkv_g2, rwkv_k_k, rwkv_k_a, rwkv_r_k, rwkv_ln_w, rwkv_ln_b, hy_w_out, gm_w_in, gm_v_norm_w, gm_v_norm_b, gm_w_s, gm_b_s, gm_w_out):
    b, s, d = x.shape
    n_ctx = ctx.shape[1]
    t_all = s + n_ctx
    assert b + 1 <= 8 and s % ROW_TILE == 0 and n_ctx % ROW_TILE == 0

    cond_rows = jnp.concatenate([c, c_ctx[None], jnp.zeros((8 - b - 1, d), F32)], axis=0)
    mod = _modulation(cond_rows, ada_w, ada_b)

    def lat_mod(l, j):
        return mod[l, :b, j * d:(j + 1) * d].reshape(b, 1, d)

    sh_all = jnp.stack([mod[0, :b, 0:d], jnp.broadcast_to(mod[0, b, 0:d], (b, d))], axis=1).reshape(2 * b, 1, d)
    sc_all = jnp.stack([mod[0, :b, d:2 * d], jnp.broadcast_to(mod[0, b, d:2 * d], (b, d))], axis=1).reshape(2 * b, 1, d)
    cos, sin = _rope_tables(s, n_ctx)
    w1, wq2, wk2 = _inproj_weights(hy_w_in[0], mla_w_uq[0], mla_w_ukv[0])
    q, k, v, prw = _inproj(x, ctx, norm_mix_pre[0], sh_all, sc_all, cos, sin, w1, mla_q_norm[0], mla_kv_norm[0],
                           wq2, wk2)
    tq = 512 if s % 512 == 0 else ROW_TILE
    tk = next((t for t in (8320, 1280) if t_all % t == 0), ROW_TILE)
    o_attn = _attention(q, k, v, s, tq, tk)

    bd_ones = _block_diag(RWKV_DIM, RWKV_HEAD, 1.0)
    bd_mean = _block_diag(RWKV_DIM, RWKV_HEAD, 1.0 / RWKV_HEAD)
    r, kx, vx, g, kkn, lw, kd, bdir = _rwkv_prepare(prw, s, rwkv_mu_prev[0], rwkv_mu_next[0], rwkv_w0[0], rwkv_w2[0],
                                                    rwkv_a0[0], rwkv_a2[0], rwkv_g2[0], rwkv_k_k[0], rwkv_k_a[0],
                                                    bd_ones)
    m_c, n_c, r_c, y_c = _wkv_chunks(r, vx, kkn, lw, kd, bdir)
    cpb = 2 if (s // WKV_CHUNK) % 2 == 0 and (n_ctx // WKV_CHUNK) % 2 == 0 else 1
    n_lat_blk = s // (WKV_CHUNK * cpb)
    n_ctx_blk = n_ctx // (WKV_CHUNK * cpb)
    lat_blk = np.arange(n_lat_blk)
    ctx_blk = n_lat_blk + np.arange(n_ctx_blk)
    order = jnp.asarray(np.stack([np.concatenate([ctx_blk, lat_blk]),
                                  np.concatenate([ctx_blk[::-1], lat_blk[::-1]])]).astype(np.int32))
    yf, yb = _wkv_states(order, m_c, n_c, r_c, y_c, t_all, cpb)

    rw_pad, rb_pad = _router_pads(router_w[0], router_b[0])
    x1, hf, top_idx, gates = _mix_out(yf, yb, r, kx, vx, g, o_attn, x, rwkv_r_k[0].reshape(-1), rwkv_ln_w[0],
                                      rwkv_ln_b[0], bd_mean, bd_ones, hy_w_out[0].astype(BF16), norm_mix_post[0],
                                      lat_mod(0, 2), norm_ffn_pre[0], lat_mod(0, 3), lat_mod(0, 4), rw_pad, rb_pad)
    yk = _moe_rows(hf, top_idx, 0, moe_w_gu, moe_b_gu, moe_w_down, moe_b_down)

    rw_pad, rb_pad = _router_pads(router_w[1], router_b[1])
    x3, hf, top_idx, gates = _gmlp(x1, yk, gates, norm_ffn_post[0], lat_mod(0, 5), norm_mix_pre[1], lat_mod(1, 0), lat_mod(1, 1), gm_w_in[0], gm_v_norm_w[0],
                                   gm_v_norm_b[0], gm_w_s[0], gm_b_s[0], gm_w_out[0], norm_mix_post[1], lat_mod(1, 2),
                                   norm_ffn_pre[1], lat_mod(1, 3), lat_mod(1, 4), rw_pad, rb_pad)
    yk = _moe_rows(hf, top_idx, 1, moe_w_gu, moe_b_gu, moe_w_down, moe_b_down)
    return _combine(x3, yk, gates, norm_ffn_post[1], lat_mod(1, 5))
```
